```python
import math
import jax, jax.numpy as jnp
from jax import lax
import numpy as np

D_MODEL = 1024
BATCH = 4
SEQ = 4096
DEPTH = 2

GRID_W = 64
CTX_LEN = 256
HEAD_DIM = 64
N_EVEN = (DEPTH + 1) // 2
N_ODD = DEPTH // 2
ROPE_THETA = 10000.0
EPS = 1e-6
ATTN_SCALE = HEAD_DIM ** -0.5

A_HEADS = 8
A_KV_HEADS = 2
A_WINDOW = 128
A_BLOCK = 128
B_HEADS = 8
B_WIN_H = 8
B_WIN_W = 16
C_HEADS = 8
C_KV_HEADS = 2
C_BLOCK = 128
D_CH = 512
HY_ORDER = 2
HY_SHORT = 3
HY_BANDS = 16
HY_EMB = 1 + 2 * HY_BANDS
HY_FFN = 64
HY_MAX_DECAY = math.log(1e-2) / 0.3
HY_MIN_DECAY = math.log(1e-2) / 1.5

A_Q = A_HEADS * HEAD_DIM
A_KV = A_KV_HEADS * HEAD_DIM
B_W = B_HEADS * HEAD_DIM
C_Q = C_HEADS * HEAD_DIM
C_KV = C_KV_HEADS * HEAD_DIM
EVEN_SPLIT = (A_Q, A_KV, A_KV, B_W, B_W, B_W)
ODD_SPLIT = (C_Q, C_KV, C_KV, 3 * D_CH)
EVEN_IN = A_Q + 2 * A_KV + 3 * B_W
ODD_IN = C_Q + 2 * C_KV + 3 * D_CH
EVEN_OUT = A_Q + B_W
ODD_OUT = C_Q + D_CH

N_EXPERTS = 16
N_GROUPS = 4
EXPERTS_PER_GROUP = N_EXPERTS // N_GROUPS
TOP_K = 2
D_EXPERT = 512

kernel_name = 'hybrid_flow_backbone_block'

F32 = jnp.float32


def rmsnorm(x, g):
    xf = x.astype(F32)
    y = xf * lax.rsqrt(jnp.mean(xf * xf, axis=-1, keepdims=True) + EPS)
    return (y * g.astype(F32)).astype(x.dtype)


def split_cols(p, sizes):
    return jnp.split(p, [int(v) for v in np.cumsum(sizes)[:-1]], axis=-1)


def heads(t, n):
    return t.reshape(t.shape[0], t.shape[1], n, HEAD_DIM)


def joint_softmax(parts):
    sizes = [p.shape[-1] for p in parts]
    probs = jax.nn.softmax(jnp.concatenate(parts, axis=-1), axis=-1)
    return jnp.split(probs, [int(v) for v in np.cumsum(sizes)[:-1]], axis=-1)


def rope_tables(S):
    t = jnp.arange(S)
    row = (t // GRID_W).astype(F32)
    col = (t % GRID_W).astype(F32)
    half = HEAD_DIM // 2
    inv = ROPE_THETA ** (-jnp.arange(0, half, 2, dtype=F32) / half)
    ar = row[:, None] * inv[None]
    ac = col[:, None] * inv[None]
    return (jnp.cos(ar)[:, None, :], jnp.sin(ar)[:, None, :], jnp.cos(ac)[:, None, :], jnp.sin(ac)[:, None, :])


def _rot(x, cos, sin):
    m = x.shape[-1] // 2
    x1, x2 = x[..., :m], x[..., m:]
    return jnp.concatenate([x1 * cos - x2 * sin, x2 * cos + x1 * sin], axis=-1)


def rope2d(x, rope):
    cr, sr, cc, sc = rope
    xf = x.astype(F32)
    half = HEAD_DIM // 2
    return jnp.concatenate([_rot(xf[..., :half], cr, sr), _rot(xf[..., half:], cc, sc)], axis=-1).astype(x.dtype)


def ctx_self_attention(q, k, v, sink=None):
    B, L, H, d = q.shape
    HKV = k.shape[2]
    G = H // HKV
    qg = q.reshape(B, L, HKV, G, d)
    s = jnp.einsum('bqkgd,bjkd->bkgqj', qg, k).astype(F32) * ATTN_SCALE
    parts = [s]
    if sink is not None:
        parts.append(jnp.broadcast_to(sink.astype(F32).reshape(1, HKV, G, 1, 1), s.shape[:-1] + (1,)))
    p = joint_softmax(parts)[0]
    o = jnp.einsum('bkgqj,bjkd->bqkgd', p.astype(v.dtype), v)
    return o.reshape(B, L, H * d)


def window_attention(q, k, v, kc, vc, sink):
    B, S, H, d = q.shape
    HKV = k.shape[2]
    G = H // HKV
    nb = S // A_BLOCK
    n_side = -(-A_WINDOW // A_BLOCK)
    pad = n_side * A_BLOCK
    nw = 2 * n_side + 1
    qb = q.reshape(B, nb, A_BLOCK, HKV, G, d)
    kp = jnp.pad(k, ((0, 0), (pad, pad), (0, 0), (0, 0))).reshape(B, nb + 2 * n_side, A_BLOCK, HKV, d)
    vp = jnp.pad(v, ((0, 0), (pad, pad), (0, 0), (0, 0))).reshape(B, nb + 2 * n_side, A_BLOCK, HKV, d)
    kw = jnp.concatenate([kp[:, i:i + nb] for i in range(nw)], axis=2)
    vw = jnp.concatenate([vp[:, i:i + nb] for i in range(nw)], axis=2)
    qpos = (jnp.arange(nb) * A_BLOCK)[:, None] + jnp.arange(A_BLOCK)[None]
    kpos = (jnp.arange(nb) * A_BLOCK - pad)[:, None] + jnp.arange(nw * A_BLOCK)[None]
    mask = ((jnp.abs(qpos[:, :, None] - kpos[:, None, :]) <= A_WINDOW)
            & (kpos[:, None, :] >= 0) & (kpos[:, None, :] < S))
    s_loc = jnp.einsum('bnqkgd,bnjkd->bnkgqj', qb, kw).astype(F32) * ATTN_SCALE
    s_loc = jnp.where(mask[None, :, None, None], s_loc, -jnp.inf)
    s_ctx = jnp.einsum('bnqkgd,bjkd->bnkgqj', qb, kc).astype(F32) * ATTN_SCALE
    s_sink = jnp.broadcast_to(sink.astype(F32).reshape(1, 1, HKV, G, 1, 1), s_loc.shape[:-1] + (1,))
    p_loc, p_ctx, _ = joint_softmax([s_loc, s_ctx, s_sink])
    o = (jnp.einsum('bnkgqj,bnjkd->bnqkgd', p_loc.astype(v.dtype), vw)
         + jnp.einsum('bnkgqj,bjkd->bnqkgd', p_ctx.astype(vc.dtype), vc))
    return o.reshape(B, S, H * d)


def neighbourhood_attention(q, k, v, kc, vc, rpb):
    B, S, H, d = q.shape
    R = S // GRID_W
    kh = min(B_WIN_H, R)
    qg = q.reshape(B, R, GRID_W, H, d)
    kg = k.reshape(B, R, GRID_W, H, d)
    vg = v.reshape(B, R, GRID_W, H, d)
    r = jnp.arange(R)
    rs = jnp.clip(r - kh // 2, 0, R - kh)
    row_idx = rs[:, None] + jnp.arange(kh)[None]
    k_rows = kg[:, row_idx]
    v_rows = vg[:, row_idx]
    col = jnp.arange(GRID_W)
    cs = jnp.clip(col - B_WIN_W // 2, 0, GRID_W - B_WIN_W)
    colmask = (col[None, :] >= cs[:, None]) & (col[None, :] < cs[:, None] + B_WIN_W)
    dr = row_idx - r[:, None]
    dc = jnp.clip(col[None, :] - col[:, None], -(B_WIN_W - 1), B_WIN_W - 1)
    bias = rpb[:, dr[:, :, None, None] + B_WIN_H - 1, dc[None, None] + B_WIN_W - 1]
    bias = bias.transpose(1, 0, 3, 2, 4).astype(F32)
    s = jnp.einsum('brqhd,brkwhd->brhqkw', qg, k_rows).astype(F32) * ATTN_SCALE + bias[None]
    s = jnp.where(colmask[:, None, :], s, -jnp.inf).reshape(B, R, H, GRID_W, kh * GRID_W)
    s_ctx = jnp.einsum('brqhd,bjhd->brhqj', qg, kc).astype(F32) * ATTN_SCALE
    p_loc, p_ctx = joint_softmax([s, s_ctx])
    p_loc = p_loc.reshape(B, R, H, GRID_W, kh, GRID_W)
    o = (jnp.einsum('brhqkw,brkwhd->brqhd', p_loc.astype(v.dtype), v_rows)
         + jnp.einsum('brhqj,bjhd->brqhd', p_ctx.astype(vc.dtype), vc))
    return o.reshape(B, S, H * d)


def block_attention(q, k, v):
    B, S, H, d = q.shape
    HKV = k.shape[2]
    G = H // HKV
    nb = S // C_BLOCK
    qb = q.reshape(B, nb, C_BLOCK, HKV, G, d).transpose(1, 0, 2, 3, 4, 5)

    def one(qi):
        s = jnp.einsum('bqkgd,bjkd->bkgqj', qi, k).astype(F32) * ATTN_SCALE
        p = jax.nn.softmax(s, axis=-1).astype(v.dtype)
        return jnp.einsum('bkgqj,bjkd->bqkgd', p, v)

    o = lax.map(one, qb)
    return o.transpose(1, 0, 2, 3, 4, 5).reshape(B, S, H * d)


def short_conv(u, w, b):
    K, C = w.shape
    pad = K // 2
    y = lax.conv_general_dilated(u, w[:, None, :].astype(u.dtype), window_strides=(1,),
                                 padding=[(pad, K - 1 - pad)], dimension_numbers=('NWC', 'WIO', 'NWC'),
                                 feature_group_count=C)
    return y + b


def implicit_filters(L, w1, b1, f1, w2, b2, f2, w3, b3):
    t = jnp.arange(L, dtype=F32)
    tn = t / max(L - 1, 1)
    bands = jnp.linspace(1e-4, HY_BANDS - 1, HY_BANDS, dtype=F32)
    ang = 2.0 * math.pi * t[:, None] * bands[None] / L
    feats = jnp.concatenate([tn[:, None], jnp.cos(ang), jnp.sin(ang)], axis=-1)
    h = jnp.sin(f1 * (feats @ w1 + b1))
    h = jnp.sin(f2 * (h @ w2 + b2))
    h = (h @ w3 + b3).astype(F32).reshape(L, 2, HY_ORDER, D_CH)
    deltas = jnp.abs(jnp.linspace(HY_MIN_DECAY, HY_MAX_DECAY, D_CH, dtype=F32))
    decay = jnp.exp(-tn[:, None] * deltas[None])
    h = h * decay[:, None, None, :]
    return h / (jnp.sum(jnp.abs(h), axis=(0, 1), keepdims=True) + EPS)


def long_conv(z, hf, hb):
    L, C = hf.shape
    kf = jnp.concatenate([hf, jnp.zeros((1, C), F32), hb[1:][::-1]], axis=0)
    Z = jnp.fft.rfft(z.astype(F32), n=2 * L, axis=1)
    Kf = jnp.fft.rfft(kf, axis=0)
    y = jnp.fft.irfft(Z * Kf[None], n=2 * L, axis=1)[:, :L]
    return y.astype(z.dtype)


def hyena(u, sw, sb, w1, b1, f1, w2, b2, f2, w3, b3, hbias):
    L = u.shape[1]
    u = short_conv(u, sw, sb)
    v, g1, g2 = split_cols(u, (D_CH, D_CH, D_CH))
    h = implicit_filters(L, w1, b1, f1, w2, b2, f2, w3, b3)
    z = v
    for o, g in enumerate((g1, g2)):
        z = g * (long_conv(z, h[:, 0, o], h[:, 1, o]) + z * hbias[o])
    return z


def even_mixer(hx, hc, w_in, w_out, sink, rpb, rope, need_ctx):
    aq, ak, av, bq, bk, bv = split_cols(hx @ w_in, EVEN_SPLIT)
    caq, cak, cav, cbq, cbk, cbv = split_cols(hc @ w_in, EVEN_SPLIT)
    aq, ak, av = heads(aq, A_HEADS), heads(ak, A_KV_HEADS), heads(av, A_KV_HEADS)
    bq, bk, bv = heads(bq, B_HEADS), heads(bk, B_HEADS), heads(bv, B_HEADS)
    caq, cak, cav = heads(caq, A_HEADS), heads(cak, A_KV_HEADS), heads(cav, A_KV_HEADS)
    cbq, cbk, cbv = heads(cbq, B_HEADS), heads(cbk, B_HEADS), heads(cbv, B_HEADS)
    ya = window_attention(rope2d(aq, rope), rope2d(ak, rope), av, cak, cav, sink)
    yb = neighbourhood_attention(bq, bk, bv, cbk, cbv, rpb)
    yx = jnp.concatenate([ya, yb], axis=-1) @ w_out
    if not need_ctx:
        return yx, None
    yc = jnp.concatenate([ctx_self_attention(caq, cak, cav, sink), ctx_self_attention(cbq, cbk, cbv)], axis=-1) @ w_out
    return yx, yc


def odd_mixer(hx, hc, w_in, w_out, qn, kn, hy, rope, need_ctx):
    qx, kx, vx, ux = split_cols(hx @ w_in, ODD_SPLIT)
    qc, kc, vc, uc = split_cols(hc @ w_in, ODD_SPLIT)
    qx = rope2d(rmsnorm(heads(qx, C_HEADS), qn), rope)
    kx = rope2d(rmsnorm(heads(kx, C_KV_HEADS), kn), rope)
    vx = heads(vx, C_KV_HEADS)
    qc = rmsnorm(heads(qc, C_HEADS), qn)
    kc = rmsnorm(heads(kc, C_KV_HEADS), kn)
    vc = heads(vc, C_KV_HEADS)
    y_attn = block_attention(qx, jnp.concatenate([kc, kx], axis=1), jnp.concatenate([vc, vx], axis=1))
    y_hy = hyena(ux, *hy)
    yx = jnp.concatenate([y_attn, y_hy], axis=-1) @ w_out
    if not need_ctx:
        return yx, None
    yc = jnp.concatenate([ctx_self_attention(qc, kc, vc), hyena(uc, *hy)], axis=-1) @ w_out
    return yx, yc


def moe(h, w_router, b_router, wg, wu, wd):
    Bsz, L, D = h.shape
    t = h.reshape(Bsz * L, D)
    T = t.shape[0]
    s = jax.nn.sigmoid((t @ w_router).astype(F32))
    sel = s + b_router.astype(F32)
    gscore = lax.top_k(sel.reshape(T, N_GROUPS, EXPERTS_PER_GROUP), TOP_K)[0].sum(-1)
    gbest = jnp.argmax(gscore, axis=-1)
    in_group = (jnp.arange(N_EXPERTS) // EXPERTS_PER_GROUP)[None, :] == gbest[:, None]
    _, idx = lax.top_k(jnp.where(in_group, sel, -jnp.inf), TOP_K)
    wsel = jnp.take_along_axis(s, idx, axis=-1)
    wsel = wsel / jnp.sum(wsel, axis=-1, keepdims=True)
    combine = jnp.sum(jax.nn.one_hot(idx, N_EXPERTS, dtype=F32) * wsel[..., None], axis=1).astype(h.dtype)
    y = jnp.zeros_like(t)
    for e in range(N_EXPERTS):
        he = jax.nn.silu(t @ wg[e]) * (t @ wu[e])
        y = y + combine[:, e:e + 1] * (he @ wd[e])
    return y.reshape(Bsz, L, D)


def setup_inputs(seed: int = 0) -> dict:
    key = jax.random.key(seed)
    ks = iter(jax.random.split(key, 40))

    def nrm(shape, scale):
        return jax.random.normal(next(ks), shape, F32) * scale

    D = D_MODEL
    return {
        'x': nrm((BATCH, SEQ, D), 1.0),
        'c': nrm((BATCH, D), 1.0),
        'ctx': nrm((BATCH, CTX_LEN, D), 1.0),
        'c_ctx': nrm((D,), 1.0),
        'w_ada': nrm((DEPTH, D, 6 * D), 0.5 * D ** -0.5),
        'b_ada': nrm((DEPTH, 6 * D), 0.02),
        'norm_g': 1.0 + nrm((DEPTH, 2, D), 0.02),
        'final_g': 1.0 + nrm((D,), 0.02),
        'w_in_even': nrm((N_EVEN, D, EVEN_IN), D ** -0.5),
        'w_out_even': nrm((N_EVEN, EVEN_OUT, D), EVEN_OUT ** -0.5),
        'a_sink': nrm((N_EVEN, A_HEADS), 0.5),
        'b_rpb': nrm((N_EVEN, B_HEADS, 2 * B_WIN_H - 1, 2 * B_WIN_W - 1), 0.1),
        'w_in_odd': nrm((N_ODD, D, ODD_IN), D ** -0.5),
        'w_out_odd': nrm((N_ODD, ODD_OUT, D), ODD_OUT ** -0.5),
        'c_qnorm': 1.0 + nrm((N_ODD, HEAD_DIM), 0.02),
        'c_knorm': 1.0 + nrm((N_ODD, HEAD_DIM), 0.02),
        'hy_short_w': nrm((N_ODD, HY_SHORT, 3 * D_CH), HY_SHORT ** -0.5),
        'hy_short_b': nrm((N_ODD, 3 * D_CH), 0.02),
        'hy_w1': nrm((N_ODD, HY_EMB, HY_FFN), HY_EMB ** -0.5),
        'hy_b1': nrm((N_ODD, HY_FFN), 0.02),
        'hy_f1': 1.0 + nrm((N_ODD, HY_FFN), 0.1),
        'hy_w2': nrm((N_ODD, HY_FFN, HY_FFN), HY_FFN ** -0.5),
        'hy_b2': nrm((N_ODD, HY_FFN), 0.02),
        'hy_f2': 1.0 + nrm((N_ODD, HY_FFN), 0.1),
        'hy_w3': nrm((N_ODD, HY_FFN, 2 * HY_ORDER * D_CH), HY_FFN ** -0.5),
        'hy_b3': nrm((N_ODD, 2 * HY_ORDER * D_CH), 0.02),
        'hy_bias': nrm((N_ODD, HY_ORDER, D_CH), 0.5),
        'w_router': nrm((D, N_EXPERTS), D ** -0.5),
        'b_router': nrm((N_EXPERTS,), 0.01),
        'moe_wg': nrm((DEPTH, N_EXPERTS, D, D_EXPERT), D ** -0.5),
        'moe_wu': nrm((DEPTH, N_EXPERTS, D, D_EXPERT), D ** -0.5),
        'moe_wd': nrm((DEPTH, N_EXPERTS, D_EXPERT, D), D_EXPERT ** -0.5),
    }


def reference(x, c, ctx, c_ctx, w_ada, b_ada, norm_g, final_g,
              w_in_even, w_out_even, a_sink, b_rpb,
              w_in_odd, w_out_odd, c_qnorm, c_knorm,
              hy_short_w, hy_short_b, hy_w1, hy_b1, hy_f1, hy_w2, hy_b2, hy_f2, hy_w3, hy_b3, hy_bias,
              w_router, b_router, moe_wg, moe_wu, moe_wd):
    B, S, D = x.shape
    rope = rope_tables(S)
    sc = jax.nn.silu(c)
    scc = jax.nn.silu(c_ctx)
    xc = ctx
    for l in range(DEPTH):
        need_ctx = l < DEPTH - 1
        mx = (sc @ w_ada[l] + b_ada[l]).reshape(B, 6, 1, D)
        mc = (scc @ w_ada[l] + b_ada[l]).reshape(6, 1, 1, D)
        hx = rmsnorm(x, norm_g[l, 0]) * (1.0 + mx[:, 1]) + mx[:, 0]
        hc = rmsnorm(xc, norm_g[l, 0]) * (1.0 + mc[1]) + mc[0]
        i = l // 2
        if l % 2 == 0:
            yx, yc = even_mixer(hx, hc, w_in_even[i], w_out_even[i], a_sink[i], b_rpb[i], rope, need_ctx)
        else:
            hy = (hy_short_w[i], hy_short_b[i], hy_w1[i], hy_b1[i], hy_f1[i], hy_w2[i], hy_b2[i], hy_f2[i],
                  hy_w3[i], hy_b3[i], hy_bias[i])
            yx, yc = odd_mixer(hx, hc, w_in_odd[i], w_out_odd[i], c_qnorm[i], c_knorm[i], hy, rope, need_ctx)
        x = x + mx[:, 2] * yx
        hx = rmsnorm(x, norm_g[l, 1]) * (1.0 + mx[:, 4]) + mx[:, 3]
        if need_ctx:
            xc = xc + mc[2] * yc
            hc = rmsnorm(xc, norm_g[l, 1]) * (1.0 + mc[4]) + mc[3]
            y_all = moe(jnp.concatenate([hc, hx], axis=1), w_router, b_router, moe_wg[l], moe_wu[l], moe_wd[l])
            xc = xc + mc[5] * y_all[:, :CTX_LEN]
            x = x + mx[:, 5] * y_all[:, CTX_LEN:]
        else:
            x = x + mx[:, 5] * moe(hx, w_router, b_router, moe_wg[l], moe_wu[l], moe_wd[l])
    return rmsnorm(x, final_g)
```

```python
import functools
import math

import numpy as np
import jax
import jax.numpy as jnp
from jax import lax
from jax.experimental import pallas as pl
from jax.experimental.pallas import tpu as pltpu

F32 = jnp.float32
BF16 = jnp.bfloat16

D_MODEL = 1024
GRID_W = 64
CTX_LEN = 256
HEAD_DIM = 64
ROPE_THETA = 10000.0
EPS = 1e-6
ATTN_SCALE = HEAD_DIM ** -0.5
A_WINDOW = 128
A_BLOCK = 128
B_WIN_H = 8
B_WIN_W = 16
D_CH = 512
HY_BANDS = 16
HY_MAX_DECAY = math.log(1e-2) / 0.3
HY_MIN_DECAY = math.log(1e-2) / 1.5
N_EXPERTS = 16
N_GROUPS = 4
EXPERTS_PER_GROUP = N_EXPERTS // N_GROUPS
TOP_K = 2
D_EXPERT = 512

LANES = 128
NEG = -1e30
VMEM_LIMIT = 48 * 1024 * 1024

FFT_N1 = 64
FFT_N2 = 128


def _cparams(sem):
    return pltpu.CompilerParams(dimension_semantics=sem, vmem_limit_bytes=VMEM_LIMIT)


def _mm_f32_kernel(x_ref, w_ref, o_ref):
    o_ref[...] = jnp.dot(x_ref[...], w_ref[...], preferred_element_type=F32)


def _mm_f32(x, w, tn):
    M, K = x.shape
    N = w.shape[1]
    return pl.pallas_call(
        _mm_f32_kernel,
        grid=(N // tn,),
        in_specs=[pl.BlockSpec((M, K), lambda j: (0, 0)),
                  pl.BlockSpec((K, tn), lambda j: (0, j))],
        out_specs=pl.BlockSpec((M, tn), lambda j: (0, j)),
        out_shape=jax.ShapeDtypeStruct((M, N), F32),
        compiler_params=_cparams(("arbitrary",)),
        name="mm_f32",
    )(x, w)


def _swap16(y):
    n = y.shape[-1]
    lane = lax.broadcasted_iota(jnp.int32, y.shape, y.ndim - 1)
    up = pltpu.roll(y, n - 16, axis=y.ndim - 1)
    dn = pltpu.roll(y, 16, axis=y.ndim - 1)
    return jnp.where((lane % 32) < 16, up, dn)


def _tile_lanes(t, width):
    reps = width // t.shape[-1]
    return t if reps == 1 else jnp.concatenate([t] * reps, axis=-1)


def _norm_proj_kernel(segs, has_rope, has_norm, *refs):
    it = iter(refs)
    x_ref, g_ref, shift_ref, scale_ref, w_ref = (next(it) for _ in range(5))
    cos_ref = sin_ref = bd_ref = gain_ref = None
    if has_rope:
        cos_ref, sin_ref = next(it), next(it)
    if has_norm:
        bd_ref, gain_ref = next(it), next(it)
    out_refs = list(it)

    x = x_ref[0]
    ms = jnp.mean(x * x, axis=-1, keepdims=True)
    h = x * lax.rsqrt(ms + EPS) * g_ref[...]
    h = h * (1.0 + scale_ref[0]) + shift_ref[0]
    y = jnp.dot(h.astype(BF16), w_ref[...], preferred_element_type=F32)

    off = 0
    goff = 0
    for (width, kind, mult), o_ref in zip(segs, out_refs):
        ys = y[:, off:off + width]
        if kind in ("norm", "normrope"):
            bd = bd_ref[...][:width, :width]
            hms = jnp.dot((ys * ys).astype(BF16), bd, preferred_element_type=F32)
            ys = ys * lax.rsqrt(hms + EPS) * gain_ref[:, goff:goff + width]
            goff += width
        if kind in ("rope", "normrope"):
            c = _tile_lanes(cos_ref[...], width)
            s = _tile_lanes(sin_ref[...], width)
            ys = ys * c + _swap16(ys) * s
        if mult != 1.0:
            ys = ys * mult
        o_ref[0] = ys.astype(o_ref.dtype)
        off += width


def _norm_proj(x, g, shift, scale, w, segs, rope=None, norm=None, tm=512):
    B, S, D = x.shape
    N = w.shape[1]
    tm = min(tm, S)
    bm = shift.shape[0]
    mod_map = (lambda b, i: (b, 0, 0)) if bm > 1 else (lambda b, i: (0, 0, 0))
    args = [x, g.reshape(1, D), shift.reshape(bm, 1, D), scale.reshape(bm, 1, D), w]
    in_specs = [pl.BlockSpec((1, tm, D), lambda b, i: (b, i, 0)),
                pl.BlockSpec((1, D), lambda b, i: (0, 0)),
                pl.BlockSpec((1, 1, D), mod_map),
                pl.BlockSpec((1, 1, D), mod_map),
                pl.BlockSpec((D, N), lambda b, i: (0, 0))]
    if rope is not None:
        args += [rope[0], rope[1]]
        in_specs += [pl.BlockSpec((tm, LANES), lambda b, i: (i, 0))] * 2
    if norm is not None:
        args += [norm[0], norm[1]]
        in_specs += [pl.BlockSpec(norm[0].shape, lambda b, i: (0, 0)),
                     pl.BlockSpec(norm[1].shape, lambda b, i: (0, 0))]
    out_shape = [jax.ShapeDtypeStruct((B, S, wd), BF16) for wd, _, _ in segs]
    out_specs = [pl.BlockSpec((1, tm, wd), lambda b, i: (b, i, 0)) for wd, _, _ in segs]
    return pl.pallas_call(
        functools.partial(_norm_proj_kernel, segs, rope is not None, norm is not None),
        grid=(B, S // tm),
        in_specs=in_specs,
        out_specs=out_specs,
        out_shape=out_shape,
        compiler_params=_cparams(("parallel", "parallel")),
        name="norm_proj",
    )(*args)


def _half_mask(shape):
    return lax.broadcasted_iota(jnp.int32, shape, len(shape) - 1) < HEAD_DIM


def _stack_halves(qp):
    lo = _half_mask(qp.shape)
    zero = jnp.zeros_like(qp)
    return jnp.concatenate([jnp.where(lo, qp, zero), jnp.where(lo, zero, qp)], axis=0)


def _merge_halves(o, m):
    return jnp.where(_half_mask((m, LANES)), o[:m], o[m:])


def _scores(q, k):
    return lax.dot_general(q, k, (((1,), (1,)), ((), ())), preferred_element_type=F32)


def _joint_softmax_pv(score_parts, value_parts, extra_logit=None):
    m = functools.reduce(jnp.maximum, [jnp.max(s, axis=-1, keepdims=True) for s in score_parts])
    if extra_logit is not None:
        m = jnp.maximum(m, extra_logit)
    den = jnp.exp(extra_logit - m) if extra_logit is not None else 0.0
    acc = None
    for s, v in zip(score_parts, value_parts):
        p = jnp.exp(s - m)
        den = den + jnp.sum(p, axis=-1, keepdims=True)
        pv = jnp.dot(p.astype(BF16), v, preferred_element_type=F32)
        acc = pv if acc is None else acc + pv
    return acc / den


def _sink_column(sink_ref, first_head, n_heads, rows_per_head):
    rows = lax.broadcasted_iota(jnp.int32, (n_heads * rows_per_head, 1), 0)
    col = jnp.zeros((n_heads * rows_per_head, 1), F32)
    for j in range(n_heads):
        in_head = (rows >= j * rows_per_head) & (rows < (j + 1) * rows_per_head)
        col = jnp.where(in_head, sink_ref[first_head + j], col)
    return col


def _window_attn_kernel(seq_len, sink_ref, q_ref, kp_ref, kc_ref, kn_ref, vp_ref, vc_ref, vn_ref,
                        ck_ref, cv_ref, o_ref):
    i = pl.program_id(1)
    blk = A_BLOCK
    q = q_ref[0]
    rows = lax.broadcasted_iota(jnp.int32, (4 * blk, 3 * blk), 0) % blk
    rel = lax.broadcasted_iota(jnp.int32, (4 * blk, 3 * blk), 1) - blk
    gpos = i * blk + rel
    valid = (jnp.abs(rows - rel) <= A_WINDOW) & (gpos >= 0) & (gpos < seq_len)
    outs = []
    for g in range(2):
        ls = slice(g * LANES, (g + 1) * LANES)
        k_loc = jnp.concatenate([kp_ref[0][:, ls], kc_ref[0][:, ls], kn_ref[0][:, ls]], axis=0)
        v_loc = jnp.concatenate([vp_ref[0][:, ls], vc_ref[0][:, ls], vn_ref[0][:, ls]], axis=0)
        qs = jnp.concatenate([_stack_halves(q[:, (2 * g + j) * LANES:(2 * g + j + 1) * LANES])
                              for j in range(2)], axis=0)
        s_loc = jnp.where(valid, _scores(qs, k_loc), NEG)
        s_ctx = _scores(qs, ck_ref[0][:, ls])
        sink = _sink_column(sink_ref, 4 * g, 4, blk)
        o = _joint_softmax_pv([s_loc, s_ctx], [v_loc, cv_ref[0][:, ls]], sink)
        outs += [_merge_halves(o[:2 * blk], blk), _merge_halves(o[2 * blk:], blk)]
    o_ref[0] = jnp.concatenate(outs, axis=-1).astype(o_ref.dtype)


def _window_attn(q, kd, vd, ckd, cvd, sink):
    B, S, _ = q.shape
    nb = S // A_BLOCK
    kv_spec = lambda f: pl.BlockSpec((1, A_BLOCK, 2 * LANES), f)
    prev_map = lambda b, i: (b, jnp.maximum(i - 1, 0), 0)
    cur_map = lambda b, i: (b, i, 0)
    next_map = lambda b, i: (b, jnp.minimum(i + 1, nb - 1), 0)
    ctx_spec = pl.BlockSpec((1, CTX_LEN, 2 * LANES), lambda b, i: (b, 0, 0))
    return pl.pallas_call(
        functools.partial(_window_attn_kernel, S),
        grid=(B, nb),
        in_specs=[pl.BlockSpec(memory_space=pltpu.SMEM),
                  pl.BlockSpec((1, A_BLOCK, 4 * LANES), cur_map),
                  kv_spec(prev_map), kv_spec(cur_map), kv_spec(next_map),
                  kv_spec(prev_map), kv_spec(cur_map), kv_spec(next_map),
                  ctx_spec, ctx_spec],
        out_specs=pl.BlockSpec((1, A_BLOCK, 4 * LANES), cur_map),
        out_shape=jax.ShapeDtypeStruct((B, S, 4 * LANES), BF16),
        compiler_params=_cparams(("parallel", "parallel")),
        name="window_attn",
    )(sink, q, kd, kd, kd, vd, vd, vd, ckd, cvd)


NBR_ROWS = 4
NBR_KROWS = 12


def _nbr_start_row(i, n_rows):
    return jnp.clip(i * NBR_ROWS - B_WIN_H // 2, 0, n_rows - NBR_KROWS)


def _nbr_attn_kernel(n_rows, q_ref, k_ref, v_ref, ck_ref, cv_ref, bias_ref, o_ref):
    i = pl.program_id(2)
    nq = NBR_ROWS * GRID_W
    nk = NBR_KROWS * GRID_W
    start = pl.multiple_of(_nbr_start_row(i, n_rows) * GRID_W, GRID_W)
    k_loc = k_ref[0, pl.ds(start, nk), :]
    v_loc = v_ref[0, pl.ds(start, nk), :]
    qs = _stack_halves(q_ref[0])
    s_loc = _scores(qs, k_loc) + bias_ref[0].reshape(2 * nq, nk)
    s_ctx = _scores(qs, ck_ref[0])
    o = _joint_softmax_pv([s_loc, s_ctx], [v_loc, cv_ref[0]])
    o_ref[0] = _merge_halves(o, nq).astype(o_ref.dtype)


def _nbr_bias_table(rpb, n_rows):
    kh = B_WIN_H
    tabs = []
    for r0 in (0, NBR_ROWS, n_rows - NBR_ROWS):
        start = int(np.clip(r0 - kh // 2, 0, n_rows - NBR_KROWS))
        r = r0 + np.arange(NBR_ROWS)
        rs = np.clip(r - kh // 2, 0, n_rows - kh)
        kr = start + np.arange(NBR_KROWS)
        row_ok = (kr[None, :] >= rs[:, None]) & (kr[None, :] < rs[:, None] + kh)
        dr = np.clip(kr[None, :] - r[:, None] + kh - 1, 0, 2 * kh - 2)
        col = np.arange(GRID_W)
        cs = np.clip(col - B_WIN_W // 2, 0, GRID_W - B_WIN_W)
        col_ok = (col[None, :] >= cs[:, None]) & (col[None, :] < cs[:, None] + B_WIN_W)
        dc = np.clip(col[None, :] - col[:, None], -(B_WIN_W - 1), B_WIN_W - 1) + B_WIN_W - 1
        bias = rpb[:, dr[:, None, :, None], dc[None, :, None, :]].astype(F32)
        ok = row_ok[:, None, :, None] & col_ok[None, :, None, :]
        bias = jnp.where(jnp.asarray(ok)[None], bias, NEG)
        tabs.append(bias.reshape(rpb.shape[0], NBR_ROWS * GRID_W, NBR_KROWS * GRID_W))
    return jnp.stack(tabs)


def _nbr_attn(q, k, v, ck, cv, bias):
    B, S, _ = q.shape
    n_rows = S // GRID_W
    nsteps = n_rows // NBR_ROWS
    nq = NBR_ROWS * GRID_W
    nk = NBR_KROWS * GRID_W
    pat = lambda i: jnp.where(i == 0, 0, jnp.where(i == nsteps - 1, 2, 1))
    return pl.pallas_call(
        functools.partial(_nbr_attn_kernel, n_rows),
        grid=(B, 4, nsteps),
        in_specs=[pl.BlockSpec((1, nq, LANES), lambda b, p, i: (b, i, p)),
                  pl.BlockSpec((1, S, LANES), lambda b, p, i: (b, 0, p)),
                  pl.BlockSpec((1, S, LANES), lambda b, p, i: (b, 0, p)),
                  pl.BlockSpec((1, CTX_LEN, LANES), lambda b, p, i: (b, 0, p)),
                  pl.BlockSpec((1, CTX_LEN, LANES), lambda b, p, i: (b, 0, p)),
                  pl.BlockSpec((1, 2, nq, nk), lambda b, p, i: (pat(i), p, 0, 0))],
        out_specs=pl.BlockSpec((1, nq, LANES), lambda b, p, i: (b, i, p)),
        out_shape=jax.ShapeDtypeStruct((B, S, 4 * LANES), BF16),
        compiler_params=_cparams(("parallel", "parallel", "arbitrary")),
        name="nbr_attn",
    )(q, k, v, ck, cv, bias)


FULL_TQ = 256
FULL_TK = 512


def _full_attn_kernel(q_ref, k_ref, v_ref, ck_ref, cv_ref, o_ref):
    tq = FULL_TQ
    q = q_ref[0]
    qs = jnp.concatenate([_stack_halves(q[:, :LANES]), _stack_halves(q[:, LANES:])], axis=0)

    def step(carry, k, v):
        m, l, acc = carry
        s = _scores(qs, k)
        m_new = jnp.maximum(m, jnp.max(s, axis=-1, keepdims=True))
        alpha = jnp.exp(m - m_new)
        p = jnp.exp(s - m_new)
        l = l * alpha + jnp.sum(p, axis=-1, keepdims=True)
        acc = acc * alpha + jnp.dot(p.astype(BF16), v, preferred_element_type=F32)
        return m_new, l, acc

    init = (jnp.full((4 * tq, 1), NEG, F32), jnp.zeros((4 * tq, 1), F32),
            jnp.zeros((4 * tq, LANES), F32))
    carry = step(init, ck_ref[0], cv_ref[0])

    def body(j, carry):
        off = pl.multiple_of(j * FULL_TK, FULL_TK)
        return step(carry, k_ref[0, pl.ds(off, FULL_TK), :], v_ref[0, pl.ds(off, FULL_TK), :])

    m, l, acc = lax.fori_loop(0, k_ref.shape[1] // FULL_TK, body, carry)
    o = acc / l
    o_ref[0] = jnp.concatenate([_merge_halves(o[:2 * tq], tq), _merge_halves(o[2 * tq:], tq)],
                               axis=-1).astype(o_ref.dtype)


def _full_attn(q, kd, vd, ckd, cvd):
    B, S, _ = q.shape
    return pl.pallas_call(
        _full_attn_kernel,
        grid=(B, 2, S // FULL_TQ),
        in_specs=[pl.BlockSpec((1, FULL_TQ, 2 * LANES), lambda b, g, i: (b, i, g)),
                  pl.BlockSpec((1, S, LANES), lambda b, g, i: (b, 0, g)),
                  pl.BlockSpec((1, S, LANES), lambda b, g, i: (b, 0, g)),
                  pl.BlockSpec((1, CTX_LEN, LANES), lambda b, g, i: (b, 0, g)),
                  pl.BlockSpec((1, CTX_LEN, LANES), lambda b, g, i: (b, 0, g))],
        out_specs=pl.BlockSpec((1, FULL_TQ, 2 * LANES), lambda b, g, i: (b, i, g)),
        out_shape=jax.ShapeDtypeStruct((B, S, 4 * LANES), BF16),
        compiler_params=_cparams(("parallel", "parallel", "arbitrary")),
        name="full_attn",
    )(q, kd, vd, ckd, cvd)


def _ctx_attn_kernel(sink_ref, aq_ref, akd_ref, avd_ref, bq_ref, bk_ref, bv_ref, o_ref):
    n = CTX_LEN
    aq = aq_ref[0]
    bq = bq_ref[0]
    outs = []
    for g in range(2):
        ls = slice(g * LANES, (g + 1) * LANES)
        qs = jnp.concatenate([_stack_halves(aq[:, (2 * g + j) * LANES:(2 * g + j + 1) * LANES])
                              for j in range(2)], axis=0)
        sink = _sink_column(sink_ref, 4 * g, 4, n)
        o = _joint_softmax_pv([_scores(qs, akd_ref[0][:, ls])], [avd_ref[0][:, ls]], sink)
        outs += [_merge_halves(o[:2 * n], n), _merge_halves(o[2 * n:], n)]
    for p in range(4):
        ls = slice(p * LANES, (p + 1) * LANES)
        qs = _stack_halves(bq[:, ls])
        o = _joint_softmax_pv([_scores(qs, bk_ref[0][:, ls])], [bv_ref[0][:, ls]])
        outs.append(_merge_halves(o, n))
    o_ref[0] = jnp.concatenate(outs, axis=-1).astype(o_ref.dtype)


def _ctx_attn(sink, aq, akd, avd, bq, bk, bv):
    B = aq.shape[0]
    spec = lambda a: pl.BlockSpec((1,) + a.shape[1:], lambda b: (b, 0, 0))
    args = (aq, akd, avd, bq, bk, bv)
    return pl.pallas_call(
        _ctx_attn_kernel,
        grid=(B,),
        in_specs=[pl.BlockSpec(memory_space=pltpu.SMEM)] + [spec(a) for a in args],
        out_specs=pl.BlockSpec((1, CTX_LEN, 8 * LANES), lambda b: (b, 0, 0)),
        out_shape=jax.ShapeDtypeStruct((B, CTX_LEN, 8 * LANES), BF16),
        compiler_params=_cparams(("parallel",)),
        name="ctx_attn",
    )(sink, *args)


def _out_proj_kernel(n_y, *refs):
    y_refs = refs[:n_y]
    w_ref, x_ref, gate_ref, g_ref, shift_ref, scale_ref, wr_ref = refs[n_y:n_y + 7]
    xo_ref, h_ref, lg_ref = refs[n_y + 7:]
    off = 0
    acc = None
    for y_ref in y_refs:
        wdt = y_ref.shape[-1]
        part = jnp.dot(y_ref[0], w_ref[off:off + wdt, :], preferred_element_type=F32)
        acc = part if acc is None else acc + part
        off += wdt
    x = x_ref[0] + gate_ref[0] * acc
    xo_ref[0] = x
    ms = jnp.mean(x * x, axis=-1, keepdims=True)
    h = x * lax.rsqrt(ms + EPS) * g_ref[...]
    h = h * (1.0 + scale_ref[0]) + shift_ref[0]
    h_ref[0] = h.astype(BF16)
    lg_ref[0] = jnp.dot(h, wr_ref[...], preferred_element_type=F32,
                        precision=lax.Precision.HIGHEST)


def _out_proj(ys, w, x, gate, g, shift, scale, w_router_pad, tm=512):
    B, S, D = x.shape
    tm = min(tm, S)
    bm = gate.shape[0]
    mod_map = (lambda b, i: (b, 0, 0)) if bm > 1 else (lambda b, i: (0, 0, 0))
    mod_spec = pl.BlockSpec((1, 1, D), mod_map)
    row_map = lambda b, i: (b, i, 0)
    in_specs = ([pl.BlockSpec((1, tm, y.shape[-1]), row_map) for y in ys]
                + [pl.BlockSpec(w.shape, lambda b, i: (0, 0)),
                   pl.BlockSpec((1, tm, D), row_map), mod_spec,
                   pl.BlockSpec((1, D), lambda b, i: (0, 0)), mod_spec, mod_spec,
                   pl.BlockSpec(w_router_pad.shape, lambda b, i: (0, 0))])
    return pl.pallas_call(
        functools.partial(_out_proj_kernel, len(ys)),
        grid=(B, S // tm),
        in_specs=in_specs,
        out_specs=[pl.BlockSpec((1, tm, D), row_map), pl.BlockSpec((1, tm, D), row_map),
                   pl.BlockSpec((1, tm, LANES), row_map)],
        out_shape=[jax.ShapeDtypeStruct((B, S, D), F32), jax.ShapeDtypeStruct((B, S, D), BF16),
                   jax.ShapeDtypeStruct((B, S, LANES), F32)],
        compiler_params=_cparams(("parallel", "parallel")),
        name="out_proj",
    )(*ys, w, x, gate.reshape(bm, 1, D), g.reshape(1, D), shift.reshape(bm, 1, D),
      scale.reshape(bm, 1, D), w_router_pad)


def _moe_kernel(h_ref, comb_ref, wg_ref, wu_ref, wd_ref, x_ref, gate_ref, o_ref, acc_ref):
    e = pl.program_id(2)

    @pl.when(e == 0)
    def _():
        acc_ref[...] = jnp.zeros_like(acc_ref)

    h = h_ref[0]
    a = jnp.dot(h, wg_ref[0], preferred_element_type=F32)
    u = jnp.dot(h, wu_ref[0], preferred_element_type=F32)
    he = (a * jax.nn.sigmoid(a) * u).astype(BF16)
    y = jnp.dot(he, wd_ref[0], preferred_element_type=F32)
    lane = lax.broadcasted_iota(jnp.int32, comb_ref.shape[1:], 1)
    c = jnp.sum(jnp.where(lane == e, comb_ref[0], 0.0), axis=-1, keepdims=True)
    acc_ref[...] += c * y

    @pl.when(e == pl.num_programs(2) - 1)
    def _():
        o_ref[0] = x_ref[0] + gate_ref[0] * acc_ref[...]


def _moe(h, comb, wg, wu, wd, x, gate, tm=1024):
    B, S, D = x.shape
    tm = min(tm, S)
    bm = gate.shape[0]
    mod_map = (lambda b, i, e: (b, 0, 0)) if bm > 1 else (lambda b, i, e: (0, 0, 0))
    row_map = lambda b, i, e: (b, i, 0)
    return pl.pallas_call(
        _moe_kernel,
        grid=(B, S // tm, N_EXPERTS),
        in_specs=[pl.BlockSpec((1, tm, D), row_map),
                  pl.BlockSpec((1, tm, LANES), row_map),
                  pl.BlockSpec((1, D, D_EXPERT), lambda b, i, e: (e, 0, 0)),
                  pl.BlockSpec((1, D, D_EXPERT), lambda b, i, e: (e, 0, 0)),
                  pl.BlockSpec((1, D_EXPERT, D), lambda b, i, e: (e, 0, 0)),
                  pl.BlockSpec((1, tm, D), row_map),
                  pl.BlockSpec((1, 1, D), mod_map)],
        out_specs=pl.BlockSpec((1, tm, D), row_map),
        out_shape=jax.ShapeDtypeStruct((B, S, D), F32),
        scratch_shapes=[pltpu.VMEM((tm, D), F32)],
        compiler_params=_cparams(("parallel", "parallel", "arbitrary")),
        name="moe",
    )(h, comb, wg, wu, wd, x, gate.reshape(bm, 1, D))


def _route(logits, b_router):
    lead = logits.shape[:-1]
    s = jax.nn.sigmoid(logits[..., :N_EXPERTS].reshape(-1, N_EXPERTS))
    T = s.shape[0]
    sel = s + b_router.astype(F32)
    gscore = lax.top_k(sel.reshape(T, N_GROUPS, EXPERTS_PER_GROUP), TOP_K)[0].sum(-1)
    gbest = jnp.argmax(gscore, axis=-1)
    in_group = (jnp.arange(N_EXPERTS) // EXPERTS_PER_GROUP)[None, :] == gbest[:, None]
    _, idx = lax.top_k(jnp.where(in_group, sel, -jnp.inf), TOP_K)
    wsel = jnp.take_along_axis(s, idx, axis=-1)
    wsel = wsel / jnp.sum(wsel, axis=-1, keepdims=True)
    comb = jnp.sum(jax.nn.one_hot(idx, LANES, dtype=F32) * wsel[..., None], axis=1)
    return comb.reshape(lead + (LANES,))


def _dft_tables():
    n1, n2, n = FFT_N1, FFT_N2, FFT_N1 * FFT_N2
    k1 = np.arange(n1)
    f1 = np.exp(-2j * np.pi * np.outer(k1, np.arange(n1)) / n1)
    tw = np.exp(-2j * np.pi * np.outer(np.arange(n2), k1) / n)
    ftw = f1[None, :, :] * tw[:, :, None]
    half = n1 // 2
    fh = ftw[:, :, :half]
    g_fwd = np.concatenate([np.concatenate([fh.real, -fh.imag], axis=2),
                            np.concatenate([fh.imag, fh.real], axis=2)], axis=1)
    g_real = np.concatenate([ftw.real, ftw.imag], axis=1)
    gi = np.conj(np.transpose(fh, (0, 2, 1))) / n
    g_inv = np.concatenate([np.concatenate([gi.real, -gi.imag], axis=2),
                            np.concatenate([gi.imag, gi.real], axis=2)], axis=1)
    f2 = np.exp(-2j * np.pi * np.outer(np.arange(n2), np.arange(n2)) / n2)
    f2_fwd = np.block([[f2.real, -f2.imag], [f2.imag, f2.real]])
    f2c = np.conj(f2)
    f2_inv = np.block([[f2c.real, -f2c.imag], [f2c.imag, f2c.real]])
    as_bf = lambda a: jnp.asarray(a, dtype=F32).astype(BF16)
    return as_bf(g_fwd), as_bf(g_real), as_bf(g_inv), as_bf(f2_fwd), as_bf(f2_inv)


def _fft_fast_stage(stage_ref, k1, f2):
    slab = 2 * FFT_N1
    m = jnp.concatenate([stage_ref[pl.ds(k1, FFT_N2, stride=slab), :],
                         stage_ref[pl.ds(FFT_N1 + k1, FFT_N2, stride=slab), :]], axis=0)
    return jnp.dot(f2, m.astype(BF16), preferred_element_type=F32)


def _filter_fft_kernel(kf_ref, g_ref, f2_ref, h_ref, stage_ref):
    slab = 2 * FFT_N1
    for n2 in range(FFT_N2):
        x = kf_ref[0, pl.ds(n2, FFT_N1, stride=FFT_N2), :].astype(BF16)
        stage_ref[n2 * slab:(n2 + 1) * slab, :] = jnp.dot(g_ref[n2], x, preferred_element_type=F32)
    f2 = f2_ref[...]
    for k1 in range(FFT_N1):
        h_ref[0, k1] = _fft_fast_stage(stage_ref, k1, f2).astype(h_ref.dtype)


def _filter_fft(kf, g_real, f2_fwd, ct=LANES):
    n_ord, n, C = kf.shape
    return pl.pallas_call(
        _filter_fft_kernel,
        grid=(n_ord, C // ct),
        in_specs=[pl.BlockSpec((1, n, ct), lambda o, c: (o, 0, c)),
                  pl.BlockSpec(g_real.shape, lambda o, c: (0, 0, 0)),
                  pl.BlockSpec(f2_fwd.shape, lambda o, c: (0, 0))],
        out_specs=pl.BlockSpec((1, FFT_N1, 2 * FFT_N2, ct), lambda o, c: (o, 0, 0, c)),
        out_shape=jax.ShapeDtypeStruct((n_ord, FFT_N1, 2 * FFT_N2, C), BF16),
        scratch_shapes=[pltpu.VMEM((FFT_N2 * 2 * FFT_N1, ct), F32)],
        compiler_params=_cparams(("parallel", "parallel")),
        name="filter_fft",
    )(kf, g_real, f2_fwd)


def _hyena_conv_kernel(z_ref, gate_ref, hb_ref, spec_ref, gf_ref, gi_ref, f2f_ref, f2i_ref, o_ref,
                       stage_ref):
    half = FFT_N1 // 2
    slab = 2 * FFT_N1
    for n2 in range(FFT_N2):
        rows = pl.ds(n2, half, stride=FFT_N2)
        x = jnp.concatenate([z_ref[0, rows, :], z_ref[1, rows, :]], axis=0).astype(BF16)
        stage_ref[n2 * slab:(n2 + 1) * slab, :] = jnp.dot(gf_ref[n2], x, preferred_element_type=F32)
    f2f = f2f_ref[...]
    f2i = f2i_ref[...]
    for k1 in range(FFT_N1):
        zf = _fft_fast_stage(stage_ref, k1, f2f)
        zr, zi = zf[:FFT_N2], zf[FFT_N2:]
        hr = spec_ref[0, k1, :FFT_N2, :].astype(F32)
        hi = spec_ref[0, k1, FFT_N2:, :].astype(F32)
        p = jnp.concatenate([zr * hr - zi * hi, zr * hi + zi * hr], axis=0).astype(BF16)
        q = jnp.dot(f2i, p, preferred_element_type=F32)
        stage_ref[pl.ds(k1, FFT_N2, stride=slab), :] = q[:FFT_N2]
        stage_ref[pl.ds(FFT_N1 + k1, FFT_N2, stride=slab), :] = q[FFT_N2:]
    hb = hb_ref[...]
    for n2 in range(FFT_N2):
        y_in = stage_ref[n2 * slab:(n2 + 1) * slab, :].astype(BF16)
        y = jnp.dot(gi_ref[n2], y_in, preferred_element_type=F32)
        rows = pl.ds(n2, half, stride=FFT_N2)
        for m in range(2):
            zm = z_ref[m, rows, :]
            o_ref[m, rows, :] = gate_ref[m, rows, :] * (y[m * half:(m + 1) * half] + zm * hb)


def _hyena_conv(z, gate, hbias, spec, order, tabs, ct=LANES):
    B, L, C = z.shape
    g_fwd, _, g_inv, f2_fwd, f2_inv = tabs
    once = pl.Buffered(1)
    const3 = lambda a: pl.BlockSpec(a.shape, lambda c, p: (0, 0, 0), pipeline_mode=once)
    const2 = lambda a: pl.BlockSpec(a.shape, lambda c, p: (0, 0), pipeline_mode=once)
    row_spec = pl.BlockSpec((2, L, ct), lambda c, p: (p, 0, c))
    return pl.pallas_call(
        _hyena_conv_kernel,
        grid=(C // ct, B // 2),
        in_specs=[row_spec, row_spec,
                  pl.BlockSpec((1, ct), lambda c, p: (0, c)),
                  pl.BlockSpec((1, FFT_N1, 2 * FFT_N2, ct), lambda c, p: (order, 0, 0, c),
                               pipeline_mode=once),
                  const3(g_fwd), const3(g_inv), const2(f2_fwd), const2(f2_inv)],
        out_specs=row_spec,
        out_shape=jax.ShapeDtypeStruct((B, L, C), F32),
        scratch_shapes=[pltpu.VMEM((FFT_N2 * 2 * FFT_N1, ct), F32)],
        compiler_params=_cparams(("parallel", "arbitrary")),
        name="hyena_conv",
    )(z, gate, hbias, spec, g_fwd, g_inv, f2_fwd, f2_inv)


def _hyena_filters(L, w1, b1, f1, w2, b2, f2, w3, b3):
    t = jnp.arange(L, dtype=F32)
    tn = t / max(L - 1, 1)
    bands = jnp.linspace(1e-4, HY_BANDS - 1, HY_BANDS, dtype=F32)
    ang = 2.0 * math.pi * t[:, None] * bands[None] / L
    feats = jnp.concatenate([tn[:, None], jnp.cos(ang), jnp.sin(ang)], axis=-1)
    h = jnp.sin(f1 * (feats @ w1 + b1))
    h = jnp.sin(f2 * (h @ w2 + b2))
    h = _mm_f32(h, w3, 512) + b3
    h = h.reshape(L, 2, 2, D_CH)
    deltas = jnp.abs(jnp.linspace(HY_MIN_DECAY, HY_MAX_DECAY, D_CH, dtype=F32))
    decay = jnp.exp(-tn[:, None] * deltas[None])
    h = h * decay[:, None, None, :]
    h = h / (jnp.sum(jnp.abs(h), axis=(0, 1), keepdims=True) + EPS)
    hf = jnp.transpose(h[:, 0], (1, 0, 2))
    hb = jnp.transpose(h[:, 1], (1, 0, 2))
    return jnp.concatenate([hf, jnp.zeros((2, 1, D_CH), F32), hb[:, 1:][:, ::-1]], axis=1)


def _short_conv(u, w, b):
    up = jnp.pad(u, ((0, 0), (1, 1), (0, 0)))
    return up[:, :-2] * w[0] + up[:, 1:-1] * w[1] + up[:, 2:] * w[2] + b


def _final_norm_kernel(x_ref, g_ref, o_ref):
    x = x_ref[0]
    ms = jnp.mean(x * x, axis=-1, keepdims=True)
    o_ref[0] = x * lax.rsqrt(ms + EPS) * g_ref[...]


def _final_norm(x, g, tm=1024):
    B, S, D = x.shape
    return pl.pallas_call(
        _final_norm_kernel,
        grid=(B, S // tm),
        in_specs=[pl.BlockSpec((1, tm, D), lambda b, i: (b, i, 0)),
                  pl.BlockSpec((1, D), lambda b, i: (0, 0))],
        out_specs=pl.BlockSpec((1, tm, D), lambda b, i: (b, i, 0)),
        out_shape=jax.ShapeDtypeStruct((B, S, D), F32),
        compiler_params=_cparams(("parallel", "parallel")),
        name="final_norm",
    )(x, g.reshape(1, D))


def _dup_heads(w):
    a, b = w[:, :HEAD_DIM], w[:, HEAD_DIM:]
    return jnp.concatenate([a, a, b, b], axis=1)


def _rope_tables(S):
    t = jnp.arange(S)
    row = (t // GRID_W).astype(F32)
    col = (t % GRID_W).astype(F32)
    half = HEAD_DIM // 2
    inv = ROPE_THETA ** (-jnp.arange(0, half, 2, dtype=F32) / half)
    ar = row[:, None] * inv[None]
    ac = col[:, None] * inv[None]
    cos = jnp.concatenate([jnp.cos(ar), jnp.cos(ar), jnp.cos(ac), jnp.cos(ac)], axis=-1)
    sin = jnp.concatenate([-jnp.sin(ar), jnp.sin(ar), -jnp.sin(ac), jnp.sin(ac)], axis=-1)
    return jnp.tile(cos, (1, 2)), jnp.tile(sin, (1, 2))


def _head_mean_matrix(width):
    blk = np.kron(np.eye(width // HEAD_DIM), np.full((HEAD_DIM, HEAD_DIM), 1.0 / HEAD_DIM))
    return jnp.asarray(blk, dtype=F32).astype(BF16)


def kernel(x, c, ctx, c_ctx, w_ada, b_ada, norm_g, final_g, w_in_even, w_out_even, a_sink, b_rpb, w_in_odd, w_out_odd, c_qnorm, c_knorm, hy_short_w, hy_short_b, hy_w1, hy_b1, hy_f1, hy_w2, hy_b2, hy_f2, hy_w3, hy_b3, hy_bias, w_router, b_router, moe_wg, moe_wu, moe_wd):
    B, S, D = x.shape
    depth = w_ada.shape[0]
    rope = _rope_tables(S)
    wr_pad = jnp.pad(w_router.astype(F32), ((0, 0), (0, LANES - N_EXPERTS)))

    mod_in = jnp.concatenate([jax.nn.silu(c), jax.nn.silu(c_ctx)[None],
                              jnp.zeros((8 - B - 1, D), F32)], axis=0)
    xc = ctx
    for l in range(depth):
        need_ctx = l < depth - 1
        mod = _mm_f32(mod_in, w_ada[l], 1536) + b_ada[l]
        mx = mod[:B].reshape(B, 6, D)
        mc = mod[B].reshape(6, D)
        i = l // 2
        if l % 2 == 0:
            w = w_in_even[i].astype(BF16)
            w_all = jnp.concatenate([w[:, :512], _dup_heads(w[:, 512:640]), _dup_heads(w[:, 640:768]),
                                     w[:, 768:]], axis=1)
            segs_x = ((512, "rope", ATTN_SCALE), (256, "rope", 1.0), (256, "plain", 1.0),
                      (512, "plain", ATTN_SCALE), (512, "plain", 1.0), (512, "plain", 1.0))
            aq, akd, avd, bq, bk, bv = _norm_proj(x, norm_g[l, 0], mx[:, 0], mx[:, 1], w_all, segs_x,
                                                  rope=rope)
            segs_c = tuple((wd, "plain", m) for wd, _, m in segs_x)
            caq, cakd, cavd, cbq, cbk, cbv = _norm_proj(xc, norm_g[l, 0], mc[0:1], mc[1:2], w_all, segs_c)
            ya = _window_attn(aq, akd, avd, cakd, cavd, a_sink[i].astype(F32))
            yb = _nbr_attn(bq, bk, bv, cbk, cbv, _nbr_bias_table(b_rpb[i], S // GRID_W))
            ys = [ya, yb]
            w_out = w_out_even[i].astype(BF16)
            if need_ctx:
                yc = [_ctx_attn(a_sink[i].astype(F32), caq, cakd, cavd, cbq, cbk, cbv)]
        else:
            w = w_in_odd[i].astype(BF16)
            w_all = jnp.concatenate([w[:, :512], _dup_heads(w[:, 512:640]), _dup_heads(w[:, 640:768]),
                                     w[:, 768:]], axis=1)
            gains = jnp.concatenate([jnp.tile(c_qnorm[i], 8), jnp.tile(c_knorm[i], 4)])[None].astype(F32)
            norm = (_head_mean_matrix(512), gains)
            segs_x = ((512, "normrope", ATTN_SCALE), (256, "normrope", 1.0), (256, "plain", 1.0),
                      (3 * D_CH, "plain", 1.0))
            qx, kxd, vxd, ux = _norm_proj(x, norm_g[l, 0], mx[:, 0], mx[:, 1], w_all, segs_x,
                                          rope=rope, norm=norm)
            w_c = w_all[:, 512:1024]
            norm_c = (_head_mean_matrix(512), jnp.tile(c_knorm[i], 4)[None].astype(F32))
            kcd, vcd = _norm_proj(xc, norm_g[l, 0], mc[0:1], mc[1:2], w_c,
                                  ((256, "norm", 1.0), (256, "plain", 1.0)), norm=norm_c)
            y_attn = _full_attn(qx, kxd, vxd, kcd, vcd)
            u = _short_conv(ux.astype(F32), hy_short_w[i], hy_short_b[i])
            v, g1, g2 = u[..., :D_CH], u[..., D_CH:2 * D_CH], u[..., 2 * D_CH:]
            tabs = _dft_tables()
            kf = _hyena_filters(S, hy_w1[i], hy_b1[i], hy_f1[i], hy_w2[i], hy_b2[i], hy_f2[i],
                                hy_w3[i], hy_b3[i])
            spec = _filter_fft(kf, tabs[1], tabs[3])
            z = _hyena_conv(v, g1, hy_bias[i, 0:1], spec, 0, tabs)
            z = _hyena_conv(z, g2, hy_bias[i, 1:2], spec, 1, tabs)
            ys = [y_attn, z.astype(BF16)]
            w_out = w_out_odd[i].astype(BF16)
            if need_ctx:
                raise NotImplementedError("context update of an odd layer is not needed at this depth")

        x, hx, lgx = _out_proj(ys, w_out, x, mx[:, 2], norm_g[l, 1], mx[:, 3], mx[:, 4], wr_pad)
        wg, wu, wd = moe_wg[l].astype(BF16), moe_wu[l].astype(BF16), moe_wd[l].astype(BF16)
        if need_ctx:
            xc, hc, lgc = _out_proj(yc, w_out, xc, mc[2:3], norm_g[l, 1], mc[3:4], mc[4:5], wr_pad)
            xc = _moe(hc, _route(lgc, b_router), wg, wu, wd, xc, mc[5:6])
        x = _moe(hx, _route(lgx, b_router), wg, wu, wd, x, mx[:, 5])
    return _final_norm(x, final_g)
```

```python
import functools
import math

import numpy as np
import jax
import jax.numpy as jnp
from jax import lax
from jax.experimental import pallas as pl
from jax.experimental.pallas import tpu as pltpu

F32 = jnp.float32
BF16 = jnp.bfloat16

D_MODEL = 1024
GRID_W = 64
CTX_LEN = 256
HEAD_DIM = 64
ROPE_THETA = 10000.0
EPS = 1e-6
ATTN_SCALE = HEAD_DIM ** -0.5
A_WINDOW = 128
A_BLOCK = 128
B_WIN_H = 8
B_WIN_W = 16
D_CH = 512
HY_BANDS = 16
HY_MAX_DECAY = math.log(1e-2) / 0.3
HY_MIN_DECAY = math.log(1e-2) / 1.5
N_EXPERTS = 16
N_GROUPS = 4
EXPERTS_PER_GROUP = N_EXPERTS // N_GROUPS
TOP_K = 2
D_EXPERT = 512

LANES = 128
NEG = -1e30
VMEM_LIMIT = 48 * 1024 * 1024

FFT_N1 = 64
FFT_N2 = 128


def _cparams(sem):
    return pltpu.CompilerParams(dimension_semantics=sem, vmem_limit_bytes=VMEM_LIMIT)


def _mm_f32_kernel(x_ref, w_ref, o_ref):
    o_ref[...] = jnp.dot(x_ref[...], w_ref[...], preferred_element_type=F32)


def _mm_f32(x, w, tn):
    M, K = x.shape
    N = w.shape[1]
    return pl.pallas_call(
        _mm_f32_kernel,
        grid=(N // tn,),
        in_specs=[pl.BlockSpec((M, K), lambda j: (0, 0)),
                  pl.BlockSpec((K, tn), lambda j: (0, j))],
        out_specs=pl.BlockSpec((M, tn), lambda j: (0, j)),
        out_shape=jax.ShapeDtypeStruct((M, N), F32),
        compiler_params=_cparams(("arbitrary",)),
        name="mm_f32",
    )(x, w)


def _swap16(y):
    n = y.shape[-1]
    lane = lax.broadcasted_iota(jnp.int32, y.shape, y.ndim - 1)
    up = pltpu.roll(y, n - 16, axis=y.ndim - 1)
    dn = pltpu.roll(y, 16, axis=y.ndim - 1)
    return jnp.where((lane % 32) < 16, up, dn)


def _tile_lanes(t, width):
    reps = width // t.shape[-1]
    return t if reps == 1 else jnp.concatenate([t] * reps, axis=-1)


def _norm_proj_kernel(segs, has_rope, has_norm, *refs):
    it = iter(refs)
    x_ref, g_ref, shift_ref, scale_ref, w_ref = (next(it) for _ in range(5))
    cos_ref = sin_ref = bd_ref = gain_ref = None
    if has_rope:
        cos_ref, sin_ref = next(it), next(it)
    if has_norm:
        bd_ref, gain_ref = next(it), next(it)
    out_refs = list(it)

    x = x_ref[0]
    ms = jnp.mean(x * x, axis=-1, keepdims=True)
    h = x * lax.rsqrt(ms + EPS) * g_ref[...]
    h = h * (1.0 + scale_ref[0]) + shift_ref[0]
    y = jnp.dot(h.astype(BF16), w_ref[...], preferred_element_type=F32)

    off = 0
    goff = 0
    for (width, kind, mult), o_ref in zip(segs, out_refs):
        ys = y[:, off:off + width]
        if kind in ("norm", "normrope"):
            bd = bd_ref[...][:width, :width]
            hms = jnp.dot((ys * ys).astype(BF16), bd, preferred_element_type=F32)
            ys = ys * lax.rsqrt(hms + EPS) * gain_ref[:, goff:goff + width]
            goff += width
        if kind in ("rope", "normrope"):
            c = _tile_lanes(cos_ref[...], width)
            s = _tile_lanes(sin_ref[...], width)
            ys = ys * c + _swap16(ys) * s
        if mult != 1.0:
            ys = ys * mult
        o_ref[0] = ys.astype(o_ref.dtype)
        off += width


def _norm_proj(x, g, shift, scale, w, segs, rope=None, norm=None, tm=512):
    B, S, D = x.shape
    N = w.shape[1]
    tm = min(tm, S)
    bm = shift.shape[0]
    mod_map = (lambda b, i: (b, 0, 0)) if bm > 1 else (lambda b, i: (0, 0, 0))
    args = [x, g.reshape(1, D), shift.reshape(bm, 1, D), scale.reshape(bm, 1, D), w]
    in_specs = [pl.BlockSpec((1, tm, D), lambda b, i: (b, i, 0)),
                pl.BlockSpec((1, D), lambda b, i: (0, 0)),
                pl.BlockSpec((1, 1, D), mod_map),
                pl.BlockSpec((1, 1, D), mod_map),
                pl.BlockSpec((D, N), lambda b, i: (0, 0))]
    if rope is not None:
        args += [rope[0], rope[1]]
        in_specs += [pl.BlockSpec((tm, LANES), lambda b, i: (i, 0))] * 2
    if norm is not None:
        args += [norm[0], norm[1]]
        in_specs += [pl.BlockSpec(norm[0].shape, lambda b, i: (0, 0)),
                     pl.BlockSpec(norm[1].shape, lambda b, i: (0, 0))]
    out_shape = [jax.ShapeDtypeStruct((B, S, wd), BF16) for wd, _, _ in segs]
    out_specs = [pl.BlockSpec((1, tm, wd), lambda b, i: (b, i, 0)) for wd, _, _ in segs]
    return pl.pallas_call(
        functools.partial(_norm_proj_kernel, segs, rope is not None, norm is not None),
        grid=(B, S // tm),
        in_specs=in_specs,
        out_specs=out_specs,
        out_shape=out_shape,
        compiler_params=_cparams(("parallel", "parallel")),
        name="norm_proj",
    )(*args)


def _half_mask(shape):
    return lax.broadcasted_iota(jnp.int32, shape, len(shape) - 1) < HEAD_DIM


def _stack_halves(qp):
    lo = _half_mask(qp.shape)
    zero = jnp.zeros_like(qp)
    return jnp.concatenate([jnp.where(lo, qp, zero), jnp.where(lo, zero, qp)], axis=0)


def _merge_halves(o, m):
    return jnp.where(_half_mask((m, LANES)), o[:m], o[m:])


def _scores(q, k):
    return lax.dot_general(q, k, (((1,), (1,)), ((), ())), preferred_element_type=F32)


def _joint_softmax_pv(score_parts, value_parts, extra_logit=None):
    m = functools.reduce(jnp.maximum, [jnp.max(s, axis=-1, keepdims=True) for s in score_parts])
    if extra_logit is not None:
        m = jnp.maximum(m, extra_logit)
    den = jnp.exp(extra_logit - m) if extra_logit is not None else 0.0
    acc = None
    for s, v in zip(score_parts, value_parts):
        p = jnp.exp(s - m)
        den = den + jnp.sum(p, axis=-1, keepdims=True)
        pv = jnp.dot(p.astype(BF16), v, preferred_element_type=F32)
        acc = pv if acc is None else acc + pv
    return acc / den


def _sink_column(sink_ref, first_head, n_heads, rows_per_head):
    rows = lax.broadcasted_iota(jnp.int32, (n_heads * rows_per_head, 1), 0)
    col = jnp.zeros((n_heads * rows_per_head, 1), F32)
    for j in range(n_heads):
        in_head = (rows >= j * rows_per_head) & (rows < (j + 1) * rows_per_head)
        col = jnp.where(in_head, sink_ref[first_head + j], col)
    return col


def _window_attn_kernel(seq_len, sink_ref, q_ref, kp_ref, kc_ref, kn_ref, vp_ref, vc_ref, vn_ref,
                        ck_ref, cv_ref, o_ref):
    i = pl.program_id(1)
    blk = A_BLOCK
    q = q_ref[0]
    rows = lax.broadcasted_iota(jnp.int32, (4 * blk, 3 * blk), 0) % blk
    rel = lax.broadcasted_iota(jnp.int32, (4 * blk, 3 * blk), 1) - blk
    gpos = i * blk + rel
    valid = (jnp.abs(rows - rel) <= A_WINDOW) & (gpos >= 0) & (gpos < seq_len)
    outs = []
    for g in range(2):
        ls = slice(g * LANES, (g + 1) * LANES)
        k_loc = jnp.concatenate([kp_ref[0][:, ls], kc_ref[0][:, ls], kn_ref[0][:, ls]], axis=0)
        v_loc = jnp.concatenate([vp_ref[0][:, ls], vc_ref[0][:, ls], vn_ref[0][:, ls]], axis=0)
        qs = jnp.concatenate([_stack_halves(q[:, (2 * g + j) * LANES:(2 * g + j + 1) * LANES])
                              for j in range(2)], axis=0)
        s_loc = jnp.where(valid, _scores(qs, k_loc), NEG)
        s_ctx = _scores(qs, ck_ref[0][:, ls])
        sink = _sink_column(sink_ref, 4 * g, 4, blk)
        o = _joint_softmax_pv([s_loc, s_ctx], [v_loc, cv_ref[0][:, ls]], sink)
        outs += [_merge_halves(o[:2 * blk], blk), _merge_halves(o[2 * blk:], blk)]
    o_ref[0] = jnp.concatenate(outs, axis=-1).astype(o_ref.dtype)


def _window_attn(q, kd, vd, ckd, cvd, sink):
    B, S, _ = q.shape
    nb = S // A_BLOCK
    kv_spec = lambda f: pl.BlockSpec((1, A_BLOCK, 2 * LANES), f)
    prev_map = lambda b, i: (b, jnp.maximum(i - 1, 0), 0)
    cur_map = lambda b, i: (b, i, 0)
    next_map = lambda b, i: (b, jnp.minimum(i + 1, nb - 1), 0)
    ctx_spec = pl.BlockSpec((1, CTX_LEN, 2 * LANES), lambda b, i: (b, 0, 0))
    return pl.pallas_call(
        functools.partial(_window_attn_kernel, S),
        grid=(B, nb),
        in_specs=[pl.BlockSpec(memory_space=pltpu.SMEM),
                  pl.BlockSpec((1, A_BLOCK, 4 * LANES), cur_map),
                  kv_spec(prev_map), kv_spec(cur_map), kv_spec(next_map),
                  kv_spec(prev_map), kv_spec(cur_map), kv_spec(next_map),
                  ctx_spec, ctx_spec],
        out_specs=pl.BlockSpec((1, A_BLOCK, 4 * LANES), cur_map),
        out_shape=jax.ShapeDtypeStruct((B, S, 4 * LANES), BF16),
        compiler_params=_cparams(("parallel", "parallel")),
        name="window_attn",
    )(sink, q, kd, kd, kd, vd, vd, vd, ckd, cvd)


NBR_ROWS = 4
NBR_KROWS = 12


def _nbr_start_row(i, n_rows):
    return jnp.clip(i * NBR_ROWS - B_WIN_H // 2, 0, n_rows - NBR_KROWS)


def _nbr_attn_kernel(n_rows, q_ref, k_ref, v_ref, ck_ref, cv_ref, bias_ref, o_ref):
    i = pl.program_id(2)
    nq = NBR_ROWS * GRID_W
    nk = NBR_KROWS * GRID_W
    start = pl.multiple_of(_nbr_start_row(i, n_rows) * GRID_W, GRID_W)
    k_loc = k_ref[0, pl.ds(start, nk), :]
    v_loc = v_ref[0, pl.ds(start, nk), :]
    qs = _stack_halves(q_ref[0])
    s_loc = _scores(qs, k_loc) + bias_ref[0].reshape(2 * nq, nk)
    s_ctx = _scores(qs, ck_ref[0])
    o = _joint_softmax_pv([s_loc, s_ctx], [v_loc, cv_ref[0]])
    o_ref[0] = _merge_halves(o, nq).astype(o_ref.dtype)


def _nbr_bias_table(rpb, n_rows):
    kh = B_WIN_H
    n_heads = rpb.shape[0]
    col = np.arange(GRID_W)
    cs = np.clip(col - B_WIN_W // 2, 0, GRID_W - B_WIN_W)
    col_ok = (col[None, :] >= cs[:, None]) & (col[None, :] < cs[:, None] + B_WIN_W)
    dc = np.clip(col[None, :] - col[:, None], -(B_WIN_W - 1), B_WIN_W - 1) + B_WIN_W - 1
    pick = (dc[..., None] == np.arange(2 * B_WIN_W - 1)).astype(np.float32)
    by_col = jnp.einsum("hdc,qkc->hdqk", rpb.astype(F32), pick, precision=lax.Precision.HIGHEST)
    by_col = jnp.where(col_ok[None, None], by_col, NEG)
    masked = jnp.full((n_heads, GRID_W, GRID_W), NEG, F32)
    tabs = []
    for r0 in (0, NBR_ROWS, n_rows - NBR_ROWS):
        start = int(np.clip(r0 - kh // 2, 0, n_rows - NBR_KROWS))
        per_row = []
        for ri in range(NBR_ROWS):
            r = r0 + ri
            rs = int(np.clip(r - kh // 2, 0, n_rows - kh))
            slabs = [by_col[:, start + kri - r + kh - 1] if rs <= start + kri < rs + kh else masked
                     for kri in range(NBR_KROWS)]
            per_row.append(jnp.stack(slabs, axis=2))
        tabs.append(jnp.stack(per_row, axis=1).reshape(n_heads, NBR_ROWS * GRID_W, NBR_KROWS * GRID_W))
    return jnp.stack(tabs)


def _nbr_attn(q, k, v, ck, cv, bias):
    B, S, _ = q.shape
    n_rows = S // GRID_W
    nsteps = n_rows // NBR_ROWS
    nq = NBR_ROWS * GRID_W
    nk = NBR_KROWS * GRID_W
    pat = lambda i: jnp.where(i == 0, 0, jnp.where(i == nsteps - 1, 2, 1))
    return pl.pallas_call(
        functools.partial(_nbr_attn_kernel, n_rows),
        grid=(B, 4, nsteps),
        in_specs=[pl.BlockSpec((1, nq, LANES), lambda b, p, i: (b, i, p)),
                  pl.BlockSpec((1, S, LANES), lambda b, p, i: (b, 0, p)),
                  pl.BlockSpec((1, S, LANES), lambda b, p, i: (b, 0, p)),
                  pl.BlockSpec((1, CTX_LEN, LANES), lambda b, p, i: (b, 0, p)),
                  pl.BlockSpec((1, CTX_LEN, LANES), lambda b, p, i: (b, 0, p)),
                  pl.BlockSpec((1, 2, nq, nk), lambda b, p, i: (pat(i), p, 0, 0))],
        out_specs=pl.BlockSpec((1, nq, LANES), lambda b, p, i: (b, i, p)),
        out_shape=jax.ShapeDtypeStruct((B, S, 4 * LANES), BF16),
        compiler_params=_cparams(("parallel", "parallel", "arbitrary")),
        name="nbr_attn",
    )(q, k, v, ck, cv, bias)


FULL_TQ = 256
FULL_TK = 512


def _full_attn_kernel(q_ref, k_ref, v_ref, ck_ref, cv_ref, o_ref):
    tq = FULL_TQ
    q = q_ref[0]
    qs = jnp.concatenate([_stack_halves(q[:, :LANES]), _stack_halves(q[:, LANES:])], axis=0)

    def step(carry, k, v):
        m, l, acc = carry
        s = _scores(qs, k)
        m_new = jnp.maximum(m, jnp.max(s, axis=-1, keepdims=True))
        alpha = jnp.exp(m - m_new)
        p = jnp.exp(s - m_new)
        l = l * alpha + jnp.sum(p, axis=-1, keepdims=True)
        acc = acc * alpha + jnp.dot(p.astype(BF16), v, preferred_element_type=F32)
        return m_new, l, acc

    init = (jnp.full((4 * tq, 1), NEG, F32), jnp.zeros((4 * tq, 1), F32),
            jnp.zeros((4 * tq, LANES), F32))
    carry = step(init, ck_ref[0], cv_ref[0])

    def body(j, carry):
        off = pl.multiple_of(j * FULL_TK, FULL_TK)
        return step(carry, k_ref[0, pl.ds(off, FULL_TK), :], v_ref[0, pl.ds(off, FULL_TK), :])

    m, l, acc = lax.fori_loop(0, k_ref.shape[1] // FULL_TK, body, carry)
    o = acc / l
    o_ref[0] = jnp.concatenate([_merge_halves(o[:2 * tq], tq), _merge_halves(o[2 * tq:], tq)],
                               axis=-1).astype(o_ref.dtype)


def _full_attn(q, kd, vd, ckd, cvd):
    B, S, _ = q.shape
    return pl.pallas_call(
        _full_attn_kernel,
        grid=(B, 2, S // FULL_TQ),
        in_specs=[pl.BlockSpec((1, FULL_TQ, 2 * LANES), lambda b, g, i: (b, i, g)),
                  pl.BlockSpec((1, S, LANES), lambda b, g, i: (b, 0, g)),
                  pl.BlockSpec((1, S, LANES), lambda b, g, i: (b, 0, g)),
                  pl.BlockSpec((1, CTX_LEN, LANES), lambda b, g, i: (b, 0, g)),
                  pl.BlockSpec((1, CTX_LEN, LANES), lambda b, g, i: (b, 0, g))],
        out_specs=pl.BlockSpec((1, FULL_TQ, 2 * LANES), lambda b, g, i: (b, i, g)),
        out_shape=jax.ShapeDtypeStruct((B, S, 4 * LANES), BF16),
        compiler_params=_cparams(("parallel", "parallel", "arbitrary")),
        name="full_attn",
    )(q, kd, vd, ckd, cvd)


def _ctx_attn_kernel(sink_ref, aq_ref, akd_ref, avd_ref, bq_ref, bk_ref, bv_ref, o_ref):
    n = CTX_LEN
    aq = aq_ref[0]
    bq = bq_ref[0]
    outs = []
    for g in range(2):
        ls = slice(g * LANES, (g + 1) * LANES)
        qs = jnp.concatenate([_stack_halves(aq[:, (2 * g + j) * LANES:(2 * g + j + 1) * LANES])
                              for j in range(2)], axis=0)
        sink = _sink_column(sink_ref, 4 * g, 4, n)
        o = _joint_softmax_pv([_scores(qs, akd_ref[0][:, ls])], [avd_ref[0][:, ls]], sink)
        outs += [_merge_halves(o[:2 * n], n), _merge_halves(o[2 * n:], n)]
    for p in range(4):
        ls = slice(p * LANES, (p + 1) * LANES)
        qs = _stack_halves(bq[:, ls])
        o = _joint_softmax_pv([_scores(qs, bk_ref[0][:, ls])], [bv_ref[0][:, ls]])
        outs.append(_merge_halves(o, n))
    o_ref[0] = jnp.concatenate(outs, axis=-1).astype(o_ref.dtype)


def _ctx_attn(sink, aq, akd, avd, bq, bk, bv):
    B = aq.shape[0]
    spec = lambda a: pl.BlockSpec((1,) + a.shape[1:], lambda b: (b, 0, 0))
    args = (aq, akd, avd, bq, bk, bv)
    return pl.pallas_call(
        _ctx_attn_kernel,
        grid=(B,),
        in_specs=[pl.BlockSpec(memory_space=pltpu.SMEM)] + [spec(a) for a in args],
        out_specs=pl.BlockSpec((1, CTX_LEN, 8 * LANES), lambda b: (b, 0, 0)),
        out_shape=jax.ShapeDtypeStruct((B, CTX_LEN, 8 * LANES), BF16),
        compiler_params=_cparams(("parallel",)),
        name="ctx_attn",
    )(sink, *args)


def _pick4(idx, vals):
    return jnp.where(idx == 0, vals[0], jnp.where(idx == 1, vals[1], jnp.where(idx == 2, vals[2], vals[3])))


def _route_rows(lg_t, b_ref):
    n_tok = lg_t.shape[1]
    s = [jax.nn.sigmoid(lg_t[e:e + 1, :]) for e in range(N_EXPERTS)]
    sel = [s[e] + b_ref[e] for e in range(N_EXPERTS)]
    n = EXPERTS_PER_GROUP
    gscore = []
    for j in range(N_GROUPS):
        v = sel[n * j:n * (j + 1)]
        pair_sums = [v[a] + v[b] for a in range(n) for b in range(a + 1, n)]
        gscore.append(functools.reduce(jnp.maximum, pair_sums))
    best, gbest = gscore[0], jnp.zeros((1, n_tok), jnp.int32)
    for j in range(1, N_GROUPS):
        upd = gscore[j] > best
        best = jnp.where(upd, gscore[j], best)
        gbest = jnp.where(upd, j, gbest)
    v = [_pick4(gbest, [sel[n * j + i] for j in range(N_GROUPS)]) for i in range(n)]
    u = [_pick4(gbest, [s[n * j + i] for j in range(N_GROUPS)]) for i in range(n)]
    m1, i1 = v[0], jnp.zeros((1, n_tok), jnp.int32)
    for i in range(1, n):
        upd = v[i] > m1
        m1 = jnp.where(upd, v[i], m1)
        i1 = jnp.where(upd, i, i1)
    m2, i2 = jnp.full((1, n_tok), -jnp.inf, F32), jnp.zeros((1, n_tok), jnp.int32)
    for i in range(n):
        upd = (i1 != i) & (v[i] > m2)
        m2 = jnp.where(upd, v[i], m2)
        i2 = jnp.where(upd, i, i2)
    u1, u2 = _pick4(i1, u), _pick4(i2, u)
    tot = u1 + u2
    e1, e2 = n * gbest + i1, n * gbest + i2
    rows = lax.broadcasted_iota(jnp.int32, lg_t.shape, 0)
    return jnp.where(rows == e1, u1 / tot, 0.0) + jnp.where(rows == e2, u2 / tot, 0.0)


def _out_proj_kernel(n_y, *refs):
    y_refs = refs[:n_y]
    br_ref, w_ref, x_ref, gate_ref, g_ref, shift_ref, scale_ref, wrh_ref, wrl_ref = refs[n_y:n_y + 9]
    xo_ref, h_ref, comb_ref = refs[n_y + 9:]
    off = 0
    acc = None
    for y_ref in y_refs:
        wdt = y_ref.shape[-1]
        part = jnp.dot(y_ref[0], w_ref[off:off + wdt, :], preferred_element_type=F32)
        acc = part if acc is None else acc + part
        off += wdt
    x = x_ref[0] + gate_ref[0] * acc
    xo_ref[0] = x
    ms = jnp.mean(x * x, axis=-1, keepdims=True)
    h = x * lax.rsqrt(ms + EPS) * g_ref[...]
    h = h * (1.0 + scale_ref[0]) + shift_ref[0]
    hh = h.astype(BF16)
    h_ref[0] = hh
    hl = (h - hh.astype(F32)).astype(BF16)
    lg = (jnp.dot(hh, wrh_ref[...], preferred_element_type=F32)
          + jnp.dot(hl, wrh_ref[...], preferred_element_type=F32)
          + jnp.dot(hh, wrl_ref[...], preferred_element_type=F32))
    comb_ref[0] = _route_rows(lg.T, br_ref).T


def _out_proj(ys, w, x, gate, g, shift, scale, router, tm=512):
    B, S, D = x.shape
    tm = min(tm, S)
    bm = gate.shape[0]
    b_router, wr_hi, wr_lo = router
    mod_map = (lambda b, i: (b, 0, 0)) if bm > 1 else (lambda b, i: (0, 0, 0))
    mod_spec = pl.BlockSpec((1, 1, D), mod_map)
    row_map = lambda b, i: (b, i, 0)
    in_specs = ([pl.BlockSpec((1, tm, y.shape[-1]), row_map) for y in ys]
                + [pl.BlockSpec(memory_space=pltpu.SMEM),
                   pl.BlockSpec(w.shape, lambda b, i: (0, 0)),
                   pl.BlockSpec((1, tm, D), row_map), mod_spec,
                   pl.BlockSpec((1, D), lambda b, i: (0, 0)), mod_spec, mod_spec,
                   pl.BlockSpec(wr_hi.shape, lambda b, i: (0, 0)),
                   pl.BlockSpec(wr_lo.shape, lambda b, i: (0, 0))])
    return pl.pallas_call(
        functools.partial(_out_proj_kernel, len(ys)),
        grid=(B, S // tm),
        in_specs=in_specs,
        out_specs=[pl.BlockSpec((1, tm, D), row_map), pl.BlockSpec((1, tm, D), row_map),
                   pl.BlockSpec((1, tm, LANES), row_map)],
        out_shape=[jax.ShapeDtypeStruct((B, S, D), F32), jax.ShapeDtypeStruct((B, S, D), BF16),
                   jax.ShapeDtypeStruct((B, S, LANES), F32)],
        compiler_params=_cparams(("parallel", "parallel")),
        name="out_proj",
    )(*ys, b_router, w, x, gate.reshape(bm, 1, D), g.reshape(1, D), shift.reshape(bm, 1, D),
      scale.reshape(bm, 1, D), wr_hi, wr_lo)


def _moe_kernel(h_ref, comb_ref, wg_ref, wu_ref, wd_ref, x_ref, gate_ref, o_ref, acc_ref):
    e = pl.program_id(2)

    @pl.when(e == 0)
    def _():
        acc_ref[...] = jnp.zeros_like(acc_ref)

    h = h_ref[0]
    a = jnp.dot(h, wg_ref[0], preferred_element_type=F32)
    u = jnp.dot(h, wu_ref[0], preferred_element_type=F32)
    he = (a * jax.nn.sigmoid(a) * u).astype(BF16)
    y = jnp.dot(he, wd_ref[0], preferred_element_type=F32)
    lane = lax.broadcasted_iota(jnp.int32, comb_ref.shape[1:], 1)
    c = jnp.sum(jnp.where(lane == e, comb_ref[0], 0.0), axis=-1, keepdims=True)
    acc_ref[...] += c * y

    @pl.when(e == pl.num_programs(2) - 1)
    def _():
        o_ref[0] = x_ref[0] + gate_ref[0] * acc_ref[...]


def _moe(h, comb, wg, wu, wd, x, gate, tm=1024):
    B, S, D = x.shape
    tm = min(tm, S)
    bm = gate.shape[0]
    mod_map = (lambda b, i, e: (b, 0, 0)) if bm > 1 else (lambda b, i, e: (0, 0, 0))
    row_map = lambda b, i, e: (b, i, 0)
    return pl.pallas_call(
        _moe_kernel,
        grid=(B, S // tm, N_EXPERTS),
        in_specs=[pl.BlockSpec((1, tm, D), row_map),
                  pl.BlockSpec((1, tm, LANES), row_map),
                  pl.BlockSpec((1, D, D_EXPERT), lambda b, i, e: (e, 0, 0)),
                  pl.BlockSpec((1, D, D_EXPERT), lambda b, i, e: (e, 0, 0)),
                  pl.BlockSpec((1, D_EXPERT, D), lambda b, i, e: (e, 0, 0)),
                  pl.BlockSpec((1, tm, D), row_map),
                  pl.BlockSpec((1, 1, D), mod_map)],
        out_specs=pl.BlockSpec((1, tm, D), row_map),
        out_shape=jax.ShapeDtypeStruct((B, S, D), F32),
        scratch_shapes=[pltpu.VMEM((tm, D), F32)],
        compiler_params=_cparams(("parallel", "parallel", "arbitrary")),
        name="moe",
    )(h, comb, wg, wu, wd, x, gate.reshape(bm, 1, D))


def _dft_tables():
    n1, n2, n = FFT_N1, FFT_N2, FFT_N1 * FFT_N2
    k1 = np.arange(n1)
    f1 = np.exp(-2j * np.pi * np.outer(k1, np.arange(n1)) / n1)
    tw = np.exp(-2j * np.pi * np.outer(np.arange(n2), k1) / n)
    ftw = f1[None, :, :] * tw[:, :, None]
    half = n1 // 2
    fh = ftw[:, :, :half]
    g_fwd = np.concatenate([np.concatenate([fh.real, -fh.imag], axis=2),
                            np.concatenate([fh.imag, fh.real], axis=2)], axis=1)
    g_real = np.concatenate([ftw.real, ftw.imag], axis=1)
    gi = np.conj(np.transpose(fh, (0, 2, 1))) / n
    g_inv = np.concatenate([np.concatenate([gi.real, -gi.imag], axis=2),
                            np.concatenate([gi.imag, gi.real], axis=2)], axis=1)
    f2 = np.exp(-2j * np.pi * np.outer(np.arange(n2), np.arange(n2)) / n2)
    f2_fwd = np.block([[f2.real, -f2.imag], [f2.imag, f2.real]])
    f2c = np.conj(f2)
    f2_inv = np.block([[f2c.real, -f2c.imag], [f2c.imag, f2c.real]])
    as_bf = lambda a: jnp.asarray(a, dtype=F32).astype(BF16)
    return as_bf(g_fwd), as_bf(g_real), as_bf(g_inv), as_bf(f2_fwd), as_bf(f2_inv)


def _fft_fast_stage(stage_ref, k1, f2):
    slab = 2 * FFT_N1
    m = jnp.concatenate([stage_ref[pl.ds(k1, FFT_N2, stride=slab), :],
                         stage_ref[pl.ds(FFT_N1 + k1, FFT_N2, stride=slab), :]], axis=0)
    return jnp.dot(f2, m.astype(BF16), preferred_element_type=F32)


def _filter_fft_kernel(kf_ref, g_ref, f2_ref, h_ref, stage_ref):
    slab = 2 * FFT_N1
    for n2 in range(FFT_N2):
        x = kf_ref[0, pl.ds(n2, FFT_N1, stride=FFT_N2), :].astype(BF16)
        stage_ref[n2 * slab:(n2 + 1) * slab, :] = jnp.dot(g_ref[n2], x, preferred_element_type=F32)
    f2 = f2_ref[...]
    for k1 in range(FFT_N1):
        h_ref[0, k1] = _fft_fast_stage(stage_ref, k1, f2).astype(h_ref.dtype)


def _filter_fft(kf, g_real, f2_fwd, ct=LANES):
    n_ord, n, C = kf.shape
    return pl.pallas_call(
        _filter_fft_kernel,
        grid=(n_ord, C // ct),
        in_specs=[pl.BlockSpec((1, n, ct), lambda o, c: (o, 0, c)),
                  pl.BlockSpec(g_real.shape, lambda o, c: (0, 0, 0)),
                  pl.BlockSpec(f2_fwd.shape, lambda o, c: (0, 0))],
        out_specs=pl.BlockSpec((1, FFT_N1, 2 * FFT_N2, ct), lambda o, c: (o, 0, 0, c)),
        out_shape=jax.ShapeDtypeStruct((n_ord, FFT_N1, 2 * FFT_N2, C), BF16),
        scratch_shapes=[pltpu.VMEM((FFT_N2 * 2 * FFT_N1, ct), F32)],
        compiler_params=_cparams(("parallel", "parallel")),
        name="filter_fft",
    )(kf, g_real, f2_fwd)


def _hyena_conv_kernel(z_ref, gate_ref, hb_ref, spec_ref, gf_ref, gi_ref, f2f_ref, f2i_ref, o_ref,
                       stage_ref):
    half = FFT_N1 // 2
    slab = 2 * FFT_N1
    for n2 in range(FFT_N2):
        rows = pl.ds(n2, half, stride=FFT_N2)
        x = jnp.concatenate([z_ref[0, rows, :], z_ref[1, rows, :]], axis=0).astype(BF16)
        stage_ref[n2 * slab:(n2 + 1) * slab, :] = jnp.dot(gf_ref[n2], x, preferred_element_type=F32)
    f2f = f2f_ref[...]
    f2i = f2i_ref[...]
    for k1 in range(FFT_N1):
        zf = _fft_fast_stage(stage_ref, k1, f2f)
        zr, zi = zf[:FFT_N2], zf[FFT_N2:]
        hr = spec_ref[0, k1, :FFT_N2, :].astype(F32)
        hi = spec_ref[0, k1, FFT_N2:, :].astype(F32)
        p = jnp.concatenate([zr * hr - zi * hi, zr * hi + zi * hr], axis=0).astype(BF16)
        q = jnp.dot(f2i, p, preferred_element_type=F32)
        stage_ref[pl.ds(k1, FFT_N2, stride=slab), :] = q[:FFT_N2]
        stage_ref[pl.ds(FFT_N1 + k1, FFT_N2, stride=slab), :] = q[FFT_N2:]
    hb = hb_ref[...]
    for n2 in range(FFT_N2):
        y_in = stage_ref[n2 * slab:(n2 + 1) * slab, :].astype(BF16)
        y = jnp.dot(gi_ref[n2], y_in, preferred_element_type=F32)
        rows = pl.ds(n2, half, stride=FFT_N2)
        for m in range(2):
            zm = z_ref[m, rows, :]
            o_ref[m, rows, :] = gate_ref[m, rows, :] * (y[m * half:(m + 1) * half] + zm * hb)


def _hyena_conv(z, gate, hbias, spec, order, tabs, ct=LANES):
    B, L, C = z.shape
    g_fwd, _, g_inv, f2_fwd, f2_inv = tabs
    once = pl.Buffered(1)
    const3 = lambda a: pl.BlockSpec(a.shape, lambda c, p: (0, 0, 0), pipeline_mode=once)
    const2 = lambda a: pl.BlockSpec(a.shape, lambda c, p: (0, 0), pipeline_mode=once)
    row_spec = pl.BlockSpec((2, L, ct), lambda c, p: (p, 0, c))
    return pl.pallas_call(
        _hyena_conv_kernel,
        grid=(C // ct, B // 2),
        in_specs=[row_spec, row_spec,
                  pl.BlockSpec((1, ct), lambda c, p: (0, c)),
                  pl.BlockSpec((1, FFT_N1, 2 * FFT_N2, ct), lambda c, p: (order, 0, 0, c),
                               pipeline_mode=once),
                  const3(g_fwd), const3(g_inv), const2(f2_fwd), const2(f2_inv)],
        out_specs=row_spec,
        out_shape=jax.ShapeDtypeStruct((B, L, C), F32),
        scratch_shapes=[pltpu.VMEM((FFT_N2 * 2 * FFT_N1, ct), F32)],
        compiler_params=_cparams(("parallel", "arbitrary")),
        name="hyena_conv",
    )(z, gate, hbias, spec, g_fwd, g_inv, f2_fwd, f2_inv)


def _hyena_filters(L, w1, b1, f1, w2, b2, f2, w3, b3):
    t = jnp.arange(L, dtype=F32)
    tn = t / max(L - 1, 1)
    bands = jnp.linspace(1e-4, HY_BANDS - 1, HY_BANDS, dtype=F32)
    ang = 2.0 * math.pi * t[:, None] * bands[None] / L
    feats = jnp.concatenate([tn[:, None], jnp.cos(ang), jnp.sin(ang)], axis=-1)
    h = jnp.sin(f1 * (feats @ w1 + b1))
    h = jnp.sin(f2 * (h @ w2 + b2))
    h = _mm_f32(h, w3, 512) + b3
    h = h.reshape(L, 2, 2, D_CH)
    deltas = jnp.abs(jnp.linspace(HY_MIN_DECAY, HY_MAX_DECAY, D_CH, dtype=F32))
    decay = jnp.exp(-tn[:, None] * deltas[None])
    h = h * decay[:, None, None, :]
    h = h / (jnp.sum(jnp.abs(h), axis=(0, 1), keepdims=True) + EPS)
    hf = jnp.transpose(h[:, 0], (1, 0, 2))
    hb = jnp.transpose(h[:, 1], (1, 0, 2))
    return jnp.concatenate([hf, jnp.zeros((2, 1, D_CH), F32), hb[:, 1:][:, ::-1]], axis=1)


def _short_conv(u, w, b):
    up = jnp.pad(u, ((0, 0), (1, 1), (0, 0)))
    return up[:, :-2] * w[0] + up[:, 1:-1] * w[1] + up[:, 2:] * w[2] + b


def _final_norm_kernel(x_ref, g_ref, o_ref):
    x = x_ref[0]
    ms = jnp.mean(x * x, axis=-1, keepdims=True)
    o_ref[0] = x * lax.rsqrt(ms + EPS) * g_ref[...]


def _final_norm(x, g, tm=1024):
    B, S, D = x.shape
    return pl.pallas_call(
        _final_norm_kernel,
        grid=(B, S // tm),
        in_specs=[pl.BlockSpec((1, tm, D), lambda b, i: (b, i, 0)),
                  pl.BlockSpec((1, D), lambda b, i: (0, 0))],
        out_specs=pl.BlockSpec((1, tm, D), lambda b, i: (b, i, 0)),
        out_shape=jax.ShapeDtypeStruct((B, S, D), F32),
        compiler_params=_cparams(("parallel", "parallel")),
        name="final_norm",
    )(x, g.reshape(1, D))


def _dup_heads(w):
    a, b = w[:, :HEAD_DIM], w[:, HEAD_DIM:]
    return jnp.concatenate([a, a, b, b], axis=1)


def _rope_tables(S):
    t = jnp.arange(S)
    row = (t // GRID_W).astype(F32)
    col = (t % GRID_W).astype(F32)
    half = HEAD_DIM // 2
    inv = ROPE_THETA ** (-jnp.arange(0, half, 2, dtype=F32) / half)
    ar = row[:, None] * inv[None]
    ac = col[:, None] * inv[None]
    cos = jnp.concatenate([jnp.cos(ar), jnp.cos(ar), jnp.cos(ac), jnp.cos(ac)], axis=-1)
    sin = jnp.concatenate([-jnp.sin(ar), jnp.sin(ar), -jnp.sin(ac), jnp.sin(ac)], axis=-1)
    return jnp.tile(cos, (1, 2)), jnp.tile(sin, (1, 2))


def _head_mean_matrix(width):
    blk = np.kron(np.eye(width // HEAD_DIM), np.full((HEAD_DIM, HEAD_DIM), 1.0 / HEAD_DIM))
    return jnp.asarray(blk, dtype=F32).astype(BF16)


def kernel(x, c, ctx, c_ctx, w_ada, b_ada, norm_g, final_g, w_in_even, w_out_even, a_sink, b_rpb, w_in_odd, w_out_odd, c_qnorm, c_knorm, hy_short_w, hy_short_b, hy_w1, hy_b1, hy_f1, hy_w2, hy_b2, hy_f2, hy_w3, hy_b3, hy_bias, w_router, b_router, moe_wg, moe_wu, moe_wd):
    B, S, D = x.shape
    depth = w_ada.shape[0]
    rope = _rope_tables(S)
    wr_pad = jnp.pad(w_router.astype(F32), ((0, 0), (0, LANES - N_EXPERTS)))
    wr_hi = wr_pad.astype(BF16)
    router = (b_router.astype(F32), wr_hi, (wr_pad - wr_hi.astype(F32)).astype(BF16))

    mod_in = jnp.concatenate([jax.nn.silu(c), jax.nn.silu(c_ctx)[None],
                              jnp.zeros((8 - B - 1, D), F32)], axis=0)
    xc = ctx
    for l in range(depth):
        need_ctx = l < depth - 1
        mod = _mm_f32(mod_in, w_ada[l], 1536) + b_ada[l]
        mx = mod[:B].reshape(B, 6, D)
        mc = mod[B].reshape(6, D)
        i = l // 2
        if l % 2 == 0:
            w = w_in_even[i].astype(BF16)
            w_all = jnp.concatenate([w[:, :512], _dup_heads(w[:, 512:640]), _dup_heads(w[:, 640:768]),
                                     w[:, 768:]], axis=1)
            segs_x = ((512, "rope", ATTN_SCALE), (256, "rope", 1.0), (256, "plain", 1.0),
                      (512, "plain", ATTN_SCALE), (512, "plain", 1.0), (512, "plain", 1.0))
            aq, akd, avd, bq, bk, bv = _norm_proj(x, norm_g[l, 0], mx[:, 0], mx[:, 1], w_all, segs_x,
                                                  rope=rope)
            segs_c = tuple((wd, "plain", m) for wd, _, m in segs_x)
            caq, cakd, cavd, cbq, cbk, cbv = _norm_proj(xc, norm_g[l, 0], mc[0:1], mc[1:2], w_all, segs_c)
            ya = _window_attn(aq, akd, avd, cakd, cavd, a_sink[i].astype(F32))
            yb = _nbr_attn(bq, bk, bv, cbk, cbv, _nbr_bias_table(b_rpb[i], S // GRID_W))
            ys = [ya, yb]
            w_out = w_out_even[i].astype(BF16)
            if need_ctx:
                yc = [_ctx_attn(a_sink[i].astype(F32), caq, cakd, cavd, cbq, cbk, cbv)]
        else:
            w = w_in_odd[i].astype(BF16)
            w_all = jnp.concatenate([w[:, :512], _dup_heads(w[:, 512:640]), _dup_heads(w[:, 640:768]),
                                     w[:, 768:]], axis=1)
            gains = jnp.concatenate([jnp.tile(c_qnorm[i], 8), jnp.tile(c_knorm[i], 4)])[None].astype(F32)
            norm = (_head_mean_matrix(512), gains)
            segs_x = ((512, "normrope", ATTN_SCALE), (256, "normrope", 1.0), (256, "plain", 1.0),
                      (3 * D_CH, "plain", 1.0))
            qx, kxd, vxd, ux = _norm_proj(x, norm_g[l, 0], mx[:, 0], mx[:, 1], w_all, segs_x,
                                          rope=rope, norm=norm)
            w_c = w_all[:, 512:1024]
            norm_c = (_head_mean_matrix(512), jnp.tile(c_knorm[i], 4)[None].astype(F32))
            kcd, vcd = _norm_proj(xc, norm_g[l, 0], mc[0:1], mc[1:2], w_c,
                                  ((256, "norm", 1.0), (256, "plain", 1.0)), norm=norm_c)
            y_attn = _full_attn(qx, kxd, vxd, kcd, vcd)
            u = _short_conv(ux.astype(F32), hy_short_w[i], hy_short_b[i])
            v, g1, g2 = u[..., :D_CH], u[..., D_CH:2 * D_CH], u[..., 2 * D_CH:]
            tabs = _dft_tables()
            kf = _hyena_filters(S, hy_w1[i], hy_b1[i], hy_f1[i], hy_w2[i], hy_b2[i], hy_f2[i],
                                hy_w3[i], hy_b3[i])
            spec = _filter_fft(kf, tabs[1], tabs[3])
            z = _hyena_conv(v, g1, hy_bias[i, 0:1], spec, 0, tabs)
            z = _hyena_conv(z, g2, hy_bias[i, 1:2], spec, 1, tabs)
            ys = [y_attn, z.astype(BF16)]
            w_out = w_out_odd[i].astype(BF16)
            if need_ctx:
                raise NotImplementedError("context update of an odd layer is not needed at this depth")

        x, hx, comb_x = _out_proj(ys, w_out, x, mx[:, 2], norm_g[l, 1], mx[:, 3], mx[:, 4], router)
        wg, wu, wd = moe_wg[l].astype(BF16), moe_wu[l].astype(BF16), moe_wd[l].astype(BF16)
        if need_ctx:
            xc, hc, comb_c = _out_proj(yc, w_out, xc, mc[2:3], norm_g[l, 1], mc[3:4], mc[4:5], router)
            xc = _moe(hc, comb_c, wg, wu, wd, xc, mc[5:6])
        x = _moe(hx, comb_x, wg, wu, wd, x, mx[:, 5])
    return _final_norm(x, final_g)
```

```python
import functools
import math

import numpy as np
import jax
import jax.numpy as jnp
from jax import lax
from jax.experimental import pallas as pl
from jax.experimental.pallas import tpu as pltpu

F32 = jnp.float32
BF16 = jnp.bfloat16

D_MODEL = 1024
GRID_W = 64
CTX_LEN = 256
HEAD_DIM = 64
ROPE_THETA = 10000.0
EPS = 1e-6
ATTN_SCALE = HEAD_DIM ** -0.5
A_WINDOW = 128
A_BLOCK = 128
B_WIN_H = 8
B_WIN_W = 16
D_CH = 512
HY_BANDS = 16
HY_MAX_DECAY = math.log(1e-2) / 0.3
HY_MIN_DECAY = math.log(1e-2) / 1.5
N_EXPERTS = 16
N_GROUPS = 4
EXPERTS_PER_GROUP = N_EXPERTS // N_GROUPS
TOP_K = 2
D_EXPERT = 512

LANES = 128
NEG = -1e30
VMEM_LIMIT = 48 * 1024 * 1024

FFT_N1 = 64
FFT_N2 = 128


def _cparams(sem):
    return pltpu.CompilerParams(dimension_semantics=sem, vmem_limit_bytes=VMEM_LIMIT)


def _mm_f32_kernel(x_ref, w_ref, o_ref):
    o_ref[...] = jnp.dot(x_ref[...], w_ref[...], preferred_element_type=F32)


def _mm_f32(x, w, tn):
    M, K = x.shape
    N = w.shape[1]
    return pl.pallas_call(
        _mm_f32_kernel,
        grid=(N // tn,),
        in_specs=[pl.BlockSpec((M, K), lambda j: (0, 0)),
                  pl.BlockSpec((K, tn), lambda j: (0, j))],
        out_specs=pl.BlockSpec((M, tn), lambda j: (0, j)),
        out_shape=jax.ShapeDtypeStruct((M, N), F32),
        compiler_params=_cparams(("arbitrary",)),
        name="mm_f32",
    )(x, w)


def _swap16(y):
    n = y.shape[-1]
    lane = lax.broadcasted_iota(jnp.int32, y.shape, y.ndim - 1)
    up = pltpu.roll(y, n - 16, axis=y.ndim - 1)
    dn = pltpu.roll(y, 16, axis=y.ndim - 1)
    return jnp.where((lane % 32) < 16, up, dn)


def _tile_lanes(t, width):
    reps = width // t.shape[-1]
    return t if reps == 1 else jnp.concatenate([t] * reps, axis=-1)


def _norm_proj_kernel(segs, has_rope, has_norm, *refs):
    it = iter(refs)
    x_ref, g_ref, shift_ref, scale_ref, w_ref = (next(it) for _ in range(5))
    cos_ref = sin_ref = bd_ref = gain_ref = None
    if has_rope:
        cos_ref, sin_ref = next(it), next(it)
    if has_norm:
        bd_ref, gain_ref = next(it), next(it)
    out_refs = list(it)

    x = x_ref[0]
    ms = jnp.mean(x * x, axis=-1, keepdims=True)
    h = x * lax.rsqrt(ms + EPS) * g_ref[...]
    h = h * (1.0 + scale_ref[0]) + shift_ref[0]
    y = jnp.dot(h.astype(BF16), w_ref[...], preferred_element_type=F32)

    off = 0
    goff = 0
    for (width, kind, mult), o_ref in zip(segs, out_refs):
        ys = y[:, off:off + width]
        if kind in ("norm", "normrope"):
            bd = bd_ref[...][:width, :width]
            hms = jnp.dot((ys * ys).astype(BF16), bd, preferred_element_type=F32)
            ys = ys * lax.rsqrt(hms + EPS) * gain_ref[:, goff:goff + width]
            goff += width
        if kind in ("rope", "normrope"):
            c = _tile_lanes(cos_ref[...], width)
            s = _tile_lanes(sin_ref[...], width)
            ys = ys * c + _swap16(ys) * s
        if mult != 1.0:
            ys = ys * mult
        o_ref[0] = ys.astype(o_ref.dtype)
        off += width


def _norm_proj(x, g, shift, scale, w, segs, rope=None, norm=None, tm=512):
    B, S, D = x.shape
    N = w.shape[1]
    tm = min(tm, S)
    bm = shift.shape[0]
    mod_map = (lambda b, i: (b, 0, 0)) if bm > 1 else (lambda b, i: (0, 0, 0))
    args = [x, g.reshape(1, D), shift.reshape(bm, 1, D), scale.reshape(bm, 1, D), w]
    in_specs = [pl.BlockSpec((1, tm, D), lambda b, i: (b, i, 0)),
                pl.BlockSpec((1, D), lambda b, i: (0, 0)),
                pl.BlockSpec((1, 1, D), mod_map),
                pl.BlockSpec((1, 1, D), mod_map),
                pl.BlockSpec((D, N), lambda b, i: (0, 0))]
    if rope is not None:
        args += [rope[0], rope[1]]
        in_specs += [pl.BlockSpec((tm, LANES), lambda b, i: (i, 0))] * 2
    if norm is not None:
        args += [norm[0], norm[1]]
        in_specs += [pl.BlockSpec(norm[0].shape, lambda b, i: (0, 0)),
                     pl.BlockSpec(norm[1].shape, lambda b, i: (0, 0))]
    out_shape = [jax.ShapeDtypeStruct((B, S, wd), BF16) for wd, _, _ in segs]
    out_specs = [pl.BlockSpec((1, tm, wd), lambda b, i: (b, i, 0)) for wd, _, _ in segs]
    return pl.pallas_call(
        functools.partial(_norm_proj_kernel, segs, rope is not None, norm is not None),
        grid=(B, S // tm),
        in_specs=in_specs,
        out_specs=out_specs,
        out_shape=out_shape,
        compiler_params=_cparams(("parallel", "parallel")),
        name="norm_proj",
    )(*args)


def _half_mask(shape):
    return lax.broadcasted_iota(jnp.int32, shape, len(shape) - 1) < HEAD_DIM


def _stack_halves(qp):
    lo = _half_mask(qp.shape)
    zero = jnp.zeros_like(qp)
    return jnp.concatenate([jnp.where(lo, qp, zero), jnp.where(lo, zero, qp)], axis=0)


def _merge_halves(o, m):
    return jnp.where(_half_mask((m, LANES)), o[:m], o[m:])


def _scores(q, k):
    return lax.dot_general(q, k, (((1,), (1,)), ((), ())), preferred_element_type=F32)


def _joint_softmax_pv(score_parts, value_parts, extra_logit=None):
    m = functools.reduce(jnp.maximum, [jnp.max(s, axis=-1, keepdims=True) for s in score_parts])
    if extra_logit is not None:
        m = jnp.maximum(m, extra_logit)
    den = jnp.exp(extra_logit - m) if extra_logit is not None else 0.0
    acc = None
    for s, v in zip(score_parts, value_parts):
        p = jnp.exp(s - m)
        den = den + jnp.sum(p, axis=-1, keepdims=True)
        pv = jnp.dot(p.astype(BF16), v, preferred_element_type=F32)
        acc = pv if acc is None else acc + pv
    return acc / den


def _sink_column(sink_ref, first_head, n_heads, rows_per_head):
    rows = lax.broadcasted_iota(jnp.int32, (n_heads * rows_per_head, 1), 0)
    col = jnp.zeros((n_heads * rows_per_head, 1), F32)
    for j in range(n_heads):
        in_head = (rows >= j * rows_per_head) & (rows < (j + 1) * rows_per_head)
        col = jnp.where(in_head, sink_ref[first_head + j], col)
    return col


def _window_attn_kernel(seq_len, sink_ref, q_ref, kp_ref, kc_ref, kn_ref, vp_ref, vc_ref, vn_ref,
                        ck_ref, cv_ref, o_ref):
    i = pl.program_id(1)
    blk = A_BLOCK
    q = q_ref[0]
    rows = lax.broadcasted_iota(jnp.int32, (4 * blk, 3 * blk), 0) % blk
    rel = lax.broadcasted_iota(jnp.int32, (4 * blk, 3 * blk), 1) - blk
    gpos = i * blk + rel
    valid = (jnp.abs(rows - rel) <= A_WINDOW) & (gpos >= 0) & (gpos < seq_len)
    outs = []
    for g in range(2):
        ls = slice(g * LANES, (g + 1) * LANES)
        k_loc = jnp.concatenate([kp_ref[0][:, ls], kc_ref[0][:, ls], kn_ref[0][:, ls]], axis=0)
        v_loc = jnp.concatenate([vp_ref[0][:, ls], vc_ref[0][:, ls], vn_ref[0][:, ls]], axis=0)
        qs = jnp.concatenate([_stack_halves(q[:, (2 * g + j) * LANES:(2 * g + j + 1) * LANES])
                              for j in range(2)], axis=0)
        s_loc = jnp.where(valid, _scores(qs, k_loc), NEG)
        s_ctx = _scores(qs, ck_ref[0][:, ls])
        sink = _sink_column(sink_ref, 4 * g, 4, blk)
        o = _joint_softmax_pv([s_loc, s_ctx], [v_loc, cv_ref[0][:, ls]], sink)
        outs += [_merge_halves(o[:2 * blk], blk), _merge_halves(o[2 * blk:], blk)]
    o_ref[0] = jnp.concatenate(outs, axis=-1).astype(o_ref.dtype)


def _window_attn(q, kd, vd, ckd, cvd, sink):
    B, S, _ = q.shape
    nb = S // A_BLOCK
    kv_spec = lambda f: pl.BlockSpec((1, A_BLOCK, 2 * LANES), f)
    prev_map = lambda b, i: (b, jnp.maximum(i - 1, 0), 0)
    cur_map = lambda b, i: (b, i, 0)
    next_map = lambda b, i: (b, jnp.minimum(i + 1, nb - 1), 0)
    ctx_spec = pl.BlockSpec((1, CTX_LEN, 2 * LANES), lambda b, i: (b, 0, 0))
    return pl.pallas_call(
        functools.partial(_window_attn_kernel, S),
        grid=(B, nb),
        in_specs=[pl.BlockSpec(memory_space=pltpu.SMEM),
                  pl.BlockSpec((1, A_BLOCK, 4 * LANES), cur_map),
                  kv_spec(prev_map), kv_spec(cur_map), kv_spec(next_map),
                  kv_spec(prev_map), kv_spec(cur_map), kv_spec(next_map),
                  ctx_spec, ctx_spec],
        out_specs=pl.BlockSpec((1, A_BLOCK, 4 * LANES), cur_map),
        out_shape=jax.ShapeDtypeStruct((B, S, 4 * LANES), BF16),
        compiler_params=_cparams(("parallel", "parallel")),
        name="window_attn",
    )(sink, q, kd, kd, kd, vd, vd, vd, ckd, cvd)


NBR_ROWS = 4
NBR_KROWS = 12


def _nbr_start_row(i, n_rows):
    return jnp.clip(i * NBR_ROWS - B_WIN_H // 2, 0, n_rows - NBR_KROWS)


def _nbr_attn_kernel(n_rows, q_ref, k_ref, v_ref, ck_ref, cv_ref, bias_ref, o_ref):
    i = pl.program_id(2)
    nq = NBR_ROWS * GRID_W
    nk = NBR_KROWS * GRID_W
    start = pl.multiple_of(_nbr_start_row(i, n_rows) * GRID_W, GRID_W)
    k_loc = k_ref[0, pl.ds(start, nk), :]
    v_loc = v_ref[0, pl.ds(start, nk), :]
    qs = _stack_halves(q_ref[0])
    s_loc = _scores(qs, k_loc) + bias_ref[0].reshape(2 * nq, nk)
    s_ctx = _scores(qs, ck_ref[0])
    o = _joint_softmax_pv([s_loc, s_ctx], [v_loc, cv_ref[0]])
    o_ref[0] = _merge_halves(o, nq).astype(o_ref.dtype)


def _nbr_bias_table(rpb, n_rows):
    kh = B_WIN_H
    n_heads = rpb.shape[0]
    col = np.arange(GRID_W)
    cs = np.clip(col - B_WIN_W // 2, 0, GRID_W - B_WIN_W)
    col_ok = (col[None, :] >= cs[:, None]) & (col[None, :] < cs[:, None] + B_WIN_W)
    dc = np.clip(col[None, :] - col[:, None], -(B_WIN_W - 1), B_WIN_W - 1) + B_WIN_W - 1
    pick = (dc[..., None] == np.arange(2 * B_WIN_W - 1)).astype(np.float32)
    by_col = jnp.einsum("hdc,qkc->hdqk", rpb.astype(F32), pick, precision=lax.Precision.HIGHEST)
    by_col = jnp.where(col_ok[None, None], by_col, NEG)
    masked = jnp.full((n_heads, GRID_W, GRID_W), NEG, F32)
    tabs = []
    for r0 in (0, NBR_ROWS, n_rows - NBR_ROWS):
        start = int(np.clip(r0 - kh // 2, 0, n_rows - NBR_KROWS))
        per_row = []
        for ri in range(NBR_ROWS):
            r = r0 + ri
            rs = int(np.clip(r - kh // 2, 0, n_rows - kh))
            slabs = [by_col[:, start + kri - r + kh - 1] if rs <= start + kri < rs + kh else masked
                     for kri in range(NBR_KROWS)]
            per_row.append(jnp.stack(slabs, axis=2))
        tabs.append(jnp.stack(per_row, axis=1).reshape(n_heads, NBR_ROWS * GRID_W, NBR_KROWS * GRID_W))
    return jnp.stack(tabs)


def _nbr_attn(q, k, v, ck, cv, bias):
    B, S, _ = q.shape
    n_rows = S // GRID_W
    nsteps = n_rows // NBR_ROWS
    nq = NBR_ROWS * GRID_W
    nk = NBR_KROWS * GRID_W
    pat = lambda i: jnp.where(i == 0, 0, jnp.where(i == nsteps - 1, 2, 1))
    return pl.pallas_call(
        functools.partial(_nbr_attn_kernel, n_rows),
        grid=(B, 4, nsteps),
        in_specs=[pl.BlockSpec((1, nq, LANES), lambda b, p, i: (b, i, p)),
                  pl.BlockSpec((1, S, LANES), lambda b, p, i: (b, 0, p)),
                  pl.BlockSpec((1, S, LANES), lambda b, p, i: (b, 0, p)),
                  pl.BlockSpec((1, CTX_LEN, LANES), lambda b, p, i: (b, 0, p)),
                  pl.BlockSpec((1, CTX_LEN, LANES), lambda b, p, i: (b, 0, p)),
                  pl.BlockSpec((1, 2, nq, nk), lambda b, p, i: (pat(i), p, 0, 0))],
        out_specs=pl.BlockSpec((1, nq, LANES), lambda b, p, i: (b, i, p)),
        out_shape=jax.ShapeDtypeStruct((B, S, 4 * LANES), BF16),
        compiler_params=_cparams(("parallel", "parallel", "arbitrary")),
        name="nbr_attn",
    )(q, k, v, ck, cv, bias)


FULL_TQ = 256
FULL_TK = 512
FULL_NOMAX_LOG2_BOUND = 60.0


def _full_attn_kernel(bounded, q_ref, k_ref, v_ref, ck_ref, cv_ref, o_ref):
    tq = FULL_TQ
    q = q_ref[0]
    qs = jnp.concatenate([_stack_halves(q[:, :LANES]), _stack_halves(q[:, LANES:])], axis=0)

    def step_bounded(carry, k, v):
        l, acc = carry
        p = jnp.exp2(_scores(qs, k))
        l = l + jnp.sum(p, axis=-1, keepdims=True)
        return l, acc + jnp.dot(p.astype(BF16), v, preferred_element_type=F32)

    def step_online(carry, k, v):
        m, l, acc = carry
        s = _scores(qs, k)
        m_new = jnp.maximum(m, jnp.max(s, axis=-1, keepdims=True))
        alpha = jnp.exp2(m - m_new)
        p = jnp.exp2(s - m_new)
        l = l * alpha + jnp.sum(p, axis=-1, keepdims=True)
        acc = acc * alpha + jnp.dot(p.astype(BF16), v, preferred_element_type=F32)
        return m_new, l, acc

    step = step_bounded if bounded else step_online
    init = (jnp.zeros((4 * tq, 1), F32), jnp.zeros((4 * tq, LANES), F32))
    if not bounded:
        init = (jnp.full((4 * tq, 1), NEG, F32),) + init
    carry = step(init, ck_ref[0], cv_ref[0])

    def body(j, carry):
        off = pl.multiple_of(j * FULL_TK, FULL_TK)
        return step(carry, k_ref[0, pl.ds(off, FULL_TK), :], v_ref[0, pl.ds(off, FULL_TK), :])

    carry = lax.fori_loop(0, k_ref.shape[1] // FULL_TK, body, carry)
    o = carry[-1] / carry[-2]
    o_ref[0] = jnp.concatenate([_merge_halves(o[:2 * tq], tq), _merge_halves(o[2 * tq:], tq)],
                               axis=-1).astype(o_ref.dtype)


def _full_attn(q, kd, vd, ckd, cvd, bounded):
    B, S, _ = q.shape
    return pl.pallas_call(
        functools.partial(_full_attn_kernel, bounded),
        grid=(B, 2, S // FULL_TQ),
        in_specs=[pl.BlockSpec((1, FULL_TQ, 2 * LANES), lambda b, g, i: (b, i, g)),
                  pl.BlockSpec((1, S, LANES), lambda b, g, i: (b, 0, g)),
                  pl.BlockSpec((1, S, LANES), lambda b, g, i: (b, 0, g)),
                  pl.BlockSpec((1, CTX_LEN, LANES), lambda b, g, i: (b, 0, g)),
                  pl.BlockSpec((1, CTX_LEN, LANES), lambda b, g, i: (b, 0, g))],
        out_specs=pl.BlockSpec((1, FULL_TQ, 2 * LANES), lambda b, g, i: (b, i, g)),
        out_shape=jax.ShapeDtypeStruct((B, S, 4 * LANES), BF16),
        compiler_params=_cparams(("parallel", "parallel", "arbitrary")),
        name="full_attn_bounded" if bounded else "full_attn_online",
    )(q, kd, vd, ckd, cvd)


def _ctx_attn_kernel(sink_ref, aq_ref, akd_ref, avd_ref, bq_ref, bk_ref, bv_ref, o_ref):
    n = CTX_LEN
    aq = aq_ref[0]
    bq = bq_ref[0]
    outs = []
    for g in range(2):
        ls = slice(g * LANES, (g + 1) * LANES)
        qs = jnp.concatenate([_stack_halves(aq[:, (2 * g + j) * LANES:(2 * g + j + 1) * LANES])
                              for j in range(2)], axis=0)
        sink = _sink_column(sink_ref, 4 * g, 4, n)
        o = _joint_softmax_pv([_scores(qs, akd_ref[0][:, ls])], [avd_ref[0][:, ls]], sink)
        outs += [_merge_halves(o[:2 * n], n), _merge_halves(o[2 * n:], n)]
    for p in range(4):
        ls = slice(p * LANES, (p + 1) * LANES)
        qs = _stack_halves(bq[:, ls])
        o = _joint_softmax_pv([_scores(qs, bk_ref[0][:, ls])], [bv_ref[0][:, ls]])
        outs.append(_merge_halves(o, n))
    o_ref[0] = jnp.concatenate(outs, axis=-1).astype(o_ref.dtype)


def _ctx_attn(sink, aq, akd, avd, bq, bk, bv):
    B = aq.shape[0]
    spec = lambda a: pl.BlockSpec((1,) + a.shape[1:], lambda b: (b, 0, 0))
    args = (aq, akd, avd, bq, bk, bv)
    return pl.pallas_call(
        _ctx_attn_kernel,
        grid=(B,),
        in_specs=[pl.BlockSpec(memory_space=pltpu.SMEM)] + [spec(a) for a in args],
        out_specs=pl.BlockSpec((1, CTX_LEN, 8 * LANES), lambda b: (b, 0, 0)),
        out_shape=jax.ShapeDtypeStruct((B, CTX_LEN, 8 * LANES), BF16),
        compiler_params=_cparams(("parallel",)),
        name="ctx_attn",
    )(sink, *args)


def _pick4(idx, vals):
    return jnp.where(idx == 0, vals[0], jnp.where(idx == 1, vals[1], jnp.where(idx == 2, vals[2], vals[3])))


def _route_rows(lg_t, b_ref):
    n_tok = lg_t.shape[1]
    s = [jax.nn.sigmoid(lg_t[e:e + 1, :]) for e in range(N_EXPERTS)]
    sel = [s[e] + b_ref[e] for e in range(N_EXPERTS)]
    n = EXPERTS_PER_GROUP
    gscore = []
    for j in range(N_GROUPS):
        v = sel[n * j:n * (j + 1)]
        pair_sums = [v[a] + v[b] for a in range(n) for b in range(a + 1, n)]
        gscore.append(functools.reduce(jnp.maximum, pair_sums))
    best, gbest = gscore[0], jnp.zeros((1, n_tok), jnp.int32)
    for j in range(1, N_GROUPS):
        upd = gscore[j] > best
        best = jnp.where(upd, gscore[j], best)
        gbest = jnp.where(upd, j, gbest)
    v = [_pick4(gbest, [sel[n * j + i] for j in range(N_GROUPS)]) for i in range(n)]
    u = [_pick4(gbest, [s[n * j + i] for j in range(N_GROUPS)]) for i in range(n)]
    m1, i1 = v[0], jnp.zeros((1, n_tok), jnp.int32)
    for i in range(1, n):
        upd = v[i] > m1
        m1 = jnp.where(upd, v[i], m1)
        i1 = jnp.where(upd, i, i1)
    m2, i2 = jnp.full((1, n_tok), -jnp.inf, F32), jnp.zeros((1, n_tok), jnp.int32)
    for i in range(n):
        upd = (i1 != i) & (v[i] > m2)
        m2 = jnp.where(upd, v[i], m2)
        i2 = jnp.where(upd, i, i2)
    u1, u2 = _pick4(i1, u), _pick4(i2, u)
    tot = u1 + u2
    e1, e2 = n * gbest + i1, n * gbest + i2
    rows = lax.broadcasted_iota(jnp.int32, lg_t.shape, 0)
    return jnp.where(rows == e1, u1 / tot, 0.0) + jnp.where(rows == e2, u2 / tot, 0.0)


def _out_proj_kernel(n_y, *refs):
    y_refs = refs[:n_y]
    br_ref, w_ref, x_ref, gate_ref, g_ref, shift_ref, scale_ref, wrh_ref, wrl_ref = refs[n_y:n_y + 9]
    xo_ref, h_ref, comb_ref = refs[n_y + 9:]
    off = 0
    acc = None
    for y_ref in y_refs:
        wdt = y_ref.shape[-1]
        part = jnp.dot(y_ref[0], w_ref[off:off + wdt, :], preferred_element_type=F32)
        acc = part if acc is None else acc + part
        off += wdt
    x = x_ref[0] + gate_ref[0] * acc
    xo_ref[0] = x
    ms = jnp.mean(x * x, axis=-1, keepdims=True)
    h = x * lax.rsqrt(ms + EPS) * g_ref[...]
    h = h * (1.0 + scale_ref[0]) + shift_ref[0]
    hh = h.astype(BF16)
    h_ref[0] = hh
    hl = (h - hh.astype(F32)).astype(BF16)
    lg = (jnp.dot(hh, wrh_ref[...], preferred_element_type=F32)
          + jnp.dot(hl, wrh_ref[...], preferred_element_type=F32)
          + jnp.dot(hh, wrl_ref[...], preferred_element_type=F32))
    comb_ref[0] = _route_rows(lg.T, br_ref).T


def _out_proj(ys, w, x, gate, g, shift, scale, router, tm=512):
    B, S, D = x.shape
    tm = min(tm, S)
    bm = gate.shape[0]
    b_router, wr_hi, wr_lo = router
    mod_map = (lambda b, i: (b, 0, 0)) if bm > 1 else (lambda b, i: (0, 0, 0))
    mod_spec = pl.BlockSpec((1, 1, D), mod_map)
    row_map = lambda b, i: (b, i, 0)
    in_specs = ([pl.BlockSpec((1, tm, y.shape[-1]), row_map) for y in ys]
                + [pl.BlockSpec(memory_space=pltpu.SMEM),
                   pl.BlockSpec(w.shape, lambda b, i: (0, 0)),
                   pl.BlockSpec((1, tm, D), row_map), mod_spec,
                   pl.BlockSpec((1, D), lambda b, i: (0, 0)), mod_spec, mod_spec,
                   pl.BlockSpec(wr_hi.shape, lambda b, i: (0, 0)),
                   pl.BlockSpec(wr_lo.shape, lambda b, i: (0, 0))])
    return pl.pallas_call(
        functools.partial(_out_proj_kernel, len(ys)),
        grid=(B, S // tm),
        in_specs=in_specs,
        out_specs=[pl.BlockSpec((1, tm, D), row_map), pl.BlockSpec((1, tm, D), row_map),
                   pl.BlockSpec((1, tm, LANES), row_map)],
        out_shape=[jax.ShapeDtypeStruct((B, S, D), F32), jax.ShapeDtypeStruct((B, S, D), BF16),
                   jax.ShapeDtypeStruct((B, S, LANES), F32)],
        compiler_params=_cparams(("parallel", "parallel")),
        name="out_proj",
    )(*ys, b_router, w, x, gate.reshape(bm, 1, D), g.reshape(1, D), shift.reshape(bm, 1, D),
      scale.reshape(bm, 1, D), wr_hi, wr_lo)


def _moe_kernel(h_ref, comb_ref, wg_ref, wu_ref, wd_ref, x_ref, gate_ref, o_ref, acc_ref):
    e = pl.program_id(2)

    @pl.when(e == 0)
    def _():
        acc_ref[...] = jnp.zeros_like(acc_ref)

    h = h_ref[0]
    a = jnp.dot(h, wg_ref[0], preferred_element_type=F32)
    u = jnp.dot(h, wu_ref[0], preferred_element_type=F32)
    he = (a * jax.nn.sigmoid(a) * u).astype(BF16)
    y = jnp.dot(he, wd_ref[0], preferred_element_type=F32)
    lane = lax.broadcasted_iota(jnp.int32, comb_ref.shape[1:], 1)
    c = jnp.sum(jnp.where(lane == e, comb_ref[0], 0.0), axis=-1, keepdims=True)
    acc_ref[...] += c * y

    @pl.when(e == pl.num_programs(2) - 1)
    def _():
        o_ref[0] = x_ref[0] + gate_ref[0] * acc_ref[...]


def _moe(h, comb, wg, wu, wd, x, gate, tm=1024):
    B, S, D = x.shape
    tm = min(tm, S)
    bm = gate.shape[0]
    mod_map = (lambda b, i, e: (b, 0, 0)) if bm > 1 else (lambda b, i, e: (0, 0, 0))
    row_map = lambda b, i, e: (b, i, 0)
    return pl.pallas_call(
        _moe_kernel,
        grid=(B, S // tm, N_EXPERTS),
        in_specs=[pl.BlockSpec((1, tm, D), row_map),
                  pl.BlockSpec((1, tm, LANES), row_map),
                  pl.BlockSpec((1, D, D_EXPERT), lambda b, i, e: (e, 0, 0)),
                  pl.BlockSpec((1, D, D_EXPERT), lambda b, i, e: (e, 0, 0)),
                  pl.BlockSpec((1, D_EXPERT, D), lambda b, i, e: (e, 0, 0)),
                  pl.BlockSpec((1, tm, D), row_map),
                  pl.BlockSpec((1, 1, D), mod_map)],
        out_specs=pl.BlockSpec((1, tm, D), row_map),
        out_shape=jax.ShapeDtypeStruct((B, S, D), F32),
        scratch_shapes=[pltpu.VMEM((tm, D), F32)],
        compiler_params=_cparams(("parallel", "parallel", "arbitrary")),
        name="moe",
    )(h, comb, wg, wu, wd, x, gate.reshape(bm, 1, D))


def _dft_tables():
    n1, n2, n = FFT_N1, FFT_N2, FFT_N1 * FFT_N2
    k1 = np.arange(n1)
    f1 = np.exp(-2j * np.pi * np.outer(k1, np.arange(n1)) / n1)
    tw = np.exp(-2j * np.pi * np.outer(np.arange(n2), k1) / n)
    ftw = f1[None, :, :] * tw[:, :, None]
    half = n1 // 2
    fh = ftw[:, :, :half]
    g_fwd = np.concatenate([np.concatenate([fh.real, -fh.imag], axis=2),
                            np.concatenate([fh.imag, fh.real], axis=2)], axis=1)
    back = ftw[:, :, ::-1][:, :, :half].copy()
    back[0] = np.roll(ftw[0], -1, axis=1)[:, ::-1][:, :half]
    back[0][:, 0] = 0.0
    fk = np.concatenate([fh, back], axis=2)
    g_real = np.concatenate([fk.real, fk.imag], axis=1)
    gi = np.conj(np.transpose(fh, (0, 2, 1))) / n
    g_inv = np.concatenate([np.concatenate([gi.real, -gi.imag], axis=2),
                            np.concatenate([gi.imag, gi.real], axis=2)], axis=1)
    f2 = np.exp(-2j * np.pi * np.outer(np.arange(n2), np.arange(n2)) / n2)
    f2_fwd = np.block([[f2.real, -f2.imag], [f2.imag, f2.real]])
    f2c = np.conj(f2)
    f2_inv = np.block([[f2c.real, -f2c.imag], [f2c.imag, f2c.real]])
    as_bf = lambda a: jnp.asarray(a, dtype=F32).astype(BF16)
    return as_bf(g_fwd), as_bf(g_real), as_bf(g_inv), as_bf(f2_fwd), as_bf(f2_inv)


def _fft_fast_stage(stage_ref, k1, f2):
    slab = 2 * FFT_N1
    m = jnp.concatenate([stage_ref[pl.ds(k1, FFT_N2, stride=slab), :],
                         stage_ref[pl.ds(FFT_N1 + k1, FFT_N2, stride=slab), :]], axis=0)
    return jnp.dot(f2, m.astype(BF16), preferred_element_type=F32)


def _filter_fft_kernel(hf_ref, hb_ref, inv_ref, g_ref, f2_ref, h_ref, stage_ref):
    slab = 2 * FFT_N1
    half = FFT_N1 // 2
    for n2 in range(FFT_N2):
        x = jnp.concatenate([hf_ref[pl.ds(n2, half, stride=FFT_N2), :],
                             hb_ref[pl.ds((FFT_N2 - n2) % FFT_N2, half, stride=FFT_N2), :]], axis=0)
        stage_ref[n2 * slab:(n2 + 1) * slab, :] = jnp.dot(g_ref[n2], x.astype(BF16),
                                                          preferred_element_type=F32)
    f2 = f2_ref[...]
    inv = inv_ref[...]
    for k1 in range(FFT_N1):
        h_ref[0, k1] = (_fft_fast_stage(stage_ref, k1, f2) * inv).astype(h_ref.dtype)


def _filter_fft(taps, inv_norm, g_real, f2_fwd, ct=LANES):
    L, cols = taps.shape
    C = D_CH
    n_ord = cols // (2 * C)
    nc = C // ct
    once = pl.Buffered(1)
    return pl.pallas_call(
        _filter_fft_kernel,
        grid=(n_ord, nc),
        in_specs=[pl.BlockSpec((L, ct), lambda o, c: (0, o * nc + c)),
                  pl.BlockSpec((L, ct), lambda o, c: (0, (n_ord + o) * nc + c)),
                  pl.BlockSpec((1, ct), lambda o, c: (0, o * nc + c)),
                  pl.BlockSpec(g_real.shape, lambda o, c: (0, 0, 0), pipeline_mode=once),
                  pl.BlockSpec(f2_fwd.shape, lambda o, c: (0, 0), pipeline_mode=once)],
        out_specs=pl.BlockSpec((1, FFT_N1, 2 * FFT_N2, ct), lambda o, c: (o, 0, 0, c)),
        out_shape=jax.ShapeDtypeStruct((n_ord, FFT_N1, 2 * FFT_N2, C), BF16),
        scratch_shapes=[pltpu.VMEM((FFT_N2 * 2 * FFT_N1, ct), F32)],
        compiler_params=_cparams(("parallel", "parallel")),
        name="filter_fft",
    )(taps, taps, inv_norm, g_real, f2_fwd)


def _hyena_conv_kernel(z_ref, gate_ref, hb_ref, spec_ref, gf_ref, gi_ref, f2f_ref, f2i_ref, o_ref,
                       stage_ref):
    half = FFT_N1 // 2
    slab = 2 * FFT_N1
    for n2 in range(FFT_N2):
        rows = pl.ds(n2, half, stride=FFT_N2)
        x = jnp.concatenate([z_ref[0, rows, :], z_ref[1, rows, :]], axis=0).astype(BF16)
        stage_ref[n2 * slab:(n2 + 1) * slab, :] = jnp.dot(gf_ref[n2], x, preferred_element_type=F32)
    f2f = f2f_ref[...]
    f2i = f2i_ref[...]
    for k1 in range(FFT_N1):
        zf = _fft_fast_stage(stage_ref, k1, f2f)
        zr, zi = zf[:FFT_N2], zf[FFT_N2:]
        hr = spec_ref[0, k1, :FFT_N2, :].astype(F32)
        hi = spec_ref[0, k1, FFT_N2:, :].astype(F32)
        p = jnp.concatenate([zr * hr - zi * hi, zr * hi + zi * hr], axis=0).astype(BF16)
        q = jnp.dot(f2i, p, preferred_element_type=F32)
        stage_ref[pl.ds(k1, FFT_N2, stride=slab), :] = q[:FFT_N2]
        stage_ref[pl.ds(FFT_N1 + k1, FFT_N2, stride=slab), :] = q[FFT_N2:]
    hb = hb_ref[...]
    for n2 in range(FFT_N2):
        y_in = stage_ref[n2 * slab:(n2 + 1) * slab, :].astype(BF16)
        y = jnp.dot(gi_ref[n2], y_in, preferred_element_type=F32)
        rows = pl.ds(n2, half, stride=FFT_N2)
        for m in range(2):
            zm = z_ref[m, rows, :]
            o_ref[m, rows, :] = gate_ref[m, rows, :] * (y[m * half:(m + 1) * half] + zm * hb)


def _hyena_conv(z, z_blk, gate, gate_blk, hbias, spec, order, tabs, ct=LANES):
    B, L, _ = z.shape
    C = D_CH
    g_fwd, _, g_inv, f2_fwd, f2_inv = tabs
    once = pl.Buffered(1)
    const3 = lambda a: pl.BlockSpec(a.shape, lambda c, p: (0, 0, 0), pipeline_mode=once)
    const2 = lambda a: pl.BlockSpec(a.shape, lambda c, p: (0, 0), pipeline_mode=once)
    row_spec = pl.BlockSpec((2, L, ct), lambda c, p: (p, 0, c))
    return pl.pallas_call(
        _hyena_conv_kernel,
        grid=(C // ct, B // 2),
        in_specs=[pl.BlockSpec((2, L, ct), lambda c, p: (p, 0, z_blk + c)),
                  pl.BlockSpec((2, L, ct), lambda c, p: (p, 0, gate_blk + c)),
                  pl.BlockSpec((1, ct), lambda c, p: (0, c)),
                  pl.BlockSpec((1, FFT_N1, 2 * FFT_N2, ct), lambda c, p: (order, 0, 0, c),
                               pipeline_mode=once),
                  const3(g_fwd), const3(g_inv), const2(f2_fwd), const2(f2_inv)],
        out_specs=row_spec,
        out_shape=jax.ShapeDtypeStruct((B, L, C), F32),
        scratch_shapes=[pltpu.VMEM((FFT_N2 * 2 * FFT_N1, ct), F32)],
        compiler_params=_cparams(("parallel", "arbitrary")),
        name="hyena_conv",
    )(z, gate, hbias, spec, g_fwd, g_inv, f2_fwd, f2_inv)


def _hyena_filters(L, w1, b1, f1, w2, b2, f2, w3, b3):
    t = jnp.arange(L, dtype=F32)
    tn = t / max(L - 1, 1)
    bands = jnp.linspace(1e-4, HY_BANDS - 1, HY_BANDS, dtype=F32)
    ang = 2.0 * math.pi * t[:, None] * bands[None] / L
    feats = jnp.concatenate([tn[:, None], jnp.cos(ang), jnp.sin(ang)], axis=-1)
    h = jnp.sin(f1 * (feats @ w1 + b1))
    h = jnp.sin(f2 * (h @ w2 + b2))
    deltas = jnp.abs(jnp.linspace(HY_MIN_DECAY, HY_MAX_DECAY, D_CH, dtype=F32))
    decay = jnp.exp(-tn[:, None] * deltas[None])
    n_rep = w3.shape[1] // D_CH
    taps = (_mm_f32(h, w3, 512) + b3) * jnp.tile(decay, (1, n_rep))
    l1 = jnp.sum(jnp.abs(taps), axis=0)
    l1 = l1[:n_rep // 2 * D_CH] + l1[n_rep // 2 * D_CH:]
    return taps, (1.0 / (l1 + EPS))[None]


def _short_conv(u, w, b):
    up = jnp.pad(u, ((0, 0), (1, 1), (0, 0)))
    return up[:, :-2] * w[0] + up[:, 1:-1] * w[1] + up[:, 2:] * w[2] + b


def _final_norm_kernel(x_ref, g_ref, o_ref):
    x = x_ref[0]
    ms = jnp.mean(x * x, axis=-1, keepdims=True)
    o_ref[0] = x * lax.rsqrt(ms + EPS) * g_ref[...]


def _final_norm(x, g, tm=1024):
    B, S, D = x.shape
    return pl.pallas_call(
        _final_norm_kernel,
        grid=(B, S // tm),
        in_specs=[pl.BlockSpec((1, tm, D), lambda b, i: (b, i, 0)),
                  pl.BlockSpec((1, D), lambda b, i: (0, 0))],
        out_specs=pl.BlockSpec((1, tm, D), lambda b, i: (b, i, 0)),
        out_shape=jax.ShapeDtypeStruct((B, S, D), F32),
        compiler_params=_cparams(("parallel", "parallel")),
        name="final_norm",
    )(x, g.reshape(1, D))


def _dup_heads(w):
    a, b = w[:, :HEAD_DIM], w[:, HEAD_DIM:]
    return jnp.concatenate([a, a, b, b], axis=1)


def _rope_tables(S):
    t = jnp.arange(S)
    row = (t // GRID_W).astype(F32)
    col = (t % GRID_W).astype(F32)
    half = HEAD_DIM // 2
    inv = ROPE_THETA ** (-jnp.arange(0, half, 2, dtype=F32) / half)
    ar = row[:, None] * inv[None]
    ac = col[:, None] * inv[None]
    cos = jnp.concatenate([jnp.cos(ar), jnp.cos(ar), jnp.cos(ac), jnp.cos(ac)], axis=-1)
    sin = jnp.concatenate([-jnp.sin(ar), jnp.sin(ar), -jnp.sin(ac), jnp.sin(ac)], axis=-1)
    return jnp.tile(cos, (1, 2)), jnp.tile(sin, (1, 2))


def _head_mean_matrix(width):
    blk = np.kron(np.eye(width // HEAD_DIM), np.full((HEAD_DIM, HEAD_DIM), 1.0 / HEAD_DIM))
    return jnp.asarray(blk, dtype=F32).astype(BF16)


def kernel(x, c, ctx, c_ctx, w_ada, b_ada, norm_g, final_g, w_in_even, w_out_even, a_sink, b_rpb, w_in_odd, w_out_odd, c_qnorm, c_knorm, hy_short_w, hy_short_b, hy_w1, hy_b1, hy_f1, hy_w2, hy_b2, hy_f2, hy_w3, hy_b3, hy_bias, w_router, b_router, moe_wg, moe_wu, moe_wd):
    B, S, D = x.shape
    depth = w_ada.shape[0]
    rope = _rope_tables(S)
    wr_pad = jnp.pad(w_router.astype(F32), ((0, 0), (0, LANES - N_EXPERTS)))
    wr_hi = wr_pad.astype(BF16)
    router = (b_router.astype(F32), wr_hi, (wr_pad - wr_hi.astype(F32)).astype(BF16))

    mod_in = jnp.concatenate([jax.nn.silu(c), jax.nn.silu(c_ctx)[None],
                              jnp.zeros((8 - B - 1, D), F32)], axis=0)
    xc = ctx
    for l in range(depth):
        need_ctx = l < depth - 1
        mod = _mm_f32(mod_in, w_ada[l], 1536) + b_ada[l]
        mx = mod[:B].reshape(B, 6, D)
        mc = mod[B].reshape(6, D)
        i = l // 2
        if l % 2 == 0:
            w = w_in_even[i].astype(BF16)
            w_all = jnp.concatenate([w[:, :512], _dup_heads(w[:, 512:640]), _dup_heads(w[:, 640:768]),
                                     w[:, 768:]], axis=1)
            segs_x = ((512, "rope", ATTN_SCALE), (256, "rope", 1.0), (256, "plain", 1.0),
                      (512, "plain", ATTN_SCALE), (512, "plain", 1.0), (512, "plain", 1.0))
            aq, akd, avd, bq, bk, bv = _norm_proj(x, norm_g[l, 0], mx[:, 0], mx[:, 1], w_all, segs_x,
                                                  rope=rope)
            segs_c = tuple((wd, "plain", m) for wd, _, m in segs_x)
            caq, cakd, cavd, cbq, cbk, cbv = _norm_proj(xc, norm_g[l, 0], mc[0:1], mc[1:2], w_all, segs_c)
            ya = _window_attn(aq, akd, avd, cakd, cavd, a_sink[i].astype(F32))
            yb = _nbr_attn(bq, bk, bv, cbk, cbv, _nbr_bias_table(b_rpb[i], S // GRID_W))
            ys = [ya, yb]
            w_out = w_out_even[i].astype(BF16)
            if need_ctx:
                yc = [_ctx_attn(a_sink[i].astype(F32), caq, cakd, cavd, cbq, cbk, cbv)]
        else:
            w = w_in_odd[i].astype(BF16)
            w_all = jnp.concatenate([w[:, :512], _dup_heads(w[:, 512:640]), _dup_heads(w[:, 640:768]),
                                     w[:, 768:]], axis=1)
            gains = jnp.concatenate([jnp.tile(c_qnorm[i], 8), jnp.tile(c_knorm[i], 4)])[None].astype(F32)
            norm = (_head_mean_matrix(512), gains)
            q_mult = ATTN_SCALE * math.log2(math.e)
            segs_x = ((512, "normrope", q_mult), (256, "normrope", 1.0), (256, "plain", 1.0),
                      (3 * D_CH, "plain", 1.0))
            qx, kxd, vxd, ux = _norm_proj(x, norm_g[l, 0], mx[:, 0], mx[:, 1], w_all, segs_x,
                                          rope=rope, norm=norm)
            w_c = w_all[:, 512:1024]
            norm_c = (_head_mean_matrix(512), jnp.tile(c_knorm[i], 4)[None].astype(F32))
            kcd, vcd = _norm_proj(xc, norm_g[l, 0], mc[0:1], mc[1:2], w_c,
                                  ((256, "norm", 1.0), (256, "plain", 1.0)), norm=norm_c)
            logit_bound = (1.02 * HEAD_DIM * q_mult * jnp.max(jnp.abs(c_qnorm[i]))
                           * jnp.max(jnp.abs(c_knorm[i])))
            y_attn = lax.cond(logit_bound <= FULL_NOMAX_LOG2_BOUND,
                              lambda *a: _full_attn(*a, bounded=True),
                              lambda *a: _full_attn(*a, bounded=False),
                              qx, kxd, vxd, kcd, vcd)
            u = _short_conv(ux.astype(F32), hy_short_w[i], hy_short_b[i])
            tabs = _dft_tables()
            taps, inv_norm = _hyena_filters(S, hy_w1[i], hy_b1[i], hy_f1[i], hy_w2[i], hy_b2[i],
                                            hy_f2[i], hy_w3[i], hy_b3[i])
            spec = _filter_fft(taps, inv_norm, tabs[1], tabs[3])
            blocks = D_CH // LANES
            z = _hyena_conv(u, 0, u, blocks, hy_bias[i, 0:1], spec, 0, tabs)
            z = _hyena_conv(z, 0, u, 2 * blocks, hy_bias[i, 1:2], spec, 1, tabs)
            ys = [y_attn, z.astype(BF16)]
            w_out = w_out_odd[i].astype(BF16)
            if need_ctx:
                raise NotImplementedError("context update of an odd layer is not needed at this depth")

        x, hx, comb_x = _out_proj(ys, w_out, x, mx[:, 2], norm_g[l, 1], mx[:, 3], mx[:, 4], router)
        wg, wu, wd = moe_wg[l].astype(BF16), moe_wu[l].astype(BF16), moe_wd[l].astype(BF16)
        if need_ctx:
            xc, hc, comb_c = _out_proj(yc, w_out, xc, mc[2:3], norm_g[l, 1], mc[3:4], mc[4:5], router)
            xc = _moe(hc, comb_c, wg, wu, wd, xc, mc[5:6])
        x = _moe(hx, comb_x, wg, wu, wd, x, mx[:, 5])
    return _final_norm(x, final_g)
```

```python
import functools
import math

import numpy as np
import jax
import jax.numpy as jnp
from jax import lax
from jax.experimental import pallas as pl
from jax.experimental.pallas import tpu as pltpu

F32 = jnp.float32
BF16 = jnp.bfloat16

D_MODEL = 1024
GRID_W = 64
CTX_LEN = 256
HEAD_DIM = 64
ROPE_THETA = 10000.0
EPS = 1e-6
ATTN_SCALE = HEAD_DIM ** -0.5
A_WINDOW = 128
A_BLOCK = 128
B_WIN_H = 8
B_WIN_W = 16
D_CH = 512
HY_BANDS = 16
HY_MAX_DECAY = math.log(1e-2) / 0.3
HY_MIN_DECAY = math.log(1e-2) / 1.5
N_EXPERTS = 16
N_GROUPS = 4
EXPERTS_PER_GROUP = N_EXPERTS // N_GROUPS
TOP_K = 2
D_EXPERT = 512

LANES = 128
NEG = -1e30
VMEM_LIMIT = 48 * 1024 * 1024

FFT_N1 = 64
FFT_N2 = 128


def _cparams(sem):
    return pltpu.CompilerParams(dimension_semantics=sem, vmem_limit_bytes=VMEM_LIMIT)


def _mm_f32_kernel(x_ref, w_ref, o_ref):
    o_ref[...] = jnp.dot(x_ref[...], w_ref[...], preferred_element_type=F32)


def _mm_f32(x, w, tn):
    M, K = x.shape
    N = w.shape[1]
    return pl.pallas_call(
        _mm_f32_kernel,
        grid=(N // tn,),
        in_specs=[pl.BlockSpec((M, K), lambda j: (0, 0)),
                  pl.BlockSpec((K, tn), lambda j: (0, j))],
        out_specs=pl.BlockSpec((M, tn), lambda j: (0, j)),
        out_shape=jax.ShapeDtypeStruct((M, N), F32),
        compiler_params=_cparams(("arbitrary",)),
        name="mm_f32",
    )(x, w)


def _swap16(y):
    n = y.shape[-1]
    lane = lax.broadcasted_iota(jnp.int32, y.shape, y.ndim - 1)
    up = pltpu.roll(y, n - 16, axis=y.ndim - 1)
    dn = pltpu.roll(y, 16, axis=y.ndim - 1)
    return jnp.where((lane % 32) < 16, up, dn)


def _tile_lanes(t, width):
    reps = width // t.shape[-1]
    return t if reps == 1 else jnp.concatenate([t] * reps, axis=-1)


def _norm_proj_kernel(segs, has_rope, has_norm, *refs):
    it = iter(refs)
    x_ref, g_ref, shift_ref, scale_ref, w_ref = (next(it) for _ in range(5))
    cos_ref = sin_ref = bd_ref = gain_ref = None
    if has_rope:
        cos_ref, sin_ref = next(it), next(it)
    if has_norm:
        bd_ref, gain_ref = next(it), next(it)
    out_refs = list(it)

    x = x_ref[0]
    ms = jnp.mean(x * x, axis=-1, keepdims=True)
    h = x * lax.rsqrt(ms + EPS) * g_ref[...]
    h = h * (1.0 + scale_ref[0]) + shift_ref[0]
    y = jnp.dot(h.astype(BF16), w_ref[...], preferred_element_type=F32)

    off = 0
    goff = 0
    for (width, kind, mult), o_ref in zip(segs, out_refs):
        ys = y[:, off:off + width]
        if kind in ("norm", "normrope"):
            bd = bd_ref[...][:width, :width]
            hms = jnp.dot((ys * ys).astype(BF16), bd, preferred_element_type=F32)
            ys = ys * lax.rsqrt(hms + EPS) * gain_ref[:, goff:goff + width]
            goff += width
        if kind in ("rope", "normrope"):
            c = _tile_lanes(cos_ref[...], width)
            s = _tile_lanes(sin_ref[...], width)
            ys = ys * c + _swap16(ys) * s
        if mult != 1.0:
            ys = ys * mult
        o_ref[0] = ys.astype(o_ref.dtype)
        off += width


def _norm_proj(x, g, shift, scale, w, segs, rope=None, norm=None, tm=512):
    B, S, D = x.shape
    N = w.shape[1]
    tm = min(tm, S)
    bm = shift.shape[0]
    mod_map = (lambda b, i: (b, 0, 0)) if bm > 1 else (lambda b, i: (0, 0, 0))
    args = [x, g.reshape(1, D), shift.reshape(bm, 1, D), scale.reshape(bm, 1, D), w]
    in_specs = [pl.BlockSpec((1, tm, D), lambda b, i: (b, i, 0)),
                pl.BlockSpec((1, D), lambda b, i: (0, 0)),
                pl.BlockSpec((1, 1, D), mod_map),
                pl.BlockSpec((1, 1, D), mod_map),
                pl.BlockSpec((D, N), lambda b, i: (0, 0))]
    if rope is not None:
        args += [rope[0], rope[1]]
        in_specs += [pl.BlockSpec((tm, LANES), lambda b, i: (i, 0))] * 2
    if norm is not None:
        args += [norm[0], norm[1]]
        in_specs += [pl.BlockSpec(norm[0].shape, lambda b, i: (0, 0)),
                     pl.BlockSpec(norm[1].shape, lambda b, i: (0, 0))]
    out_shape = [jax.ShapeDtypeStruct((B, S, wd), BF16) for wd, _, _ in segs]
    out_specs = [pl.BlockSpec((1, tm, wd), lambda b, i: (b, i, 0)) for wd, _, _ in segs]
    return pl.pallas_call(
        functools.partial(_norm_proj_kernel, segs, rope is not None, norm is not None),
        grid=(B, S // tm),
        in_specs=in_specs,
        out_specs=out_specs,
        out_shape=out_shape,
        compiler_params=_cparams(("parallel", "parallel")),
        name="norm_proj",
    )(*args)


def _half_mask(shape):
    return lax.broadcasted_iota(jnp.int32, shape, len(shape) - 1) < HEAD_DIM


def _stack_halves(qp):
    lo = _half_mask(qp.shape)
    zero = jnp.zeros_like(qp)
    return jnp.concatenate([jnp.where(lo, qp, zero), jnp.where(lo, zero, qp)], axis=0)


def _merge_halves(o, m):
    return jnp.where(_half_mask((m, LANES)), o[:m], o[m:])


def _scores(q, k):
    return lax.dot_general(q, k, (((1,), (1,)), ((), ())), preferred_element_type=F32)


def _joint_softmax_pv(score_parts, value_parts, extra_logit=None):
    m = functools.reduce(jnp.maximum, [jnp.max(s, axis=-1, keepdims=True) for s in score_parts])
    if extra_logit is not None:
        m = jnp.maximum(m, extra_logit)
    den = jnp.exp(extra_logit - m) if extra_logit is not None else 0.0
    acc = None
    for s, v in zip(score_parts, value_parts):
        p = jnp.exp(s - m)
        den = den + jnp.sum(p, axis=-1, keepdims=True)
        pv = jnp.dot(p.astype(BF16), v, preferred_element_type=F32)
        acc = pv if acc is None else acc + pv
    return acc / den


def _sink_column(sink_ref, first_head, n_heads, rows_per_head):
    rows = lax.broadcasted_iota(jnp.int32, (n_heads * rows_per_head, 1), 0)
    col = jnp.zeros((n_heads * rows_per_head, 1), F32)
    for j in range(n_heads):
        in_head = (rows >= j * rows_per_head) & (rows < (j + 1) * rows_per_head)
        col = jnp.where(in_head, sink_ref[first_head + j], col)
    return col


def _window_attn_kernel(seq_len, sink_ref, q_ref, kp_ref, kc_ref, kn_ref, vp_ref, vc_ref, vn_ref,
                        ck_ref, cv_ref, o_ref):
    i = pl.program_id(1)
    blk = A_BLOCK
    q = q_ref[0]
    rows = lax.broadcasted_iota(jnp.int32, (4 * blk, 3 * blk), 0) % blk
    rel = lax.broadcasted_iota(jnp.int32, (4 * blk, 3 * blk), 1) - blk
    gpos = i * blk + rel
    valid = (jnp.abs(rows - rel) <= A_WINDOW) & (gpos >= 0) & (gpos < seq_len)
    outs = []
    for g in range(2):
        ls = slice(g * LANES, (g + 1) * LANES)
        k_loc = jnp.concatenate([kp_ref[0][:, ls], kc_ref[0][:, ls], kn_ref[0][:, ls]], axis=0)
        v_loc = jnp.concatenate([vp_ref[0][:, ls], vc_ref[0][:, ls], vn_ref[0][:, ls]], axis=0)
        qs = jnp.concatenate([_stack_halves(q[:, (2 * g + j) * LANES:(2 * g + j + 1) * LANES])
                              for j in range(2)], axis=0)
        s_loc = jnp.where(valid, _scores(qs, k_loc), NEG)
        s_ctx = _scores(qs, ck_ref[0][:, ls])
        sink = _sink_column(sink_ref, 4 * g, 4, blk)
        o = _joint_softmax_pv([s_loc, s_ctx], [v_loc, cv_ref[0][:, ls]], sink)
        outs += [_merge_halves(o[:2 * blk], blk), _merge_halves(o[2 * blk:], blk)]
    o_ref[0] = jnp.concatenate(outs, axis=-1).astype(o_ref.dtype)


def _window_attn(q, kd, vd, ckd, cvd, sink):
    B, S, _ = q.shape
    nb = S // A_BLOCK
    kv_spec = lambda f: pl.BlockSpec((1, A_BLOCK, 2 * LANES), f)
    prev_map = lambda b, i: (b, jnp.maximum(i - 1, 0), 0)
    cur_map = lambda b, i: (b, i, 0)
    next_map = lambda b, i: (b, jnp.minimum(i + 1, nb - 1), 0)
    ctx_spec = pl.BlockSpec((1, CTX_LEN, 2 * LANES), lambda b, i: (b, 0, 0))
    return pl.pallas_call(
        functools.partial(_window_attn_kernel, S),
        grid=(B, nb),
        in_specs=[pl.BlockSpec(memory_space=pltpu.SMEM),
                  pl.BlockSpec((1, A_BLOCK, 4 * LANES), cur_map),
                  kv_spec(prev_map), kv_spec(cur_map), kv_spec(next_map),
                  kv_spec(prev_map), kv_spec(cur_map), kv_spec(next_map),
                  ctx_spec, ctx_spec],
        out_specs=pl.BlockSpec((1, A_BLOCK, 4 * LANES), cur_map),
        out_shape=jax.ShapeDtypeStruct((B, S, 4 * LANES), BF16),
        compiler_params=_cparams(("parallel", "parallel")),
        name="window_attn",
    )(sink, q, kd, kd, kd, vd, vd, vd, ckd, cvd)


NBR_ROWS = 4
NBR_KROWS = 12


def _nbr_start_row(i, n_rows):
    return jnp.clip(i * NBR_ROWS - B_WIN_H // 2, 0, n_rows - NBR_KROWS)


def _nbr_attn_kernel(n_rows, q_ref, k_ref, v_ref, ck_ref, cv_ref, bias_ref, o_ref):
    i = pl.program_id(2)
    nq = NBR_ROWS * GRID_W
    nk = NBR_KROWS * GRID_W
    start = pl.multiple_of(_nbr_start_row(i, n_rows) * GRID_W, GRID_W)
    k_loc = k_ref[0, pl.ds(start, nk), :]
    v_loc = v_ref[0, pl.ds(start, nk), :]
    qs = _stack_halves(q_ref[0])
    s_loc = _scores(qs, k_loc) + bias_ref[0].reshape(2 * nq, nk)
    s_ctx = _scores(qs, ck_ref[0])
    o = _joint_softmax_pv([s_loc, s_ctx], [v_loc, cv_ref[0]])
    o_ref[0] = _merge_halves(o, nq).astype(o_ref.dtype)


def _nbr_bias_table(rpb, n_rows):
    kh = B_WIN_H
    n_heads = rpb.shape[0]
    col = np.arange(GRID_W)
    cs = np.clip(col - B_WIN_W // 2, 0, GRID_W - B_WIN_W)
    col_ok = (col[None, :] >= cs[:, None]) & (col[None, :] < cs[:, None] + B_WIN_W)
    dc = np.clip(col[None, :] - col[:, None], -(B_WIN_W - 1), B_WIN_W - 1) + B_WIN_W - 1
    pick = (dc[..., None] == np.arange(2 * B_WIN_W - 1)).astype(np.float32)
    by_col = jnp.einsum("hdc,qkc->hdqk", rpb.astype(F32), pick, precision=lax.Precision.HIGHEST)
    by_col = jnp.where(col_ok[None, None], by_col, NEG)
    masked = jnp.full((n_heads, GRID_W, GRID_W), NEG, F32)
    tabs = []
    for r0 in (0, NBR_ROWS, n_rows - NBR_ROWS):
        start = int(np.clip(r0 - kh // 2, 0, n_rows - NBR_KROWS))
        per_row = []
        for ri in range(NBR_ROWS):
            r = r0 + ri
            rs = int(np.clip(r - kh // 2, 0, n_rows - kh))
            slabs = [by_col[:, start + kri - r + kh - 1] if rs <= start + kri < rs + kh else masked
                     for kri in range(NBR_KROWS)]
            per_row.append(jnp.stack(slabs, axis=2))
        tabs.append(jnp.stack(per_row, axis=1).reshape(n_heads, NBR_ROWS * GRID_W, NBR_KROWS * GRID_W))
    return jnp.stack(tabs)


def _nbr_attn(q, k, v, ck, cv, bias):
    B, S, _ = q.shape
    n_rows = S // GRID_W
    nsteps = n_rows // NBR_ROWS
    nq = NBR_ROWS * GRID_W
    nk = NBR_KROWS * GRID_W
    pat = lambda i: jnp.where(i == 0, 0, jnp.where(i == nsteps - 1, 2, 1))
    return pl.pallas_call(
        functools.partial(_nbr_attn_kernel, n_rows),
        grid=(B, 4, nsteps),
        in_specs=[pl.BlockSpec((1, nq, LANES), lambda b, p, i: (b, i, p)),
                  pl.BlockSpec((1, S, LANES), lambda b, p, i: (b, 0, p)),
                  pl.BlockSpec((1, S, LANES), lambda b, p, i: (b, 0, p)),
                  pl.BlockSpec((1, CTX_LEN, LANES), lambda b, p, i: (b, 0, p)),
                  pl.BlockSpec((1, CTX_LEN, LANES), lambda b, p, i: (b, 0, p)),
                  pl.BlockSpec((1, 2, nq, nk), lambda b, p, i: (pat(i), p, 0, 0))],
        out_specs=pl.BlockSpec((1, nq, LANES), lambda b, p, i: (b, i, p)),
        out_shape=jax.ShapeDtypeStruct((B, S, 4 * LANES), BF16),
        compiler_params=_cparams(("parallel", "parallel", "arbitrary")),
        name="nbr_attn",
    )(q, k, v, ck, cv, bias)


FULL_TQ = 256
FULL_TK = 512
FULL_NOMAX_LOG2_BOUND = 60.0


def _full_attn_kernel(bounded, q_ref, k_ref, v_ref, ck_ref, cv_ref, o_ref):
    tq = FULL_TQ
    q = q_ref[0]
    qs = jnp.concatenate([_stack_halves(q[:, :LANES]), _stack_halves(q[:, LANES:])], axis=0)

    def step_bounded(acc, k, v):
        v_ones = jnp.where(_half_mask(v.shape), v, jnp.ones_like(v))
        p = jnp.exp2(_scores(qs, k))
        return acc + jnp.dot(p.astype(BF16), v_ones, preferred_element_type=F32)

    def step_online(carry, k, v):
        m, l, acc = carry
        s = _scores(qs, k)
        m_new = jnp.maximum(m, jnp.max(s, axis=-1, keepdims=True))
        alpha = jnp.exp2(m - m_new)
        p = jnp.exp2(s - m_new)
        l = l * alpha + jnp.sum(p, axis=-1, keepdims=True)
        acc = acc * alpha + jnp.dot(p.astype(BF16), v, preferred_element_type=F32)
        return m_new, l, acc

    step = step_bounded if bounded else step_online
    init = jnp.zeros((4 * tq, LANES), F32)
    if not bounded:
        init = (jnp.full((4 * tq, 1), NEG, F32), jnp.zeros((4 * tq, 1), F32), init)
    carry = step(init, ck_ref[0], cv_ref[0])

    def body(j, carry):
        off = pl.multiple_of(j * FULL_TK, FULL_TK)
        return step(carry, k_ref[0, pl.ds(off, FULL_TK), :], v_ref[0, pl.ds(off, FULL_TK), :])

    carry = lax.fori_loop(0, k_ref.shape[1] // FULL_TK, body, carry)
    if bounded:
        o = carry * pltpu.roll(1.0 / carry, HEAD_DIM, axis=1)
        merge = lambda a, b: jnp.where(_half_mask((tq, LANES)), a, pltpu.roll(b, HEAD_DIM, axis=1))
        pairs = [merge(o[0:tq], o[tq:2 * tq]), merge(o[2 * tq:3 * tq], o[3 * tq:])]
    else:
        o = carry[2] / carry[1]
        pairs = [_merge_halves(o[:2 * tq], tq), _merge_halves(o[2 * tq:], tq)]
    o_ref[0] = jnp.concatenate(pairs, axis=-1).astype(o_ref.dtype)


def _full_attn(q, kd, vd, ckd, cvd, bounded):
    B, S, _ = q.shape
    return pl.pallas_call(
        functools.partial(_full_attn_kernel, bounded),
        grid=(B, 2, S // FULL_TQ),
        in_specs=[pl.BlockSpec((1, FULL_TQ, 2 * LANES), lambda b, g, i: (b, i, g)),
                  pl.BlockSpec((1, S, LANES), lambda b, g, i: (b, 0, g)),
                  pl.BlockSpec((1, S, LANES), lambda b, g, i: (b, 0, g)),
                  pl.BlockSpec((1, CTX_LEN, LANES), lambda b, g, i: (b, 0, g)),
                  pl.BlockSpec((1, CTX_LEN, LANES), lambda b, g, i: (b, 0, g))],
        out_specs=pl.BlockSpec((1, FULL_TQ, 2 * LANES), lambda b, g, i: (b, i, g)),
        out_shape=jax.ShapeDtypeStruct((B, S, 4 * LANES), BF16),
        compiler_params=_cparams(("parallel", "parallel", "arbitrary")),
        name="full_attn_bounded" if bounded else "full_attn_online",
    )(q, kd, vd, ckd, cvd)


def _ctx_attn_kernel(sink_ref, aq_ref, akd_ref, avd_ref, bq_ref, bk_ref, bv_ref, o_ref):
    n = CTX_LEN
    aq = aq_ref[0]
    bq = bq_ref[0]
    outs = []
    for g in range(2):
        ls = slice(g * LANES, (g + 1) * LANES)
        qs = jnp.concatenate([_stack_halves(aq[:, (2 * g + j) * LANES:(2 * g + j + 1) * LANES])
                              for j in range(2)], axis=0)
        sink = _sink_column(sink_ref, 4 * g, 4, n)
        o = _joint_softmax_pv([_scores(qs, akd_ref[0][:, ls])], [avd_ref[0][:, ls]], sink)
        outs += [_merge_halves(o[:2 * n], n), _merge_halves(o[2 * n:], n)]
    for p in range(4):
        ls = slice(p * LANES, (p + 1) * LANES)
        qs = _stack_halves(bq[:, ls])
        o = _joint_softmax_pv([_scores(qs, bk_ref[0][:, ls])], [bv_ref[0][:, ls]])
        outs.append(_merge_halves(o, n))
    o_ref[0] = jnp.concatenate(outs, axis=-1).astype(o_ref.dtype)


def _ctx_attn(sink, aq, akd, avd, bq, bk, bv):
    B = aq.shape[0]
    spec = lambda a: pl.BlockSpec((1,) + a.shape[1:], lambda b: (b, 0, 0))
    args = (aq, akd, avd, bq, bk, bv)
    return pl.pallas_call(
        _ctx_attn_kernel,
        grid=(B,),
        in_specs=[pl.BlockSpec(memory_space=pltpu.SMEM)] + [spec(a) for a in args],
        out_specs=pl.BlockSpec((1, CTX_LEN, 8 * LANES), lambda b: (b, 0, 0)),
        out_shape=jax.ShapeDtypeStruct((B, CTX_LEN, 8 * LANES), BF16),
        compiler_params=_cparams(("parallel",)),
        name="ctx_attn",
    )(sink, *args)


def _pick4(idx, vals):
    return jnp.where(idx == 0, vals[0], jnp.where(idx == 1, vals[1], jnp.where(idx == 2, vals[2], vals[3])))


def _route_rows(lg_t, b_ref):
    n_tok = lg_t.shape[1]
    s = [jax.nn.sigmoid(lg_t[e:e + 1, :]) for e in range(N_EXPERTS)]
    sel = [s[e] + b_ref[e] for e in range(N_EXPERTS)]
    n = EXPERTS_PER_GROUP
    gscore = []
    for j in range(N_GROUPS):
        v = sel[n * j:n * (j + 1)]
        pair_sums = [v[a] + v[b] for a in range(n) for b in range(a + 1, n)]
        gscore.append(functools.reduce(jnp.maximum, pair_sums))
    best, gbest = gscore[0], jnp.zeros((1, n_tok), jnp.int32)
    for j in range(1, N_GROUPS):
        upd = gscore[j] > best
        best = jnp.where(upd, gscore[j], best)
        gbest = jnp.where(upd, j, gbest)
    v = [_pick4(gbest, [sel[n * j + i] for j in range(N_GROUPS)]) for i in range(n)]
    u = [_pick4(gbest, [s[n * j + i] for j in range(N_GROUPS)]) for i in range(n)]
    m1, i1 = v[0], jnp.zeros((1, n_tok), jnp.int32)
    for i in range(1, n):
        upd = v[i] > m1
        m1 = jnp.where(upd, v[i], m1)
        i1 = jnp.where(upd, i, i1)
    m2, i2 = jnp.full((1, n_tok), -jnp.inf, F32), jnp.zeros((1, n_tok), jnp.int32)
    for i in range(n):
        upd = (i1 != i) & (v[i] > m2)
        m2 = jnp.where(upd, v[i], m2)
        i2 = jnp.where(upd, i, i2)
    u1, u2 = _pick4(i1, u), _pick4(i2, u)
    tot = u1 + u2
    return n * gbest + i1, n * gbest + i2, u1 / tot, u2 / tot


def _out_proj_kernel(n_y, sparse, *refs):
    y_refs = refs[:n_y]
    br_ref, w_ref, x_ref, gate_ref, g_ref, shift_ref, scale_ref, wrh_ref, wrl_ref = refs[n_y:n_y + 9]
    n_in = n_y + 9
    if sparse:
        tri_ref = refs[n_in]
        n_in += 1
    outs = refs[n_in:]
    xo_ref = outs[0]
    off = 0
    acc = None
    for y_ref in y_refs:
        wdt = y_ref.shape[-1]
        part = jnp.dot(y_ref[0], w_ref[off:off + wdt, :], preferred_element_type=F32)
        acc = part if acc is None else acc + part
        off += wdt
    x = x_ref[0] + gate_ref[0] * acc
    xo_ref[0] = x
    ms = jnp.mean(x * x, axis=-1, keepdims=True)
    h = x * lax.rsqrt(ms + EPS) * g_ref[...]
    h = h * (1.0 + scale_ref[0]) + shift_ref[0]
    hh = h.astype(BF16)
    hl = (h - hh.astype(F32)).astype(BF16)
    lg = (jnp.dot(hh, wrh_ref[...], preferred_element_type=F32)
          + jnp.dot(hl, wrh_ref[...], preferred_element_type=F32)
          + jnp.dot(hh, wrl_ref[...], preferred_element_type=F32))
    lg_t = lg.T[:N_EXPERTS]
    e1, e2, w1, w2 = _route_rows(lg_t, br_ref)
    rows = lax.broadcasted_iota(jnp.int32, lg_t.shape, 0)
    if not sparse:
        h_ref, comb_ref = outs[1:]
        h_ref[0] = hh
        comb_t = jnp.where(rows == e1, w1, 0.0) + jnp.where(rows == e2, w2, 0.0)
        comb_ref[0] = jnp.concatenate(
            [comb_t, jnp.zeros((LANES - N_EXPERTS, comb_t.shape[1]), F32)], axis=0).T
        return

    h3_ref, route_ref, count_ref, run_ref = outs[1:]
    tm = h.shape[0]
    for s in range(D_MODEL // LANES):
        h3_ref[pl.ds(s, tm, stride=D_MODEL // LANES), :] = h[:, s * LANES:(s + 1) * LANES]

    first = (pl.program_id(0) == 0) & (pl.program_id(1) == 0)

    @pl.when(first)
    def _():
        run_ref[...] = jnp.zeros_like(run_ref)

    member = (rows == e1) | (rows == e2)
    before = jnp.dot(jnp.where(member, 1.0, 0.0).astype(BF16), tri_ref[...],
                     preferred_element_type=F32) + run_ref[:, 0:1]
    r1 = jnp.sum(jnp.where(rows == e1, before, 0.0), axis=0, keepdims=True)
    r2 = jnp.sum(jnp.where(rows == e2, before, 0.0), axis=0, keepdims=True)
    run = run_ref[...] + jnp.sum(jnp.where(member, 1.0, 0.0), axis=1, keepdims=True)
    run_ref[...] = run
    count_ref[...] = run
    field = lax.broadcasted_iota(jnp.int32, route_ref.shape, 0)
    vals = (e1.astype(F32), e2.astype(F32), w1, w2, r1, r2)
    route = jnp.zeros(route_ref.shape, F32)
    for k, v in enumerate(vals):
        route = jnp.where(field == k, v, route)
    route_ref[...] = route


ROUTE_FIELDS = 8


def _out_proj(ys, w, x, gate, g, shift, scale, router, sparse, tm=512):
    B, S, D = x.shape
    tm = min(tm, S)
    bm = gate.shape[0]
    b_router, wr_hi, wr_lo = router
    nt = S // tm
    mod_map = (lambda b, i: (b, 0, 0)) if bm > 1 else (lambda b, i: (0, 0, 0))
    mod_spec = pl.BlockSpec((1, 1, D), mod_map)
    row_map = lambda b, i: (b, i, 0)
    in_specs = ([pl.BlockSpec((1, tm, y.shape[-1]), row_map) for y in ys]
                + [pl.BlockSpec(memory_space=pltpu.SMEM),
                   pl.BlockSpec(w.shape, lambda b, i: (0, 0)),
                   pl.BlockSpec((1, tm, D), row_map), mod_spec,
                   pl.BlockSpec((1, D), lambda b, i: (0, 0)), mod_spec, mod_spec,
                   pl.BlockSpec(wr_hi.shape, lambda b, i: (0, 0)),
                   pl.BlockSpec(wr_lo.shape, lambda b, i: (0, 0))])
    args = list(ys) + [b_router, w, x, gate.reshape(bm, 1, D), g.reshape(1, D), shift.reshape(bm, 1, D),
                       scale.reshape(bm, 1, D), wr_hi, wr_lo]
    out_specs = [pl.BlockSpec((1, tm, D), row_map)]
    out_shape = [jax.ShapeDtypeStruct((B, S, D), F32)]
    scratch = []
    if sparse:
        slab = D // LANES
        tri = jnp.asarray(np.triu(np.ones((tm, tm), np.float32), 1)).astype(BF16)
        args.append(tri)
        in_specs.append(pl.BlockSpec(tri.shape, lambda b, i: (0, 0)))
        out_specs += [pl.BlockSpec((tm * slab, LANES), lambda b, i: (b * nt + i, 0)),
                      pl.BlockSpec((ROUTE_FIELDS, tm), lambda b, i: (0, b * nt + i)),
                      pl.BlockSpec((N_EXPERTS, LANES), lambda b, i: (0, 0))]
        out_shape += [jax.ShapeDtypeStruct((B * S * slab, LANES), F32),
                      jax.ShapeDtypeStruct((ROUTE_FIELDS, B * S), F32),
                      jax.ShapeDtypeStruct((N_EXPERTS, LANES), F32)]
        scratch = [pltpu.VMEM((N_EXPERTS, LANES), F32)]
    else:
        out_specs += [pl.BlockSpec((1, tm, D), row_map), pl.BlockSpec((1, tm, LANES), row_map)]
        out_shape += [jax.ShapeDtypeStruct((B, S, D), BF16), jax.ShapeDtypeStruct((B, S, LANES), F32)]
    return pl.pallas_call(
        functools.partial(_out_proj_kernel, len(ys), sparse),
        grid=(B, nt),
        in_specs=in_specs,
        out_specs=out_specs,
        out_shape=out_shape,
        scratch_shapes=scratch,
        compiler_params=_cparams(("arbitrary", "arbitrary") if sparse else ("parallel", "parallel")),
        name="out_proj_sparse" if sparse else "out_proj",
    )(*args)


def _moe_kernel(h_ref, comb_ref, wg_ref, wu_ref, wd_ref, x_ref, gate_ref, o_ref, acc_ref):
    e = pl.program_id(2)

    @pl.when(e == 0)
    def _():
        acc_ref[...] = jnp.zeros_like(acc_ref)

    h = h_ref[0]
    a = jnp.dot(h, wg_ref[0], preferred_element_type=F32)
    u = jnp.dot(h, wu_ref[0], preferred_element_type=F32)
    he = (a * jax.nn.sigmoid(a) * u).astype(BF16)
    y = jnp.dot(he, wd_ref[0], preferred_element_type=F32)
    lane = lax.broadcasted_iota(jnp.int32, comb_ref.shape[1:], 1)
    c = jnp.sum(jnp.where(lane == e, comb_ref[0], 0.0), axis=-1, keepdims=True)
    acc_ref[...] += c * y

    @pl.when(e == pl.num_programs(2) - 1)
    def _():
        o_ref[0] = x_ref[0] + gate_ref[0] * acc_ref[...]


def _moe(h, comb, wg, wu, wd, x, gate, tm=1024):
    B, S, D = x.shape
    tm = min(tm, S)
    bm = gate.shape[0]
    mod_map = (lambda b, i, e: (b, 0, 0)) if bm > 1 else (lambda b, i, e: (0, 0, 0))
    row_map = lambda b, i, e: (b, i, 0)
    return pl.pallas_call(
        _moe_kernel,
        grid=(B, S // tm, N_EXPERTS),
        in_specs=[pl.BlockSpec((1, tm, D), row_map),
                  pl.BlockSpec((1, tm, LANES), row_map),
                  pl.BlockSpec((1, D, D_EXPERT), lambda b, i, e: (e, 0, 0)),
                  pl.BlockSpec((1, D, D_EXPERT), lambda b, i, e: (e, 0, 0)),
                  pl.BlockSpec((1, D_EXPERT, D), lambda b, i, e: (e, 0, 0)),
                  pl.BlockSpec((1, tm, D), row_map),
                  pl.BlockSpec((1, 1, D), mod_map)],
        out_specs=pl.BlockSpec((1, tm, D), row_map),
        out_shape=jax.ShapeDtypeStruct((B, S, D), F32),
        scratch_shapes=[pltpu.VMEM((tm, D), F32)],
        compiler_params=_cparams(("parallel", "parallel", "arbitrary")),
        name="moe",
    )(h, comb, wg, wu, wd, x, gate.reshape(bm, 1, D))


MOE_TM = 512
SLAB = D_MODEL // LANES
CMB_TM = 256


def _moe_plan(route, counts, n_tok):
    cnt = counts[:, 0].astype(jnp.int32)
    padded = (cnt + MOE_TM - 1) // MOE_TM * MOE_TM
    ends = jnp.cumsum(padded)
    starts = ends - padded
    experts = jnp.arange(N_EXPERTS, dtype=jnp.int32)

    def slot_of(e_row, rank_row):
        e = e_row.astype(jnp.int32)
        start = jnp.sum(jnp.where(e[:, None] == experts[None], starts[None], 0), axis=1)
        return start + rank_row.astype(jnp.int32)

    slot1, slot2 = slot_of(route[0], route[4]), slot_of(route[1], route[5])
    n_tiles = 2 * n_tok // MOE_TM + N_EXPERTS
    tile_first = jnp.arange(n_tiles, dtype=jnp.int32) * MOE_TM
    tile_expert = jnp.minimum(jnp.sum(ends[None, :] <= tile_first[:, None], axis=1), N_EXPERTS - 1)
    n_valid = (ends[-1:] // MOE_TM).astype(jnp.int32)
    tok = jnp.arange(n_tok, dtype=jnp.int32)
    tok_of_slot = jnp.zeros((n_tiles * MOE_TM,), jnp.int32).at[jnp.concatenate([slot1, slot2])].set(
        jnp.concatenate([tok, tok]))
    slots = jnp.stack([slot1.reshape(-1, CMB_TM), slot2.reshape(-1, CMB_TM)], axis=1)
    return (tile_expert.astype(jnp.int32), n_valid, tok_of_slot.reshape(n_tiles, 1, MOE_TM),
            slots.reshape(-1, 1, 2 * CMB_TM))


def _slab_gather(src_ref, idx_smem, n_rows, buf_ref, sem):
    def body(r, carry):
        src = pl.multiple_of(idx_smem[r] * SLAB, SLAB)
        dst = pl.multiple_of(r * SLAB, SLAB)
        pltpu.make_async_copy(src_ref.at[pl.ds(src, SLAB)], buf_ref.at[pl.ds(dst, SLAB)], sem).start()
        return carry
    lax.fori_loop(0, n_rows, body, 0, unroll=8)


def _slab_gather_wait(src_ref, n_rows, buf_ref, sem):
    pltpu.make_async_copy(src_ref.at[pl.ds(0, n_rows * SLAB)], buf_ref, sem).wait()


def _unslab(buf_ref, first, n_rows):
    return jnp.concatenate([buf_ref[pl.ds(first + s, n_rows, stride=SLAB), :] for s in range(SLAB)], axis=1)


def _moe_grouped_kernel(te_ref, nv_ref, idx_ref, idx_next_ref, h3_ref, wg_ref, wu_ref, wd_ref, o_ref,
                        buf_ref, idx_smem, sem):
    j = pl.program_id(0)
    n_valid = nv_ref[0]
    slot = j % 2

    @pl.when(j == 0)
    def _():
        pltpu.sync_copy(idx_ref.at[0, 0], idx_smem)
        _slab_gather(h3_ref, idx_smem, MOE_TM, buf_ref.at[0], sem.at[0])

    @pl.when(j < n_valid)
    def _():
        _slab_gather_wait(h3_ref, MOE_TM, buf_ref.at[slot], sem.at[slot])

        @pl.when(j + 1 < n_valid)
        def _():
            pltpu.sync_copy(idx_next_ref.at[0, 0], idx_smem)
            _slab_gather(h3_ref, idx_smem, MOE_TM, buf_ref.at[1 - slot], sem.at[1 - slot])

        x = _unslab(buf_ref.at[slot], 0, MOE_TM).astype(BF16)
        a = jnp.dot(x, wg_ref[0], preferred_element_type=F32)
        u = jnp.dot(x, wu_ref[0], preferred_element_type=F32)
        he = (a * jax.nn.sigmoid(a) * u).astype(BF16)
        y = jnp.dot(he, wd_ref[0], preferred_element_type=F32)
        for s in range(SLAB):
            o_ref[pl.ds(s, MOE_TM, stride=SLAB), :] = y[:, s * LANES:(s + 1) * LANES]

    @pl.when(j >= n_valid)
    def _():
        o_ref[...] = jnp.zeros_like(o_ref)


def _moe_grouped(h3, plan, wg, wu, wd):
    tile_expert, n_valid, tok_of_slot, _ = plan
    n_tiles = tok_of_slot.shape[0]
    D = D_MODEL
    grid_spec = pltpu.PrefetchScalarGridSpec(
        num_scalar_prefetch=2,
        grid=(n_tiles,),
        in_specs=[pl.BlockSpec((1, 1, MOE_TM), lambda j, te, nv: (j, 0, 0)),
                  pl.BlockSpec((1, 1, MOE_TM), lambda j, te, nv: (jnp.minimum(j + 1, n_tiles - 1), 0, 0)),
                  pl.BlockSpec(memory_space=pl.ANY),
                  pl.BlockSpec((1, D, D_EXPERT), lambda j, te, nv: (te[j], 0, 0)),
                  pl.BlockSpec((1, D, D_EXPERT), lambda j, te, nv: (te[j], 0, 0)),
                  pl.BlockSpec((1, D_EXPERT, D), lambda j, te, nv: (te[j], 0, 0))],
        out_specs=pl.BlockSpec((MOE_TM * SLAB, LANES), lambda j, te, nv: (j, 0)),
        scratch_shapes=[pltpu.VMEM((2, MOE_TM * SLAB, LANES), F32),
                        pltpu.SMEM((MOE_TM,), jnp.int32),
                        pltpu.SemaphoreType.DMA((2,))])
    return pl.pallas_call(
        _moe_grouped_kernel,
        grid_spec=grid_spec,
        out_shape=jax.ShapeDtypeStruct((n_tiles * MOE_TM * SLAB, LANES), F32),
        compiler_params=_cparams(("arbitrary",)),
        name="moe_grouped",
    )(tile_expert, n_valid, tok_of_slot, tok_of_slot, h3, wg, wu, wd)


def _moe_combine_kernel(n_steps, idx_ref, idx_next_ref, ys_ref, route_ref, x_ref, gate_ref, o_ref,
                        buf_ref, idx_smem, sem):
    j = pl.program_id(0) * pl.num_programs(1) + pl.program_id(1)
    slot = j % 2
    n_rows = 2 * CMB_TM

    @pl.when(j == 0)
    def _():
        pltpu.sync_copy(idx_ref.at[0, 0], idx_smem)
        _slab_gather(ys_ref, idx_smem, n_rows, buf_ref.at[0], sem.at[0])

    _slab_gather_wait(ys_ref, n_rows, buf_ref.at[slot], sem.at[slot])

    @pl.when(j + 1 < n_steps)
    def _():
        pltpu.sync_copy(idx_next_ref.at[0, 0], idx_smem)
        _slab_gather(ys_ref, idx_smem, n_rows, buf_ref.at[1 - slot], sem.at[1 - slot])

    y1 = _unslab(buf_ref.at[slot], 0, CMB_TM)
    y2 = _unslab(buf_ref.at[slot], CMB_TM * SLAB, CMB_TM)
    eye = (lax.broadcasted_iota(jnp.int32, (CMB_TM, CMB_TM), 0)
           == lax.broadcasted_iota(jnp.int32, (CMB_TM, CMB_TM), 1))
    w1 = jnp.sum(jnp.where(eye, route_ref[2:3, :], 0.0), axis=1, keepdims=True)
    w2 = jnp.sum(jnp.where(eye, route_ref[3:4, :], 0.0), axis=1, keepdims=True)
    o_ref[0] = x_ref[0] + gate_ref[0] * (w1 * y1 + w2 * y2)


def _moe_combine(ys3, plan, route, x, gate):
    B, S, D = x.shape
    slots = plan[3]
    nt = S // CMB_TM
    n_steps = B * nt
    lin = lambda b, i: b * nt + i
    return pl.pallas_call(
        functools.partial(_moe_combine_kernel, n_steps),
        grid=(B, nt),
        in_specs=[pl.BlockSpec((1, 1, 2 * CMB_TM), lambda b, i: (lin(b, i), 0, 0)),
                  pl.BlockSpec((1, 1, 2 * CMB_TM), lambda b, i: (jnp.minimum(lin(b, i) + 1, n_steps - 1), 0, 0)),
                  pl.BlockSpec(memory_space=pl.ANY),
                  pl.BlockSpec((ROUTE_FIELDS, CMB_TM), lambda b, i: (0, lin(b, i))),
                  pl.BlockSpec((1, CMB_TM, D), lambda b, i: (b, i, 0)),
                  pl.BlockSpec((1, 1, D), lambda b, i: (b, 0, 0))],
        out_specs=pl.BlockSpec((1, CMB_TM, D), lambda b, i: (b, i, 0)),
        out_shape=jax.ShapeDtypeStruct((B, S, D), F32),
        scratch_shapes=[pltpu.VMEM((2, 2 * CMB_TM * SLAB, LANES), F32),
                        pltpu.SMEM((2 * CMB_TM,), jnp.int32),
                        pltpu.SemaphoreType.DMA((2,))],
        compiler_params=_cparams(("arbitrary", "arbitrary")),
        name="moe_combine",
    )(slots, slots, ys3, route, x, gate.reshape(B, 1, D))


def _dft_tables():
    n1, n2, n = FFT_N1, FFT_N2, FFT_N1 * FFT_N2
    k1 = np.arange(n1)
    f1 = np.exp(-2j * np.pi * np.outer(k1, np.arange(n1)) / n1)
    tw = np.exp(-2j * np.pi * np.outer(np.arange(n2), k1) / n)
    ftw = f1[None, :, :] * tw[:, :, None]
    half = n1 // 2
    fh = ftw[:, :, :half]
    g_fwd = np.concatenate([np.concatenate([fh.real, -fh.imag], axis=2),
                            np.concatenate([fh.imag, fh.real], axis=2)], axis=1)
    back = ftw[:, :, ::-1][:, :, :half].copy()
    back[0] = np.roll(ftw[0], -1, axis=1)[:, ::-1][:, :half]
    back[0][:, 0] = 0.0
    fk = np.concatenate([fh, back], axis=2)
    g_real = np.concatenate([fk.real, fk.imag], axis=1)
    gi = np.conj(np.transpose(fh, (0, 2, 1))) / n
    g_inv = np.concatenate([np.concatenate([gi.real, -gi.imag], axis=2),
                            np.concatenate([gi.imag, gi.real], axis=2)], axis=1)
    f2 = np.exp(-2j * np.pi * np.outer(np.arange(n2), np.arange(n2)) / n2)
    f2_fwd = np.block([[f2.real, -f2.imag], [f2.imag, f2.real]])
    f2c = np.conj(f2)
    f2_inv = np.block([[f2c.real, -f2c.imag], [f2c.imag, f2c.real]])
    as_bf = lambda a: jnp.asarray(a, dtype=F32).astype(BF16)
    return as_bf(g_fwd), as_bf(g_real), as_bf(g_inv), as_bf(f2_fwd), as_bf(f2_inv)


def _fft_fast_stage(stage_ref, k1, f2):
    slab = 2 * FFT_N1
    m = jnp.concatenate([stage_ref[pl.ds(k1, FFT_N2, stride=slab), :],
                         stage_ref[pl.ds(FFT_N1 + k1, FFT_N2, stride=slab), :]], axis=0)
    return jnp.dot(f2, m.astype(BF16), preferred_element_type=F32)


def _filter_fft_kernel(hf_ref, hb_ref, inv_ref, g_ref, f2_ref, h_ref, stage_ref):
    slab = 2 * FFT_N1
    half = FFT_N1 // 2
    for n2 in range(FFT_N2):
        x = jnp.concatenate([hf_ref[pl.ds(n2, half, stride=FFT_N2), :],
                             hb_ref[pl.ds((FFT_N2 - n2) % FFT_N2, half, stride=FFT_N2), :]], axis=0)
        stage_ref[n2 * slab:(n2 + 1) * slab, :] = jnp.dot(g_ref[n2], x.astype(BF16),
                                                          preferred_element_type=F32)
    f2 = f2_ref[...]
    inv = inv_ref[...]
    for k1 in range(FFT_N1):
        h_ref[0, k1] = (_fft_fast_stage(stage_ref, k1, f2) * inv).astype(h_ref.dtype)


def _filter_fft(taps, inv_norm, g_real, f2_fwd, ct=LANES):
    L, cols = taps.shape
    C = D_CH
    n_ord = cols // (2 * C)
    nc = C // ct
    once = pl.Buffered(1)
    return pl.pallas_call(
        _filter_fft_kernel,
        grid=(n_ord, nc),
        in_specs=[pl.BlockSpec((L, ct), lambda o, c: (0, o * nc + c)),
                  pl.BlockSpec((L, ct), lambda o, c: (0, (n_ord + o) * nc + c)),
                  pl.BlockSpec((1, ct), lambda o, c: (0, o * nc + c)),
                  pl.BlockSpec(g_real.shape, lambda o, c: (0, 0, 0), pipeline_mode=once),
                  pl.BlockSpec(f2_fwd.shape, lambda o, c: (0, 0), pipeline_mode=once)],
        out_specs=pl.BlockSpec((1, FFT_N1, 2 * FFT_N2, ct), lambda o, c: (o, 0, 0, c)),
        out_shape=jax.ShapeDtypeStruct((n_ord, FFT_N1, 2 * FFT_N2, C), BF16),
        scratch_shapes=[pltpu.VMEM((FFT_N2 * 2 * FFT_N1, ct), F32)],
        compiler_params=_cparams(("parallel", "parallel")),
        name="filter_fft",
    )(taps, taps, inv_norm, g_real, f2_fwd)


def _hyena_conv_kernel(z_ref, gate_ref, hb_ref, spec_ref, gf_ref, gi_ref, f2f_ref, f2i_ref, o_ref,
                       stage_ref):
    half = FFT_N1 // 2
    slab = 2 * FFT_N1
    for n2 in range(FFT_N2):
        rows = pl.ds(n2, half, stride=FFT_N2)
        x = jnp.concatenate([z_ref[0, rows, :], z_ref[1, rows, :]], axis=0).astype(BF16)
        stage_ref[n2 * slab:(n2 + 1) * slab, :] = jnp.dot(gf_ref[n2], x, preferred_element_type=F32)
    f2f = f2f_ref[...]
    f2i = f2i_ref[...]
    for k1 in range(FFT_N1):
        zf = _fft_fast_stage(stage_ref, k1, f2f)
        zr, zi = zf[:FFT_N2], zf[FFT_N2:]
        hr = spec_ref[0, k1, :FFT_N2, :].astype(F32)
        hi = spec_ref[0, k1, FFT_N2:, :].astype(F32)
        p = jnp.concatenate([zr * hr - zi * hi, zr * hi + zi * hr], axis=0).astype(BF16)
        q = jnp.dot(f2i, p, preferred_element_type=F32)
        stage_ref[pl.ds(k1, FFT_N2, stride=slab), :] = q[:FFT_N2]
        stage_ref[pl.ds(FFT_N1 + k1, FFT_N2, stride=slab), :] = q[FFT_N2:]
    hb = hb_ref[...]
    for n2 in range(FFT_N2):
        y_in = stage_ref[n2 * slab:(n2 + 1) * slab, :].astype(BF16)
        y = jnp.dot(gi_ref[n2], y_in, preferred_element_type=F32)
        rows = pl.ds(n2, half, stride=FFT_N2)
        for m in range(2):
            zm = z_ref[m, rows, :]
            o_ref[m, rows, :] = gate_ref[m, rows, :] * (y[m * half:(m + 1) * half] + zm * hb)


def _hyena_conv(z, z_blk, gate, gate_blk, hbias, spec, order, tabs, ct=LANES):
    B, L, _ = z.shape
    C = D_CH
    g_fwd, _, g_inv, f2_fwd, f2_inv = tabs
    once = pl.Buffered(1)
    const3 = lambda a: pl.BlockSpec(a.shape, lambda c, p: (0, 0, 0), pipeline_mode=once)
    const2 = lambda a: pl.BlockSpec(a.shape, lambda c, p: (0, 0), pipeline_mode=once)
    row_spec = pl.BlockSpec((2, L, ct), lambda c, p: (p, 0, c))
    return pl.pallas_call(
        _hyena_conv_kernel,
        grid=(C // ct, B // 2),
        in_specs=[pl.BlockSpec((2, L, ct), lambda c, p: (p, 0, z_blk + c)),
                  pl.BlockSpec((2, L, ct), lambda c, p: (p, 0, gate_blk + c)),
                  pl.BlockSpec((1, ct), lambda c, p: (0, c)),
                  pl.BlockSpec((1, FFT_N1, 2 * FFT_N2, ct), lambda c, p: (order, 0, 0, c),
                               pipeline_mode=once),
                  const3(g_fwd), const3(g_inv), const2(f2_fwd), const2(f2_inv)],
        out_specs=row_spec,
        out_shape=jax.ShapeDtypeStruct((B, L, C), F32),
        scratch_shapes=[pltpu.VMEM((FFT_N2 * 2 * FFT_N1, ct), F32)],
        compiler_params=_cparams(("parallel", "arbitrary")),
        name="hyena_conv",
    )(z, gate, hbias, spec, g_fwd, g_inv, f2_fwd, f2_inv)


def _hyena_filters(L, w1, b1, f1, w2, b2, f2, w3, b3):
    t = jnp.arange(L, dtype=F32)
    tn = t / max(L - 1, 1)
    bands = jnp.linspace(1e-4, HY_BANDS - 1, HY_BANDS, dtype=F32)
    ang = 2.0 * math.pi * t[:, None] * bands[None] / L
    feats = jnp.concatenate([tn[:, None], jnp.cos(ang), jnp.sin(ang)], axis=-1)
    h = jnp.sin(f1 * (feats @ w1 + b1))
    h = jnp.sin(f2 * (h @ w2 + b2))
    deltas = jnp.abs(jnp.linspace(HY_MIN_DECAY, HY_MAX_DECAY, D_CH, dtype=F32))
    decay = jnp.exp(-tn[:, None] * deltas[None])
    n_rep = w3.shape[1] // D_CH
    taps = (_mm_f32(h, w3, 512) + b3) * jnp.tile(decay, (1, n_rep))
    l1 = jnp.sum(jnp.abs(taps), axis=0)
    l1 = l1[:n_rep // 2 * D_CH] + l1[n_rep // 2 * D_CH:]
    return taps, (1.0 / (l1 + EPS))[None]


def _short_conv(u, w, b):
    up = jnp.pad(u, ((0, 0), (1, 1), (0, 0)))
    return up[:, :-2] * w[0] + up[:, 1:-1] * w[1] + up[:, 2:] * w[2] + b


def _final_norm_kernel(x_ref, g_ref, o_ref):
    x = x_ref[0]
    ms = jnp.mean(x * x, axis=-1, keepdims=True)
    o_ref[0] = x * lax.rsqrt(ms + EPS) * g_ref[...]


def _final_norm(x, g, tm=1024):
    B, S, D = x.shape
    return pl.pallas_call(
        _final_norm_kernel,
        grid=(B, S // tm),
        in_specs=[pl.BlockSpec((1, tm, D), lambda b, i: (b, i, 0)),
                  pl.BlockSpec((1, D), lambda b, i: (0, 0))],
        out_specs=pl.BlockSpec((1, tm, D), lambda b, i: (b, i, 0)),
        out_shape=jax.ShapeDtypeStruct((B, S, D), F32),
        compiler_params=_cparams(("parallel", "parallel")),
        name="final_norm",
    )(x, g.reshape(1, D))


def _dup_heads(w):
    a, b = w[:, :HEAD_DIM], w[:, HEAD_DIM:]
    return jnp.concatenate([a, a, b, b], axis=1)


def _rope_tables(S):
    t = jnp.arange(S)
    row = (t // GRID_W).astype(F32)
    col = (t % GRID_W).astype(F32)
    half = HEAD_DIM // 2
    inv = ROPE_THETA ** (-jnp.arange(0, half, 2, dtype=F32) / half)
    ar = row[:, None] * inv[None]
    ac = col[:, None] * inv[None]
    cos = jnp.concatenate([jnp.cos(ar), jnp.cos(ar), jnp.cos(ac), jnp.cos(ac)], axis=-1)
    sin = jnp.concatenate([-jnp.sin(ar), jnp.sin(ar), -jnp.sin(ac), jnp.sin(ac)], axis=-1)
    return jnp.tile(cos, (1, 2)), jnp.tile(sin, (1, 2))


def _head_mean_matrix(width):
    blk = np.kron(np.eye(width // HEAD_DIM), np.full((HEAD_DIM, HEAD_DIM), 1.0 / HEAD_DIM))
    return jnp.asarray(blk, dtype=F32).astype(BF16)


def kernel(x, c, ctx, c_ctx, w_ada, b_ada, norm_g, final_g, w_in_even, w_out_even, a_sink, b_rpb, w_in_odd, w_out_odd, c_qnorm, c_knorm, hy_short_w, hy_short_b, hy_w1, hy_b1, hy_f1, hy_w2, hy_b2, hy_f2, hy_w3, hy_b3, hy_bias, w_router, b_router, moe_wg, moe_wu, moe_wd):
    B, S, D = x.shape
    depth = w_ada.shape[0]
    rope = _rope_tables(S)
    wr_pad = jnp.pad(w_router.astype(F32), ((0, 0), (0, LANES - N_EXPERTS)))
    wr_hi = wr_pad.astype(BF16)
    router = (b_router.astype(F32), wr_hi, (wr_pad - wr_hi.astype(F32)).astype(BF16))

    mod_in = jnp.concatenate([jax.nn.silu(c), jax.nn.silu(c_ctx)[None],
                              jnp.zeros((8 - B - 1, D), F32)], axis=0)
    xc = ctx
    for l in range(depth):
        need_ctx = l < depth - 1
        mod = _mm_f32(mod_in, w_ada[l], 1536) + b_ada[l]
        mx = mod[:B].reshape(B, 6, D)
        mc = mod[B].reshape(6, D)
        i = l // 2
        if l % 2 == 0:
            w = w_in_even[i].astype(BF16)
            w_all = jnp.concatenate([w[:, :512], _dup_heads(w[:, 512:640]), _dup_heads(w[:, 640:768]),
                                     w[:, 768:]], axis=1)
            segs_x = ((512, "rope", ATTN_SCALE), (256, "rope", 1.0), (256, "plain", 1.0),
                      (512, "plain", ATTN_SCALE), (512, "plain", 1.0), (512, "plain", 1.0))
            aq, akd, avd, bq, bk, bv = _norm_proj(x, norm_g[l, 0], mx[:, 0], mx[:, 1], w_all, segs_x,
                                                  rope=rope)
            segs_c = tuple((wd, "plain", m) for wd, _, m in segs_x)
            caq, cakd, cavd, cbq, cbk, cbv = _norm_proj(xc, norm_g[l, 0], mc[0:1], mc[1:2], w_all, segs_c)
            ya = _window_attn(aq, akd, avd, cakd, cavd, a_sink[i].astype(F32))
            yb = _nbr_attn(bq, bk, bv, cbk, cbv, _nbr_bias_table(b_rpb[i], S // GRID_W))
            ys = [ya, yb]
            w_out = w_out_even[i].astype(BF16)
            if need_ctx:
                yc = [_ctx_attn(a_sink[i].astype(F32), caq, cakd, cavd, cbq, cbk, cbv)]
        else:
            w = w_in_odd[i].astype(BF16)
            w_all = jnp.concatenate([w[:, :512], _dup_heads(w[:, 512:640]), _dup_heads(w[:, 640:768]),
                                     w[:, 768:]], axis=1)
            gains = jnp.concatenate([jnp.tile(c_qnorm[i], 8), jnp.tile(c_knorm[i], 4)])[None].astype(F32)
            norm = (_head_mean_matrix(512), gains)
            q_mult = ATTN_SCALE * math.log2(math.e)
            segs_x = ((512, "normrope", q_mult), (256, "normrope", 1.0), (256, "plain", 1.0),
                      (3 * D_CH, "plain", 1.0))
            qx, kxd, vxd, ux = _norm_proj(x, norm_g[l, 0], mx[:, 0], mx[:, 1], w_all, segs_x,
                                          rope=rope, norm=norm)
            w_c = w_all[:, 512:1024]
            norm_c = (_head_mean_matrix(512), jnp.tile(c_knorm[i], 4)[None].astype(F32))
            kcd, vcd = _norm_proj(xc, norm_g[l, 0], mc[0:1], mc[1:2], w_c,
                                  ((256, "norm", 1.0), (256, "plain", 1.0)), norm=norm_c)
            logit_bound = (1.02 * HEAD_DIM * q_mult * jnp.max(jnp.abs(c_qnorm[i]))
                           * jnp.max(jnp.abs(c_knorm[i])))
            y_attn = lax.cond(logit_bound <= FULL_NOMAX_LOG2_BOUND,
                              lambda *a: _full_attn(*a, bounded=True),
                              lambda *a: _full_attn(*a, bounded=False),
                              qx, kxd, vxd, kcd, vcd)
            u = _short_conv(ux.astype(F32), hy_short_w[i], hy_short_b[i])
            tabs = _dft_tables()
            taps, inv_norm = _hyena_filters(S, hy_w1[i], hy_b1[i], hy_f1[i], hy_w2[i], hy_b2[i],
                                            hy_f2[i], hy_w3[i], hy_b3[i])
            spec = _filter_fft(taps, inv_norm, tabs[1], tabs[3])
            blocks = D_CH // LANES
            z = _hyena_conv(u, 0, u, blocks, hy_bias[i, 0:1], spec, 0, tabs)
            z = _hyena_conv(z, 0, u, 2 * blocks, hy_bias[i, 1:2], spec, 1, tabs)
            ys = [y_attn, z.astype(BF16)]
            w_out = w_out_odd[i].astype(BF16)
            if need_ctx:
                raise NotImplementedError("context update of an odd layer is not needed at this depth")

        x, h3, route, counts = _out_proj(ys, w_out, x, mx[:, 2], norm_g[l, 1], mx[:, 3], mx[:, 4], router,
                                         sparse=True)
        wg, wu, wd = moe_wg[l].astype(BF16), moe_wu[l].astype(BF16), moe_wd[l].astype(BF16)
        if need_ctx:
            xc, hc, comb_c = _out_proj(yc, w_out, xc, mc[2:3], norm_g[l, 1], mc[3:4], mc[4:5], router,
                                       sparse=False)
            xc = _moe(hc, comb_c, wg, wu, wd, xc, mc[5:6])
        plan = _moe_plan(route, counts, B * S)
        x = _moe_combine(_moe_grouped(h3, plan, wg, wu, wd), plan, route, x, mx[:, 5])
    return _final_norm(x, final_g)
```

```python
import functools
import math

import numpy as np
import jax
import jax.numpy as jnp
from jax import lax
from jax.experimental import pallas as pl
from jax.experimental.pallas import tpu as pltpu

F32 = jnp.float32
BF16 = jnp.bfloat16

D_MODEL = 1024
GRID_W = 64
CTX_LEN = 256
HEAD_DIM = 64
ROPE_THETA = 10000.0
EPS = 1e-6
ATTN_SCALE = HEAD_DIM ** -0.5
A_WINDOW = 128
A_BLOCK = 128
B_WIN_H = 8
B_WIN_W = 16
D_CH = 512
HY_BANDS = 16
HY_MAX_DECAY = math.log(1e-2) / 0.3
HY_MIN_DECAY = math.log(1e-2) / 1.5
N_EXPERTS = 16
N_GROUPS = 4
EXPERTS_PER_GROUP = N_EXPERTS // N_GROUPS
TOP_K = 2
D_EXPERT = 512

LANES = 128
NEG = -1e30
VMEM_LIMIT = 48 * 1024 * 1024

FFT_N1 = 64
FFT_N2 = 128


def _cparams(sem):
    return pltpu.CompilerParams(dimension_semantics=sem, vmem_limit_bytes=VMEM_LIMIT)


def _mm_f32_kernel(x_ref, w_ref, o_ref):
    o_ref[...] = jnp.dot(x_ref[...], w_ref[...], preferred_element_type=F32)


def _mm_f32(x, w, tn):
    M, K = x.shape
    N = w.shape[1]
    return pl.pallas_call(
        _mm_f32_kernel,
        grid=(N // tn,),
        in_specs=[pl.BlockSpec((M, K), lambda j: (0, 0)),
                  pl.BlockSpec((K, tn), lambda j: (0, j))],
        out_specs=pl.BlockSpec((M, tn), lambda j: (0, j)),
        out_shape=jax.ShapeDtypeStruct((M, N), F32),
        compiler_params=_cparams(("arbitrary",)),
        name="mm_f32",
    )(x, w)


def _swap16(y):
    n = y.shape[-1]
    lane = lax.broadcasted_iota(jnp.int32, y.shape, y.ndim - 1)
    up = pltpu.roll(y, n - 16, axis=y.ndim - 1)
    dn = pltpu.roll(y, 16, axis=y.ndim - 1)
    return jnp.where((lane % 32) < 16, up, dn)


def _tile_lanes(t, width):
    reps = width // t.shape[-1]
    return t if reps == 1 else jnp.concatenate([t] * reps, axis=-1)


def _norm_proj_kernel(segs, has_rope, has_norm, *refs):
    it = iter(refs)
    x_ref, g_ref, shift_ref, scale_ref, w_ref = (next(it) for _ in range(5))
    cos_ref = sin_ref = bd_ref = gain_ref = None
    if has_rope:
        cos_ref, sin_ref = next(it), next(it)
    if has_norm:
        bd_ref, gain_ref = next(it), next(it)
    out_refs = list(it)

    x = x_ref[0]
    ms = jnp.mean(x * x, axis=-1, keepdims=True)
    h = x * lax.rsqrt(ms + EPS) * g_ref[...]
    h = h * (1.0 + scale_ref[0]) + shift_ref[0]
    y = jnp.dot(h.astype(BF16), w_ref[...], preferred_element_type=F32)

    off = 0
    goff = 0
    for (width, kind, mult), o_ref in zip(segs, out_refs):
        ys = y[:, off:off + width]
        if kind in ("norm", "normrope"):
            bd = bd_ref[...][:width, :width]
            hms = jnp.dot((ys * ys).astype(BF16), bd, preferred_element_type=F32)
            ys = ys * lax.rsqrt(hms + EPS) * gain_ref[:, goff:goff + width]
            goff += width
        if kind in ("rope", "normrope"):
            c = _tile_lanes(cos_ref[...], width)
            s = _tile_lanes(sin_ref[...], width)
            ys = ys * c + _swap16(ys) * s
        if mult != 1.0:
            ys = ys * mult
        o_ref[0] = ys.astype(o_ref.dtype)
        off += width


def _norm_proj(x, g, shift, scale, w, segs, rope=None, norm=None, tm=512):
    B, S, D = x.shape
    N = w.shape[1]
    tm = min(tm, S)
    bm = shift.shape[0]
    mod_map = (lambda b, i: (b, 0, 0)) if bm > 1 else (lambda b, i: (0, 0, 0))
    args = [x, g.reshape(1, D), shift.reshape(bm, 1, D), scale.reshape(bm, 1, D), w]
    in_specs = [pl.BlockSpec((1, tm, D), lambda b, i: (b, i, 0)),
                pl.BlockSpec((1, D), lambda b, i: (0, 0)),
                pl.BlockSpec((1, 1, D), mod_map),
                pl.BlockSpec((1, 1, D), mod_map),
                pl.BlockSpec((D, N), lambda b, i: (0, 0))]
    if rope is not None:
        args += [rope[0], rope[1]]
        in_specs += [pl.BlockSpec((tm, LANES), lambda b, i: (i, 0))] * 2
    if norm is not None:
        args += [norm[0], norm[1]]
        in_specs += [pl.BlockSpec(norm[0].shape, lambda b, i: (0, 0)),
                     pl.BlockSpec(norm[1].shape, lambda b, i: (0, 0))]
    out_shape = [jax.ShapeDtypeStruct((B, S, wd), BF16) for wd, _, _ in segs]
    out_specs = [pl.BlockSpec((1, tm, wd), lambda b, i: (b, i, 0)) for wd, _, _ in segs]
    return pl.pallas_call(
        functools.partial(_norm_proj_kernel, segs, rope is not None, norm is not None),
        grid=(B, S // tm),
        in_specs=in_specs,
        out_specs=out_specs,
        out_shape=out_shape,
        compiler_params=_cparams(("parallel", "parallel")),
        name="norm_proj",
    )(*args)


def _half_mask(shape):
    return lax.broadcasted_iota(jnp.int32, shape, len(shape) - 1) < HEAD_DIM


def _stack_halves(qp):
    lo = _half_mask(qp.shape)
    zero = jnp.zeros_like(qp)
    return jnp.concatenate([jnp.where(lo, qp, zero), jnp.where(lo, zero, qp)], axis=0)


def _merge_halves(o, m):
    return jnp.where(_half_mask((m, LANES)), o[:m], o[m:])


def _scores(q, k):
    return lax.dot_general(q, k, (((1,), (1,)), ((), ())), preferred_element_type=F32)


def _joint_softmax_pv(score_parts, value_parts, extra_logit=None):
    m = functools.reduce(jnp.maximum, [jnp.max(s, axis=-1, keepdims=True) for s in score_parts])
    if extra_logit is not None:
        m = jnp.maximum(m, extra_logit)
    den = jnp.exp(extra_logit - m) if extra_logit is not None else 0.0
    acc = None
    for s, v in zip(score_parts, value_parts):
        p = jnp.exp(s - m)
        den = den + jnp.sum(p, axis=-1, keepdims=True)
        pv = jnp.dot(p.astype(BF16), v, preferred_element_type=F32)
        acc = pv if acc is None else acc + pv
    return acc / den


def _sink_column(sink_ref, first_head, n_heads, rows_per_head):
    rows = lax.broadcasted_iota(jnp.int32, (n_heads * rows_per_head, 1), 0)
    col = jnp.zeros((n_heads * rows_per_head, 1), F32)
    for j in range(n_heads):
        in_head = (rows >= j * rows_per_head) & (rows < (j + 1) * rows_per_head)
        col = jnp.where(in_head, sink_ref[first_head + j], col)
    return col


def _window_attn_kernel(seq_len, sink_ref, q_ref, kp_ref, kc_ref, kn_ref, vp_ref, vc_ref, vn_ref,
                        ck_ref, cv_ref, o_ref):
    i = pl.program_id(1)
    blk = A_BLOCK
    q = q_ref[0]
    rows = lax.broadcasted_iota(jnp.int32, (4 * blk, 3 * blk), 0) % blk
    rel = lax.broadcasted_iota(jnp.int32, (4 * blk, 3 * blk), 1) - blk
    gpos = i * blk + rel
    valid = (jnp.abs(rows - rel) <= A_WINDOW) & (gpos >= 0) & (gpos < seq_len)
    outs = []
    for g in range(2):
        ls = slice(g * LANES, (g + 1) * LANES)
        k_loc = jnp.concatenate([kp_ref[0][:, ls], kc_ref[0][:, ls], kn_ref[0][:, ls]], axis=0)
        v_loc = jnp.concatenate([vp_ref[0][:, ls], vc_ref[0][:, ls], vn_ref[0][:, ls]], axis=0)
        qs = jnp.concatenate([_stack_halves(q[:, (2 * g + j) * LANES:(2 * g + j + 1) * LANES])
                              for j in range(2)], axis=0)
        s_loc = jnp.where(valid, _scores(qs, k_loc), NEG)
        s_ctx = _scores(qs, ck_ref[0][:, ls])
        sink = _sink_column(sink_ref, 4 * g, 4, blk)
        o = _joint_softmax_pv([s_loc, s_ctx], [v_loc, cv_ref[0][:, ls]], sink)
        outs += [_merge_halves(o[:2 * blk], blk), _merge_halves(o[2 * blk:], blk)]
    o_ref[0] = jnp.concatenate(outs, axis=-1).astype(o_ref.dtype)


def _window_attn(q, kd, vd, ckd, cvd, sink):
    B, S, _ = q.shape
    nb = S // A_BLOCK
    kv_spec = lambda f: pl.BlockSpec((1, A_BLOCK, 2 * LANES), f)
    prev_map = lambda b, i: (b, jnp.maximum(i - 1, 0), 0)
    cur_map = lambda b, i: (b, i, 0)
    next_map = lambda b, i: (b, jnp.minimum(i + 1, nb - 1), 0)
    ctx_spec = pl.BlockSpec((1, CTX_LEN, 2 * LANES), lambda b, i: (b, 0, 0))
    return pl.pallas_call(
        functools.partial(_window_attn_kernel, S),
        grid=(B, nb),
        in_specs=[pl.BlockSpec(memory_space=pltpu.SMEM),
                  pl.BlockSpec((1, A_BLOCK, 4 * LANES), cur_map),
                  kv_spec(prev_map), kv_spec(cur_map), kv_spec(next_map),
                  kv_spec(prev_map), kv_spec(cur_map), kv_spec(next_map),
                  ctx_spec, ctx_spec],
        out_specs=pl.BlockSpec((1, A_BLOCK, 4 * LANES), cur_map),
        out_shape=jax.ShapeDtypeStruct((B, S, 4 * LANES), BF16),
        compiler_params=_cparams(("parallel", "parallel")),
        name="window_attn",
    )(sink, q, kd, kd, kd, vd, vd, vd, ckd, cvd)


NBR_ROWS = 4
NBR_KROWS = 12


def _nbr_start_row(i, n_rows):
    return jnp.clip(i * NBR_ROWS - B_WIN_H // 2, 0, n_rows - NBR_KROWS)


def _nbr_attn_kernel(n_rows, q_ref, k_ref, v_ref, ck_ref, cv_ref, bias_ref, o_ref):
    i = pl.program_id(2)
    nq = NBR_ROWS * GRID_W
    nk = NBR_KROWS * GRID_W
    start = pl.multiple_of(_nbr_start_row(i, n_rows) * GRID_W, GRID_W)
    k_loc = k_ref[0, pl.ds(start, nk), :]
    v_loc = v_ref[0, pl.ds(start, nk), :]
    qs = _stack_halves(q_ref[0])
    s_loc = _scores(qs, k_loc) + bias_ref[0].reshape(2 * nq, nk)
    s_ctx = _scores(qs, ck_ref[0])
    o = _joint_softmax_pv([s_loc, s_ctx], [v_loc, cv_ref[0]])
    o_ref[0] = _merge_halves(o, nq).astype(o_ref.dtype)


def _nbr_bias_table(rpb, n_rows):
    kh = B_WIN_H
    n_heads = rpb.shape[0]
    col = np.arange(GRID_W)
    cs = np.clip(col - B_WIN_W // 2, 0, GRID_W - B_WIN_W)
    col_ok = (col[None, :] >= cs[:, None]) & (col[None, :] < cs[:, None] + B_WIN_W)
    dc = np.clip(col[None, :] - col[:, None], -(B_WIN_W - 1), B_WIN_W - 1) + B_WIN_W - 1
    pick = (dc[..., None] == np.arange(2 * B_WIN_W - 1)).astype(np.float32)
    by_col = jnp.einsum("hdc,qkc->hdqk", rpb.astype(F32), pick, precision=lax.Precision.HIGHEST)
    by_col = jnp.where(col_ok[None, None], by_col, NEG)
    masked = jnp.full((n_heads, GRID_W, GRID_W), NEG, F32)
    tabs = []
    for r0 in (0, NBR_ROWS, n_rows - NBR_ROWS):
        start = int(np.clip(r0 - kh // 2, 0, n_rows - NBR_KROWS))
        per_row = []
        for ri in range(NBR_ROWS):
            r = r0 + ri
            rs = int(np.clip(r - kh // 2, 0, n_rows - kh))
            slabs = [by_col[:, start + kri - r + kh - 1] if rs <= start + kri < rs + kh else masked
                     for kri in range(NBR_KROWS)]
            per_row.append(jnp.stack(slabs, axis=2))
        tabs.append(jnp.stack(per_row, axis=1).reshape(n_heads, NBR_ROWS * GRID_W, NBR_KROWS * GRID_W))
    return jnp.stack(tabs)


def _nbr_attn(q, k, v, ck, cv, bias):
    B, S, _ = q.shape
    n_rows = S // GRID_W
    nsteps = n_rows // NBR_ROWS
    nq = NBR_ROWS * GRID_W
    nk = NBR_KROWS * GRID_W
    pat = lambda i: jnp.where(i == 0, 0, jnp.where(i == nsteps - 1, 2, 1))
    return pl.pallas_call(
        functools.partial(_nbr_attn_kernel, n_rows),
        grid=(B, 4, nsteps),
        in_specs=[pl.BlockSpec((1, nq, LANES), lambda b, p, i: (b, i, p)),
                  pl.BlockSpec((1, S, LANES), lambda b, p, i: (b, 0, p)),
                  pl.BlockSpec((1, S, LANES), lambda b, p, i: (b, 0, p)),
                  pl.BlockSpec((1, CTX_LEN, LANES), lambda b, p, i: (b, 0, p)),
                  pl.BlockSpec((1, CTX_LEN, LANES), lambda b, p, i: (b, 0, p)),
                  pl.BlockSpec((1, 2, nq, nk), lambda b, p, i: (pat(i), p, 0, 0))],
        out_specs=pl.BlockSpec((1, nq, LANES), lambda b, p, i: (b, i, p)),
        out_shape=jax.ShapeDtypeStruct((B, S, 4 * LANES), BF16),
        compiler_params=_cparams(("parallel", "parallel", "arbitrary")),
        name="nbr_attn",
    )(q, k, v, ck, cv, bias)


FULL_TQ = 256
FULL_TK = 512
FULL_NOMAX_LOG2_BOUND = 60.0


def _full_attn_kernel(bounded, q_ref, k_ref, v_ref, ck_ref, cv_ref, o_ref):
    tq = FULL_TQ
    q = q_ref[0]
    qs = jnp.concatenate([_stack_halves(q[:, :LANES]), _stack_halves(q[:, LANES:])], axis=0)

    def step_bounded(acc, k, v):
        v_ones = jnp.where(_half_mask(v.shape), v, jnp.ones_like(v))
        p = jnp.exp2(_scores(qs, k))
        return acc + jnp.dot(p.astype(BF16), v_ones, preferred_element_type=F32)

    def step_online(carry, k, v):
        m, l, acc = carry
        s = _scores(qs, k)
        m_new = jnp.maximum(m, jnp.max(s, axis=-1, keepdims=True))
        alpha = jnp.exp2(m - m_new)
        p = jnp.exp2(s - m_new)
        l = l * alpha + jnp.sum(p, axis=-1, keepdims=True)
        acc = acc * alpha + jnp.dot(p.astype(BF16), v, preferred_element_type=F32)
        return m_new, l, acc

    step = step_bounded if bounded else step_online
    init = jnp.zeros((4 * tq, LANES), F32)
    if not bounded:
        init = (jnp.full((4 * tq, 1), NEG, F32), jnp.zeros((4 * tq, 1), F32), init)
    carry = step(init, ck_ref[0], cv_ref[0])

    def body(j, carry):
        off = pl.multiple_of(j * FULL_TK, FULL_TK)
        return step(carry, k_ref[0, pl.ds(off, FULL_TK), :], v_ref[0, pl.ds(off, FULL_TK), :])

    carry = lax.fori_loop(0, k_ref.shape[1] // FULL_TK, body, carry)
    if bounded:
        o = carry * pltpu.roll(1.0 / carry, HEAD_DIM, axis=1)
        merge = lambda a, b: jnp.where(_half_mask((tq, LANES)), a, pltpu.roll(b, HEAD_DIM, axis=1))
        pairs = [merge(o[0:tq], o[tq:2 * tq]), merge(o[2 * tq:3 * tq], o[3 * tq:])]
    else:
        o = carry[2] / carry[1]
        pairs = [_merge_halves(o[:2 * tq], tq), _merge_halves(o[2 * tq:], tq)]
    o_ref[0] = jnp.concatenate(pairs, axis=-1).astype(o_ref.dtype)


def _full_attn(q, kd, vd, ckd, cvd, bounded):
    B, S, _ = q.shape
    return pl.pallas_call(
        functools.partial(_full_attn_kernel, bounded),
        grid=(B, 2, S // FULL_TQ),
        in_specs=[pl.BlockSpec((1, FULL_TQ, 2 * LANES), lambda b, g, i: (b, i, g)),
                  pl.BlockSpec((1, S, LANES), lambda b, g, i: (b, 0, g)),
                  pl.BlockSpec((1, S, LANES), lambda b, g, i: (b, 0, g)),
                  pl.BlockSpec((1, CTX_LEN, LANES), lambda b, g, i: (b, 0, g)),
                  pl.BlockSpec((1, CTX_LEN, LANES), lambda b, g, i: (b, 0, g))],
        out_specs=pl.BlockSpec((1, FULL_TQ, 2 * LANES), lambda b, g, i: (b, i, g)),
        out_shape=jax.ShapeDtypeStruct((B, S, 4 * LANES), BF16),
        compiler_params=_cparams(("parallel", "parallel", "arbitrary")),
        name="full_attn_bounded" if bounded else "full_attn_online",
    )(q, kd, vd, ckd, cvd)


def _ctx_attn_kernel(sink_ref, aq_ref, akd_ref, avd_ref, bq_ref, bk_ref, bv_ref, o_ref):
    n = CTX_LEN
    aq = aq_ref[0]
    bq = bq_ref[0]
    outs = []
    for g in range(2):
        ls = slice(g * LANES, (g + 1) * LANES)
        qs = jnp.concatenate([_stack_halves(aq[:, (2 * g + j) * LANES:(2 * g + j + 1) * LANES])
                              for j in range(2)], axis=0)
        sink = _sink_column(sink_ref, 4 * g, 4, n)
        o = _joint_softmax_pv([_scores(qs, akd_ref[0][:, ls])], [avd_ref[0][:, ls]], sink)
        outs += [_merge_halves(o[:2 * n], n), _merge_halves(o[2 * n:], n)]
    for p in range(4):
        ls = slice(p * LANES, (p + 1) * LANES)
        qs = _stack_halves(bq[:, ls])
        o = _joint_softmax_pv([_scores(qs, bk_ref[0][:, ls])], [bv_ref[0][:, ls]])
        outs.append(_merge_halves(o, n))
    o_ref[0] = jnp.concatenate(outs, axis=-1).astype(o_ref.dtype)


def _ctx_attn(sink, aq, akd, avd, bq, bk, bv):
    B = aq.shape[0]
    spec = lambda a: pl.BlockSpec((1,) + a.shape[1:], lambda b: (b, 0, 0))
    args = (aq, akd, avd, bq, bk, bv)
    return pl.pallas_call(
        _ctx_attn_kernel,
        grid=(B,),
        in_specs=[pl.BlockSpec(memory_space=pltpu.SMEM)] + [spec(a) for a in args],
        out_specs=pl.BlockSpec((1, CTX_LEN, 8 * LANES), lambda b: (b, 0, 0)),
        out_shape=jax.ShapeDtypeStruct((B, CTX_LEN, 8 * LANES), BF16),
        compiler_params=_cparams(("parallel",)),
        name="ctx_attn",
    )(sink, *args)


def _pick4(idx, vals):
    return jnp.where(idx == 0, vals[0], jnp.where(idx == 1, vals[1], jnp.where(idx == 2, vals[2], vals[3])))


def _route_rows(lg_t, b_ref):
    n_tok = lg_t.shape[1]
    s = [jax.nn.sigmoid(lg_t[e:e + 1, :]) for e in range(N_EXPERTS)]
    sel = [s[e] + b_ref[e] for e in range(N_EXPERTS)]
    n = EXPERTS_PER_GROUP
    gscore = []
    for j in range(N_GROUPS):
        v = sel[n * j:n * (j + 1)]
        pair_sums = [v[a] + v[b] for a in range(n) for b in range(a + 1, n)]
        gscore.append(functools.reduce(jnp.maximum, pair_sums))
    best, gbest = gscore[0], jnp.zeros((1, n_tok), jnp.int32)
    for j in range(1, N_GROUPS):
        upd = gscore[j] > best
        best = jnp.where(upd, gscore[j], best)
        gbest = jnp.where(upd, j, gbest)
    v = [_pick4(gbest, [sel[n * j + i] for j in range(N_GROUPS)]) for i in range(n)]
    u = [_pick4(gbest, [s[n * j + i] for j in range(N_GROUPS)]) for i in range(n)]
    m1, i1 = v[0], jnp.zeros((1, n_tok), jnp.int32)
    for i in range(1, n):
        upd = v[i] > m1
        m1 = jnp.where(upd, v[i], m1)
        i1 = jnp.where(upd, i, i1)
    m2, i2 = jnp.full((1, n_tok), -jnp.inf, F32), jnp.zeros((1, n_tok), jnp.int32)
    for i in range(n):
        upd = (i1 != i) & (v[i] > m2)
        m2 = jnp.where(upd, v[i], m2)
        i2 = jnp.where(upd, i, i2)
    u1, u2 = _pick4(i1, u), _pick4(i2, u)
    tot = u1 + u2
    return n * gbest + i1, n * gbest + i2, u1 / tot, u2 / tot


def _out_proj_kernel(n_y, sparse, *refs):
    y_refs = refs[:n_y]
    br_ref, w_ref, x_ref, gate_ref, g_ref, shift_ref, scale_ref, wrh_ref, wrl_ref = refs[n_y:n_y + 9]
    n_in = n_y + 9
    if sparse:
        tri_ref, tril_ref = refs[n_in:n_in + 2]
        n_in += 2
    outs = refs[n_in:]
    xo_ref = outs[0]
    off = 0
    acc = None
    for y_ref in y_refs:
        wdt = y_ref.shape[-1]
        part = jnp.dot(y_ref[0], w_ref[off:off + wdt, :], preferred_element_type=F32)
        acc = part if acc is None else acc + part
        off += wdt
    x = x_ref[0] + gate_ref[0] * acc
    xo_ref[0] = x
    ms = jnp.mean(x * x, axis=-1, keepdims=True)
    h = x * lax.rsqrt(ms + EPS) * g_ref[...]
    h = h * (1.0 + scale_ref[0]) + shift_ref[0]
    hh = h.astype(BF16)
    hl = (h - hh.astype(F32)).astype(BF16)
    lg = (jnp.dot(hh, wrh_ref[...], preferred_element_type=F32)
          + jnp.dot(hl, wrh_ref[...], preferred_element_type=F32)
          + jnp.dot(hh, wrl_ref[...], preferred_element_type=F32))
    lg_t = lg.T[:N_EXPERTS]
    e1, e2, w1, w2 = _route_rows(lg_t, br_ref)
    rows = lax.broadcasted_iota(jnp.int32, lg_t.shape, 0)
    if not sparse:
        h_ref, comb_ref = outs[1:]
        h_ref[0] = hh
        comb_t = jnp.where(rows == e1, w1, 0.0) + jnp.where(rows == e2, w2, 0.0)
        comb_ref[0] = jnp.concatenate(
            [comb_t, jnp.zeros((LANES - N_EXPERTS, comb_t.shape[1]), F32)], axis=0).T
        return

    h_ref, route_t_ref, route_c_ref, ng_ref = outs[1:]
    h_ref[0] = hh
    member = jnp.where((rows == e1) | (rows == e2), 1.0, 0.0)
    before = jnp.dot(member.astype(BF16), tri_ref[...], preferred_element_type=F32)
    groups = jnp.floor((jnp.sum(member, axis=1, keepdims=True) + (MOE_G - 1)) * (1.0 / MOE_G))
    groups = jnp.broadcast_to(groups, (N_EXPERTS, LANES))
    run_start = MOE_G * jnp.dot(tril_ref[...], groups.astype(BF16), preferred_element_type=F32)[:, 0:1]
    pos = run_start + before
    p1 = jnp.sum(jnp.where(rows == e1, pos, 0.0), axis=0, keepdims=True)
    p2 = jnp.sum(jnp.where(rows == e2, pos, 0.0), axis=0, keepdims=True)
    ng_ref[0] = groups
    field = lax.broadcasted_iota(jnp.int32, (LANES, h.shape[0]), 0)
    route = jnp.zeros((LANES, h.shape[0]), F32)
    for k, v in enumerate((p1, p2, w1, w2)):
        route = jnp.where(field == k, v, route)
    route_t_ref[...] = route[:ROUTE_FIELDS]
    route_c_ref[...] = route.T


ROUTE_FIELDS = 8
MOE_G = 8
MOE_TT = 512
MOE_LOCAL = 1152


def _out_proj(ys, w, x, gate, g, shift, scale, router, sparse, tm=512):
    B, S, D = x.shape
    tm = min(tm, S)
    bm = gate.shape[0]
    b_router, wr_hi, wr_lo = router
    nt = S // tm
    mod_map = (lambda b, i: (b, 0, 0)) if bm > 1 else (lambda b, i: (0, 0, 0))
    mod_spec = pl.BlockSpec((1, 1, D), mod_map)
    row_map = lambda b, i: (b, i, 0)
    in_specs = ([pl.BlockSpec((1, tm, y.shape[-1]), row_map) for y in ys]
                + [pl.BlockSpec(memory_space=pltpu.SMEM),
                   pl.BlockSpec(w.shape, lambda b, i: (0, 0)),
                   pl.BlockSpec((1, tm, D), row_map), mod_spec,
                   pl.BlockSpec((1, D), lambda b, i: (0, 0)), mod_spec, mod_spec,
                   pl.BlockSpec(wr_hi.shape, lambda b, i: (0, 0)),
                   pl.BlockSpec(wr_lo.shape, lambda b, i: (0, 0))])
    args = list(ys) + [b_router, w, x, gate.reshape(bm, 1, D), g.reshape(1, D), shift.reshape(bm, 1, D),
                       scale.reshape(bm, 1, D), wr_hi, wr_lo]
    out_specs = [pl.BlockSpec((1, tm, D), row_map)]
    out_shape = [jax.ShapeDtypeStruct((B, S, D), F32)]
    out_specs.append(pl.BlockSpec((1, tm, D), row_map))
    out_shape.append(jax.ShapeDtypeStruct((B, S, D), BF16))
    if sparse:
        assert tm == MOE_TT
        tri = jnp.asarray(np.triu(np.ones((tm, tm), np.float32), 1)).astype(BF16)
        tril = jnp.asarray(np.tril(np.ones((N_EXPERTS, N_EXPERTS), np.float32), -1)).astype(BF16)
        args += [tri, tril]
        in_specs += [pl.BlockSpec(tri.shape, lambda b, i: (0, 0)),
                     pl.BlockSpec(tril.shape, lambda b, i: (0, 0))]
        out_specs += [pl.BlockSpec((ROUTE_FIELDS, tm), lambda b, i: (0, b * nt + i)),
                      pl.BlockSpec((tm, LANES), lambda b, i: (b * nt + i, 0)),
                      pl.BlockSpec((1, N_EXPERTS, LANES), lambda b, i: (b * nt + i, 0, 0))]
        out_shape += [jax.ShapeDtypeStruct((ROUTE_FIELDS, B * S), F32),
                      jax.ShapeDtypeStruct((B * S, LANES), F32),
                      jax.ShapeDtypeStruct((B * nt, N_EXPERTS, LANES), F32)]
    else:
        out_specs.append(pl.BlockSpec((1, tm, LANES), row_map))
        out_shape.append(jax.ShapeDtypeStruct((B, S, LANES), F32))
    return pl.pallas_call(
        functools.partial(_out_proj_kernel, len(ys), sparse),
        grid=(B, nt),
        in_specs=in_specs,
        out_specs=out_specs,
        out_shape=out_shape,
        compiler_params=_cparams(("parallel", "parallel")),
        name="out_proj_sparse" if sparse else "out_proj",
    )(*args)


def _moe_kernel(h_ref, comb_ref, wg_ref, wu_ref, wd_ref, x_ref, gate_ref, o_ref, acc_ref):
    e = pl.program_id(2)

    @pl.when(e == 0)
    def _():
        acc_ref[...] = jnp.zeros_like(acc_ref)

    h = h_ref[0]
    a = jnp.dot(h, wg_ref[0], preferred_element_type=F32)
    u = jnp.dot(h, wu_ref[0], preferred_element_type=F32)
    he = (a * jax.nn.sigmoid(a) * u).astype(BF16)
    y = jnp.dot(he, wd_ref[0], preferred_element_type=F32)
    lane = lax.broadcasted_iota(jnp.int32, comb_ref.shape[1:], 1)
    c = jnp.sum(jnp.where(lane == e, comb_ref[0], 0.0), axis=-1, keepdims=True)
    acc_ref[...] += c * y

    @pl.when(e == pl.num_programs(2) - 1)
    def _():
        o_ref[0] = x_ref[0] + gate_ref[0] * acc_ref[...]


def _moe(h, comb, wg, wu, wd, x, gate, tm=1024):
    B, S, D = x.shape
    tm = min(tm, S)
    bm = gate.shape[0]
    mod_map = (lambda b, i, e: (b, 0, 0)) if bm > 1 else (lambda b, i, e: (0, 0, 0))
    row_map = lambda b, i, e: (b, i, 0)
    return pl.pallas_call(
        _moe_kernel,
        grid=(B, S // tm, N_EXPERTS),
        in_specs=[pl.BlockSpec((1, tm, D), row_map),
                  pl.BlockSpec((1, tm, LANES), row_map),
                  pl.BlockSpec((1, D, D_EXPERT), lambda b, i, e: (e, 0, 0)),
                  pl.BlockSpec((1, D, D_EXPERT), lambda b, i, e: (e, 0, 0)),
                  pl.BlockSpec((1, D_EXPERT, D), lambda b, i, e: (e, 0, 0)),
                  pl.BlockSpec((1, tm, D), row_map),
                  pl.BlockSpec((1, 1, D), mod_map)],
        out_specs=pl.BlockSpec((1, tm, D), row_map),
        out_shape=jax.ShapeDtypeStruct((B, S, D), F32),
        scratch_shapes=[pltpu.VMEM((tm, D), F32)],
        compiler_params=_cparams(("parallel", "parallel", "arbitrary")),
        name="moe",
    )(h, comb, wg, wu, wd, x, gate.reshape(bm, 1, D))


MOE_TM = 512
MOE_TG = MOE_TM // MOE_G


def _moe_rows(n_tok):
    rows = 2 * n_tok + (n_tok // MOE_TT) * N_EXPERTS * (MOE_G - 1) + N_EXPERTS * (MOE_TM - 1)
    return (rows + MOE_TM - 1) // MOE_TM * MOE_TM


def _moe_plan(ng):
    ng = ng[:, :, 0].astype(jnp.int32)
    n_tt = ng.shape[0]
    total = jnp.sum(ng, axis=0)
    region = (total + MOE_TG - 1) // MOE_TG * MOE_TG
    region_end = jnp.cumsum(region)
    region_start = region_end - region
    dst = region_start[None, :] + jnp.cumsum(ng, axis=0) - ng
    local = jnp.cumsum(ng, axis=1) - ng
    n_tiles = _moe_rows(n_tt * MOE_TT) // MOE_TM
    tile_first = jnp.arange(n_tiles, dtype=jnp.int32) * MOE_TG
    tile_expert = jnp.minimum(jnp.sum(region_end[None, :] <= tile_first[:, None], axis=1), N_EXPERTS - 1)
    n_valid = region_end[-1:] // MOE_TG
    i32 = lambda a: a.astype(jnp.int32).reshape(-1)
    return dict(ng=i32(ng), dst=i32(dst), local=i32(local), tile_groups=i32(jnp.sum(ng, axis=1)),
                pad_first=i32(region_start + total),
                pad_count=i32(region - total), tile_expert=i32(tile_expert), n_valid=i32(n_valid),
                n_tiles=n_tiles)


def _run_copies(plan_refs, tile, local_ref, sorted_ref, sem, to_sorted):
    ng_ref, dst_ref, loc_ref = plan_refs
    for e in range(N_EXPERTS):
        k = tile * N_EXPERTS + e
        loc, dst = loc_ref[k], dst_ref[k]

        def body(g, carry):
            lrows = local_ref.at[pl.ds(pl.multiple_of((loc + g) * MOE_G, MOE_G), MOE_G)]
            srows = sorted_ref.at[pl.ds(pl.multiple_of((dst + g) * MOE_G, MOE_G), MOE_G)]
            src, tgt = (lrows, srows) if to_sorted else (srows, lrows)
            pltpu.make_async_copy(src, tgt, sem).start()
            return carry
        lax.fori_loop(0, ng_ref[k], body, 0)


def _group_waits(n_groups, local_ref, sorted_ref, sem):
    def body(g, carry):
        pltpu.make_async_copy(sorted_ref.at[pl.ds(0, MOE_G)], local_ref.at[pl.ds(0, MOE_G)], sem).wait()
        return carry
    lax.fori_loop(0, n_groups, body, 0)


def _moe_dispatch_kernel(ng_ref, dst_ref, loc_ref, tot_ref, padf_ref, padc_ref, nv_ref,
                         h_ref, route_ref, xs_ref, local_ref, zero_ref, sem):
    i = pl.program_id(0)

    @pl.when(i == 0)
    def _():
        zero_ref[...] = jnp.zeros_like(zero_ref)
        n_pad = 0
        for e in range(N_EXPERTS):
            first = padf_ref[e]

            def body(g, carry):
                rows = xs_ref.at[pl.ds(pl.multiple_of((first + g) * MOE_G, MOE_G), MOE_G)]
                pltpu.make_async_copy(zero_ref.at[pl.ds(0, MOE_G)], rows, sem.at[1]).start()
                return carry
            lax.fori_loop(0, padc_ref[e], body, 0)
            n_pad = n_pad + padc_ref[e]

        def tile_copy(j):
            rows = xs_ref.at[pl.ds(pl.multiple_of(j * MOE_TM, MOE_TM), MOE_TM)]
            return pltpu.make_async_copy(zero_ref, rows, sem.at[2])

        n_tiles = xs_ref.shape[0] // MOE_TM
        lax.fori_loop(nv_ref[0], n_tiles, lambda j, c: (tile_copy(j).start(), c)[1], 0)
        _group_waits(n_pad, zero_ref, xs_ref, sem.at[1])
        lax.fori_loop(nv_ref[0], n_tiles, lambda j, c: (tile_copy(j).wait(), c)[1], 0)

    pos = lax.broadcasted_iota(jnp.int32, (MOE_LOCAL, MOE_TT), 0)
    p1 = route_ref[0:1, :].astype(jnp.int32)
    p2 = route_ref[1:2, :].astype(jnp.int32)
    pick = jnp.where((pos == p1) | (pos == p2), 1.0, 0.0).astype(BF16)
    local_ref[...] = jnp.dot(pick, h_ref[0], preferred_element_type=F32)
    _run_copies((ng_ref, dst_ref, loc_ref), i, local_ref, xs_ref, sem.at[0], to_sorted=True)
    _group_waits(tot_ref[i], local_ref, xs_ref, sem.at[0])


def _moe_dispatch(h, route_t, plan):
    B, S, D = h.shape
    nt = S // MOE_TT
    grid_spec = pltpu.PrefetchScalarGridSpec(
        num_scalar_prefetch=7,
        grid=(B * nt,),
        in_specs=[pl.BlockSpec((1, MOE_TT, D), lambda i, *_: (i // nt, i % nt, 0)),
                  pl.BlockSpec((ROUTE_FIELDS, MOE_TT), lambda i, *_: (0, i))],
        out_specs=pl.BlockSpec(memory_space=pl.ANY),
        scratch_shapes=[pltpu.VMEM((MOE_LOCAL, D), F32), pltpu.VMEM((MOE_TM, D), F32),
                        pltpu.SemaphoreType.DMA((3,))])
    return pl.pallas_call(
        _moe_dispatch_kernel,
        grid_spec=grid_spec,
        out_shape=jax.ShapeDtypeStruct((plan["n_tiles"] * MOE_TM, D), F32),
        compiler_params=_cparams(("arbitrary",)),
        name="moe_dispatch",
    )(plan["ng"], plan["dst"], plan["local"], plan["tile_groups"], plan["pad_first"], plan["pad_count"],
      plan["n_valid"], h, route_t)


def _moe_grouped_kernel(te_ref, nv_ref, x_ref, wg_ref, wu_ref, wd_ref, o_ref):
    used = pl.program_id(0) < nv_ref[0]

    @pl.when(used)
    def _():
        x = x_ref[...].astype(BF16)
        a = jnp.dot(x, wg_ref[0], preferred_element_type=F32)
        u = jnp.dot(x, wu_ref[0], preferred_element_type=F32)
        he = (a * jax.nn.sigmoid(a) * u).astype(BF16)
        o_ref[...] = jnp.dot(he, wd_ref[0], preferred_element_type=F32)

    @pl.when(jnp.logical_not(used))
    def _():
        o_ref[...] = jnp.zeros_like(o_ref)


def _moe_grouped(xs, plan, wg, wu, wd):
    n_tiles = plan["n_tiles"]
    D = D_MODEL
    tile = lambda j, nv: jnp.minimum(j, nv[0] - 1)
    grid_spec = pltpu.PrefetchScalarGridSpec(
        num_scalar_prefetch=2,
        grid=(n_tiles,),
        in_specs=[pl.BlockSpec((MOE_TM, D), lambda j, te, nv: (tile(j, nv), 0)),
                  pl.BlockSpec((1, D, D_EXPERT), lambda j, te, nv: (te[tile(j, nv)], 0, 0)),
                  pl.BlockSpec((1, D, D_EXPERT), lambda j, te, nv: (te[tile(j, nv)], 0, 0)),
                  pl.BlockSpec((1, D_EXPERT, D), lambda j, te, nv: (te[tile(j, nv)], 0, 0))],
        out_specs=pl.BlockSpec((MOE_TM, D), lambda j, te, nv: (j, 0)))
    return pl.pallas_call(
        _moe_grouped_kernel,
        grid_spec=grid_spec,
        out_shape=jax.ShapeDtypeStruct((n_tiles * MOE_TM, D), F32),
        compiler_params=_cparams(("arbitrary",)),
        name="moe_grouped",
    )(plan["tile_expert"], plan["n_valid"], xs, wg, wu, wd)


def _moe_combine_kernel(ng_ref, dst_ref, loc_ref, tot_ref, ys_ref, route_ref, x_ref, gate_ref, o_ref,
                        local_ref, sem):
    i = pl.program_id(0)

    @pl.when(i == 0)
    def _():
        local_ref[...] = jnp.zeros_like(local_ref)

    _run_copies((ng_ref, dst_ref, loc_ref), i, local_ref, ys_ref, sem, to_sorted=False)
    _group_waits(tot_ref[i], local_ref, ys_ref, sem)
    pos = lax.broadcasted_iota(jnp.int32, (MOE_TT, MOE_LOCAL), 1)
    route = route_ref[...]
    p1 = route[:, 0:1].astype(jnp.int32)
    p2 = route[:, 1:2].astype(jnp.int32)
    weigh = (jnp.where(pos == p1, route[:, 2:3], 0.0) + jnp.where(pos == p2, route[:, 3:4], 0.0)).astype(BF16)
    y = jnp.dot(weigh, local_ref[...].astype(BF16), preferred_element_type=F32)
    o_ref[0] = x_ref[0] + gate_ref[0] * y


def _moe_combine(ys, route_c, plan, x, gate):
    B, S, D = x.shape
    nt = S // MOE_TT
    grid_spec = pltpu.PrefetchScalarGridSpec(
        num_scalar_prefetch=4,
        grid=(B * nt,),
        in_specs=[pl.BlockSpec(memory_space=pl.ANY),
                  pl.BlockSpec((MOE_TT, LANES), lambda i, *_: (i, 0)),
                  pl.BlockSpec((1, MOE_TT, D), lambda i, *_: (i // nt, i % nt, 0)),
                  pl.BlockSpec((1, 1, D), lambda i, *_: (i // nt, 0, 0))],
        out_specs=pl.BlockSpec((1, MOE_TT, D), lambda i, *_: (i // nt, i % nt, 0)),
        scratch_shapes=[pltpu.VMEM((MOE_LOCAL, D), F32), pltpu.SemaphoreType.DMA(())])
    return pl.pallas_call(
        _moe_combine_kernel,
        grid_spec=grid_spec,
        out_shape=jax.ShapeDtypeStruct((B, S, D), F32),
        compiler_params=_cparams(("arbitrary",)),
        name="moe_combine",
    )(plan["ng"], plan["dst"], plan["local"], plan["tile_groups"], ys, route_c, x, gate.reshape(B, 1, D))


def _dft_tables():
    n1, n2, n = FFT_N1, FFT_N2, FFT_N1 * FFT_N2
    k1 = np.arange(n1)
    f1 = np.exp(-2j * np.pi * np.outer(k1, np.arange(n1)) / n1)
    tw = np.exp(-2j * np.pi * np.outer(np.arange(n2), k1) / n)
    ftw = f1[None, :, :] * tw[:, :, None]
    half = n1 // 2
    fh = ftw[:, :, :half]
    g_fwd = np.concatenate([np.concatenate([fh.real, -fh.imag], axis=2),
                            np.concatenate([fh.imag, fh.real], axis=2)], axis=1)
    back = ftw[:, :, ::-1][:, :, :half].copy()
    back[0] = np.roll(ftw[0], -1, axis=1)[:, ::-1][:, :half]
    back[0][:, 0] = 0.0
    fk = np.concatenate([fh, back], axis=2)
    g_real = np.concatenate([fk.real, fk.imag], axis=1)
    gi = np.conj(np.transpose(fh, (0, 2, 1))) / n
    g_inv = np.concatenate([np.concatenate([gi.real, -gi.imag], axis=2),
                            np.concatenate([gi.imag, gi.real], axis=2)], axis=1)
    f2 = np.exp(-2j * np.pi * np.outer(np.arange(n2), np.arange(n2)) / n2)
    f2_fwd = np.block([[f2.real, -f2.imag], [f2.imag, f2.real]])
    f2c = np.conj(f2)
    f2_inv = np.block([[f2c.real, -f2c.imag], [f2c.imag, f2c.real]])
    as_bf = lambda a: jnp.asarray(a, dtype=F32).astype(BF16)
    return as_bf(g_fwd), as_bf(g_real), as_bf(g_inv), as_bf(f2_fwd), as_bf(f2_inv)


def _fft_fast_stage(stage_ref, k1, f2):
    slab = 2 * FFT_N1
    m = jnp.concatenate([stage_ref[pl.ds(k1, FFT_N2, stride=slab), :],
                         stage_ref[pl.ds(FFT_N1 + k1, FFT_N2, stride=slab), :]], axis=0)
    return jnp.dot(f2, m.astype(BF16), preferred_element_type=F32)


def _filter_fft_kernel(hf_ref, hb_ref, inv_ref, g_ref, f2_ref, h_ref, stage_ref):
    slab = 2 * FFT_N1
    half = FFT_N1 // 2
    for n2 in range(FFT_N2):
        x = jnp.concatenate([hf_ref[pl.ds(n2, half, stride=FFT_N2), :],
                             hb_ref[pl.ds((FFT_N2 - n2) % FFT_N2, half, stride=FFT_N2), :]], axis=0)
        stage_ref[n2 * slab:(n2 + 1) * slab, :] = jnp.dot(g_ref[n2], x.astype(BF16),
                                                          preferred_element_type=F32)
    f2 = f2_ref[...]
    inv = inv_ref[...]
    for k1 in range(FFT_N1):
        h_ref[0, k1] = (_fft_fast_stage(stage_ref, k1, f2) * inv).astype(h_ref.dtype)


def _filter_fft(taps, inv_norm, g_real, f2_fwd, ct=LANES):
    L, cols = taps.shape
    C = D_CH
    n_ord = cols // (2 * C)
    nc = C // ct
    once = pl.Buffered(1)
    return pl.pallas_call(
        _filter_fft_kernel,
        grid=(n_ord, nc),
        in_specs=[pl.BlockSpec((L, ct), lambda o, c: (0, o * nc + c)),
                  pl.BlockSpec((L, ct), lambda o, c: (0, (n_ord + o) * nc + c)),
                  pl.BlockSpec((1, ct), lambda o, c: (0, o * nc + c)),
                  pl.BlockSpec(g_real.shape, lambda o, c: (0, 0, 0), pipeline_mode=once),
                  pl.BlockSpec(f2_fwd.shape, lambda o, c: (0, 0), pipeline_mode=once)],
        out_specs=pl.BlockSpec((1, FFT_N1, 2 * FFT_N2, ct), lambda o, c: (o, 0, 0, c)),
        out_shape=jax.ShapeDtypeStruct((n_ord, FFT_N1, 2 * FFT_N2, C), BF16),
        scratch_shapes=[pltpu.VMEM((FFT_N2 * 2 * FFT_N1, ct), F32)],
        compiler_params=_cparams(("parallel", "parallel")),
        name="filter_fft",
    )(taps, taps, inv_norm, g_real, f2_fwd)


def _hyena_conv_kernel(z_ref, gate_ref, hb_ref, spec_ref, gf_ref, gi_ref, f2f_ref, f2i_ref, o_ref,
                       stage_ref):
    half = FFT_N1 // 2
    slab = 2 * FFT_N1
    for n2 in range(FFT_N2):
        rows = pl.ds(n2, half, stride=FFT_N2)
        x = jnp.concatenate([z_ref[0, rows, :], z_ref[1, rows, :]], axis=0).astype(BF16)
        stage_ref[n2 * slab:(n2 + 1) * slab, :] = jnp.dot(gf_ref[n2], x, preferred_element_type=F32)
    f2f = f2f_ref[...]
    f2i = f2i_ref[...]
    for k1 in range(FFT_N1):
        zf = _fft_fast_stage(stage_ref, k1, f2f)
        zr, zi = zf[:FFT_N2], zf[FFT_N2:]
        hr = spec_ref[0, k1, :FFT_N2, :].astype(F32)
        hi = spec_ref[0, k1, FFT_N2:, :].astype(F32)
        p = jnp.concatenate([zr * hr - zi * hi, zr * hi + zi * hr], axis=0).astype(BF16)
        q = jnp.dot(f2i, p, preferred_element_type=F32)
        stage_ref[pl.ds(k1, FFT_N2, stride=slab), :] = q[:FFT_N2]
        stage_ref[pl.ds(FFT_N1 + k1, FFT_N2, stride=slab), :] = q[FFT_N2:]
    hb = hb_ref[...]
    for n2 in range(FFT_N2):
        y_in = stage_ref[n2 * slab:(n2 + 1) * slab, :].astype(BF16)
        y = jnp.dot(gi_ref[n2], y_in, preferred_element_type=F32)
        rows = pl.ds(n2, half, stride=FFT_N2)
        for m in range(2):
            zm = z_ref[m, rows, :]
            o_ref[m, rows, :] = gate_ref[m, rows, :] * (y[m * half:(m + 1) * half] + zm * hb)


def _hyena_conv(z, z_blk, gate, gate_blk, hbias, spec, order, tabs, ct=LANES):
    B, L, _ = z.shape
    C = D_CH
    g_fwd, _, g_inv, f2_fwd, f2_inv = tabs
    once = pl.Buffered(1)
    const3 = lambda a: pl.BlockSpec(a.shape, lambda c, p: (0, 0, 0), pipeline_mode=once)
    const2 = lambda a: pl.BlockSpec(a.shape, lambda c, p: (0, 0), pipeline_mode=once)
    row_spec = pl.BlockSpec((2, L, ct), lambda c, p: (p, 0, c))
    return pl.pallas_call(
        _hyena_conv_kernel,
        grid=(C // ct, B // 2),
        in_specs=[pl.BlockSpec((2, L, ct), lambda c, p: (p, 0, z_blk + c)),
                  pl.BlockSpec((2, L, ct), lambda c, p: (p, 0, gate_blk + c)),
                  pl.BlockSpec((1, ct), lambda c, p: (0, c)),
                  pl.BlockSpec((1, FFT_N1, 2 * FFT_N2, ct), lambda c, p: (order, 0, 0, c),
                               pipeline_mode=once),
                  const3(g_fwd), const3(g_inv), const2(f2_fwd), const2(f2_inv)],
        out_specs=row_spec,
        out_shape=jax.ShapeDtypeStruct((B, L, C), F32),
        scratch_shapes=[pltpu.VMEM((FFT_N2 * 2 * FFT_N1, ct), F32)],
        compiler_params=_cparams(("parallel", "arbitrary")),
        name="hyena_conv",
    )(z, gate, hbias, spec, g_fwd, g_inv, f2_fwd, f2_inv)


def _hyena_filters(L, w1, b1, f1, w2, b2, f2, w3, b3):
    t = jnp.arange(L, dtype=F32)
    tn = t / max(L - 1, 1)
    bands = jnp.linspace(1e-4, HY_BANDS - 1, HY_BANDS, dtype=F32)
    ang = 2.0 * math.pi * t[:, None] * bands[None] / L
    feats = jnp.concatenate([tn[:, None], jnp.cos(ang), jnp.sin(ang)], axis=-1)
    h = jnp.sin(f1 * (feats @ w1 + b1))
    h = jnp.sin(f2 * (h @ w2 + b2))
    deltas = jnp.abs(jnp.linspace(HY_MIN_DECAY, HY_MAX_DECAY, D_CH, dtype=F32))
    decay = jnp.exp(-tn[:, None] * deltas[None])
    n_rep = w3.shape[1] // D_CH
    taps = (_mm_f32(h, w3, 512) + b3) * jnp.tile(decay, (1, n_rep))
    l1 = jnp.sum(jnp.abs(taps), axis=0)
    l1 = l1[:n_rep // 2 * D_CH] + l1[n_rep // 2 * D_CH:]
    return taps, (1.0 / (l1 + EPS))[None]


def _short_conv(u, w, b):
    up = jnp.pad(u, ((0, 0), (1, 1), (0, 0)))
    return up[:, :-2] * w[0] + up[:, 1:-1] * w[1] + up[:, 2:] * w[2] + b


def _final_norm_kernel(x_ref, g_ref, o_ref):
    x = x_ref[0]
    ms = jnp.mean(x * x, axis=-1, keepdims=True)
    o_ref[0] = x * lax.rsqrt(ms + EPS) * g_ref[...]


def _final_norm(x, g, tm=1024):
    B, S, D = x.shape
    return pl.pallas_call(
        _final_norm_kernel,
        grid=(B, S // tm),
        in_specs=[pl.BlockSpec((1, tm, D), lambda b, i: (b, i, 0)),
                  pl.BlockSpec((1, D), lambda b, i: (0, 0))],
        out_specs=pl.BlockSpec((1, tm, D), lambda b, i: (b, i, 0)),
        out_shape=jax.ShapeDtypeStruct((B, S, D), F32),
        compiler_params=_cparams(("parallel", "parallel")),
        name="final_norm",
    )(x, g.reshape(1, D))


def _dup_heads(w):
    a, b = w[:, :HEAD_DIM], w[:, HEAD_DIM:]
    return jnp.concatenate([a, a, b, b], axis=1)


def _rope_tables(S):
    t = jnp.arange(S)
    row = (t // GRID_W).astype(F32)
    col = (t % GRID_W).astype(F32)
    half = HEAD_DIM // 2
    inv = ROPE_THETA ** (-jnp.arange(0, half, 2, dtype=F32) / half)
    ar = row[:, None] * inv[None]
    ac = col[:, None] * inv[None]
    cos = jnp.concatenate([jnp.cos(ar), jnp.cos(ar), jnp.cos(ac), jnp.cos(ac)], axis=-1)
    sin = jnp.concatenate([-jnp.sin(ar), jnp.sin(ar), -jnp.sin(ac), jnp.sin(ac)], axis=-1)
    return jnp.tile(cos, (1, 2)), jnp.tile(sin, (1, 2))


def _head_mean_matrix(width):
    blk = np.kron(np.eye(width // HEAD_DIM), np.full((HEAD_DIM, HEAD_DIM), 1.0 / HEAD_DIM))
    return jnp.asarray(blk, dtype=F32).astype(BF16)


def kernel(x, c, ctx, c_ctx, w_ada, b_ada, norm_g, final_g, w_in_even, w_out_even, a_sink, b_rpb, w_in_odd, w_out_odd, c_qnorm, c_knorm, hy_short_w, hy_short_b, hy_w1, hy_b1, hy_f1, hy_w2, hy_b2, hy_f2, hy_w3, hy_b3, hy_bias, w_router, b_router, moe_wg, moe_wu, moe_wd):
    B, S, D = x.shape
    depth = w_ada.shape[0]
    rope = _rope_tables(S)
    wr_pad = jnp.pad(w_router.astype(F32), ((0, 0), (0, LANES - N_EXPERTS)))
    wr_hi = wr_pad.astype(BF16)
    router = (b_router.astype(F32), wr_hi, (wr_pad - wr_hi.astype(F32)).astype(BF16))

    mod_in = jnp.concatenate([jax.nn.silu(c), jax.nn.silu(c_ctx)[None],
                              jnp.zeros((8 - B - 1, D), F32)], axis=0)
    xc = ctx
    for l in range(depth):
        need_ctx = l < depth - 1
        mod = _mm_f32(mod_in, w_ada[l], 1536) + b_ada[l]
        mx = mod[:B].reshape(B, 6, D)
        mc = mod[B].reshape(6, D)
        i = l // 2
        if l % 2 == 0:
            w = w_in_even[i].astype(BF16)
            w_all = jnp.concatenate([w[:, :512], _dup_heads(w[:, 512:640]), _dup_heads(w[:, 640:768]),
                                     w[:, 768:]], axis=1)
            segs_x = ((512, "rope", ATTN_SCALE), (256, "rope", 1.0), (256, "plain", 1.0),
                      (512, "plain", ATTN_SCALE), (512, "plain", 1.0), (512, "plain", 1.0))
            aq, akd, avd, bq, bk, bv = _norm_proj(x, norm_g[l, 0], mx[:, 0], mx[:, 1], w_all, segs_x,
                                                  rope=rope)
            segs_c = tuple((wd, "plain", m) for wd, _, m in segs_x)
            caq, cakd, cavd, cbq, cbk, cbv = _norm_proj(xc, norm_g[l, 0], mc[0:1], mc[1:2], w_all, segs_c)
            ya = _window_attn(aq, akd, avd, cakd, cavd, a_sink[i].astype(F32))
            yb = _nbr_attn(bq, bk, bv, cbk, cbv, _nbr_bias_table(b_rpb[i], S // GRID_W))
            ys = [ya, yb]
            w_out = w_out_even[i].astype(BF16)
            if need_ctx:
                yc = [_ctx_attn(a_sink[i].astype(F32), caq, cakd, cavd, cbq, cbk, cbv)]
        else:
            w = w_in_odd[i].astype(BF16)
            w_all = jnp.concatenate([w[:, :512], _dup_heads(w[:, 512:640]), _dup_heads(w[:, 640:768]),
                                     w[:, 768:]], axis=1)
            gains = jnp.concatenate([jnp.tile(c_qnorm[i], 8), jnp.tile(c_knorm[i], 4)])[None].astype(F32)
            norm = (_head_mean_matrix(512), gains)
            q_mult = ATTN_SCALE * math.log2(math.e)
            segs_x = ((512, "normrope", q_mult), (256, "normrope", 1.0), (256, "plain", 1.0),
                      (3 * D_CH, "plain", 1.0))
            qx, kxd, vxd, ux = _norm_proj(x, norm_g[l, 0], mx[:, 0], mx[:, 1], w_all, segs_x,
                                          rope=rope, norm=norm)
            w_c = w_all[:, 512:1024]
            norm_c = (_head_mean_matrix(512), jnp.tile(c_knorm[i], 4)[None].astype(F32))
            kcd, vcd = _norm_proj(xc, norm_g[l, 0], mc[0:1], mc[1:2], w_c,
                                  ((256, "norm", 1.0), (256, "plain", 1.0)), norm=norm_c)
            logit_bound = (1.02 * HEAD_DIM * q_mult * jnp.max(jnp.abs(c_qnorm[i]))
                           * jnp.max(jnp.abs(c_knorm[i])))
            y_attn = lax.cond(logit_bound <= FULL_NOMAX_LOG2_BOUND,
                              lambda *a: _full_attn(*a, bounded=True),
                              lambda *a: _full_attn(*a, bounded=False),
                              qx, kxd, vxd, kcd, vcd)
            u = _short_conv(ux.astype(F32), hy_short_w[i], hy_short_b[i])
            tabs = _dft_tables()
            taps, inv_norm = _hyena_filters(S, hy_w1[i], hy_b1[i], hy_f1[i], hy_w2[i], hy_b2[i],
                                            hy_f2[i], hy_w3[i], hy_b3[i])
            spec = _filter_fft(taps, inv_norm, tabs[1], tabs[3])
            blocks = D_CH // LANES
            z = _hyena_conv(u, 0, u, blocks, hy_bias[i, 0:1], spec, 0, tabs)
            z = _hyena_conv(z, 0, u, 2 * blocks, hy_bias[i, 1:2], spec, 1, tabs)
            ys = [y_attn, z.astype(BF16)]
            w_out = w_out_odd[i].astype(BF16)
            if need_ctx:
                raise NotImplementedError("context update of an odd layer is not needed at this depth")

        x, hx, route_t, route_c, ng = _out_proj(ys, w_out, x, mx[:, 2], norm_g[l, 1], mx[:, 3], mx[:, 4],
                                                router, sparse=True)
        wg, wu, wd = moe_wg[l].astype(BF16), moe_wu[l].astype(BF16), moe_wd[l].astype(BF16)
        if need_ctx:
            xc, hc, comb_c = _out_proj(yc, w_out, xc, mc[2:3], norm_g[l, 1], mc[3:4], mc[4:5], router,
                                       sparse=False)
            xc = _moe(hc, comb_c, wg, wu, wd, xc, mc[5:6])
        plan = _moe_plan(ng)
        sorted_rows = _moe_dispatch(hx, route_t, plan)
        x = _moe_combine(_moe_grouped(sorted_rows, plan, wg, wu, wd), route_c, plan, x, mx[:, 5])
    return _final_norm(x, final_g)
```

```python
import functools
import math

import numpy as np
import jax
import jax.numpy as jnp
from jax import lax
from jax.experimental import pallas as pl
from jax.experimental.pallas import tpu as pltpu

F32 = jnp.float32
BF16 = jnp.bfloat16

D_MODEL = 1024
GRID_W = 64
CTX_LEN = 256
HEAD_DIM = 64
ROPE_THETA = 10000.0
EPS = 1e-6
ATTN_SCALE = HEAD_DIM ** -0.5
A_WINDOW = 128
A_BLOCK = 128
B_WIN_H = 8
B_WIN_W = 16
D_CH = 512
HY_BANDS = 16
HY_MAX_DECAY = math.log(1e-2) / 0.3
HY_MIN_DECAY = math.log(1e-2) / 1.5
N_EXPERTS = 16
N_GROUPS = 4
EXPERTS_PER_GROUP = N_EXPERTS // N_GROUPS
TOP_K = 2
D_EXPERT = 512

LANES = 128
NEG = -1e30
VMEM_LIMIT = 48 * 1024 * 1024

FFT_N1 = 64
FFT_N2 = 128


def _cparams(sem):
    return pltpu.CompilerParams(dimension_semantics=sem, vmem_limit_bytes=VMEM_LIMIT)


def _mm_f32_kernel(x_ref, w_ref, o_ref):
    o_ref[...] = jnp.dot(x_ref[...], w_ref[...], preferred_element_type=F32)


def _mm_f32(x, w, tn):
    M, K = x.shape
    N = w.shape[1]
    return pl.pallas_call(
        _mm_f32_kernel,
        grid=(N // tn,),
        in_specs=[pl.BlockSpec((M, K), lambda j: (0, 0)),
                  pl.BlockSpec((K, tn), lambda j: (0, j))],
        out_specs=pl.BlockSpec((M, tn), lambda j: (0, j)),
        out_shape=jax.ShapeDtypeStruct((M, N), F32),
        compiler_params=_cparams(("arbitrary",)),
        name="mm_f32",
    )(x, w)


def _swap16(y):
    n = y.shape[-1]
    lane = lax.broadcasted_iota(jnp.int32, y.shape, y.ndim - 1)
    up = pltpu.roll(y, n - 16, axis=y.ndim - 1)
    dn = pltpu.roll(y, 16, axis=y.ndim - 1)
    return jnp.where((lane % 32) < 16, up, dn)


def _tile_lanes(t, width):
    reps = width // t.shape[-1]
    return t if reps == 1 else jnp.concatenate([t] * reps, axis=-1)


def _norm_proj_kernel(segs, has_rope, has_norm, *refs):
    it = iter(refs)
    x_ref, g_ref, shift_ref, scale_ref, w_ref = (next(it) for _ in range(5))
    cos_ref = sin_ref = bd_ref = gain_ref = None
    if has_rope:
        cos_ref, sin_ref = next(it), next(it)
    if has_norm:
        bd_ref, gain_ref = next(it), next(it)
    out_refs = list(it)

    x = x_ref[0]
    ms = jnp.mean(x * x, axis=-1, keepdims=True)
    h = x * lax.rsqrt(ms + EPS) * g_ref[...]
    h = h * (1.0 + scale_ref[0]) + shift_ref[0]
    y = jnp.dot(h.astype(BF16), w_ref[...], preferred_element_type=F32)

    off = 0
    goff = 0
    for (width, kind, mult, *_), o_ref in zip(segs, out_refs):
        ys = y[:, off:off + width]
        if kind in ("norm", "normrope"):
            bd = bd_ref[...][:width, :width]
            hms = jnp.dot((ys * ys).astype(BF16), bd, preferred_element_type=F32)
            ys = ys * lax.rsqrt(hms + EPS) * gain_ref[:, goff:goff + width]
            goff += width
        if kind in ("rope", "normrope"):
            c = _tile_lanes(cos_ref[...], width)
            s = _tile_lanes(sin_ref[...], width)
            ys = ys * c + _swap16(ys) * s
        if mult != 1.0:
            ys = ys * mult
        o_ref[0] = ys.astype(o_ref.dtype)
        off += width


def _norm_proj(x, g, shift, scale, w, segs, rope=None, norm=None, tm=512):
    B, S, D = x.shape
    N = w.shape[1]
    tm = min(tm, S)
    bm = shift.shape[0]
    mod_map = (lambda b, i: (b, 0, 0)) if bm > 1 else (lambda b, i: (0, 0, 0))
    args = [x, g.reshape(1, D), shift.reshape(bm, 1, D), scale.reshape(bm, 1, D), w]
    in_specs = [pl.BlockSpec((1, tm, D), lambda b, i: (b, i, 0)),
                pl.BlockSpec((1, D), lambda b, i: (0, 0)),
                pl.BlockSpec((1, 1, D), mod_map),
                pl.BlockSpec((1, 1, D), mod_map),
                pl.BlockSpec((D, N), lambda b, i: (0, 0))]
    if rope is not None:
        args += [rope[0], rope[1]]
        in_specs += [pl.BlockSpec((tm, LANES), lambda b, i: (i, 0))] * 2
    if norm is not None:
        args += [norm[0], norm[1]]
        in_specs += [pl.BlockSpec(norm[0].shape, lambda b, i: (0, 0)),
                     pl.BlockSpec(norm[1].shape, lambda b, i: (0, 0))]
    out_shape = [jax.ShapeDtypeStruct((B, S, seg[0]), seg[3] if len(seg) > 3 else BF16) for seg in segs]
    out_specs = [pl.BlockSpec((1, tm, seg[0]), lambda b, i: (b, i, 0)) for seg in segs]
    return pl.pallas_call(
        functools.partial(_norm_proj_kernel, segs, rope is not None, norm is not None),
        grid=(B, S // tm),
        in_specs=in_specs,
        out_specs=out_specs,
        out_shape=out_shape,
        compiler_params=_cparams(("parallel", "parallel")),
        name="norm_proj",
    )(*args)


def _half_mask(shape):
    return lax.broadcasted_iota(jnp.int32, shape, len(shape) - 1) < HEAD_DIM


def _stack_halves(qp):
    lo = _half_mask(qp.shape)
    zero = jnp.zeros_like(qp)
    return jnp.concatenate([jnp.where(lo, qp, zero), jnp.where(lo, zero, qp)], axis=0)


def _merge_halves(o, m):
    return jnp.where(_half_mask((m, LANES)), o[:m], o[m:])


def _scores(q, k):
    return lax.dot_general(q, k, (((1,), (1,)), ((), ())), preferred_element_type=F32)


def _joint_softmax_pv(score_parts, value_parts, extra_logit=None):
    m = functools.reduce(jnp.maximum, [jnp.max(s, axis=-1, keepdims=True) for s in score_parts])
    if extra_logit is not None:
        m = jnp.maximum(m, extra_logit)
    den = jnp.exp(extra_logit - m) if extra_logit is not None else 0.0
    acc = None
    for s, v in zip(score_parts, value_parts):
        p = jnp.exp(s - m)
        den = den + jnp.sum(p, axis=-1, keepdims=True)
        pv = jnp.dot(p.astype(BF16), v, preferred_element_type=F32)
        acc = pv if acc is None else acc + pv
    return acc / den


def _sink_column(sink_ref, first_head, n_heads, rows_per_head):
    rows = lax.broadcasted_iota(jnp.int32, (n_heads * rows_per_head, 1), 0)
    col = jnp.zeros((n_heads * rows_per_head, 1), F32)
    for j in range(n_heads):
        in_head = (rows >= j * rows_per_head) & (rows < (j + 1) * rows_per_head)
        col = jnp.where(in_head, sink_ref[first_head + j], col)
    return col


def _window_attn_kernel(seq_len, sink_ref, q_ref, kp_ref, kc_ref, kn_ref, vp_ref, vc_ref, vn_ref,
                        ck_ref, cv_ref, o_ref):
    i = pl.program_id(1)
    blk = A_BLOCK
    q = q_ref[0]
    rows = lax.broadcasted_iota(jnp.int32, (4 * blk, 3 * blk), 0) % blk
    rel = lax.broadcasted_iota(jnp.int32, (4 * blk, 3 * blk), 1) - blk
    gpos = i * blk + rel
    valid = (jnp.abs(rows - rel) <= A_WINDOW) & (gpos >= 0) & (gpos < seq_len)
    outs = []
    for g in range(2):
        ls = slice(g * LANES, (g + 1) * LANES)
        k_loc = jnp.concatenate([kp_ref[0][:, ls], kc_ref[0][:, ls], kn_ref[0][:, ls]], axis=0)
        v_loc = jnp.concatenate([vp_ref[0][:, ls], vc_ref[0][:, ls], vn_ref[0][:, ls]], axis=0)
        qs = jnp.concatenate([_stack_halves(q[:, (2 * g + j) * LANES:(2 * g + j + 1) * LANES])
                              for j in range(2)], axis=0)
        s_loc = jnp.where(valid, _scores(qs, k_loc), NEG)
        s_ctx = _scores(qs, ck_ref[0][:, ls])
        sink = _sink_column(sink_ref, 4 * g, 4, blk)
        o = _joint_softmax_pv([s_loc, s_ctx], [v_loc, cv_ref[0][:, ls]], sink)
        outs += [_merge_halves(o[:2 * blk], blk), _merge_halves(o[2 * blk:], blk)]
    o_ref[0] = jnp.concatenate(outs, axis=-1).astype(o_ref.dtype)


def _window_attn(q, kd, vd, ckd, cvd, sink):
    B, S, _ = q.shape
    nb = S // A_BLOCK
    kv_spec = lambda f: pl.BlockSpec((1, A_BLOCK, 2 * LANES), f)
    prev_map = lambda b, i: (b, jnp.maximum(i - 1, 0), 0)
    cur_map = lambda b, i: (b, i, 0)
    next_map = lambda b, i: (b, jnp.minimum(i + 1, nb - 1), 0)
    ctx_spec = pl.BlockSpec((1, CTX_LEN, 2 * LANES), lambda b, i: (b, 0, 0))
    return pl.pallas_call(
        functools.partial(_window_attn_kernel, S),
        grid=(B, nb),
        in_specs=[pl.BlockSpec(memory_space=pltpu.SMEM),
                  pl.BlockSpec((1, A_BLOCK, 4 * LANES), cur_map),
                  kv_spec(prev_map), kv_spec(cur_map), kv_spec(next_map),
                  kv_spec(prev_map), kv_spec(cur_map), kv_spec(next_map),
                  ctx_spec, ctx_spec],
        out_specs=pl.BlockSpec((1, A_BLOCK, 4 * LANES), cur_map),
        out_shape=jax.ShapeDtypeStruct((B, S, 4 * LANES), BF16),
        compiler_params=_cparams(("parallel", "parallel")),
        name="window_attn",
    )(sink, q, kd, kd, kd, vd, vd, vd, ckd, cvd)


NBR_ROWS = 4
NBR_KROWS = 12


def _nbr_start_row(i, n_rows):
    return jnp.clip(i * NBR_ROWS - B_WIN_H // 2, 0, n_rows - NBR_KROWS)


def _nbr_attn_kernel(n_rows, q_ref, k_ref, v_ref, ck_ref, cv_ref, bias_ref, o_ref):
    i = pl.program_id(2)
    nq = NBR_ROWS * GRID_W
    nk = NBR_KROWS * GRID_W
    start = pl.multiple_of(_nbr_start_row(i, n_rows) * GRID_W, GRID_W)
    k_loc = k_ref[0, pl.ds(start, nk), :]
    v_loc = v_ref[0, pl.ds(start, nk), :]
    qs = _stack_halves(q_ref[0])
    s_loc = _scores(qs, k_loc) + bias_ref[0].reshape(2 * nq, nk)
    s_ctx = _scores(qs, ck_ref[0])
    o = _joint_softmax_pv([s_loc, s_ctx], [v_loc, cv_ref[0]])
    o_ref[0] = _merge_halves(o, nq).astype(o_ref.dtype)


def _nbr_bias_table(rpb, n_rows):
    kh = B_WIN_H
    n_heads = rpb.shape[0]
    col = np.arange(GRID_W)
    cs = np.clip(col - B_WIN_W // 2, 0, GRID_W - B_WIN_W)
    col_ok = (col[None, :] >= cs[:, None]) & (col[None, :] < cs[:, None] + B_WIN_W)
    dc = np.clip(col[None, :] - col[:, None], -(B_WIN_W - 1), B_WIN_W - 1) + B_WIN_W - 1
    pick = (dc[..., None] == np.arange(2 * B_WIN_W - 1)).astype(np.float32)
    by_col = jnp.einsum("hdc,qkc->hdqk", rpb.astype(F32), pick, precision=lax.Precision.HIGHEST)
    by_col = jnp.where(col_ok[None, None], by_col, NEG)
    masked = jnp.full((n_heads, GRID_W, GRID_W), NEG, F32)
    tabs = []
    for r0 in (0, NBR_ROWS, n_rows - NBR_ROWS):
        start = int(np.clip(r0 - kh // 2, 0, n_rows - NBR_KROWS))
        per_row = []
        for ri in range(NBR_ROWS):
            r = r0 + ri
            rs = int(np.clip(r - kh // 2, 0, n_rows - kh))
            slabs = [by_col[:, start + kri - r + kh - 1] if rs <= start + kri < rs + kh else masked
                     for kri in range(NBR_KROWS)]
            per_row.append(jnp.stack(slabs, axis=2))
        tabs.append(jnp.stack(per_row, axis=1).reshape(n_heads, NBR_ROWS * GRID_W, NBR_KROWS * GRID_W))
    return jnp.stack(tabs)


def _nbr_attn(q, k, v, ck, cv, bias):
    B, S, _ = q.shape
    n_rows = S // GRID_W
    nsteps = n_rows // NBR_ROWS
    nq = NBR_ROWS * GRID_W
    nk = NBR_KROWS * GRID_W
    pat = lambda i: jnp.where(i == 0, 0, jnp.where(i == nsteps - 1, 2, 1))
    return pl.pallas_call(
        functools.partial(_nbr_attn_kernel, n_rows),
        grid=(B, 4, nsteps),
        in_specs=[pl.BlockSpec((1, nq, LANES), lambda b, p, i: (b, i, p)),
                  pl.BlockSpec((1, S, LANES), lambda b, p, i: (b, 0, p)),
                  pl.BlockSpec((1, S, LANES), lambda b, p, i: (b, 0, p)),
                  pl.BlockSpec((1, CTX_LEN, LANES), lambda b, p, i: (b, 0, p)),
                  pl.BlockSpec((1, CTX_LEN, LANES), lambda b, p, i: (b, 0, p)),
                  pl.BlockSpec((1, 2, nq, nk), lambda b, p, i: (pat(i), p, 0, 0))],
        out_specs=pl.BlockSpec((1, nq, LANES), lambda b, p, i: (b, i, p)),
        out_shape=jax.ShapeDtypeStruct((B, S, 4 * LANES), BF16),
        compiler_params=_cparams(("parallel", "parallel", "arbitrary")),
        name="nbr_attn",
    )(q, k, v, ck, cv, bias)


FULL_TQ = 512
FULL_TK = 512
FULL_NOMAX_LOG2_BOUND = 60.0


def _full_attn_kernel(bounded, q_ref, k_ref, v_ref, ck_ref, cv_ref, o_ref):
    tq = FULL_TQ
    q = q_ref[0]
    qs = jnp.concatenate([_stack_halves(q[:, :LANES]), _stack_halves(q[:, LANES:])], axis=0)

    def step_bounded(acc, k, v):
        v_ones = jnp.where(_half_mask(v.shape), v, jnp.ones_like(v))
        p = jnp.exp2(_scores(qs, k))
        return acc + jnp.dot(p.astype(BF16), v_ones, preferred_element_type=F32)

    def step_online(carry, k, v):
        m, l, acc = carry
        s = _scores(qs, k)
        m_new = jnp.maximum(m, jnp.max(s, axis=-1, keepdims=True))
        alpha = jnp.exp2(m - m_new)
        p = jnp.exp2(s - m_new)
        l = l * alpha + jnp.sum(p, axis=-1, keepdims=True)
        acc = acc * alpha + jnp.dot(p.astype(BF16), v, preferred_element_type=F32)
        return m_new, l, acc

    step = step_bounded if bounded else step_online
    init = jnp.zeros((4 * tq, LANES), F32)
    if not bounded:
        init = (jnp.full((4 * tq, 1), NEG, F32), jnp.zeros((4 * tq, 1), F32), init)
    carry = step(init, ck_ref[0], cv_ref[0])

    def body(j, carry):
        off = pl.multiple_of(j * FULL_TK, FULL_TK)
        return step(carry, k_ref[0, pl.ds(off, FULL_TK), :], v_ref[0, pl.ds(off, FULL_TK), :])

    carry = lax.fori_loop(0, k_ref.shape[1] // FULL_TK, body, carry)
    if bounded:
        o = carry * pltpu.roll(1.0 / carry, HEAD_DIM, axis=1)
        merge = lambda a, b: jnp.where(_half_mask((tq, LANES)), a, pltpu.roll(b, HEAD_DIM, axis=1))
        pairs = [merge(o[0:tq], o[tq:2 * tq]), merge(o[2 * tq:3 * tq], o[3 * tq:])]
    else:
        o = carry[2] / carry[1]
        pairs = [_merge_halves(o[:2 * tq], tq), _merge_halves(o[2 * tq:], tq)]
    o_ref[0] = jnp.concatenate(pairs, axis=-1).astype(o_ref.dtype)


def _full_attn(q, kd, vd, ckd, cvd, bounded):
    B, S, _ = q.shape
    return pl.pallas_call(
        functools.partial(_full_attn_kernel, bounded),
        grid=(B, 2, S // FULL_TQ),
        in_specs=[pl.BlockSpec((1, FULL_TQ, 2 * LANES), lambda b, g, i: (b, i, g)),
                  pl.BlockSpec((1, S, LANES), lambda b, g, i: (b, 0, g)),
                  pl.BlockSpec((1, S, LANES), lambda b, g, i: (b, 0, g)),
                  pl.BlockSpec((1, CTX_LEN, LANES), lambda b, g, i: (b, 0, g)),
                  pl.BlockSpec((1, CTX_LEN, LANES), lambda b, g, i: (b, 0, g))],
        out_specs=pl.BlockSpec((1, FULL_TQ, 2 * LANES), lambda b, g, i: (b, i, g)),
        out_shape=jax.ShapeDtypeStruct((B, S, 4 * LANES), BF16),
        compiler_params=_cparams(("parallel", "parallel", "arbitrary")),
        name="full_attn_bounded" if bounded else "full_attn_online",
    )(q, kd, vd, ckd, cvd)


def _ctx_attn_kernel(sink_ref, aq_ref, akd_ref, avd_ref, bq_ref, bk_ref, bv_ref, o_ref):
    n = CTX_LEN
    aq = aq_ref[0]
    bq = bq_ref[0]
    outs = []
    for g in range(2):
        ls = slice(g * LANES, (g + 1) * LANES)
        qs = jnp.concatenate([_stack_halves(aq[:, (2 * g + j) * LANES:(2 * g + j + 1) * LANES])
                              for j in range(2)], axis=0)
        sink = _sink_column(sink_ref, 4 * g, 4, n)
        o = _joint_softmax_pv([_scores(qs, akd_ref[0][:, ls])], [avd_ref[0][:, ls]], sink)
        outs += [_merge_halves(o[:2 * n], n), _merge_halves(o[2 * n:], n)]
    for p in range(4):
        ls = slice(p * LANES, (p + 1) * LANES)
        qs = _stack_halves(bq[:, ls])
        o = _joint_softmax_pv([_scores(qs, bk_ref[0][:, ls])], [bv_ref[0][:, ls]])
        outs.append(_merge_halves(o, n))
    o_ref[0] = jnp.concatenate(outs, axis=-1).astype(o_ref.dtype)


def _ctx_attn(sink, aq, akd, avd, bq, bk, bv):
    B = aq.shape[0]
    spec = lambda a: pl.BlockSpec((1,) + a.shape[1:], lambda b: (b, 0, 0))
    args = (aq, akd, avd, bq, bk, bv)
    return pl.pallas_call(
        _ctx_attn_kernel,
        grid=(B,),
        in_specs=[pl.BlockSpec(memory_space=pltpu.SMEM)] + [spec(a) for a in args],
        out_specs=pl.BlockSpec((1, CTX_LEN, 8 * LANES), lambda b: (b, 0, 0)),
        out_shape=jax.ShapeDtypeStruct((B, CTX_LEN, 8 * LANES), BF16),
        compiler_params=_cparams(("parallel",)),
        name="ctx_attn",
    )(sink, *args)


def _pick4(idx, vals):
    return jnp.where(idx == 0, vals[0], jnp.where(idx == 1, vals[1], jnp.where(idx == 2, vals[2], vals[3])))


def _route_rows(lg_t, b_ref):
    n_tok = lg_t.shape[1]
    s = [jax.nn.sigmoid(lg_t[e:e + 1, :]) for e in range(N_EXPERTS)]
    sel = [s[e] + b_ref[e] for e in range(N_EXPERTS)]
    n = EXPERTS_PER_GROUP
    gscore = []
    for j in range(N_GROUPS):
        v = sel[n * j:n * (j + 1)]
        pair_sums = [v[a] + v[b] for a in range(n) for b in range(a + 1, n)]
        gscore.append(functools.reduce(jnp.maximum, pair_sums))
    best, gbest = gscore[0], jnp.zeros((1, n_tok), jnp.int32)
    for j in range(1, N_GROUPS):
        upd = gscore[j] > best
        best = jnp.where(upd, gscore[j], best)
        gbest = jnp.where(upd, j, gbest)
    v = [_pick4(gbest, [sel[n * j + i] for j in range(N_GROUPS)]) for i in range(n)]
    u = [_pick4(gbest, [s[n * j + i] for j in range(N_GROUPS)]) for i in range(n)]
    m1, i1 = v[0], jnp.zeros((1, n_tok), jnp.int32)
    for i in range(1, n):
        upd = v[i] > m1
        m1 = jnp.where(upd, v[i], m1)
        i1 = jnp.where(upd, i, i1)
    m2, i2 = jnp.full((1, n_tok), -jnp.inf, F32), jnp.zeros((1, n_tok), jnp.int32)
    for i in range(n):
        upd = (i1 != i) & (v[i] > m2)
        m2 = jnp.where(upd, v[i], m2)
        i2 = jnp.where(upd, i, i2)
    u1, u2 = _pick4(i1, u), _pick4(i2, u)
    tot = u1 + u2
    return n * gbest + i1, n * gbest + i2, u1 / tot, u2 / tot


def _out_proj_kernel(n_y, sparse, *refs):
    y_refs = refs[:n_y]
    br_ref, w_ref, x_ref, gate_ref, g_ref, shift_ref, scale_ref, wrh_ref, wrl_ref = refs[n_y:n_y + 9]
    n_in = n_y + 9
    if sparse:
        tri_ref, tril_ref = refs[n_in:n_in + 2]
        n_in += 2
    outs = refs[n_in:]
    xo_ref = outs[0]
    off = 0
    acc = None
    for y_ref in y_refs:
        wdt = y_ref.shape[-1]
        part = jnp.dot(y_ref[0], w_ref[off:off + wdt, :], preferred_element_type=F32)
        acc = part if acc is None else acc + part
        off += wdt
    x = x_ref[0] + gate_ref[0] * acc
    xo_ref[0] = x
    ms = jnp.mean(x * x, axis=-1, keepdims=True)
    h = x * lax.rsqrt(ms + EPS) * g_ref[...]
    h = h * (1.0 + scale_ref[0]) + shift_ref[0]
    hh = h.astype(BF16)
    hl = (h - hh.astype(F32)).astype(BF16)
    lg = (jnp.dot(hh, wrh_ref[...], preferred_element_type=F32)
          + jnp.dot(hl, wrh_ref[...], preferred_element_type=F32)
          + jnp.dot(hh, wrl_ref[...], preferred_element_type=F32))
    lg_t = lg.T[:N_EXPERTS]
    e1, e2, w1, w2 = _route_rows(lg_t, br_ref)
    rows = lax.broadcasted_iota(jnp.int32, lg_t.shape, 0)
    if not sparse:
        h_ref, comb_ref = outs[1:]
        h_ref[0] = hh
        comb_t = jnp.where(rows == e1, w1, 0.0) + jnp.where(rows == e2, w2, 0.0)
        comb_ref[0] = jnp.concatenate(
            [comb_t, jnp.zeros((LANES - N_EXPERTS, comb_t.shape[1]), F32)], axis=0).T
        return

    h_ref, route_t_ref, route_c_ref, ng_ref = outs[1:]
    h_ref[0] = hh
    member = jnp.where((rows == e1) | (rows == e2), 1.0, 0.0)
    before = jnp.dot(member.astype(BF16), tri_ref[...], preferred_element_type=F32)
    groups = jnp.floor((jnp.sum(member, axis=1, keepdims=True) + (MOE_G - 1)) * (1.0 / MOE_G))
    groups = jnp.broadcast_to(groups, (N_EXPERTS, LANES))
    run_start = MOE_G * jnp.dot(tril_ref[...], groups.astype(BF16), preferred_element_type=F32)[:, 0:1]
    pos = run_start + before
    p1 = jnp.sum(jnp.where(rows == e1, pos, 0.0), axis=0, keepdims=True)
    p2 = jnp.sum(jnp.where(rows == e2, pos, 0.0), axis=0, keepdims=True)
    ng_ref[0] = groups
    field = lax.broadcasted_iota(jnp.int32, (LANES, h.shape[0]), 0)
    route = jnp.zeros((LANES, h.shape[0]), F32)
    for k, v in enumerate((p1, p2, w1, w2)):
        route = jnp.where(field == k, v, route)
    route_t_ref[...] = route[:ROUTE_FIELDS]
    route_c_ref[...] = route.T


ROUTE_FIELDS = 8
MOE_G = 8
MOE_TT = 512
MOE_LOCAL = 1152


def _out_proj(ys, w, x, gate, g, shift, scale, router, sparse, tm=512):
    B, S, D = x.shape
    tm = min(tm, S)
    bm = gate.shape[0]
    b_router, wr_hi, wr_lo = router
    nt = S // tm
    mod_map = (lambda b, i: (b, 0, 0)) if bm > 1 else (lambda b, i: (0, 0, 0))
    mod_spec = pl.BlockSpec((1, 1, D), mod_map)
    row_map = lambda b, i: (b, i, 0)
    in_specs = ([pl.BlockSpec((1, tm, y.shape[-1]), row_map) for y in ys]
                + [pl.BlockSpec(memory_space=pltpu.SMEM),
                   pl.BlockSpec(w.shape, lambda b, i: (0, 0)),
                   pl.BlockSpec((1, tm, D), row_map), mod_spec,
                   pl.BlockSpec((1, D), lambda b, i: (0, 0)), mod_spec, mod_spec,
                   pl.BlockSpec(wr_hi.shape, lambda b, i: (0, 0)),
                   pl.BlockSpec(wr_lo.shape, lambda b, i: (0, 0))])
    args = list(ys) + [b_router, w, x, gate.reshape(bm, 1, D), g.reshape(1, D), shift.reshape(bm, 1, D),
                       scale.reshape(bm, 1, D), wr_hi, wr_lo]
    out_specs = [pl.BlockSpec((1, tm, D), row_map)]
    out_shape = [jax.ShapeDtypeStruct((B, S, D), F32)]
    out_specs.append(pl.BlockSpec((1, tm, D), row_map))
    out_shape.append(jax.ShapeDtypeStruct((B, S, D), BF16))
    if sparse:
        assert tm == MOE_TT
        tri = jnp.asarray(np.triu(np.ones((tm, tm), np.float32), 1)).astype(BF16)
        tril = jnp.asarray(np.tril(np.ones((N_EXPERTS, N_EXPERTS), np.float32), -1)).astype(BF16)
        args += [tri, tril]
        in_specs += [pl.BlockSpec(tri.shape, lambda b, i: (0, 0)),
                     pl.BlockSpec(tril.shape, lambda b, i: (0, 0))]
        out_specs += [pl.BlockSpec((ROUTE_FIELDS, tm), lambda b, i: (0, b * nt + i)),
                      pl.BlockSpec((tm, LANES), lambda b, i: (b * nt + i, 0)),
                      pl.BlockSpec((1, N_EXPERTS, LANES), lambda b, i: (b * nt + i, 0, 0))]
        out_shape += [jax.ShapeDtypeStruct((ROUTE_FIELDS, B * S), F32),
                      jax.ShapeDtypeStruct((B * S, LANES), F32),
                      jax.ShapeDtypeStruct((B * nt, N_EXPERTS, LANES), F32)]
    else:
        out_specs.append(pl.BlockSpec((1, tm, LANES), row_map))
        out_shape.append(jax.ShapeDtypeStruct((B, S, LANES), F32))
    return pl.pallas_call(
        functools.partial(_out_proj_kernel, len(ys), sparse),
        grid=(B, nt),
        in_specs=in_specs,
        out_specs=out_specs,
        out_shape=out_shape,
        compiler_params=_cparams(("parallel", "parallel")),
        name="out_proj_sparse" if sparse else "out_proj",
    )(*args)


def _moe_kernel(h_ref, comb_ref, wg_ref, wu_ref, wd_ref, x_ref, gate_ref, o_ref, acc_ref):
    e = pl.program_id(2)

    @pl.when(e == 0)
    def _():
        acc_ref[...] = jnp.zeros_like(acc_ref)

    h = h_ref[0]
    a = jnp.dot(h, wg_ref[0].astype(BF16), preferred_element_type=F32)
    u = jnp.dot(h, wu_ref[0].astype(BF16), preferred_element_type=F32)
    he = (a * jax.nn.sigmoid(a) * u).astype(BF16)
    y = jnp.dot(he, wd_ref[0].astype(BF16), preferred_element_type=F32)
    lane = lax.broadcasted_iota(jnp.int32, comb_ref.shape[1:], 1)
    c = jnp.sum(jnp.where(lane == e, comb_ref[0], 0.0), axis=-1, keepdims=True)
    acc_ref[...] += c * y

    @pl.when(e == pl.num_programs(2) - 1)
    def _():
        o_ref[0] = x_ref[0] + gate_ref[0] * acc_ref[...]


def _moe(h, comb, wg, wu, wd, x, gate, tm=1024):
    B, S, D = x.shape
    tm = min(tm, S)
    bm = gate.shape[0]
    mod_map = (lambda b, i, e: (b, 0, 0)) if bm > 1 else (lambda b, i, e: (0, 0, 0))
    row_map = lambda b, i, e: (b, i, 0)
    return pl.pallas_call(
        _moe_kernel,
        grid=(B, S // tm, N_EXPERTS),
        in_specs=[pl.BlockSpec((1, tm, D), row_map),
                  pl.BlockSpec((1, tm, LANES), row_map),
                  pl.BlockSpec((1, D, D_EXPERT), lambda b, i, e: (e, 0, 0)),
                  pl.BlockSpec((1, D, D_EXPERT), lambda b, i, e: (e, 0, 0)),
                  pl.BlockSpec((1, D_EXPERT, D), lambda b, i, e: (e, 0, 0)),
                  pl.BlockSpec((1, tm, D), row_map),
                  pl.BlockSpec((1, 1, D), mod_map)],
        out_specs=pl.BlockSpec((1, tm, D), row_map),
        out_shape=jax.ShapeDtypeStruct((B, S, D), F32),
        scratch_shapes=[pltpu.VMEM((tm, D), F32)],
        compiler_params=_cparams(("parallel", "parallel", "arbitrary")),
        name="moe",
    )(h, comb, wg, wu, wd, x, gate.reshape(bm, 1, D))


MOE_TM = 512
MOE_TG = MOE_TM // MOE_G


def _moe_rows(n_tok):
    rows = 2 * n_tok + (n_tok // MOE_TT) * N_EXPERTS * (MOE_G - 1) + N_EXPERTS * (MOE_TM - 1)
    return (rows + MOE_TM - 1) // MOE_TM * MOE_TM


def _moe_plan(ng):
    ng = ng[:, :, 0].astype(jnp.int32)
    n_tt = ng.shape[0]
    total = jnp.sum(ng, axis=0)
    region = (total + MOE_TG - 1) // MOE_TG * MOE_TG
    region_end = jnp.cumsum(region)
    region_start = region_end - region
    dst = region_start[None, :] + jnp.cumsum(ng, axis=0) - ng
    local = jnp.cumsum(ng, axis=1) - ng
    n_tiles = _moe_rows(n_tt * MOE_TT) // MOE_TM
    tile_first = jnp.arange(n_tiles, dtype=jnp.int32) * MOE_TG
    tile_expert = jnp.minimum(jnp.sum(region_end[None, :] <= tile_first[:, None], axis=1), N_EXPERTS - 1)
    n_valid = region_end[-1:] // MOE_TG
    i32 = lambda a: a.astype(jnp.int32).reshape(-1)
    return dict(ng=i32(ng), dst=i32(dst), local=i32(local), tile_groups=i32(jnp.sum(ng, axis=1)),
                pad_first=i32(region_start + total),
                pad_count=i32(region - total), tile_expert=i32(tile_expert), n_valid=i32(n_valid),
                n_tiles=n_tiles)


def _run_copies(plan_refs, tile, local_ref, sorted_ref, sem, to_sorted):
    ng_ref, dst_ref, loc_ref = plan_refs
    for e in range(N_EXPERTS):
        k = tile * N_EXPERTS + e
        loc, dst = loc_ref[k], dst_ref[k]

        def body(g, carry):
            lrows = local_ref.at[pl.ds(pl.multiple_of((loc + g) * MOE_G, MOE_G), MOE_G)]
            srows = sorted_ref.at[pl.ds(pl.multiple_of((dst + g) * MOE_G, MOE_G), MOE_G)]
            src, tgt = (lrows, srows) if to_sorted else (srows, lrows)
            pltpu.make_async_copy(src, tgt, sem).start()
            return carry
        lax.fori_loop(0, ng_ref[k], body, 0)


def _group_waits(n_groups, local_ref, sorted_ref, sem):
    def body(g, carry):
        pltpu.make_async_copy(sorted_ref.at[pl.ds(0, MOE_G)], local_ref.at[pl.ds(0, MOE_G)], sem).wait()
        return carry
    lax.fori_loop(0, n_groups, body, 0)


def _moe_dispatch_kernel(ng_ref, dst_ref, loc_ref, tot_ref, padf_ref, padc_ref, nv_ref,
                         h_ref, route_ref, xs_ref, local_ref, zero_ref, sem):
    i = pl.program_id(0)

    @pl.when(i == 0)
    def _():
        zero_ref[...] = jnp.zeros_like(zero_ref)
        n_pad = 0
        for e in range(N_EXPERTS):
            first = padf_ref[e]

            def body(g, carry):
                rows = xs_ref.at[pl.ds(pl.multiple_of((first + g) * MOE_G, MOE_G), MOE_G)]
                pltpu.make_async_copy(zero_ref.at[pl.ds(0, MOE_G)], rows, sem.at[2]).start()
                return carry
            lax.fori_loop(0, padc_ref[e], body, 0)
            n_pad = n_pad + padc_ref[e]

        def tile_copy(j):
            rows = xs_ref.at[pl.ds(pl.multiple_of(j * MOE_TM, MOE_TM), MOE_TM)]
            return pltpu.make_async_copy(zero_ref, rows, sem.at[3])

        n_tiles = xs_ref.shape[0] // MOE_TM
        lax.fori_loop(nv_ref[0], n_tiles, lambda j, c: (tile_copy(j).start(), c)[1], 0)
        _group_waits(n_pad, zero_ref, xs_ref, sem.at[2])
        lax.fori_loop(nv_ref[0], n_tiles, lambda j, c: (tile_copy(j).wait(), c)[1], 0)

    pos = lax.broadcasted_iota(jnp.int32, (MOE_LOCAL, MOE_TT), 0)
    p1 = route_ref[0:1, :].astype(jnp.int32)
    p2 = route_ref[1:2, :].astype(jnp.int32)
    pick = jnp.where((pos == p1) | (pos == p2), 1.0, 0.0).astype(BF16)
    slot = i % 2
    last = pl.num_programs(0) - 1

    @pl.when(i >= 2)
    def _():
        _group_waits(tot_ref[i - 2], local_ref.at[slot], xs_ref, sem.at[slot])

    local_ref[slot] = jnp.dot(pick, h_ref[0], preferred_element_type=F32)
    _run_copies((ng_ref, dst_ref, loc_ref), i, local_ref.at[slot], xs_ref, sem.at[slot], to_sorted=True)

    @pl.when(i == last)
    def _():
        _group_waits(tot_ref[i], local_ref.at[slot], xs_ref, sem.at[slot])

    @pl.when((i == last) & (last >= 1))
    def _():
        _group_waits(tot_ref[i - 1], local_ref.at[1 - slot], xs_ref, sem.at[1 - slot])


def _moe_dispatch(h, route_t, plan):
    B, S, D = h.shape
    nt = S // MOE_TT
    grid_spec = pltpu.PrefetchScalarGridSpec(
        num_scalar_prefetch=7,
        grid=(B * nt,),
        in_specs=[pl.BlockSpec((1, MOE_TT, D), lambda i, *_: (i // nt, i % nt, 0)),
                  pl.BlockSpec((ROUTE_FIELDS, MOE_TT), lambda i, *_: (0, i))],
        out_specs=pl.BlockSpec(memory_space=pl.ANY),
        scratch_shapes=[pltpu.VMEM((2, MOE_LOCAL, D), F32), pltpu.VMEM((MOE_TM, D), F32),
                        pltpu.SemaphoreType.DMA((4,))])
    return pl.pallas_call(
        _moe_dispatch_kernel,
        grid_spec=grid_spec,
        out_shape=jax.ShapeDtypeStruct((plan["n_tiles"] * MOE_TM, D), F32),
        compiler_params=_cparams(("arbitrary",)),
        name="moe_dispatch",
    )(plan["ng"], plan["dst"], plan["local"], plan["tile_groups"], plan["pad_first"], plan["pad_count"],
      plan["n_valid"], h, route_t)


def _moe_grouped_kernel(te_ref, nv_ref, x_ref, wg_ref, wu_ref, wd_ref, o_ref, wgb_ref, wub_ref, wdb_ref):
    j = pl.program_id(0)
    used = j < nv_ref[0]

    @pl.when(used & ((j == 0) | (te_ref[j] != te_ref[jnp.maximum(j - 1, 0)])))
    def _():
        wgb_ref[...] = wg_ref[0].astype(BF16)
        wub_ref[...] = wu_ref[0].astype(BF16)
        wdb_ref[...] = wd_ref[0].astype(BF16)

    @pl.when(used)
    def _():
        x = x_ref[...].astype(BF16)
        a = jnp.dot(x, wgb_ref[...], preferred_element_type=F32)
        u = jnp.dot(x, wub_ref[...], preferred_element_type=F32)
        he = (a * jax.nn.sigmoid(a) * u).astype(BF16)
        o_ref[...] = jnp.dot(he, wdb_ref[...], preferred_element_type=F32)

    @pl.when(jnp.logical_not(used))
    def _():
        o_ref[...] = jnp.zeros_like(o_ref)


def _moe_grouped(xs, plan, wg, wu, wd):
    n_tiles = plan["n_tiles"]
    D = D_MODEL
    tile = lambda j, nv: jnp.minimum(j, nv[0] - 1)
    grid_spec = pltpu.PrefetchScalarGridSpec(
        num_scalar_prefetch=2,
        grid=(n_tiles,),
        in_specs=[pl.BlockSpec((MOE_TM, D), lambda j, te, nv: (tile(j, nv), 0)),
                  pl.BlockSpec((1, D, D_EXPERT), lambda j, te, nv: (te[tile(j, nv)], 0, 0)),
                  pl.BlockSpec((1, D, D_EXPERT), lambda j, te, nv: (te[tile(j, nv)], 0, 0)),
                  pl.BlockSpec((1, D_EXPERT, D), lambda j, te, nv: (te[tile(j, nv)], 0, 0))],
        out_specs=pl.BlockSpec((MOE_TM, D), lambda j, te, nv: (j, 0)),
        scratch_shapes=[pltpu.VMEM((D, D_EXPERT), BF16), pltpu.VMEM((D, D_EXPERT), BF16),
                        pltpu.VMEM((D_EXPERT, D), BF16)])
    return pl.pallas_call(
        _moe_grouped_kernel,
        grid_spec=grid_spec,
        out_shape=jax.ShapeDtypeStruct((n_tiles * MOE_TM, D), F32),
        compiler_params=_cparams(("arbitrary",)),
        name="moe_grouped",
    )(plan["tile_expert"], plan["n_valid"], xs, wg, wu, wd)


def _moe_combine_kernel(final_norm, ng_ref, dst_ref, loc_ref, tot_ref, ys_ref, route_ref, x_ref, gate_ref,
                        *rest):
    if final_norm:
        fg_ref, o_ref, local_ref, sem = rest
    else:
        o_ref, local_ref, sem = rest
    i = pl.program_id(0)
    slot = i % 2
    plan_refs = (ng_ref, dst_ref, loc_ref)

    @pl.when(i == 0)
    def _():
        local_ref[...] = jnp.zeros_like(local_ref)
        _run_copies(plan_refs, 0, local_ref.at[0], ys_ref, sem.at[0], to_sorted=False)

    @pl.when(i + 1 < pl.num_programs(0))
    def _():
        _run_copies(plan_refs, i + 1, local_ref.at[1 - slot], ys_ref, sem.at[1 - slot], to_sorted=False)

    _group_waits(tot_ref[i], local_ref.at[slot], ys_ref, sem.at[slot])
    pos = lax.broadcasted_iota(jnp.int32, (MOE_TT, MOE_LOCAL), 1)
    route = route_ref[...]
    p1 = route[:, 0:1].astype(jnp.int32)
    p2 = route[:, 1:2].astype(jnp.int32)
    weigh = (jnp.where(pos == p1, route[:, 2:3], 0.0) + jnp.where(pos == p2, route[:, 3:4], 0.0)).astype(BF16)
    y = jnp.dot(weigh, local_ref[slot].astype(BF16), preferred_element_type=F32)
    o = x_ref[0] + gate_ref[0] * y
    if final_norm:
        o = o * lax.rsqrt(jnp.mean(o * o, axis=-1, keepdims=True) + EPS) * fg_ref[...]
    o_ref[0] = o


def _moe_combine(ys, route_c, plan, x, gate, final_g=None):
    B, S, D = x.shape
    nt = S // MOE_TT
    in_specs = [pl.BlockSpec(memory_space=pl.ANY),
                pl.BlockSpec((MOE_TT, LANES), lambda i, *_: (i, 0)),
                pl.BlockSpec((1, MOE_TT, D), lambda i, *_: (i // nt, i % nt, 0)),
                pl.BlockSpec((1, 1, D), lambda i, *_: (i // nt, 0, 0))]
    args = [ys, route_c, x, gate.reshape(B, 1, D)]
    if final_g is not None:
        in_specs.append(pl.BlockSpec((1, D), lambda i, *_: (0, 0)))
        args.append(final_g.reshape(1, D))
    grid_spec = pltpu.PrefetchScalarGridSpec(
        num_scalar_prefetch=4,
        grid=(B * nt,),
        in_specs=in_specs,
        out_specs=pl.BlockSpec((1, MOE_TT, D), lambda i, *_: (i // nt, i % nt, 0)),
        scratch_shapes=[pltpu.VMEM((2, MOE_LOCAL, D), F32), pltpu.SemaphoreType.DMA((2,))])
    return pl.pallas_call(
        functools.partial(_moe_combine_kernel, final_g is not None),
        grid_spec=grid_spec,
        out_shape=jax.ShapeDtypeStruct((B, S, D), F32),
        compiler_params=_cparams(("arbitrary",)),
        name="moe_combine",
    )(plan["ng"], plan["dst"], plan["local"], plan["tile_groups"], *args)


def _dft_tables():
    n1, n2, n = FFT_N1, FFT_N2, FFT_N1 * FFT_N2
    k1 = np.arange(n1)
    f1 = np.exp(-2j * np.pi * np.outer(k1, np.arange(n1)) / n1)
    tw = np.exp(-2j * np.pi * np.outer(np.arange(n2), k1) / n)
    ftw = f1[None, :, :] * tw[:, :, None]
    half = n1 // 2
    fh = ftw[:, :, :half]
    g_fwd = np.concatenate([np.concatenate([fh.real, -fh.imag], axis=2),
                            np.concatenate([fh.imag, fh.real], axis=2)], axis=1)
    back = ftw[:, :, ::-1][:, :, :half].copy()
    back[0] = np.roll(ftw[0], -1, axis=1)[:, ::-1][:, :half]
    back[0][:, 0] = 0.0
    fk = np.concatenate([fh, back], axis=2)
    g_real = np.concatenate([fk.real, fk.imag], axis=1)
    gi = np.conj(np.transpose(fh, (0, 2, 1))) / n
    g_inv = np.concatenate([np.concatenate([gi.real, -gi.imag], axis=2),
                            np.concatenate([gi.imag, gi.real], axis=2)], axis=1)
    f2 = np.exp(-2j * np.pi * np.outer(np.arange(n2), np.arange(n2)) / n2)
    f2_fwd = np.block([[f2.real, -f2.imag], [f2.imag, f2.real]])
    f2c = np.conj(f2)
    f2_inv = np.block([[f2c.real, -f2c.imag], [f2c.imag, f2c.real]])
    as_bf = lambda a: jnp.asarray(a, dtype=F32).astype(BF16)
    return as_bf(g_fwd), as_bf(g_real), as_bf(g_inv), as_bf(f2_fwd), as_bf(f2_inv)


def _fft_fast_stage(stage_ref, k1, f2):
    slab = 2 * FFT_N1
    m = jnp.concatenate([stage_ref[pl.ds(k1, FFT_N2, stride=slab), :],
                         stage_ref[pl.ds(FFT_N1 + k1, FFT_N2, stride=slab), :]], axis=0)
    return jnp.dot(f2, m.astype(BF16), preferred_element_type=F32)


def _filter_fft_kernel(hf_ref, hb_ref, inv_ref, g_ref, f2_ref, h_ref, stage_ref):
    slab = 2 * FFT_N1
    half = FFT_N1 // 2
    for n2 in range(FFT_N2):
        x = jnp.concatenate([hf_ref[pl.ds(n2, half, stride=FFT_N2), :],
                             hb_ref[pl.ds((FFT_N2 - n2) % FFT_N2, half, stride=FFT_N2), :]], axis=0)
        stage_ref[n2 * slab:(n2 + 1) * slab, :] = jnp.dot(g_ref[n2], x.astype(BF16),
                                                          preferred_element_type=F32)
    f2 = f2_ref[...]
    inv = inv_ref[...]
    for k1 in range(FFT_N1):
        h_ref[0, k1] = (_fft_fast_stage(stage_ref, k1, f2) * inv).astype(h_ref.dtype)


def _filter_fft(taps, inv_norm, g_real, f2_fwd, ct=LANES):
    L, cols = taps.shape
    C = D_CH
    n_ord = cols // (2 * C)
    nc = C // ct
    once = pl.Buffered(1)
    return pl.pallas_call(
        _filter_fft_kernel,
        grid=(n_ord, nc),
        in_specs=[pl.BlockSpec((L, ct), lambda o, c: (0, o * nc + c)),
                  pl.BlockSpec((L, ct), lambda o, c: (0, (n_ord + o) * nc + c)),
                  pl.BlockSpec((1, ct), lambda o, c: (0, o * nc + c)),
                  pl.BlockSpec(g_real.shape, lambda o, c: (0, 0, 0), pipeline_mode=once),
                  pl.BlockSpec(f2_fwd.shape, lambda o, c: (0, 0), pipeline_mode=once)],
        out_specs=pl.BlockSpec((1, FFT_N1, 2 * FFT_N2, ct), lambda o, c: (o, 0, 0, c)),
        out_shape=jax.ShapeDtypeStruct((n_ord, FFT_N1, 2 * FFT_N2, C), BF16),
        scratch_shapes=[pltpu.VMEM((FFT_N2 * 2 * FFT_N1, ct), F32)],
        compiler_params=_cparams(("parallel", "parallel")),
        name="filter_fft",
    )(taps, taps, inv_norm, g_real, f2_fwd)


def _strided_rows(ref, member, n2, taps):
    half = FFT_N1 // 2
    at = lambda start: ref[member, pl.ds(start, half, stride=FFT_N2), :]
    cur = at(n2)
    if taps is None:
        return cur
    w_ref, b_ref = taps
    n1 = lax.broadcasted_iota(jnp.int32, cur.shape, 0)
    if n2 > 0:
        prev = at(n2 - 1)
    else:
        prev = jnp.where(n1 == 0, 0.0, pltpu.roll(at(FFT_N2 - 1), 1, axis=0))
    if n2 < FFT_N2 - 1:
        nxt = at(n2 + 1)
    else:
        nxt = jnp.where(n1 == half - 1, 0.0, pltpu.roll(at(0), half - 1, axis=0))
    return prev * w_ref[0:1, :] + cur * w_ref[1:2, :] + nxt * w_ref[2:3, :] + b_ref[...]


def _hyena_conv_kernel(conv_z, *refs):
    it = iter(refs)
    z_ref, gate_ref = next(it), next(it)
    z_taps = (next(it), next(it)) if conv_z else None
    gate_taps = (next(it), next(it))
    hb_ref, spec_ref, gf_ref, gi_ref, f2f_ref, f2i_ref, o_ref, stage_ref = it
    half = FFT_N1 // 2
    slab = 2 * FFT_N1
    for n2 in range(FFT_N2):
        x = jnp.concatenate([_strided_rows(z_ref, 0, n2, z_taps), _strided_rows(z_ref, 1, n2, z_taps)],
                            axis=0).astype(BF16)
        stage_ref[n2 * slab:(n2 + 1) * slab, :] = jnp.dot(gf_ref[n2], x, preferred_element_type=F32)
    f2f = f2f_ref[...]
    f2i = f2i_ref[...]
    for k1 in range(FFT_N1):
        zf = _fft_fast_stage(stage_ref, k1, f2f)
        zr, zi = zf[:FFT_N2], zf[FFT_N2:]
        hr = spec_ref[0, k1, :FFT_N2, :].astype(F32)
        hi = spec_ref[0, k1, FFT_N2:, :].astype(F32)
        p = jnp.concatenate([zr * hr - zi * hi, zr * hi + zi * hr], axis=0).astype(BF16)
        q = jnp.dot(f2i, p, preferred_element_type=F32)
        stage_ref[pl.ds(k1, FFT_N2, stride=slab), :] = q[:FFT_N2]
        stage_ref[pl.ds(FFT_N1 + k1, FFT_N2, stride=slab), :] = q[FFT_N2:]
    hb = hb_ref[...]
    for n2 in range(FFT_N2):
        y_in = stage_ref[n2 * slab:(n2 + 1) * slab, :].astype(BF16)
        y = jnp.dot(gi_ref[n2], y_in, preferred_element_type=F32)
        for m in range(2):
            zm = _strided_rows(z_ref, m, n2, z_taps)
            gm = _strided_rows(gate_ref, m, n2, gate_taps)
            o_ref[m, pl.ds(n2, half, stride=FFT_N2), :] = gm * (y[m * half:(m + 1) * half] + zm * hb)


def _hyena_conv(z, z_blk, gate, gate_blk, short_w, short_b, conv_z, hbias, spec, order, tabs, ct=LANES):
    B, L, _ = z.shape
    C = D_CH
    g_fwd, _, g_inv, f2_fwd, f2_inv = tabs
    once = pl.Buffered(1)
    const3 = lambda a: pl.BlockSpec(a.shape, lambda c, p: (0, 0, 0), pipeline_mode=once)
    const2 = lambda a: pl.BlockSpec(a.shape, lambda c, p: (0, 0), pipeline_mode=once)
    taps_specs = lambda blk: [pl.BlockSpec((3, ct), lambda c, p: (0, blk + c)),
                              pl.BlockSpec((1, ct), lambda c, p: (0, blk + c))]
    in_specs = [pl.BlockSpec((2, L, ct), lambda c, p: (p, 0, z_blk + c)),
                pl.BlockSpec((2, L, ct), lambda c, p: (p, 0, gate_blk + c))]
    args = [z, gate]
    if conv_z:
        in_specs += taps_specs(z_blk)
        args += [short_w, short_b]
    in_specs += taps_specs(gate_blk)
    args += [short_w, short_b]
    in_specs += [pl.BlockSpec((1, ct), lambda c, p: (0, c)),
                 pl.BlockSpec((1, FFT_N1, 2 * FFT_N2, ct), lambda c, p: (order, 0, 0, c), pipeline_mode=once),
                 const3(g_fwd), const3(g_inv), const2(f2_fwd), const2(f2_inv)]
    args += [hbias, spec, g_fwd, g_inv, f2_fwd, f2_inv]
    return pl.pallas_call(
        functools.partial(_hyena_conv_kernel, conv_z),
        grid=(C // ct, B // 2),
        in_specs=in_specs,
        out_specs=pl.BlockSpec((2, L, ct), lambda c, p: (p, 0, c)),
        out_shape=jax.ShapeDtypeStruct((B, L, C), F32),
        scratch_shapes=[pltpu.VMEM((FFT_N2 * 2 * FFT_N1, ct), F32)],
        compiler_params=_cparams(("parallel", "arbitrary")),
        name="hyena_conv",
    )(*args)


def _hyena_filters(L, w1, b1, f1, w2, b2, f2, w3, b3):
    t = jnp.arange(L, dtype=F32)
    tn = t / max(L - 1, 1)
    bands = jnp.linspace(1e-4, HY_BANDS - 1, HY_BANDS, dtype=F32)
    ang = 2.0 * math.pi * t[:, None] * bands[None] / L
    feats = jnp.concatenate([tn[:, None], jnp.cos(ang), jnp.sin(ang)], axis=-1)
    h = jnp.sin(f1 * (feats @ w1 + b1))
    h = jnp.sin(f2 * (h @ w2 + b2))
    deltas = jnp.abs(jnp.linspace(HY_MIN_DECAY, HY_MAX_DECAY, D_CH, dtype=F32))
    decay = jnp.exp(-tn[:, None] * deltas[None])
    n_rep = w3.shape[1] // D_CH
    taps = (_mm_f32(h, w3, 512) + b3) * jnp.tile(decay, (1, n_rep))
    l1 = jnp.sum(jnp.abs(taps), axis=0)
    l1 = l1[:n_rep // 2 * D_CH] + l1[n_rep // 2 * D_CH:]
    return taps, (1.0 / (l1 + EPS))[None]


def _dup_heads(w):
    a, b = w[:, :HEAD_DIM], w[:, HEAD_DIM:]
    return jnp.concatenate([a, a, b, b], axis=1)


def _rope_tables(S):
    t = jnp.arange(S)
    row = (t // GRID_W).astype(F32)
    col = (t % GRID_W).astype(F32)
    half = HEAD_DIM // 2
    inv = ROPE_THETA ** (-jnp.arange(0, half, 2, dtype=F32) / half)
    ar = row[:, None] * inv[None]
    ac = col[:, None] * inv[None]
    cos = jnp.concatenate([jnp.cos(ar), jnp.cos(ar), jnp.cos(ac), jnp.cos(ac)], axis=-1)
    sin = jnp.concatenate([-jnp.sin(ar), jnp.sin(ar), -jnp.sin(ac), jnp.sin(ac)], axis=-1)
    return jnp.tile(cos, (1, 2)), jnp.tile(sin, (1, 2))


def _head_mean_matrix(width):
    blk = np.kron(np.eye(width // HEAD_DIM), np.full((HEAD_DIM, HEAD_DIM), 1.0 / HEAD_DIM))
    return jnp.asarray(blk, dtype=F32).astype(BF16)


def kernel(x, c, ctx, c_ctx, w_ada, b_ada, norm_g, final_g, w_in_even, w_out_even, a_sink, b_rpb, w_in_odd, w_out_odd, c_qnorm, c_knorm, hy_short_w, hy_short_b, hy_w1, hy_b1, hy_f1, hy_w2, hy_b2, hy_f2, hy_w3, hy_b3, hy_bias, w_router, b_router, moe_wg, moe_wu, moe_wd):
    B, S, D = x.shape
    depth = w_ada.shape[0]
    rope = _rope_tables(S)
    wr_pad = jnp.pad(w_router.astype(F32), ((0, 0), (0, LANES - N_EXPERTS)))
    wr_hi = wr_pad.astype(BF16)
    router = (b_router.astype(F32), wr_hi, (wr_pad - wr_hi.astype(F32)).astype(BF16))

    mod_in = jnp.concatenate([jax.nn.silu(c), jax.nn.silu(c_ctx)[None],
                              jnp.zeros((8 - B - 1, D), F32)], axis=0)
    xc = ctx
    for l in range(depth):
        need_ctx = l < depth - 1
        mod = _mm_f32(mod_in, w_ada[l], 1536) + b_ada[l]
        mx = mod[:B].reshape(B, 6, D)
        mc = mod[B].reshape(6, D)
        i = l // 2
        if l % 2 == 0:
            w = w_in_even[i].astype(BF16)
            w_all = jnp.concatenate([w[:, :512], _dup_heads(w[:, 512:640]), _dup_heads(w[:, 640:768]),
                                     w[:, 768:]], axis=1)
            segs_x = ((512, "rope", ATTN_SCALE), (256, "rope", 1.0), (256, "plain", 1.0),
                      (512, "plain", ATTN_SCALE), (512, "plain", 1.0), (512, "plain", 1.0))
            aq, akd, avd, bq, bk, bv = _norm_proj(x, norm_g[l, 0], mx[:, 0], mx[:, 1], w_all, segs_x,
                                                  rope=rope)
            segs_c = tuple((wd, "plain", m) for wd, _, m in segs_x)
            caq, cakd, cavd, cbq, cbk, cbv = _norm_proj(xc, norm_g[l, 0], mc[0:1], mc[1:2], w_all, segs_c)
            ya = _window_attn(aq, akd, avd, cakd, cavd, a_sink[i].astype(F32))
            yb = _nbr_attn(bq, bk, bv, cbk, cbv, _nbr_bias_table(b_rpb[i], S // GRID_W))
            ys = [ya, yb]
            w_out = w_out_even[i].astype(BF16)
            if need_ctx:
                yc = [_ctx_attn(a_sink[i].astype(F32), caq, cakd, cavd, cbq, cbk, cbv)]
        else:
            w = w_in_odd[i].astype(BF16)
            w_all = jnp.concatenate([w[:, :512], _dup_heads(w[:, 512:640]), _dup_heads(w[:, 640:768]),
                                     w[:, 768:]], axis=1)
            gains = jnp.concatenate([jnp.tile(c_qnorm[i], 8), jnp.tile(c_knorm[i], 4)])[None].astype(F32)
            norm = (_head_mean_matrix(512), gains)
            q_mult = ATTN_SCALE * math.log2(math.e)
            segs_x = ((512, "normrope", q_mult), (256, "normrope", 1.0), (256, "plain", 1.0),
                      (3 * D_CH, "plain", 1.0, F32))
            qx, kxd, vxd, ux = _norm_proj(x, norm_g[l, 0], mx[:, 0], mx[:, 1], w_all, segs_x,
                                          rope=rope, norm=norm)
            w_c = w_all[:, 512:1024]
            norm_c = (_head_mean_matrix(512), jnp.tile(c_knorm[i], 4)[None].astype(F32))
            kcd, vcd = _norm_proj(xc, norm_g[l, 0], mc[0:1], mc[1:2], w_c,
                                  ((256, "norm", 1.0), (256, "plain", 1.0)), norm=norm_c)
            logit_bound = (1.02 * HEAD_DIM * q_mult * jnp.max(jnp.abs(c_qnorm[i]))
                           * jnp.max(jnp.abs(c_knorm[i])))
            y_attn = lax.cond(logit_bound <= FULL_NOMAX_LOG2_BOUND,
                              lambda *a: _full_attn(*a, bounded=True),
                              lambda *a: _full_attn(*a, bounded=False),
                              qx, kxd, vxd, kcd, vcd)
            sw, sb = hy_short_w[i].astype(F32), hy_short_b[i].astype(F32)[None]
            tabs = _dft_tables()
            taps, inv_norm = _hyena_filters(S, hy_w1[i], hy_b1[i], hy_f1[i], hy_w2[i], hy_b2[i],
                                            hy_f2[i], hy_w3[i], hy_b3[i])
            spec = _filter_fft(taps, inv_norm, tabs[1], tabs[3])
            blocks = D_CH // LANES
            z = _hyena_conv(ux, 0, ux, blocks, sw, sb, True, hy_bias[i, 0:1], spec, 0, tabs)
            z = _hyena_conv(z, 0, ux, 2 * blocks, sw, sb, False, hy_bias[i, 1:2], spec, 1, tabs)
            ys = [y_attn, z.astype(BF16)]
            w_out = w_out_odd[i].astype(BF16)
            if need_ctx:
                raise NotImplementedError("context update of an odd layer is not needed at this depth")

        x, hx, route_t, route_c, ng = _out_proj(ys, w_out, x, mx[:, 2], norm_g[l, 1], mx[:, 3], mx[:, 4],
                                                router, sparse=True)
        wg, wu, wd = moe_wg[l], moe_wu[l], moe_wd[l]
        if need_ctx:
            xc, hc, comb_c = _out_proj(yc, w_out, xc, mc[2:3], norm_g[l, 1], mc[3:4], mc[4:5], router,
                                       sparse=False)
            flat = lambda a: a.reshape(1, -1, a.shape[-1])
            xc = _moe(flat(hc), flat(comb_c), wg, wu, wd, flat(xc), mc[5:6]).reshape(xc.shape)
        plan = _moe_plan(ng)
        sorted_rows = _moe_dispatch(hx, route_t, plan)
        x = _moe_combine(_moe_grouped(sorted_rows, plan, wg, wu, wd), route_c, plan, x, mx[:, 5],
                         final_g=None if need_ctx else final_g)
    return x
```

```python
import functools
import math

import numpy as np
import jax
import jax.numpy as jnp
from jax import lax
from jax.experimental import pallas as pl
from jax.experimental.pallas import tpu as pltpu

F32 = jnp.float32
BF16 = jnp.bfloat16

D_MODEL = 1024
GRID_W = 64
CTX_LEN = 256
HEAD_DIM = 64
ROPE_THETA = 10000.0
EPS = 1e-6
ATTN_SCALE = HEAD_DIM ** -0.5
A_WINDOW = 128
A_BLOCK = 128
B_WIN_H = 8
B_WIN_W = 16
D_CH = 512
HY_BANDS = 16
HY_MAX_DECAY = math.log(1e-2) / 0.3
HY_MIN_DECAY = math.log(1e-2) / 1.5
N_EXPERTS = 16
N_GROUPS = 4
EXPERTS_PER_GROUP = N_EXPERTS // N_GROUPS
TOP_K = 2
D_EXPERT = 512

LANES = 128
NEG = -1e30
VMEM_LIMIT = 48 * 1024 * 1024

FFT_N1 = 64
FFT_N2 = 128


def _cparams(sem):
    return pltpu.CompilerParams(dimension_semantics=sem, vmem_limit_bytes=VMEM_LIMIT)


def _mm_f32_kernel(x_ref, w_ref, o_ref):
    o_ref[...] = jnp.dot(x_ref[...], w_ref[...], preferred_element_type=F32)


def _mm_f32(x, w, tn):
    M, K = x.shape
    N = w.shape[1]
    return pl.pallas_call(
        _mm_f32_kernel,
        grid=(N // tn,),
        in_specs=[pl.BlockSpec((M, K), lambda j: (0, 0)),
                  pl.BlockSpec((K, tn), lambda j: (0, j))],
        out_specs=pl.BlockSpec((M, tn), lambda j: (0, j)),
        out_shape=jax.ShapeDtypeStruct((M, N), F32),
        compiler_params=_cparams(("arbitrary",)),
        name="mm_f32",
    )(x, w)


def _swap16(y):
    n = y.shape[-1]
    lane = lax.broadcasted_iota(jnp.int32, y.shape, y.ndim - 1)
    up = pltpu.roll(y, n - 16, axis=y.ndim - 1)
    dn = pltpu.roll(y, 16, axis=y.ndim - 1)
    return jnp.where((lane % 32) < 16, up, dn)


def _tile_lanes(t, width):
    reps = width // t.shape[-1]
    return t if reps == 1 else jnp.concatenate([t] * reps, axis=-1)


def _norm_proj_kernel(segs, has_rope, has_norm, *refs):
    it = iter(refs)
    x_ref, g_ref, shift_ref, scale_ref, w_ref = (next(it) for _ in range(5))
    cos_ref = sin_ref = bd_ref = gain_ref = None
    if has_rope:
        cos_ref, sin_ref = next(it), next(it)
    if has_norm:
        bd_ref, gain_ref = next(it), next(it)
    out_refs = list(it)

    x = x_ref[0]
    ms = jnp.mean(x * x, axis=-1, keepdims=True)
    h = x * lax.rsqrt(ms + EPS) * g_ref[...]
    h = h * (1.0 + scale_ref[0]) + shift_ref[0]
    y = jnp.dot(h.astype(BF16), w_ref[...], preferred_element_type=F32)

    off = 0
    goff = 0
    for (width, kind, mult, *_), o_ref in zip(segs, out_refs):
        ys = y[:, off:off + width]
        if kind in ("norm", "normrope"):
            bd = bd_ref[...][:width, :width]
            hms = jnp.dot((ys * ys).astype(BF16), bd, preferred_element_type=F32)
            ys = ys * lax.rsqrt(hms + EPS) * gain_ref[:, goff:goff + width]
            goff += width
        if kind in ("rope", "normrope"):
            c = _tile_lanes(cos_ref[...], width)
            s = _tile_lanes(sin_ref[...], width)
            ys = ys * c + _swap16(ys) * s
        if mult != 1.0:
            ys = ys * mult
        o_ref[0] = ys.astype(o_ref.dtype)
        off += width


def _norm_proj(x, g, shift, scale, w, segs, rope=None, norm=None, tm=512):
    B, S, D = x.shape
    N = w.shape[1]
    tm = min(tm, S)
    bm = shift.shape[0]
    mod_map = (lambda b, i: (b, 0, 0)) if bm > 1 else (lambda b, i: (0, 0, 0))
    args = [x, g.reshape(1, D), shift.reshape(bm, 1, D), scale.reshape(bm, 1, D), w]
    in_specs = [pl.BlockSpec((1, tm, D), lambda b, i: (b, i, 0)),
                pl.BlockSpec((1, D), lambda b, i: (0, 0)),
                pl.BlockSpec((1, 1, D), mod_map),
                pl.BlockSpec((1, 1, D), mod_map),
                pl.BlockSpec((D, N), lambda b, i: (0, 0))]
    if rope is not None:
        args += [rope[0], rope[1]]
        in_specs += [pl.BlockSpec((tm, LANES), lambda b, i: (i, 0))] * 2
    if norm is not None:
        args += [norm[0], norm[1]]
        in_specs += [pl.BlockSpec(norm[0].shape, lambda b, i: (0, 0)),
                     pl.BlockSpec(norm[1].shape, lambda b, i: (0, 0))]
    out_shape = [jax.ShapeDtypeStruct((B, S, seg[0]), seg[3] if len(seg) > 3 else BF16) for seg in segs]
    out_specs = [pl.BlockSpec((1, tm, seg[0]), lambda b, i: (b, i, 0)) for seg in segs]
    return pl.pallas_call(
        functools.partial(_norm_proj_kernel, segs, rope is not None, norm is not None),
        grid=(B, S // tm),
        in_specs=in_specs,
        out_specs=out_specs,
        out_shape=out_shape,
        compiler_params=_cparams(("parallel", "parallel")),
        name="norm_proj",
    )(*args)


def _half_mask(shape):
    return lax.broadcasted_iota(jnp.int32, shape, len(shape) - 1) < HEAD_DIM


def _stack_halves(qp):
    lo = _half_mask(qp.shape)
    zero = jnp.zeros_like(qp)
    return jnp.concatenate([jnp.where(lo, qp, zero), jnp.where(lo, zero, qp)], axis=0)


def _merge_halves(o, m):
    return jnp.where(_half_mask((m, LANES)), o[:m], o[m:])


def _scores(q, k):
    return lax.dot_general(q, k, (((1,), (1,)), ((), ())), preferred_element_type=F32)


def _joint_softmax_pv(score_parts, value_parts, extra_logit=None):
    m = functools.reduce(jnp.maximum, [jnp.max(s, axis=-1, keepdims=True) for s in score_parts])
    if extra_logit is not None:
        m = jnp.maximum(m, extra_logit)
    den = jnp.exp(extra_logit - m) if extra_logit is not None else 0.0
    acc = None
    for s, v in zip(score_parts, value_parts):
        p = jnp.exp(s - m)
        den = den + jnp.sum(p, axis=-1, keepdims=True)
        pv = jnp.dot(p.astype(BF16), v, preferred_element_type=F32)
        acc = pv if acc is None else acc + pv
    return acc / den


def _sink_column(sink_ref, first_head, n_heads, rows_per_head):
    rows = lax.broadcasted_iota(jnp.int32, (n_heads * rows_per_head, 1), 0)
    col = jnp.zeros((n_heads * rows_per_head, 1), F32)
    for j in range(n_heads):
        in_head = (rows >= j * rows_per_head) & (rows < (j + 1) * rows_per_head)
        col = jnp.where(in_head, sink_ref[first_head + j], col)
    return col


def _window_attn_kernel(seq_len, sink_ref, q_ref, kp_ref, kc_ref, kn_ref, vp_ref, vc_ref, vn_ref,
                        ck_ref, cv_ref, o_ref):
    i = pl.program_id(1)
    blk = A_BLOCK
    q = q_ref[0]
    rows = lax.broadcasted_iota(jnp.int32, (4 * blk, 3 * blk), 0) % blk
    rel = lax.broadcasted_iota(jnp.int32, (4 * blk, 3 * blk), 1) - blk
    gpos = i * blk + rel
    valid = (jnp.abs(rows - rel) <= A_WINDOW) & (gpos >= 0) & (gpos < seq_len)
    outs = []
    for g in range(2):
        ls = slice(g * LANES, (g + 1) * LANES)
        k_loc = jnp.concatenate([kp_ref[0][:, ls], kc_ref[0][:, ls], kn_ref[0][:, ls]], axis=0)
        v_loc = jnp.concatenate([vp_ref[0][:, ls], vc_ref[0][:, ls], vn_ref[0][:, ls]], axis=0)
        qs = jnp.concatenate([_stack_halves(q[:, (2 * g + j) * LANES:(2 * g + j + 1) * LANES])
                              for j in range(2)], axis=0)
        s_loc = jnp.where(valid, _scores(qs, k_loc), NEG)
        s_ctx = _scores(qs, ck_ref[0][:, ls])
        sink = _sink_column(sink_ref, 4 * g, 4, blk)
        o = _joint_softmax_pv([s_loc, s_ctx], [v_loc, cv_ref[0][:, ls]], sink)
        outs += [_merge_halves(o[:2 * blk], blk), _merge_halves(o[2 * blk:], blk)]
    o_ref[0] = jnp.concatenate(outs, axis=-1).astype(o_ref.dtype)


def _window_attn(q, kd, vd, ckd, cvd, sink):
    B, S, _ = q.shape
    nb = S // A_BLOCK
    kv_spec = lambda f: pl.BlockSpec((1, A_BLOCK, 2 * LANES), f)
    prev_map = lambda b, i: (b, jnp.maximum(i - 1, 0), 0)
    cur_map = lambda b, i: (b, i, 0)
    next_map = lambda b, i: (b, jnp.minimum(i + 1, nb - 1), 0)
    ctx_spec = pl.BlockSpec((1, CTX_LEN, 2 * LANES), lambda b, i: (b, 0, 0))
    return pl.pallas_call(
        functools.partial(_window_attn_kernel, S),
        grid=(B, nb),
        in_specs=[pl.BlockSpec(memory_space=pltpu.SMEM),
                  pl.BlockSpec((1, A_BLOCK, 4 * LANES), cur_map),
                  kv_spec(prev_map), kv_spec(cur_map), kv_spec(next_map),
                  kv_spec(prev_map), kv_spec(cur_map), kv_spec(next_map),
                  ctx_spec, ctx_spec],
        out_specs=pl.BlockSpec((1, A_BLOCK, 4 * LANES), cur_map),
        out_shape=jax.ShapeDtypeStruct((B, S, 4 * LANES), BF16),
        compiler_params=_cparams(("parallel", "parallel")),
        name="window_attn",
    )(sink, q, kd, kd, kd, vd, vd, vd, ckd, cvd)


NBR_ROWS = 4
NBR_KROWS = 12
NBR_PAIRS = 2


def _nbr_start_row(i, n_rows):
    return jnp.clip(i * NBR_ROWS - B_WIN_H // 2, 0, n_rows - NBR_KROWS)


def _nbr_attn_kernel(n_rows, q_ref, k_ref, v_ref, ck_ref, cv_ref, bias_ref, o_ref):
    i = pl.program_id(2)
    nq = NBR_ROWS * GRID_W
    nk = NBR_KROWS * GRID_W
    start = pl.multiple_of(_nbr_start_row(i, n_rows) * GRID_W, GRID_W)
    outs = []
    for pp in range(NBR_PAIRS):
        ls = slice(pp * LANES, (pp + 1) * LANES)
        k_loc = k_ref[0, pl.ds(start, nk), ls]
        v_loc = v_ref[0, pl.ds(start, nk), ls]
        qs = _stack_halves(q_ref[0][:, ls])
        s_loc = _scores(qs, k_loc) + bias_ref[0, 2 * pp:2 * pp + 2].reshape(2 * nq, nk)
        s_ctx = _scores(qs, ck_ref[0][:, ls])
        o = _joint_softmax_pv([s_loc, s_ctx], [v_loc, cv_ref[0][:, ls]])
        outs.append(_merge_halves(o, nq))
    o_ref[0] = jnp.concatenate(outs, axis=-1).astype(o_ref.dtype)


def _nbr_bias_table(rpb, n_rows):
    kh = B_WIN_H
    n_heads = rpb.shape[0]
    col = np.arange(GRID_W)
    cs = np.clip(col - B_WIN_W // 2, 0, GRID_W - B_WIN_W)
    col_ok = (col[None, :] >= cs[:, None]) & (col[None, :] < cs[:, None] + B_WIN_W)
    dc = np.clip(col[None, :] - col[:, None], -(B_WIN_W - 1), B_WIN_W - 1) + B_WIN_W - 1
    pick = (dc[..., None] == np.arange(2 * B_WIN_W - 1)).astype(np.float32)
    by_col = jnp.einsum("hdc,qkc->hdqk", rpb.astype(F32), pick, precision=lax.Precision.HIGHEST)
    by_col = jnp.where(col_ok[None, None], by_col, NEG)
    masked = jnp.full((n_heads, GRID_W, GRID_W), NEG, F32)
    tabs = []
    for r0 in (0, NBR_ROWS, n_rows - NBR_ROWS):
        start = int(np.clip(r0 - kh // 2, 0, n_rows - NBR_KROWS))
        per_row = []
        for ri in range(NBR_ROWS):
            r = r0 + ri
            rs = int(np.clip(r - kh // 2, 0, n_rows - kh))
            slabs = [by_col[:, start + kri - r + kh - 1] if rs <= start + kri < rs + kh else masked
                     for kri in range(NBR_KROWS)]
            per_row.append(jnp.stack(slabs, axis=2))
        tabs.append(jnp.stack(per_row, axis=1).reshape(n_heads, NBR_ROWS * GRID_W, NBR_KROWS * GRID_W))
    return jnp.stack(tabs)


def _nbr_attn(q, k, v, ck, cv, bias):
    B, S, _ = q.shape
    n_rows = S // GRID_W
    nsteps = n_rows // NBR_ROWS
    nq = NBR_ROWS * GRID_W
    nk = NBR_KROWS * GRID_W
    pat = lambda i: jnp.where(i == 0, 0, jnp.where(i == nsteps - 1, 2, 1))
    wl = NBR_PAIRS * LANES
    return pl.pallas_call(
        functools.partial(_nbr_attn_kernel, n_rows),
        grid=(B, 4 // NBR_PAIRS, nsteps),
        in_specs=[pl.BlockSpec((1, nq, wl), lambda b, p, i: (b, i, p)),
                  pl.BlockSpec((1, S, wl), lambda b, p, i: (b, 0, p)),
                  pl.BlockSpec((1, S, wl), lambda b, p, i: (b, 0, p)),
                  pl.BlockSpec((1, CTX_LEN, wl), lambda b, p, i: (b, 0, p)),
                  pl.BlockSpec((1, CTX_LEN, wl), lambda b, p, i: (b, 0, p)),
                  pl.BlockSpec((1, 2 * NBR_PAIRS, nq, nk), lambda b, p, i: (pat(i), p, 0, 0))],
        out_specs=pl.BlockSpec((1, nq, wl), lambda b, p, i: (b, i, p)),
        out_shape=jax.ShapeDtypeStruct((B, S, 4 * LANES), BF16),
        compiler_params=_cparams(("parallel", "parallel", "arbitrary")),
        name="nbr_attn",
    )(q, k, v, ck, cv, bias)


FULL_TQ = 512
FULL_TK = 512
FULL_NOMAX_LOG2_BOUND = 60.0


def _full_attn_kernel(bounded, q_ref, k_ref, v_ref, ck_ref, cv_ref, o_ref):
    tq = FULL_TQ
    q = q_ref[0]
    qs = jnp.concatenate([_stack_halves(q[:, :LANES]), _stack_halves(q[:, LANES:])], axis=0)

    def step_bounded(acc, k, v):
        v_ones = jnp.where(_half_mask(v.shape), v, jnp.ones_like(v))
        p = jnp.exp2(_scores(qs, k))
        return acc + jnp.dot(p.astype(BF16), v_ones, preferred_element_type=F32)

    def step_online(carry, k, v):
        m, l, acc = carry
        s = _scores(qs, k)
        m_new = jnp.maximum(m, jnp.max(s, axis=-1, keepdims=True))
        alpha = jnp.exp2(m - m_new)
        p = jnp.exp2(s - m_new)
        l = l * alpha + jnp.sum(p, axis=-1, keepdims=True)
        acc = acc * alpha + jnp.dot(p.astype(BF16), v, preferred_element_type=F32)
        return m_new, l, acc

    step = step_bounded if bounded else step_online
    init = jnp.zeros((4 * tq, LANES), F32)
    if not bounded:
        init = (jnp.full((4 * tq, 1), NEG, F32), jnp.zeros((4 * tq, 1), F32), init)
    carry = step(init, ck_ref[0], cv_ref[0])

    def body(j, carry):
        off = pl.multiple_of(j * FULL_TK, FULL_TK)
        return step(carry, k_ref[0, pl.ds(off, FULL_TK), :], v_ref[0, pl.ds(off, FULL_TK), :])

    carry = lax.fori_loop(0, k_ref.shape[1] // FULL_TK, body, carry)
    if bounded:
        o = carry * pltpu.roll(1.0 / carry, HEAD_DIM, axis=1)
        merge = lambda a, b: jnp.where(_half_mask((tq, LANES)), a, pltpu.roll(b, HEAD_DIM, axis=1))
        pairs = [merge(o[0:tq], o[tq:2 * tq]), merge(o[2 * tq:3 * tq], o[3 * tq:])]
    else:
        o = carry[2] / carry[1]
        pairs = [_merge_halves(o[:2 * tq], tq), _merge_halves(o[2 * tq:], tq)]
    o_ref[0] = jnp.concatenate(pairs, axis=-1).astype(o_ref.dtype)


def _full_attn(q, kd, vd, ckd, cvd, bounded):
    B, S, _ = q.shape
    return pl.pallas_call(
        functools.partial(_full_attn_kernel, bounded),
        grid=(B, 2, S // FULL_TQ),
        in_specs=[pl.BlockSpec((1, FULL_TQ, 2 * LANES), lambda b, g, i: (b, i, g)),
                  pl.BlockSpec((1, S, LANES), lambda b, g, i: (b, 0, g)),
                  pl.BlockSpec((1, S, LANES), lambda b, g, i: (b, 0, g)),
                  pl.BlockSpec((1, CTX_LEN, LANES), lambda b, g, i: (b, 0, g)),
                  pl.BlockSpec((1, CTX_LEN, LANES), lambda b, g, i: (b, 0, g))],
        out_specs=pl.BlockSpec((1, FULL_TQ, 2 * LANES), lambda b, g, i: (b, i, g)),
        out_shape=jax.ShapeDtypeStruct((B, S, 4 * LANES), BF16),
        compiler_params=_cparams(("parallel", "parallel", "arbitrary")),
        name="full_attn_bounded" if bounded else "full_attn_online",
    )(q, kd, vd, ckd, cvd)


def _ctx_attn_kernel(sink_ref, aq_ref, akd_ref, avd_ref, bq_ref, bk_ref, bv_ref, o_ref):
    n = CTX_LEN
    aq = aq_ref[0]
    bq = bq_ref[0]
    outs = []
    for g in range(2):
        ls = slice(g * LANES, (g + 1) * LANES)
        qs = jnp.concatenate([_stack_halves(aq[:, (2 * g + j) * LANES:(2 * g + j + 1) * LANES])
                              for j in range(2)], axis=0)
        sink = _sink_column(sink_ref, 4 * g, 4, n)
        o = _joint_softmax_pv([_scores(qs, akd_ref[0][:, ls])], [avd_ref[0][:, ls]], sink)
        outs += [_merge_halves(o[:2 * n], n), _merge_halves(o[2 * n:], n)]
    for p in range(4):
        ls = slice(p * LANES, (p + 1) * LANES)
        qs = _stack_halves(bq[:, ls])
        o = _joint_softmax_pv([_scores(qs, bk_ref[0][:, ls])], [bv_ref[0][:, ls]])
        outs.append(_merge_halves(o, n))
    o_ref[0] = jnp.concatenate(outs, axis=-1).astype(o_ref.dtype)


def _ctx_attn(sink, aq, akd, avd, bq, bk, bv):
    B = aq.shape[0]
    spec = lambda a: pl.BlockSpec((1,) + a.shape[1:], lambda b: (b, 0, 0))
    args = (aq, akd, avd, bq, bk, bv)
    return pl.pallas_call(
        _ctx_attn_kernel,
        grid=(B,),
        in_specs=[pl.BlockSpec(memory_space=pltpu.SMEM)] + [spec(a) for a in args],
        out_specs=pl.BlockSpec((1, CTX_LEN, 8 * LANES), lambda b: (b, 0, 0)),
        out_shape=jax.ShapeDtypeStruct((B, CTX_LEN, 8 * LANES), BF16),
        compiler_params=_cparams(("parallel",)),
        name="ctx_attn",
    )(sink, *args)


def _pick4(idx, vals):
    return jnp.where(idx == 0, vals[0], jnp.where(idx == 1, vals[1], jnp.where(idx == 2, vals[2], vals[3])))


def _route_rows(lg_t, b_ref):
    n_tok = lg_t.shape[1]
    s = [jax.nn.sigmoid(lg_t[e:e + 1, :]) for e in range(N_EXPERTS)]
    sel = [s[e] + b_ref[e] for e in range(N_EXPERTS)]
    n = EXPERTS_PER_GROUP
    gscore = []
    for j in range(N_GROUPS):
        v = sel[n * j:n * (j + 1)]
        pair_sums = [v[a] + v[b] for a in range(n) for b in range(a + 1, n)]
        gscore.append(functools.reduce(jnp.maximum, pair_sums))
    best, gbest = gscore[0], jnp.zeros((1, n_tok), jnp.int32)
    for j in range(1, N_GROUPS):
        upd = gscore[j] > best
        best = jnp.where(upd, gscore[j], best)
        gbest = jnp.where(upd, j, gbest)
    v = [_pick4(gbest, [sel[n * j + i] for j in range(N_GROUPS)]) for i in range(n)]
    u = [_pick4(gbest, [s[n * j + i] for j in range(N_GROUPS)]) for i in range(n)]
    m1, i1 = v[0], jnp.zeros((1, n_tok), jnp.int32)
    for i in range(1, n):
        upd = v[i] > m1
        m1 = jnp.where(upd, v[i], m1)
        i1 = jnp.where(upd, i, i1)
    m2, i2 = jnp.full((1, n_tok), -jnp.inf, F32), jnp.zeros((1, n_tok), jnp.int32)
    for i in range(n):
        upd = (i1 != i) & (v[i] > m2)
        m2 = jnp.where(upd, v[i], m2)
        i2 = jnp.where(upd, i, i2)
    u1, u2 = _pick4(i1, u), _pick4(i2, u)
    tot = u1 + u2
    return n * gbest + i1, n * gbest + i2, u1 / tot, u2 / tot


def _out_proj_kernel(n_y, sparse, *refs):
    y_refs = refs[:n_y]
    br_ref, w_ref, x_ref, gate_ref, g_ref, shift_ref, scale_ref, wrh_ref, wrl_ref = refs[n_y:n_y + 9]
    n_in = n_y + 9
    if sparse:
        tri_ref, tril_ref = refs[n_in:n_in + 2]
        n_in += 2
    outs = refs[n_in:]
    xo_ref = outs[0]
    off = 0
    acc = None
    for y_ref in y_refs:
        wdt = y_ref.shape[-1]
        part = jnp.dot(y_ref[0], w_ref[off:off + wdt, :], preferred_element_type=F32)
        acc = part if acc is None else acc + part
        off += wdt
    x = x_ref[0] + gate_ref[0] * acc
    xo_ref[0] = x
    ms = jnp.mean(x * x, axis=-1, keepdims=True)
    h = x * lax.rsqrt(ms + EPS) * g_ref[...]
    h = h * (1.0 + scale_ref[0]) + shift_ref[0]
    hh = h.astype(BF16)
    hl = (h - hh.astype(F32)).astype(BF16)
    lg = (jnp.dot(hh, wrh_ref[...], preferred_element_type=F32)
          + jnp.dot(hl, wrh_ref[...], preferred_element_type=F32)
          + jnp.dot(hh, wrl_ref[...], preferred_element_type=F32))
    lg_t = lg.T[:N_EXPERTS]
    e1, e2, w1, w2 = _route_rows(lg_t, br_ref)
    rows = lax.broadcasted_iota(jnp.int32, lg_t.shape, 0)
    if not sparse:
        h_ref, comb_ref = outs[1:]
        h_ref[0] = hh
        comb_t = jnp.where(rows == e1, w1, 0.0) + jnp.where(rows == e2, w2, 0.0)
        comb_ref[0] = jnp.concatenate(
            [comb_t, jnp.zeros((LANES - N_EXPERTS, comb_t.shape[1]), F32)], axis=0).T
        return

    h_ref, route_t_ref, route_c_ref, ng_ref = outs[1:]
    h_ref[0] = hh
    member = jnp.where((rows == e1) | (rows == e2), 1.0, 0.0)
    before = jnp.dot(member.astype(BF16), tri_ref[...], preferred_element_type=F32)
    groups = jnp.floor((jnp.sum(member, axis=1, keepdims=True) + (MOE_G - 1)) * (1.0 / MOE_G))
    groups = jnp.broadcast_to(groups, (N_EXPERTS, LANES))
    run_start = MOE_G * jnp.dot(tril_ref[...], groups.astype(BF16), preferred_element_type=F32)[:, 0:1]
    pos = run_start + before
    p1 = jnp.sum(jnp.where(rows == e1, pos, 0.0), axis=0, keepdims=True)
    p2 = jnp.sum(jnp.where(rows == e2, pos, 0.0), axis=0, keepdims=True)
    ng_ref[0] = groups
    field = lax.broadcasted_iota(jnp.int32, (LANES, h.shape[0]), 0)
    route = jnp.zeros((LANES, h.shape[0]), F32)
    for k, v in enumerate((p1, p2, w1, w2)):
        route = jnp.where(field == k, v, route)
    route_t_ref[...] = route[:ROUTE_FIELDS]
    route_c_ref[...] = route.T


ROUTE_FIELDS = 8
MOE_G = 8
MOE_TT = 512
MOE_LOCAL = 1152


def _out_proj(ys, w, x, gate, g, shift, scale, router, sparse, tm=512):
    B, S, D = x.shape
    tm = min(tm, S)
    bm = gate.shape[0]
    b_router, wr_hi, wr_lo = router
    nt = S // tm
    mod_map = (lambda b, i: (b, 0, 0)) if bm > 1 else (lambda b, i: (0, 0, 0))
    mod_spec = pl.BlockSpec((1, 1, D), mod_map)
    row_map = lambda b, i: (b, i, 0)
    in_specs = ([pl.BlockSpec((1, tm, y.shape[-1]), row_map) for y in ys]
                + [pl.BlockSpec(memory_space=pltpu.SMEM),
                   pl.BlockSpec(w.shape, lambda b, i: (0, 0)),
                   pl.BlockSpec((1, tm, D), row_map), mod_spec,
                   pl.BlockSpec((1, D), lambda b, i: (0, 0)), mod_spec, mod_spec,
                   pl.BlockSpec(wr_hi.shape, lambda b, i: (0, 0)),
                   pl.BlockSpec(wr_lo.shape, lambda b, i: (0, 0))])
    args = list(ys) + [b_router, w, x, gate.reshape(bm, 1, D), g.reshape(1, D), shift.reshape(bm, 1, D),
                       scale.reshape(bm, 1, D), wr_hi, wr_lo]
    out_specs = [pl.BlockSpec((1, tm, D), row_map)]
    out_shape = [jax.ShapeDtypeStruct((B, S, D), F32)]
    out_specs.append(pl.BlockSpec((1, tm, D), row_map))
    out_shape.append(jax.ShapeDtypeStruct((B, S, D), BF16))
    if sparse:
        assert tm == MOE_TT
        tri = jnp.asarray(np.triu(np.ones((tm, tm), np.float32), 1)).astype(BF16)
        tril = jnp.asarray(np.tril(np.ones((N_EXPERTS, N_EXPERTS), np.float32), -1)).astype(BF16)
        args += [tri, tril]
        in_specs += [pl.BlockSpec(tri.shape, lambda b, i: (0, 0)),
                     pl.BlockSpec(tril.shape, lambda b, i: (0, 0))]
        out_specs += [pl.BlockSpec((ROUTE_FIELDS, tm), lambda b, i: (0, b * nt + i)),
                      pl.BlockSpec((tm, LANES), lambda b, i: (b * nt + i, 0)),
                      pl.BlockSpec((1, N_EXPERTS, LANES), lambda b, i: (b * nt + i, 0, 0))]
        out_shape += [jax.ShapeDtypeStruct((ROUTE_FIELDS, B * S), F32),
                      jax.ShapeDtypeStruct((B * S, LANES), F32),
                      jax.ShapeDtypeStruct((B * nt, N_EXPERTS, LANES), F32)]
    else:
        out_specs.append(pl.BlockSpec((1, tm, LANES), row_map))
        out_shape.append(jax.ShapeDtypeStruct((B, S, LANES), F32))
    return pl.pallas_call(
        functools.partial(_out_proj_kernel, len(ys), sparse),
        grid=(B, nt),
        in_specs=in_specs,
        out_specs=out_specs,
        out_shape=out_shape,
        compiler_params=_cparams(("parallel", "parallel")),
        name="out_proj_sparse" if sparse else "out_proj",
    )(*args)


def _moe_kernel(h_ref, comb_ref, wg_ref, wu_ref, wd_ref, x_ref, gate_ref, o_ref, acc_ref):
    e = pl.program_id(2)

    @pl.when(e == 0)
    def _():
        acc_ref[...] = jnp.zeros_like(acc_ref)

    h = h_ref[0]
    a = jnp.dot(h, wg_ref[0, 0].astype(BF16), preferred_element_type=F32)
    u = jnp.dot(h, wu_ref[0, 0].astype(BF16), preferred_element_type=F32)
    he = (a * jax.nn.sigmoid(a) * u).astype(BF16)
    y = jnp.dot(he, wd_ref[0, 0].astype(BF16), preferred_element_type=F32)
    lane = lax.broadcasted_iota(jnp.int32, comb_ref.shape[1:], 1)
    c = jnp.sum(jnp.where(lane == e, comb_ref[0], 0.0), axis=-1, keepdims=True)
    acc_ref[...] += c * y

    @pl.when(e == pl.num_programs(2) - 1)
    def _():
        o_ref[0] = x_ref[0] + gate_ref[0] * acc_ref[...]


def _moe(h, comb, wg, wu, wd, layer, x, gate, tm=1024):
    B, S, D = x.shape
    tm = min(tm, S)
    bm = gate.shape[0]
    mod_map = (lambda b, i, e: (b, 0, 0)) if bm > 1 else (lambda b, i, e: (0, 0, 0))
    row_map = lambda b, i, e: (b, i, 0)
    return pl.pallas_call(
        _moe_kernel,
        grid=(B, S // tm, N_EXPERTS),
        in_specs=[pl.BlockSpec((1, tm, D), row_map),
                  pl.BlockSpec((1, tm, LANES), row_map),
                  pl.BlockSpec((1, 1, D, D_EXPERT), lambda b, i, e: (layer, e, 0, 0)),
                  pl.BlockSpec((1, 1, D, D_EXPERT), lambda b, i, e: (layer, e, 0, 0)),
                  pl.BlockSpec((1, 1, D_EXPERT, D), lambda b, i, e: (layer, e, 0, 0)),
                  pl.BlockSpec((1, tm, D), row_map),
                  pl.BlockSpec((1, 1, D), mod_map)],
        out_specs=pl.BlockSpec((1, tm, D), row_map),
        out_shape=jax.ShapeDtypeStruct((B, S, D), F32),
        scratch_shapes=[pltpu.VMEM((tm, D), F32)],
        compiler_params=_cparams(("parallel", "parallel", "arbitrary")),
        name="moe",
    )(h, comb, wg, wu, wd, x, gate.reshape(bm, 1, D))


MOE_TM = 512
MOE_TG = MOE_TM // MOE_G


def _moe_rows(n_tok):
    rows = 2 * n_tok + (n_tok // MOE_TT) * N_EXPERTS * (MOE_G - 1) + N_EXPERTS * (MOE_TM - 1)
    return (rows + MOE_TM - 1) // MOE_TM * MOE_TM


def _moe_plan(ng):
    ng = ng[:, :, 0].astype(jnp.int32)
    n_tt = ng.shape[0]
    total = jnp.sum(ng, axis=0)
    region = (total + MOE_TG - 1) // MOE_TG * MOE_TG
    region_end = jnp.cumsum(region)
    region_start = region_end - region
    dst = region_start[None, :] + jnp.cumsum(ng, axis=0) - ng
    local = jnp.cumsum(ng, axis=1) - ng
    n_tiles = _moe_rows(n_tt * MOE_TT) // MOE_TM
    tile_first = jnp.arange(n_tiles, dtype=jnp.int32) * MOE_TG
    tile_expert = jnp.minimum(jnp.sum(region_end[None, :] <= tile_first[:, None], axis=1), N_EXPERTS - 1)
    n_valid = region_end[-1:] // MOE_TG
    i32 = lambda a: a.astype(jnp.int32).reshape(-1)
    return dict(ng=i32(ng), dst=i32(dst), local=i32(local), tile_groups=i32(jnp.sum(ng, axis=1)),
                pad_first=i32(region_start + total),
                pad_count=i32(region - total), tile_expert=i32(tile_expert), n_valid=i32(n_valid),
                n_tiles=n_tiles)


def _run_copies(plan_refs, tile, local_ref, sorted_ref, sem, to_sorted):
    ng_ref, dst_ref, loc_ref = plan_refs
    for e in range(N_EXPERTS):
        k = tile * N_EXPERTS + e
        loc, dst = loc_ref[k], dst_ref[k]

        def body(g, carry):
            lrows = local_ref.at[pl.ds(pl.multiple_of((loc + g) * MOE_G, MOE_G), MOE_G)]
            srows = sorted_ref.at[pl.ds(pl.multiple_of((dst + g) * MOE_G, MOE_G), MOE_G)]
            src, tgt = (lrows, srows) if to_sorted else (srows, lrows)
            pltpu.make_async_copy(src, tgt, sem).start()
            return carry
        lax.fori_loop(0, ng_ref[k], body, 0)


def _group_waits(n_groups, local_ref, sorted_ref, sem):
    def body(g, carry):
        pltpu.make_async_copy(sorted_ref.at[pl.ds(0, MOE_G)], local_ref.at[pl.ds(0, MOE_G)], sem).wait()
        return carry
    lax.fori_loop(0, n_groups, body, 0)


def _moe_dispatch_kernel(ng_ref, dst_ref, loc_ref, tot_ref, padf_ref, padc_ref, nv_ref,
                         h_ref, route_ref, xs_ref, local_ref, zero_ref, sem):
    i = pl.program_id(0)

    @pl.when(i == 0)
    def _():
        zero_ref[...] = jnp.zeros_like(zero_ref)
        n_pad = 0
        for e in range(N_EXPERTS):
            first = padf_ref[e]

            def body(g, carry):
                rows = xs_ref.at[pl.ds(pl.multiple_of((first + g) * MOE_G, MOE_G), MOE_G)]
                pltpu.make_async_copy(zero_ref.at[pl.ds(0, MOE_G)], rows, sem.at[2]).start()
                return carry
            lax.fori_loop(0, padc_ref[e], body, 0)
            n_pad = n_pad + padc_ref[e]

        def tile_copy(j):
            rows = xs_ref.at[pl.ds(pl.multiple_of(j * MOE_TM, MOE_TM), MOE_TM)]
            return pltpu.make_async_copy(zero_ref, rows, sem.at[3])

        n_tiles = xs_ref.shape[0] // MOE_TM
        lax.fori_loop(nv_ref[0], n_tiles, lambda j, c: (tile_copy(j).start(), c)[1], 0)
        _group_waits(n_pad, zero_ref, xs_ref, sem.at[2])
        lax.fori_loop(nv_ref[0], n_tiles, lambda j, c: (tile_copy(j).wait(), c)[1], 0)

    pos = lax.broadcasted_iota(jnp.int32, (MOE_LOCAL, MOE_TT), 0)
    p1 = route_ref[0:1, :].astype(jnp.int32)
    p2 = route_ref[1:2, :].astype(jnp.int32)
    pick = jnp.where((pos == p1) | (pos == p2), 1.0, 0.0).astype(BF16)
    slot = i % 2
    last = pl.num_programs(0) - 1

    @pl.when(i >= 2)
    def _():
        _group_waits(tot_ref[i - 2], local_ref.at[slot], xs_ref, sem.at[slot])

    local_ref[slot] = jnp.dot(pick, h_ref[0], preferred_element_type=F32)
    _run_copies((ng_ref, dst_ref, loc_ref), i, local_ref.at[slot], xs_ref, sem.at[slot], to_sorted=True)

    @pl.when(i == last)
    def _():
        _group_waits(tot_ref[i], local_ref.at[slot], xs_ref, sem.at[slot])

    @pl.when((i == last) & (last >= 1))
    def _():
        _group_waits(tot_ref[i - 1], local_ref.at[1 - slot], xs_ref, sem.at[1 - slot])


def _moe_dispatch(h, route_t, plan):
    B, S, D = h.shape
    nt = S // MOE_TT
    grid_spec = pltpu.PrefetchScalarGridSpec(
        num_scalar_prefetch=7,
        grid=(B * nt,),
        in_specs=[pl.BlockSpec((1, MOE_TT, D), lambda i, *_: (i // nt, i % nt, 0)),
                  pl.BlockSpec((ROUTE_FIELDS, MOE_TT), lambda i, *_: (0, i))],
        out_specs=pl.BlockSpec(memory_space=pl.ANY),
        scratch_shapes=[pltpu.VMEM((2, MOE_LOCAL, D), F32), pltpu.VMEM((MOE_TM, D), F32),
                        pltpu.SemaphoreType.DMA((4,))])
    return pl.pallas_call(
        _moe_dispatch_kernel,
        grid_spec=grid_spec,
        out_shape=jax.ShapeDtypeStruct((plan["n_tiles"] * MOE_TM, D), F32),
        compiler_params=_cparams(("arbitrary",)),
        name="moe_dispatch",
    )(plan["ng"], plan["dst"], plan["local"], plan["tile_groups"], plan["pad_first"], plan["pad_count"],
      plan["n_valid"], h, route_t)


def _moe_grouped_kernel(te_ref, nv_ref, x_ref, wg_ref, wu_ref, wd_ref, o_ref, wgb_ref, wub_ref, wdb_ref):
    j = pl.program_id(0)
    used = j < nv_ref[0]

    @pl.when(used & ((j == 0) | (te_ref[j] != te_ref[jnp.maximum(j - 1, 0)])))
    def _():
        wgb_ref[...] = wg_ref[0, 0].astype(BF16)
        wub_ref[...] = wu_ref[0, 0].astype(BF16)
        wdb_ref[...] = wd_ref[0, 0].astype(BF16)

    @pl.when(used)
    def _():
        x = x_ref[...].astype(BF16)
        a = jnp.dot(x, wgb_ref[...], preferred_element_type=F32)
        u = jnp.dot(x, wub_ref[...], preferred_element_type=F32)
        he = (a * jax.nn.sigmoid(a) * u).astype(BF16)
        o_ref[...] = jnp.dot(he, wdb_ref[...], preferred_element_type=F32)

    @pl.when(jnp.logical_not(used))
    def _():
        o_ref[...] = jnp.zeros_like(o_ref)


def _moe_grouped(xs, plan, wg, wu, wd, layer):
    n_tiles = plan["n_tiles"]
    D = D_MODEL
    tile = lambda j, nv: jnp.minimum(j, nv[0] - 1)
    grid_spec = pltpu.PrefetchScalarGridSpec(
        num_scalar_prefetch=2,
        grid=(n_tiles,),
        in_specs=[pl.BlockSpec((MOE_TM, D), lambda j, te, nv: (tile(j, nv), 0)),
                  pl.BlockSpec((1, 1, D, D_EXPERT), lambda j, te, nv: (layer, te[tile(j, nv)], 0, 0)),
                  pl.BlockSpec((1, 1, D, D_EXPERT), lambda j, te, nv: (layer, te[tile(j, nv)], 0, 0)),
                  pl.BlockSpec((1, 1, D_EXPERT, D), lambda j, te, nv: (layer, te[tile(j, nv)], 0, 0))],
        out_specs=pl.BlockSpec((MOE_TM, D), lambda j, te, nv: (j, 0)),
        scratch_shapes=[pltpu.VMEM((D, D_EXPERT), BF16), pltpu.VMEM((D, D_EXPERT), BF16),
                        pltpu.VMEM((D_EXPERT, D), BF16)])
    return pl.pallas_call(
        _moe_grouped_kernel,
        grid_spec=grid_spec,
        out_shape=jax.ShapeDtypeStruct((n_tiles * MOE_TM, D), F32),
        compiler_params=_cparams(("arbitrary",)),
        name="moe_grouped",
    )(plan["tile_expert"], plan["n_valid"], xs, wg, wu, wd)


def _moe_combine_kernel(final_norm, ng_ref, dst_ref, loc_ref, tot_ref, ys_ref, route_ref, x_ref, gate_ref,
                        *rest):
    if final_norm:
        fg_ref, o_ref, local_ref, sem = rest
    else:
        o_ref, local_ref, sem = rest
    i = pl.program_id(0)
    slot = i % 2
    plan_refs = (ng_ref, dst_ref, loc_ref)

    @pl.when(i == 0)
    def _():
        local_ref[...] = jnp.zeros_like(local_ref)
        _run_copies(plan_refs, 0, local_ref.at[0], ys_ref, sem.at[0], to_sorted=False)

    @pl.when(i + 1 < pl.num_programs(0))
    def _():
        _run_copies(plan_refs, i + 1, local_ref.at[1 - slot], ys_ref, sem.at[1 - slot], to_sorted=False)

    _group_waits(tot_ref[i], local_ref.at[slot], ys_ref, sem.at[slot])
    pos = lax.broadcasted_iota(jnp.int32, (MOE_TT, MOE_LOCAL), 1)
    route = route_ref[...]
    p1 = route[:, 0:1].astype(jnp.int32)
    p2 = route[:, 1:2].astype(jnp.int32)
    weigh = (jnp.where(pos == p1, route[:, 2:3], 0.0) + jnp.where(pos == p2, route[:, 3:4], 0.0)).astype(BF16)
    y = jnp.dot(weigh, local_ref[slot].astype(BF16), preferred_element_type=F32)
    o = x_ref[0] + gate_ref[0] * y
    if final_norm:
        o = o * lax.rsqrt(jnp.mean(o * o, axis=-1, keepdims=True) + EPS) * fg_ref[...]
    o_ref[0] = o


def _moe_combine(ys, route_c, plan, x, gate, final_g=None):
    B, S, D = x.shape
    nt = S // MOE_TT
    in_specs = [pl.BlockSpec(memory_space=pl.ANY),
                pl.BlockSpec((MOE_TT, LANES), lambda i, *_: (i, 0)),
                pl.BlockSpec((1, MOE_TT, D), lambda i, *_: (i // nt, i % nt, 0)),
                pl.BlockSpec((1, 1, D), lambda i, *_: (i // nt, 0, 0))]
    args = [ys, route_c, x, gate.reshape(B, 1, D)]
    if final_g is not None:
        in_specs.append(pl.BlockSpec((1, D), lambda i, *_: (0, 0)))
        args.append(final_g.reshape(1, D))
    grid_spec = pltpu.PrefetchScalarGridSpec(
        num_scalar_prefetch=4,
        grid=(B * nt,),
        in_specs=in_specs,
        out_specs=pl.BlockSpec((1, MOE_TT, D), lambda i, *_: (i // nt, i % nt, 0)),
        scratch_shapes=[pltpu.VMEM((2, MOE_LOCAL, D), F32), pltpu.SemaphoreType.DMA((2,))])
    return pl.pallas_call(
        functools.partial(_moe_combine_kernel, final_g is not None),
        grid_spec=grid_spec,
        out_shape=jax.ShapeDtypeStruct((B, S, D), F32),
        compiler_params=_cparams(("arbitrary",)),
        name="moe_combine",
    )(plan["ng"], plan["dst"], plan["local"], plan["tile_groups"], *args)


def _dft_tables():
    n1, n2, n = FFT_N1, FFT_N2, FFT_N1 * FFT_N2
    k1 = np.arange(n1)
    f1 = np.exp(-2j * np.pi * np.outer(k1, np.arange(n1)) / n1)
    tw = np.exp(-2j * np.pi * np.outer(np.arange(n2), k1) / n)
    ftw = f1[None, :, :] * tw[:, :, None]
    half = n1 // 2
    fh = ftw[:, :, :half]
    g_fwd = np.concatenate([np.concatenate([fh.real, -fh.imag], axis=2),
                            np.concatenate([fh.imag, fh.real], axis=2)], axis=1)
    back = ftw[:, :, ::-1][:, :, :half].copy()
    back[0] = np.roll(ftw[0], -1, axis=1)[:, ::-1][:, :half]
    back[0][:, 0] = 0.0
    fk = np.concatenate([fh, back], axis=2)
    g_real = np.concatenate([fk.real, fk.imag], axis=1)
    gi = np.conj(np.transpose(fh, (0, 2, 1))) / n
    g_inv = np.concatenate([np.concatenate([gi.real, -gi.imag], axis=2),
                            np.concatenate([gi.imag, gi.real], axis=2)], axis=1)
    f2 = np.exp(-2j * np.pi * np.outer(np.arange(n2), np.arange(n2)) / n2)
    f2_fwd = np.block([[f2.real, -f2.imag], [f2.imag, f2.real]])
    f2c = np.conj(f2)
    f2_inv = np.block([[f2c.real, -f2c.imag], [f2c.imag, f2c.real]])
    as_bf = lambda a: jnp.asarray(a, dtype=F32).astype(BF16)
    return as_bf(g_fwd), as_bf(g_real), as_bf(g_inv), as_bf(f2_fwd), as_bf(f2_inv)


def _fft_fast_stage(stage_ref, k1, f2):
    slab = 2 * FFT_N1
    m = jnp.concatenate([stage_ref[pl.ds(k1, FFT_N2, stride=slab), :],
                         stage_ref[pl.ds(FFT_N1 + k1, FFT_N2, stride=slab), :]], axis=0)
    return jnp.dot(f2, m.astype(BF16), preferred_element_type=F32)


def _filter_fft_kernel(hf_ref, hb_ref, inv_ref, g_ref, f2_ref, h_ref, stage_ref):
    slab = 2 * FFT_N1
    half = FFT_N1 // 2
    for n2 in range(FFT_N2):
        x = jnp.concatenate([hf_ref[pl.ds(n2, half, stride=FFT_N2), :],
                             hb_ref[pl.ds((FFT_N2 - n2) % FFT_N2, half, stride=FFT_N2), :]], axis=0)
        stage_ref[n2 * slab:(n2 + 1) * slab, :] = jnp.dot(g_ref[n2], x.astype(BF16),
                                                          preferred_element_type=F32)
    f2 = f2_ref[...]
    inv = inv_ref[...]
    for k1 in range(FFT_N1):
        h_ref[0, k1] = (_fft_fast_stage(stage_ref, k1, f2) * inv).astype(h_ref.dtype)


def _filter_fft(taps, inv_norm, g_real, f2_fwd, ct=LANES):
    L, cols = taps.shape
    C = D_CH
    n_ord = cols // (2 * C)
    nc = C // ct
    once = pl.Buffered(1)
    return pl.pallas_call(
        _filter_fft_kernel,
        grid=(n_ord, nc),
        in_specs=[pl.BlockSpec((L, ct), lambda o, c: (0, o * nc + c)),
                  pl.BlockSpec((L, ct), lambda o, c: (0, (n_ord + o) * nc + c)),
                  pl.BlockSpec((1, ct), lambda o, c: (0, o * nc + c)),
                  pl.BlockSpec(g_real.shape, lambda o, c: (0, 0, 0), pipeline_mode=once),
                  pl.BlockSpec(f2_fwd.shape, lambda o, c: (0, 0), pipeline_mode=once)],
        out_specs=pl.BlockSpec((1, FFT_N1, 2 * FFT_N2, ct), lambda o, c: (o, 0, 0, c)),
        out_shape=jax.ShapeDtypeStruct((n_ord, FFT_N1, 2 * FFT_N2, C), BF16),
        scratch_shapes=[pltpu.VMEM((FFT_N2 * 2 * FFT_N1, ct), F32)],
        compiler_params=_cparams(("parallel", "parallel")),
        name="filter_fft",
    )(taps, taps, inv_norm, g_real, f2_fwd)


class _RowSets:
    def __init__(self, ref, member):
        self.ref, self.member, self.loaded = ref, member, {}

    def __call__(self, start):
        if start not in self.loaded:
            self.loaded = {k: v for k, v in self.loaded.items() if abs(k - start) <= 2}
            self.loaded[start] = self.ref[self.member, pl.ds(start, FFT_N1 // 2, stride=FFT_N2), :]
        return self.loaded[start]


def _strided_rows(at, n2, taps):
    half = FFT_N1 // 2
    cur = at(n2)
    if taps is None:
        return cur
    w_ref, b_ref = taps
    n1 = lax.broadcasted_iota(jnp.int32, cur.shape, 0)
    if n2 > 0:
        prev = at(n2 - 1)
    else:
        prev = jnp.where(n1 == 0, 0.0, pltpu.roll(at(FFT_N2 - 1), 1, axis=0))
    if n2 < FFT_N2 - 1:
        nxt = at(n2 + 1)
    else:
        nxt = jnp.where(n1 == half - 1, 0.0, pltpu.roll(at(0), half - 1, axis=0))
    return prev * w_ref[0:1, :] + cur * w_ref[1:2, :] + nxt * w_ref[2:3, :] + b_ref[...]


def _hyena_conv_kernel(conv_z, *refs):
    it = iter(refs)
    z_ref, gate_ref = next(it), next(it)
    z_taps = (next(it), next(it)) if conv_z else None
    gate_taps = (next(it), next(it))
    hb_ref, spec_ref, gf_ref, gi_ref, f2f_ref, f2i_ref, o_ref, stage_ref = it
    half = FFT_N1 // 2
    slab = 2 * FFT_N1
    z_sets = [_RowSets(z_ref, m) for m in range(2)]
    for n2 in range(FFT_N2):
        x = jnp.concatenate([_strided_rows(z_sets[m], n2, z_taps) for m in range(2)], axis=0).astype(BF16)
        stage_ref[n2 * slab:(n2 + 1) * slab, :] = jnp.dot(gf_ref[n2], x, preferred_element_type=F32)
    f2f = f2f_ref[...]
    f2i = f2i_ref[...]
    for k1 in range(FFT_N1):
        zf = _fft_fast_stage(stage_ref, k1, f2f)
        zr, zi = zf[:FFT_N2], zf[FFT_N2:]
        hr = spec_ref[0, k1, :FFT_N2, :].astype(F32)
        hi = spec_ref[0, k1, FFT_N2:, :].astype(F32)
        p = jnp.concatenate([zr * hr - zi * hi, zr * hi + zi * hr], axis=0).astype(BF16)
        q = jnp.dot(f2i, p, preferred_element_type=F32)
        stage_ref[pl.ds(k1, FFT_N2, stride=slab), :] = q[:FFT_N2]
        stage_ref[pl.ds(FFT_N1 + k1, FFT_N2, stride=slab), :] = q[FFT_N2:]
    hb = hb_ref[...]
    z_sets = [_RowSets(z_ref, m) for m in range(2)]
    gate_sets = [_RowSets(gate_ref, m) for m in range(2)]
    for n2 in range(FFT_N2):
        y_in = stage_ref[n2 * slab:(n2 + 1) * slab, :].astype(BF16)
        y = jnp.dot(gi_ref[n2], y_in, preferred_element_type=F32)
        for m in range(2):
            zm = _strided_rows(z_sets[m], n2, z_taps)
            gm = _strided_rows(gate_sets[m], n2, gate_taps)
            o_ref[m, pl.ds(n2, half, stride=FFT_N2), :] = gm * (y[m * half:(m + 1) * half] + zm * hb)


def _hyena_conv(z, z_blk, gate, gate_blk, short_w, short_b, conv_z, hbias, spec, order, tabs, ct=LANES):
    B, L, _ = z.shape
    C = D_CH
    g_fwd, _, g_inv, f2_fwd, f2_inv = tabs
    once = pl.Buffered(1)
    const3 = lambda a: pl.BlockSpec(a.shape, lambda c, p: (0, 0, 0), pipeline_mode=once)
    const2 = lambda a: pl.BlockSpec(a.shape, lambda c, p: (0, 0), pipeline_mode=once)
    taps_specs = lambda blk: [pl.BlockSpec((3, ct), lambda c, p: (0, blk + c)),
                              pl.BlockSpec((1, ct), lambda c, p: (0, blk + c))]
    in_specs = [pl.BlockSpec((2, L, ct), lambda c, p: (p, 0, z_blk + c)),
                pl.BlockSpec((2, L, ct), lambda c, p: (p, 0, gate_blk + c))]
    args = [z, gate]
    if conv_z:
        in_specs += taps_specs(z_blk)
        args += [short_w, short_b]
    in_specs += taps_specs(gate_blk)
    args += [short_w, short_b]
    in_specs += [pl.BlockSpec((1, ct), lambda c, p: (0, c)),
                 pl.BlockSpec((1, FFT_N1, 2 * FFT_N2, ct), lambda c, p: (order, 0, 0, c), pipeline_mode=once),
                 const3(g_fwd), const3(g_inv), const2(f2_fwd), const2(f2_inv)]
    args += [hbias, spec, g_fwd, g_inv, f2_fwd, f2_inv]
    return pl.pallas_call(
        functools.partial(_hyena_conv_kernel, conv_z),
        grid=(C // ct, B // 2),
        in_specs=in_specs,
        out_specs=pl.BlockSpec((2, L, ct), lambda c, p: (p, 0, c)),
        out_shape=jax.ShapeDtypeStruct((B, L, C), F32),
        scratch_shapes=[pltpu.VMEM((FFT_N2 * 2 * FFT_N1, ct), F32)],
        compiler_params=_cparams(("parallel", "arbitrary")),
        name="hyena_conv",
    )(*args)


def _hyena_filters(L, w1, b1, f1, w2, b2, f2, w3, b3):
    t = jnp.arange(L, dtype=F32)
    tn = t / max(L - 1, 1)
    bands = jnp.linspace(1e-4, HY_BANDS - 1, HY_BANDS, dtype=F32)
    ang = 2.0 * math.pi * t[:, None] * bands[None] / L
    feats = jnp.concatenate([tn[:, None], jnp.cos(ang), jnp.sin(ang)], axis=-1)
    h = jnp.sin(f1 * (feats @ w1 + b1))
    h = jnp.sin(f2 * (h @ w2 + b2))
    deltas = jnp.abs(jnp.linspace(HY_MIN_DECAY, HY_MAX_DECAY, D_CH, dtype=F32))
    decay = jnp.exp(-tn[:, None] * deltas[None])
    n_rep = w3.shape[1] // D_CH
    taps = (_mm_f32(h, w3, 512) + b3) * jnp.tile(decay, (1, n_rep))
    l1 = jnp.sum(jnp.abs(taps), axis=0)
    l1 = l1[:n_rep // 2 * D_CH] + l1[n_rep // 2 * D_CH:]
    return taps, (1.0 / (l1 + EPS))[None]


def _dup_heads(w):
    a, b = w[:, :HEAD_DIM], w[:, HEAD_DIM:]
    return jnp.concatenate([a, a, b, b], axis=1)


def _rope_tables(S):
    t = jnp.arange(S)
    row = (t // GRID_W).astype(F32)
    col = (t % GRID_W).astype(F32)
    half = HEAD_DIM // 2
    inv = ROPE_THETA ** (-jnp.arange(0, half, 2, dtype=F32) / half)
    ar = row[:, None] * inv[None]
    ac = col[:, None] * inv[None]
    cos = jnp.concatenate([jnp.cos(ar), jnp.cos(ar), jnp.cos(ac), jnp.cos(ac)], axis=-1)
    sin = jnp.concatenate([-jnp.sin(ar), jnp.sin(ar), -jnp.sin(ac), jnp.sin(ac)], axis=-1)
    return jnp.tile(cos, (1, 2)), jnp.tile(sin, (1, 2))


def _head_mean_matrix(width):
    blk = np.kron(np.eye(width // HEAD_DIM), np.full((HEAD_DIM, HEAD_DIM), 1.0 / HEAD_DIM))
    return jnp.asarray(blk, dtype=F32).astype(BF16)


def kernel(x, c, ctx, c_ctx, w_ada, b_ada, norm_g, final_g, w_in_even, w_out_even, a_sink, b_rpb, w_in_odd, w_out_odd, c_qnorm, c_knorm, hy_short_w, hy_short_b, hy_w1, hy_b1, hy_f1, hy_w2, hy_b2, hy_f2, hy_w3, hy_b3, hy_bias, w_router, b_router, moe_wg, moe_wu, moe_wd):
    B, S, D = x.shape
    depth = w_ada.shape[0]
    rope = _rope_tables(S)
    wr_pad = jnp.pad(w_router.astype(F32), ((0, 0), (0, LANES - N_EXPERTS)))
    wr_hi = wr_pad.astype(BF16)
    router = (b_router.astype(F32), wr_hi, (wr_pad - wr_hi.astype(F32)).astype(BF16))

    mod_in = jnp.concatenate([jax.nn.silu(c), jax.nn.silu(c_ctx)[None],
                              jnp.zeros((8 - B - 1, D), F32)], axis=0)
    xc = ctx
    for l in range(depth):
        need_ctx = l < depth - 1
        mod = _mm_f32(mod_in, w_ada[l], 1536) + b_ada[l]
        mx = mod[:B].reshape(B, 6, D)
        mc = mod[B].reshape(6, D)
        i = l // 2
        if l % 2 == 0:
            w = w_in_even[i].astype(BF16)
            w_all = jnp.concatenate([w[:, :512], _dup_heads(w[:, 512:640]), _dup_heads(w[:, 640:768]),
                                     w[:, 768:]], axis=1)
            segs_x = ((512, "rope", ATTN_SCALE), (256, "rope", 1.0), (256, "plain", 1.0),
                      (512, "plain", ATTN_SCALE), (512, "plain", 1.0), (512, "plain", 1.0))
            aq, akd, avd, bq, bk, bv = _norm_proj(x, norm_g[l, 0], mx[:, 0], mx[:, 1], w_all, segs_x,
                                                  rope=rope)
            segs_c = tuple((wd, "plain", m) for wd, _, m in segs_x)
            caq, cakd, cavd, cbq, cbk, cbv = _norm_proj(xc, norm_g[l, 0], mc[0:1], mc[1:2], w_all, segs_c)
            ya = _window_attn(aq, akd, avd, cakd, cavd, a_sink[i].astype(F32))
            yb = _nbr_attn(bq, bk, bv, cbk, cbv, _nbr_bias_table(b_rpb[i], S // GRID_W))
            ys = [ya, yb]
            w_out = w_out_even[i].astype(BF16)
            if need_ctx:
                yc = [_ctx_attn(a_sink[i].astype(F32), caq, cakd, cavd, cbq, cbk, cbv)]
        else:
            w = w_in_odd[i].astype(BF16)
            w_all = jnp.concatenate([w[:, :512], _dup_heads(w[:, 512:640]), _dup_heads(w[:, 640:768]),
                                     w[:, 768:]], axis=1)
            gains = jnp.concatenate([jnp.tile(c_qnorm[i], 8), jnp.tile(c_knorm[i], 4)])[None].astype(F32)
            norm = (_head_mean_matrix(512), gains)
            q_mult = ATTN_SCALE * math.log2(math.e)
            segs_x = ((512, "normrope", q_mult), (256, "normrope", 1.0), (256, "plain", 1.0),
                      (3 * D_CH, "plain", 1.0, F32))
            qx, kxd, vxd, ux = _norm_proj(x, norm_g[l, 0], mx[:, 0], mx[:, 1], w_all, segs_x,
                                          rope=rope, norm=norm)
            w_c = w_all[:, 512:1024]
            norm_c = (_head_mean_matrix(512), jnp.tile(c_knorm[i], 4)[None].astype(F32))
            kcd, vcd = _norm_proj(xc, norm_g[l, 0], mc[0:1], mc[1:2], w_c,
                                  ((256, "norm", 1.0), (256, "plain", 1.0)), norm=norm_c)
            logit_bound = (1.02 * HEAD_DIM * q_mult * jnp.max(jnp.abs(c_qnorm[i]))
                           * jnp.max(jnp.abs(c_knorm[i])))
            y_attn = lax.cond(logit_bound <= FULL_NOMAX_LOG2_BOUND,
                              lambda *a: _full_attn(*a, bounded=True),
                              lambda *a: _full_attn(*a, bounded=False),
                              qx, kxd, vxd, kcd, vcd)
            sw, sb = hy_short_w[i].astype(F32), hy_short_b[i].astype(F32)[None]
            tabs = _dft_tables()
            taps, inv_norm = _hyena_filters(S, hy_w1[i], hy_b1[i], hy_f1[i], hy_w2[i], hy_b2[i],
                                            hy_f2[i], hy_w3[i], hy_b3[i])
            spec = _filter_fft(taps, inv_norm, tabs[1], tabs[3])
            blocks = D_CH // LANES
            z = _hyena_conv(ux, 0, ux, blocks, sw, sb, True, hy_bias[i, 0:1], spec, 0, tabs)
            z = _hyena_conv(z, 0, ux, 2 * blocks, sw, sb, False, hy_bias[i, 1:2], spec, 1, tabs)
            ys = [y_attn, z.astype(BF16)]
            w_out = w_out_odd[i].astype(BF16)
            if need_ctx:
                raise NotImplementedError("context update of an odd layer is not needed at this depth")

        x, hx, route_t, route_c, ng = _out_proj(ys, w_out, x, mx[:, 2], norm_g[l, 1], mx[:, 3], mx[:, 4],
                                                router, sparse=True)
        if need_ctx:
            xc, hc, comb_c = _out_proj(yc, w_out, xc, mc[2:3], norm_g[l, 1], mc[3:4], mc[4:5], router,
                                       sparse=False)
            flat = lambda a: a.reshape(1, -1, a.shape[-1])
            xc = _moe(flat(hc), flat(comb_c), moe_wg, moe_wu, moe_wd, l, flat(xc), mc[5:6]).reshape(xc.shape)
        plan = _moe_plan(ng)
        sorted_rows = _moe_dispatch(hx, route_t, plan)
        x = _moe_combine(_moe_grouped(sorted_rows, plan, moe_wg, moe_wu, moe_wd, l), route_c, plan, x, mx[:, 5],
                         final_g=None if need_ctx else final_g)
    return x
```

```python
import functools
import math

import numpy as np
import jax
import jax.numpy as jnp
from jax import lax
from jax.experimental import pallas as pl
from jax.experimental.pallas import tpu as pltpu

F32 = jnp.float32
BF16 = jnp.bfloat16

D_MODEL = 1024
GRID_W = 64
CTX_LEN = 256
HEAD_DIM = 64
ROPE_THETA = 10000.0
EPS = 1e-6
ATTN_SCALE = HEAD_DIM ** -0.5
A_WINDOW = 128
A_BLOCK = 128
B_WIN_H = 8
B_WIN_W = 16
D_CH = 512
HY_BANDS = 16
HY_MAX_DECAY = math.log(1e-2) / 0.3
HY_MIN_DECAY = math.log(1e-2) / 1.5
N_EXPERTS = 16
N_GROUPS = 4
EXPERTS_PER_GROUP = N_EXPERTS // N_GROUPS
TOP_K = 2
D_EXPERT = 512

LANES = 128
NEG = -1e30
VMEM_LIMIT = 48 * 1024 * 1024

FFT_N1 = 64
FFT_N2 = 128
STAGE_PITCH = 2 * FFT_N1 + 8


def _cparams(sem):
    return pltpu.CompilerParams(dimension_semantics=sem, vmem_limit_bytes=VMEM_LIMIT)


def _mm_f32_kernel(x_ref, w_ref, o_ref):
    o_ref[...] = jnp.dot(x_ref[...], w_ref[...], preferred_element_type=F32)


def _mm_f32(x, w, tn):
    M, K = x.shape
    N = w.shape[1]
    return pl.pallas_call(
        _mm_f32_kernel,
        grid=(N // tn,),
        in_specs=[pl.BlockSpec((M, K), lambda j: (0, 0)),
                  pl.BlockSpec((K, tn), lambda j: (0, j))],
        out_specs=pl.BlockSpec((M, tn), lambda j: (0, j)),
        out_shape=jax.ShapeDtypeStruct((M, N), F32),
        compiler_params=_cparams(("arbitrary",)),
        name="mm_f32",
    )(x, w)


def _swap16(y):
    n = y.shape[-1]
    lane = lax.broadcasted_iota(jnp.int32, y.shape, y.ndim - 1)
    up = pltpu.roll(y, n - 16, axis=y.ndim - 1)
    dn = pltpu.roll(y, 16, axis=y.ndim - 1)
    return jnp.where((lane % 32) < 16, up, dn)


def _tile_lanes(t, width):
    reps = width // t.shape[-1]
    return t if reps == 1 else jnp.concatenate([t] * reps, axis=-1)


def _norm_proj_kernel(segs, has_rope, has_norm, *refs):
    it = iter(refs)
    x_ref, g_ref, shift_ref, scale_ref, w_ref = (next(it) for _ in range(5))
    cos_ref = sin_ref = bd_ref = gain_ref = None
    if has_rope:
        cos_ref, sin_ref = next(it), next(it)
    if has_norm:
        bd_ref, gain_ref = next(it), next(it)
    out_refs = list(it)

    x = x_ref[0]
    ms = jnp.mean(x * x, axis=-1, keepdims=True)
    h = x * lax.rsqrt(ms + EPS) * g_ref[...]
    h = h * (1.0 + scale_ref[0]) + shift_ref[0]
    y = jnp.dot(h.astype(BF16), w_ref[...], preferred_element_type=F32)

    off = 0
    goff = 0
    for (width, kind, mult, *_), o_ref in zip(segs, out_refs):
        ys = y[:, off:off + width]
        if kind in ("norm", "normrope"):
            bd = bd_ref[...][:width, :width]
            hms = jnp.dot((ys * ys).astype(BF16), bd, preferred_element_type=F32)
            ys = ys * lax.rsqrt(hms + EPS) * gain_ref[:, goff:goff + width]
            goff += width
        if kind in ("rope", "normrope"):
            c = _tile_lanes(cos_ref[...], width)
            s = _tile_lanes(sin_ref[...], width)
            ys = ys * c + _swap16(ys) * s
        if mult != 1.0:
            ys = ys * mult
        o_ref[0] = ys.astype(o_ref.dtype)
        off += width


def _norm_proj(x, g, shift, scale, w, segs, rope=None, norm=None, tm=512):
    B, S, D = x.shape
    N = w.shape[1]
    tm = min(tm, S)
    bm = shift.shape[0]
    mod_map = (lambda b, i: (b, 0, 0)) if bm > 1 else (lambda b, i: (0, 0, 0))
    args = [x, g.reshape(1, D), shift.reshape(bm, 1, D), scale.reshape(bm, 1, D), w]
    in_specs = [pl.BlockSpec((1, tm, D), lambda b, i: (b, i, 0)),
                pl.BlockSpec((1, D), lambda b, i: (0, 0)),
                pl.BlockSpec((1, 1, D), mod_map),
                pl.BlockSpec((1, 1, D), mod_map),
                pl.BlockSpec((D, N), lambda b, i: (0, 0))]
    if rope is not None:
        args += [rope[0], rope[1]]
        in_specs += [pl.BlockSpec((tm, LANES), lambda b, i: (i, 0))] * 2
    if norm is not None:
        args += [norm[0], norm[1]]
        in_specs += [pl.BlockSpec(norm[0].shape, lambda b, i: (0, 0)),
                     pl.BlockSpec(norm[1].shape, lambda b, i: (0, 0))]
    out_shape = [jax.ShapeDtypeStruct((B, S, seg[0]), seg[3] if len(seg) > 3 else BF16) for seg in segs]
    out_specs = [pl.BlockSpec((1, tm, seg[0]), lambda b, i: (b, i, 0)) for seg in segs]
    return pl.pallas_call(
        functools.partial(_norm_proj_kernel, segs, rope is not None, norm is not None),
        grid=(B, S // tm),
        in_specs=in_specs,
        out_specs=out_specs,
        out_shape=out_shape,
        compiler_params=_cparams(("parallel", "parallel")),
        name="norm_proj",
    )(*args)


def _half_mask(shape):
    return lax.broadcasted_iota(jnp.int32, shape, len(shape) - 1) < HEAD_DIM


def _stack_halves(qp):
    lo = _half_mask(qp.shape)
    zero = jnp.zeros_like(qp)
    return jnp.concatenate([jnp.where(lo, qp, zero), jnp.where(lo, zero, qp)], axis=0)


def _merge_halves(o, m):
    return jnp.where(_half_mask((m, LANES)), o[:m], o[m:])


def _scores(q, k):
    return lax.dot_general(q, k, (((1,), (1,)), ((), ())), preferred_element_type=F32)


def _joint_softmax_pv(score_parts, value_parts, extra_logit=None):
    m = functools.reduce(jnp.maximum, [jnp.max(s, axis=-1, keepdims=True) for s in score_parts])
    if extra_logit is not None:
        m = jnp.maximum(m, extra_logit)
    den = jnp.exp(extra_logit - m) if extra_logit is not None else 0.0
    acc = None
    for s, v in zip(score_parts, value_parts):
        p = jnp.exp(s - m)
        den = den + jnp.sum(p, axis=-1, keepdims=True)
        pv = jnp.dot(p.astype(BF16), v, preferred_element_type=F32)
        acc = pv if acc is None else acc + pv
    return acc / den


def _sink_column(sink_ref, first_head, n_heads, rows_per_head):
    rows = lax.broadcasted_iota(jnp.int32, (n_heads * rows_per_head, 1), 0)
    col = jnp.zeros((n_heads * rows_per_head, 1), F32)
    for j in range(n_heads):
        in_head = (rows >= j * rows_per_head) & (rows < (j + 1) * rows_per_head)
        col = jnp.where(in_head, sink_ref[first_head + j], col)
    return col


def _window_attn_kernel(seq_len, sink_ref, q_ref, kp_ref, kc_ref, kn_ref, vp_ref, vc_ref, vn_ref,
                        ck_ref, cv_ref, o_ref):
    i = pl.program_id(1)
    blk = A_BLOCK
    q = q_ref[0]
    rows = lax.broadcasted_iota(jnp.int32, (4 * blk, 3 * blk), 0) % blk
    rel = lax.broadcasted_iota(jnp.int32, (4 * blk, 3 * blk), 1) - blk
    gpos = i * blk + rel
    valid = (jnp.abs(rows - rel) <= A_WINDOW) & (gpos >= 0) & (gpos < seq_len)
    outs = []
    for g in range(2):
        ls = slice(g * LANES, (g + 1) * LANES)
        k_loc = jnp.concatenate([kp_ref[0][:, ls], kc_ref[0][:, ls], kn_ref[0][:, ls]], axis=0)
        v_loc = jnp.concatenate([vp_ref[0][:, ls], vc_ref[0][:, ls], vn_ref[0][:, ls]], axis=0)
        qs = jnp.concatenate([_stack_halves(q[:, (2 * g + j) * LANES:(2 * g + j + 1) * LANES])
                              for j in range(2)], axis=0)
        s_loc = jnp.where(valid, _scores(qs, k_loc), NEG)
        s_ctx = _scores(qs, ck_ref[0][:, ls])
        sink = _sink_column(sink_ref, 4 * g, 4, blk)
        o = _joint_softmax_pv([s_loc, s_ctx], [v_loc, cv_ref[0][:, ls]], sink)
        outs += [_merge_halves(o[:2 * blk], blk), _merge_halves(o[2 * blk:], blk)]
    o_ref[0] = jnp.concatenate(outs, axis=-1).astype(o_ref.dtype)


def _window_attn(q, kd, vd, ckd, cvd, sink):
    B, S, _ = q.shape
    nb = S // A_BLOCK
    kv_spec = lambda f: pl.BlockSpec((1, A_BLOCK, 2 * LANES), f)
    prev_map = lambda b, i: (b, jnp.maximum(i - 1, 0), 0)
    cur_map = lambda b, i: (b, i, 0)
    next_map = lambda b, i: (b, jnp.minimum(i + 1, nb - 1), 0)
    ctx_spec = pl.BlockSpec((1, CTX_LEN, 2 * LANES), lambda b, i: (b, 0, 0))
    return pl.pallas_call(
        functools.partial(_window_attn_kernel, S),
        grid=(B, nb),
        in_specs=[pl.BlockSpec(memory_space=pltpu.SMEM),
                  pl.BlockSpec((1, A_BLOCK, 4 * LANES), cur_map),
                  kv_spec(prev_map), kv_spec(cur_map), kv_spec(next_map),
                  kv_spec(prev_map), kv_spec(cur_map), kv_spec(next_map),
                  ctx_spec, ctx_spec],
        out_specs=pl.BlockSpec((1, A_BLOCK, 4 * LANES), cur_map),
        out_shape=jax.ShapeDtypeStruct((B, S, 4 * LANES), BF16),
        compiler_params=_cparams(("parallel", "parallel")),
        name="window_attn",
    )(sink, q, kd, kd, kd, vd, vd, vd, ckd, cvd)


NBR_ROWS = 4
NBR_KROWS = 12
NBR_PAIRS = 2


def _nbr_start_row(i, n_rows):
    return jnp.clip(i * NBR_ROWS - B_WIN_H // 2, 0, n_rows - NBR_KROWS)


def _nbr_attn_kernel(n_rows, q_ref, k_ref, v_ref, ck_ref, cv_ref, bias_ref, o_ref):
    i = pl.program_id(2)
    nq = NBR_ROWS * GRID_W
    nk = NBR_KROWS * GRID_W
    start = pl.multiple_of(_nbr_start_row(i, n_rows) * GRID_W, GRID_W)
    outs = []
    for pp in range(NBR_PAIRS):
        ls = slice(pp * LANES, (pp + 1) * LANES)
        k_loc = k_ref[0, pl.ds(start, nk), ls]
        v_loc = v_ref[0, pl.ds(start, nk), ls]
        qs = _stack_halves(q_ref[0][:, ls])
        s_loc = _scores(qs, k_loc) + bias_ref[0, 2 * pp:2 * pp + 2].reshape(2 * nq, nk)
        s_ctx = _scores(qs, ck_ref[0][:, ls])
        o = _joint_softmax_pv([s_loc, s_ctx], [v_loc, cv_ref[0][:, ls]])
        outs.append(_merge_halves(o, nq))
    o_ref[0] = jnp.concatenate(outs, axis=-1).astype(o_ref.dtype)


def _nbr_bias_table(rpb, n_rows):
    kh = B_WIN_H
    n_heads = rpb.shape[0]
    col = np.arange(GRID_W)
    cs = np.clip(col - B_WIN_W // 2, 0, GRID_W - B_WIN_W)
    col_ok = (col[None, :] >= cs[:, None]) & (col[None, :] < cs[:, None] + B_WIN_W)
    dc = np.clip(col[None, :] - col[:, None], -(B_WIN_W - 1), B_WIN_W - 1) + B_WIN_W - 1
    pick = (dc[..., None] == np.arange(2 * B_WIN_W - 1)).astype(np.float32)
    by_col = jnp.einsum("hdc,qkc->hdqk", rpb.astype(F32), pick, precision=lax.Precision.HIGHEST)
    by_col = jnp.where(col_ok[None, None], by_col, NEG)
    masked = jnp.full((n_heads, GRID_W, GRID_W), NEG, F32)
    tabs = []
    for r0 in (0, NBR_ROWS, n_rows - NBR_ROWS):
        start = int(np.clip(r0 - kh // 2, 0, n_rows - NBR_KROWS))
        per_row = []
        for ri in range(NBR_ROWS):
            r = r0 + ri
            rs = int(np.clip(r - kh // 2, 0, n_rows - kh))
            slabs = [by_col[:, start + kri - r + kh - 1] if rs <= start + kri < rs + kh else masked
                     for kri in range(NBR_KROWS)]
            per_row.append(jnp.stack(slabs, axis=2))
        tabs.append(jnp.stack(per_row, axis=1).reshape(n_heads, NBR_ROWS * GRID_W, NBR_KROWS * GRID_W))
    return jnp.stack(tabs)


def _nbr_attn(q, k, v, ck, cv, bias):
    B, S, _ = q.shape
    n_rows = S // GRID_W
    nsteps = n_rows // NBR_ROWS
    nq = NBR_ROWS * GRID_W
    nk = NBR_KROWS * GRID_W
    pat = lambda i: jnp.where(i == 0, 0, jnp.where(i == nsteps - 1, 2, 1))
    wl = NBR_PAIRS * LANES
    return pl.pallas_call(
        functools.partial(_nbr_attn_kernel, n_rows),
        grid=(B, 4 // NBR_PAIRS, nsteps),
        in_specs=[pl.BlockSpec((1, nq, wl), lambda b, p, i: (b, i, p)),
                  pl.BlockSpec((1, S, wl), lambda b, p, i: (b, 0, p)),
                  pl.BlockSpec((1, S, wl), lambda b, p, i: (b, 0, p)),
                  pl.BlockSpec((1, CTX_LEN, wl), lambda b, p, i: (b, 0, p)),
                  pl.BlockSpec((1, CTX_LEN, wl), lambda b, p, i: (b, 0, p)),
                  pl.BlockSpec((1, 2 * NBR_PAIRS, nq, nk), lambda b, p, i: (pat(i), p, 0, 0))],
        out_specs=pl.BlockSpec((1, nq, wl), lambda b, p, i: (b, i, p)),
        out_shape=jax.ShapeDtypeStruct((B, S, 4 * LANES), BF16),
        compiler_params=_cparams(("parallel", "parallel", "arbitrary")),
        name="nbr_attn",
    )(q, k, v, ck, cv, bias)


FULL_TQ = 256
FULL_TK = 512
FULL_NOMAX_LOG2_BOUND = 60.0


def _full_attn_kernel(bounded, q_ref, k_ref, v_ref, ck_ref, cv_ref, o_ref):
    tq = FULL_TQ
    q = q_ref[0]
    groups = range(2)
    lanes = [slice(g * LANES, (g + 1) * LANES) for g in groups]
    qs = [jnp.concatenate([_stack_halves(q[:, (2 * g + j) * LANES:(2 * g + j + 1) * LANES]) for j in range(2)],
                          axis=0) for g in groups]

    def step_bounded(g, acc, k, v):
        v_ones = jnp.where(_half_mask(v.shape), v, jnp.ones_like(v))
        p = jnp.exp2(_scores(qs[g], k))
        return acc + jnp.dot(p.astype(BF16), v_ones, preferred_element_type=F32)

    def step_online(g, carry, k, v):
        m, l, acc = carry
        s = _scores(qs[g], k)
        m_new = jnp.maximum(m, jnp.max(s, axis=-1, keepdims=True))
        alpha = jnp.exp2(m - m_new)
        p = jnp.exp2(s - m_new)
        l = l * alpha + jnp.sum(p, axis=-1, keepdims=True)
        acc = acc * alpha + jnp.dot(p.astype(BF16), v, preferred_element_type=F32)
        return m_new, l, acc

    step = step_bounded if bounded else step_online
    init = jnp.zeros((4 * tq, LANES), F32)
    if not bounded:
        init = (jnp.full((4 * tq, 1), NEG, F32), jnp.zeros((4 * tq, 1), F32), init)
    carry = tuple(step(g, init, ck_ref[0][:, lanes[g]], cv_ref[0][:, lanes[g]]) for g in groups)

    def body(j, carry):
        rows = pl.ds(pl.multiple_of(j * FULL_TK, FULL_TK), FULL_TK)
        return tuple(step(g, carry[g], k_ref[0, rows, lanes[g]], v_ref[0, rows, lanes[g]]) for g in groups)

    carry = lax.fori_loop(0, k_ref.shape[1] // FULL_TK, body, carry)
    pairs = []
    for g in groups:
        if bounded:
            o = carry[g] * pltpu.roll(1.0 / carry[g], HEAD_DIM, axis=1)
            merge = lambda a, b: jnp.where(_half_mask((tq, LANES)), a, pltpu.roll(b, HEAD_DIM, axis=1))
            pairs += [merge(o[0:tq], o[tq:2 * tq]), merge(o[2 * tq:3 * tq], o[3 * tq:])]
        else:
            o = carry[g][2] / carry[g][1]
            pairs += [_merge_halves(o[:2 * tq], tq), _merge_halves(o[2 * tq:], tq)]
    o_ref[0] = jnp.concatenate(pairs, axis=-1).astype(o_ref.dtype)


def _full_attn(q, kd, vd, ckd, cvd, bounded):
    B, S, W = q.shape
    return pl.pallas_call(
        functools.partial(_full_attn_kernel, bounded),
        grid=(B, S // FULL_TQ),
        in_specs=[pl.BlockSpec((1, FULL_TQ, W), lambda b, i: (b, i, 0)),
                  pl.BlockSpec((1, S, 2 * LANES), lambda b, i: (b, 0, 0)),
                  pl.BlockSpec((1, S, 2 * LANES), lambda b, i: (b, 0, 0)),
                  pl.BlockSpec((1, CTX_LEN, 2 * LANES), lambda b, i: (b, 0, 0)),
                  pl.BlockSpec((1, CTX_LEN, 2 * LANES), lambda b, i: (b, 0, 0))],
        out_specs=pl.BlockSpec((1, FULL_TQ, W), lambda b, i: (b, i, 0)),
        out_shape=jax.ShapeDtypeStruct((B, S, W), BF16),
        compiler_params=_cparams(("parallel", "arbitrary")),
        name="full_attn_bounded" if bounded else "full_attn_online",
    )(q, kd, vd, ckd, cvd)


def _ctx_attn_kernel(sink_ref, aq_ref, akd_ref, avd_ref, bq_ref, bk_ref, bv_ref, o_ref):
    n = CTX_LEN
    aq = aq_ref[0]
    bq = bq_ref[0]
    outs = []
    for g in range(2):
        ls = slice(g * LANES, (g + 1) * LANES)
        qs = jnp.concatenate([_stack_halves(aq[:, (2 * g + j) * LANES:(2 * g + j + 1) * LANES])
                              for j in range(2)], axis=0)
        sink = _sink_column(sink_ref, 4 * g, 4, n)
        o = _joint_softmax_pv([_scores(qs, akd_ref[0][:, ls])], [avd_ref[0][:, ls]], sink)
        outs += [_merge_halves(o[:2 * n], n), _merge_halves(o[2 * n:], n)]
    for p in range(4):
        ls = slice(p * LANES, (p + 1) * LANES)
        qs = _stack_halves(bq[:, ls])
        o = _joint_softmax_pv([_scores(qs, bk_ref[0][:, ls])], [bv_ref[0][:, ls]])
        outs.append(_merge_halves(o, n))
    o_ref[0] = jnp.concatenate(outs, axis=-1).astype(o_ref.dtype)


def _ctx_attn(sink, aq, akd, avd, bq, bk, bv):
    B = aq.shape[0]
    spec = lambda a: pl.BlockSpec((1,) + a.shape[1:], lambda b: (b, 0, 0))
    args = (aq, akd, avd, bq, bk, bv)
    return pl.pallas_call(
        _ctx_attn_kernel,
        grid=(B,),
        in_specs=[pl.BlockSpec(memory_space=pltpu.SMEM)] + [spec(a) for a in args],
        out_specs=pl.BlockSpec((1, CTX_LEN, 8 * LANES), lambda b: (b, 0, 0)),
        out_shape=jax.ShapeDtypeStruct((B, CTX_LEN, 8 * LANES), BF16),
        compiler_params=_cparams(("parallel",)),
        name="ctx_attn",
    )(sink, *args)


def _pick4(idx, vals):
    return jnp.where(idx == 0, vals[0], jnp.where(idx == 1, vals[1], jnp.where(idx == 2, vals[2], vals[3])))


def _route_rows(lg_t, b_ref):
    n_tok = lg_t.shape[1]
    s = [jax.nn.sigmoid(lg_t[e:e + 1, :]) for e in range(N_EXPERTS)]
    sel = [s[e] + b_ref[e] for e in range(N_EXPERTS)]
    n = EXPERTS_PER_GROUP
    gscore = []
    for j in range(N_GROUPS):
        v = sel[n * j:n * (j + 1)]
        pair_sums = [v[a] + v[b] for a in range(n) for b in range(a + 1, n)]
        gscore.append(functools.reduce(jnp.maximum, pair_sums))
    best, gbest = gscore[0], jnp.zeros((1, n_tok), jnp.int32)
    for j in range(1, N_GROUPS):
        upd = gscore[j] > best
        best = jnp.where(upd, gscore[j], best)
        gbest = jnp.where(upd, j, gbest)
    v = [_pick4(gbest, [sel[n * j + i] for j in range(N_GROUPS)]) for i in range(n)]
    u = [_pick4(gbest, [s[n * j + i] for j in range(N_GROUPS)]) for i in range(n)]
    m1, i1 = v[0], jnp.zeros((1, n_tok), jnp.int32)
    for i in range(1, n):
        upd = v[i] > m1
        m1 = jnp.where(upd, v[i], m1)
        i1 = jnp.where(upd, i, i1)
    m2, i2 = jnp.full((1, n_tok), -jnp.inf, F32), jnp.zeros((1, n_tok), jnp.int32)
    for i in range(n):
        upd = (i1 != i) & (v[i] > m2)
        m2 = jnp.where(upd, v[i], m2)
        i2 = jnp.where(upd, i, i2)
    u1, u2 = _pick4(i1, u), _pick4(i2, u)
    tot = u1 + u2
    return n * gbest + i1, n * gbest + i2, u1 / tot, u2 / tot


def _out_proj_kernel(n_y, sparse, *refs):
    y_refs = refs[:n_y]
    br_ref, w_ref, x_ref, gate_ref, g_ref, shift_ref, scale_ref, wrh_ref, wrl_ref = refs[n_y:n_y + 9]
    n_in = n_y + 9
    if sparse:
        tri_ref, tril_ref = refs[n_in:n_in + 2]
        n_in += 2
    outs = refs[n_in:]
    xo_ref = outs[0]
    off = 0
    acc = None
    for y_ref in y_refs:
        wdt = y_ref.shape[-1]
        part = jnp.dot(y_ref[0].astype(BF16), w_ref[off:off + wdt, :], preferred_element_type=F32)
        acc = part if acc is None else acc + part
        off += wdt
    x = x_ref[0] + gate_ref[0] * acc
    xo_ref[0] = x
    ms = jnp.mean(x * x, axis=-1, keepdims=True)
    h = x * lax.rsqrt(ms + EPS) * g_ref[...]
    h = h * (1.0 + scale_ref[0]) + shift_ref[0]
    hh = h.astype(BF16)
    hl = (h - hh.astype(F32)).astype(BF16)
    lg = (jnp.dot(hh, wrh_ref[...], preferred_element_type=F32)
          + jnp.dot(hl, wrh_ref[...], preferred_element_type=F32)
          + jnp.dot(hh, wrl_ref[...], preferred_element_type=F32))
    lg_t = lg.T[:N_EXPERTS]
    e1, e2, w1, w2 = _route_rows(lg_t, br_ref)
    rows = lax.broadcasted_iota(jnp.int32, lg_t.shape, 0)
    if not sparse:
        h_ref, comb_ref = outs[1:]
        h_ref[0] = hh
        comb_t = jnp.where(rows == e1, w1, 0.0) + jnp.where(rows == e2, w2, 0.0)
        comb_ref[0] = jnp.concatenate(
            [comb_t, jnp.zeros((LANES - N_EXPERTS, comb_t.shape[1]), F32)], axis=0).T
        return

    h_ref, route_t_ref, route_c_ref, ng_ref = outs[1:]
    h_ref[0] = hh
    member = jnp.where((rows == e1) | (rows == e2), 1.0, 0.0)
    before = jnp.dot(member.astype(BF16), tri_ref[...], preferred_element_type=F32)
    groups = jnp.floor((jnp.sum(member, axis=1, keepdims=True) + (MOE_G - 1)) * (1.0 / MOE_G))
    groups = jnp.broadcast_to(groups, (N_EXPERTS, LANES))
    run_start = MOE_G * jnp.dot(tril_ref[...], groups.astype(BF16), preferred_element_type=F32)[:, 0:1]
    pos = run_start + before
    p1 = jnp.sum(jnp.where(rows == e1, pos, 0.0), axis=0, keepdims=True)
    p2 = jnp.sum(jnp.where(rows == e2, pos, 0.0), axis=0, keepdims=True)
    ng_ref[0] = groups
    field = lax.broadcasted_iota(jnp.int32, (LANES, h.shape[0]), 0)
    route = jnp.zeros((LANES, h.shape[0]), F32)
    for k, v in enumerate((p1, p2, w1, w2)):
        route = jnp.where(field == k, v, route)
    route_t_ref[...] = route[:ROUTE_FIELDS]
    route_c_ref[...] = route.T


ROUTE_FIELDS = 8
MOE_G = 8
MOE_TT = 512
MOE_LOCAL = 1152


def _out_proj(ys, w, x, gate, g, shift, scale, router, sparse, tm=512):
    B, S, D = x.shape
    tm = min(tm, S)
    bm = gate.shape[0]
    b_router, wr_hi, wr_lo = router
    nt = S // tm
    mod_map = (lambda b, i: (b, 0, 0)) if bm > 1 else (lambda b, i: (0, 0, 0))
    mod_spec = pl.BlockSpec((1, 1, D), mod_map)
    row_map = lambda b, i: (b, i, 0)
    in_specs = ([pl.BlockSpec((1, tm, y.shape[-1]), row_map) for y in ys]
                + [pl.BlockSpec(memory_space=pltpu.SMEM),
                   pl.BlockSpec(w.shape, lambda b, i: (0, 0)),
                   pl.BlockSpec((1, tm, D), row_map), mod_spec,
                   pl.BlockSpec((1, D), lambda b, i: (0, 0)), mod_spec, mod_spec,
                   pl.BlockSpec(wr_hi.shape, lambda b, i: (0, 0)),
                   pl.BlockSpec(wr_lo.shape, lambda b, i: (0, 0))])
    args = list(ys) + [b_router, w, x, gate.reshape(bm, 1, D), g.reshape(1, D), shift.reshape(bm, 1, D),
                       scale.reshape(bm, 1, D), wr_hi, wr_lo]
    out_specs = [pl.BlockSpec((1, tm, D), row_map)]
    out_shape = [jax.ShapeDtypeStruct((B, S, D), F32)]
    out_specs.append(pl.BlockSpec((1, tm, D), row_map))
    out_shape.append(jax.ShapeDtypeStruct((B, S, D), BF16))
    if sparse:
        assert tm == MOE_TT
        tri = jnp.asarray(np.triu(np.ones((tm, tm), np.float32), 1)).astype(BF16)
        tril = jnp.asarray(np.tril(np.ones((N_EXPERTS, N_EXPERTS), np.float32), -1)).astype(BF16)
        args += [tri, tril]
        in_specs += [pl.BlockSpec(tri.shape, lambda b, i: (0, 0)),
                     pl.BlockSpec(tril.shape, lambda b, i: (0, 0))]
        out_specs += [pl.BlockSpec((ROUTE_FIELDS, tm), lambda b, i: (0, b * nt + i)),
                      pl.BlockSpec((tm, LANES), lambda b, i: (b * nt + i, 0)),
                      pl.BlockSpec((1, N_EXPERTS, LANES), lambda b, i: (b * nt + i, 0, 0))]
        out_shape += [jax.ShapeDtypeStruct((ROUTE_FIELDS, B * S), F32),
                      jax.ShapeDtypeStruct((B * S, LANES), F32),
                      jax.ShapeDtypeStruct((B * nt, N_EXPERTS, LANES), F32)]
    else:
        out_specs.append(pl.BlockSpec((1, tm, LANES), row_map))
        out_shape.append(jax.ShapeDtypeStruct((B, S, LANES), F32))
    return pl.pallas_call(
        functools.partial(_out_proj_kernel, len(ys), sparse),
        grid=(B, nt),
        in_specs=in_specs,
        out_specs=out_specs,
        out_shape=out_shape,
        compiler_params=_cparams(("parallel", "parallel")),
        name="out_proj_sparse" if sparse else "out_proj",
    )(*args)


def _moe_kernel(h_ref, comb_ref, wg_ref, wu_ref, wd_ref, x_ref, gate_ref, o_ref, acc_ref):
    e = pl.program_id(2)

    @pl.when(e == 0)
    def _():
        acc_ref[...] = jnp.zeros_like(acc_ref)

    h = h_ref[0]
    a = jnp.dot(h, wg_ref[0, 0].astype(BF16), preferred_element_type=F32)
    u = jnp.dot(h, wu_ref[0, 0].astype(BF16), preferred_element_type=F32)
    he = (a * jax.nn.sigmoid(a) * u).astype(BF16)
    y = jnp.dot(he, wd_ref[0, 0].astype(BF16), preferred_element_type=F32)
    lane = lax.broadcasted_iota(jnp.int32, comb_ref.shape[1:], 1)
    c = jnp.sum(jnp.where(lane == e, comb_ref[0], 0.0), axis=-1, keepdims=True)
    acc_ref[...] += c * y

    @pl.when(e == pl.num_programs(2) - 1)
    def _():
        o_ref[0] = x_ref[0] + gate_ref[0] * acc_ref[...]


def _moe(h, comb, wg, wu, wd, layer, x, gate, tm=1024):
    B, S, D = x.shape
    tm = min(tm, S)
    bm = gate.shape[0]
    mod_map = (lambda b, i, e: (b, 0, 0)) if bm > 1 else (lambda b, i, e: (0, 0, 0))
    row_map = lambda b, i, e: (b, i, 0)
    return pl.pallas_call(
        _moe_kernel,
        grid=(B, S // tm, N_EXPERTS),
        in_specs=[pl.BlockSpec((1, tm, D), row_map),
                  pl.BlockSpec((1, tm, LANES), row_map),
                  pl.BlockSpec((1, 1, D, D_EXPERT), lambda b, i, e: (layer, e, 0, 0)),
                  pl.BlockSpec((1, 1, D, D_EXPERT), lambda b, i, e: (layer, e, 0, 0)),
                  pl.BlockSpec((1, 1, D_EXPERT, D), lambda b, i, e: (layer, e, 0, 0)),
                  pl.BlockSpec((1, tm, D), row_map),
                  pl.BlockSpec((1, 1, D), mod_map)],
        out_specs=pl.BlockSpec((1, tm, D), row_map),
        out_shape=jax.ShapeDtypeStruct((B, S, D), F32),
        scratch_shapes=[pltpu.VMEM((tm, D), F32)],
        compiler_params=_cparams(("parallel", "parallel", "arbitrary")),
        name="moe",
    )(h, comb, wg, wu, wd, x, gate.reshape(bm, 1, D))


MOE_TM = 512
MOE_TG = MOE_TM // MOE_G


def _moe_rows(n_tok):
    rows = 2 * n_tok + (n_tok // MOE_TT) * N_EXPERTS * (MOE_G - 1) + N_EXPERTS * (MOE_TM - 1)
    return (rows + MOE_TM - 1) // MOE_TM * MOE_TM


def _moe_plan(ng):
    ng = ng[:, :, 0].astype(jnp.int32)
    n_tt = ng.shape[0]
    total = jnp.sum(ng, axis=0)
    region = (total + MOE_TG - 1) // MOE_TG * MOE_TG
    region_end = jnp.cumsum(region)
    region_start = region_end - region
    dst = region_start[None, :] + jnp.cumsum(ng, axis=0) - ng
    local = jnp.cumsum(ng, axis=1) - ng
    n_tiles = _moe_rows(n_tt * MOE_TT) // MOE_TM
    tile_first = jnp.arange(n_tiles, dtype=jnp.int32) * MOE_TG
    tile_expert = jnp.minimum(jnp.sum(region_end[None, :] <= tile_first[:, None], axis=1), N_EXPERTS - 1)
    n_valid = region_end[-1:] // MOE_TG
    i32 = lambda a: a.astype(jnp.int32).reshape(-1)
    return dict(ng=i32(ng), dst=i32(dst), local=i32(local), tile_groups=i32(jnp.sum(ng, axis=1)),
                pad_first=i32(region_start + total),
                pad_count=i32(region - total), tile_expert=i32(tile_expert), n_valid=i32(n_valid),
                n_tiles=n_tiles)


def _run_copies(plan_refs, tile, local_ref, sorted_ref, sem, to_sorted):
    ng_ref, dst_ref, loc_ref = plan_refs
    for e in range(N_EXPERTS):
        k = tile * N_EXPERTS + e
        loc, dst = loc_ref[k], dst_ref[k]

        def body(g, carry):
            lrows = local_ref.at[pl.ds(pl.multiple_of((loc + g) * MOE_G, MOE_G), MOE_G)]
            srows = sorted_ref.at[pl.ds(pl.multiple_of((dst + g) * MOE_G, MOE_G), MOE_G)]
            src, tgt = (lrows, srows) if to_sorted else (srows, lrows)
            pltpu.make_async_copy(src, tgt, sem).start()
            return carry
        lax.fori_loop(0, ng_ref[k], body, 0)


def _group_waits(n_groups, local_ref, sorted_ref, sem):
    def body(g, carry):
        pltpu.make_async_copy(sorted_ref.at[pl.ds(0, MOE_G)], local_ref.at[pl.ds(0, MOE_G)], sem).wait()
        return carry
    lax.fori_loop(0, n_groups, body, 0)


def _moe_dispatch_kernel(ng_ref, dst_ref, loc_ref, tot_ref, padf_ref, padc_ref, nv_ref,
                         h_ref, route_ref, xs_ref, local_ref, zero_ref, sem):
    i = pl.program_id(0)

    @pl.when(i == 0)
    def _():
        zero_ref[...] = jnp.zeros_like(zero_ref)
        n_pad = 0
        for e in range(N_EXPERTS):
            first = padf_ref[e]

            def body(g, carry):
                rows = xs_ref.at[pl.ds(pl.multiple_of((first + g) * MOE_G, MOE_G), MOE_G)]
                pltpu.make_async_copy(zero_ref.at[pl.ds(0, MOE_G)], rows, sem.at[2]).start()
                return carry
            lax.fori_loop(0, padc_ref[e], body, 0)
            n_pad = n_pad + padc_ref[e]

        def tile_copy(j):
            rows = xs_ref.at[pl.ds(pl.multiple_of(j * MOE_TM, MOE_TM), MOE_TM)]
            return pltpu.make_async_copy(zero_ref, rows, sem.at[3])

        n_tiles = xs_ref.shape[0] // MOE_TM
        lax.fori_loop(nv_ref[0], n_tiles, lambda j, c: (tile_copy(j).start(), c)[1], 0)
        _group_waits(n_pad, zero_ref, xs_ref, sem.at[2])
        lax.fori_loop(nv_ref[0], n_tiles, lambda j, c: (tile_copy(j).wait(), c)[1], 0)

    pos = lax.broadcasted_iota(jnp.int32, (MOE_LOCAL, MOE_TT), 0)
    p1 = route_ref[0:1, :].astype(jnp.int32)
    p2 = route_ref[1:2, :].astype(jnp.int32)
    pick = jnp.where((pos == p1) | (pos == p2), 1.0, 0.0).astype(BF16)
    slot = i % 2
    last = pl.num_programs(0) - 1

    @pl.when(i >= 2)
    def _():
        _group_waits(tot_ref[i - 2], local_ref.at[slot], xs_ref, sem.at[slot])

    local_ref[slot] = jnp.dot(pick, h_ref[0], preferred_element_type=F32)
    _run_copies((ng_ref, dst_ref, loc_ref), i, local_ref.at[slot], xs_ref, sem.at[slot], to_sorted=True)

    @pl.when(i == last)
    def _():
        _group_waits(tot_ref[i], local_ref.at[slot], xs_ref, sem.at[slot])

    @pl.when((i == last) & (last >= 1))
    def _():
        _group_waits(tot_ref[i - 1], local_ref.at[1 - slot], xs_ref, sem.at[1 - slot])


def _moe_dispatch(h, route_t, plan):
    B, S, D = h.shape
    nt = S // MOE_TT
    grid_spec = pltpu.PrefetchScalarGridSpec(
        num_scalar_prefetch=7,
        grid=(B * nt,),
        in_specs=[pl.BlockSpec((1, MOE_TT, D), lambda i, *_: (i // nt, i % nt, 0)),
                  pl.BlockSpec((ROUTE_FIELDS, MOE_TT), lambda i, *_: (0, i))],
        out_specs=pl.BlockSpec(memory_space=pl.ANY),
        scratch_shapes=[pltpu.VMEM((2, MOE_LOCAL, D), F32), pltpu.VMEM((MOE_TM, D), F32),
                        pltpu.SemaphoreType.DMA((4,))])
    return pl.pallas_call(
        _moe_dispatch_kernel,
        grid_spec=grid_spec,
        out_shape=jax.ShapeDtypeStruct((plan["n_tiles"] * MOE_TM, D), F32),
        compiler_params=_cparams(("arbitrary",)),
        name="moe_dispatch",
    )(plan["ng"], plan["dst"], plan["local"], plan["tile_groups"], plan["pad_first"], plan["pad_count"],
      plan["n_valid"], h, route_t)


def _moe_grouped_kernel(te_ref, nv_ref, x_ref, wg_ref, wu_ref, wd_ref, o_ref, wgb_ref, wub_ref, wdb_ref):
    j = pl.program_id(0)
    used = j < nv_ref[0]

    @pl.when(used & ((j == 0) | (te_ref[j] != te_ref[jnp.maximum(j - 1, 0)])))
    def _():
        wgb_ref[...] = wg_ref[0, 0].astype(BF16)
        wub_ref[...] = wu_ref[0, 0].astype(BF16)
        wdb_ref[...] = wd_ref[0, 0].astype(BF16)

    @pl.when(used)
    def _():
        x = x_ref[...].astype(BF16)
        a = jnp.dot(x, wgb_ref[...], preferred_element_type=F32)
        u = jnp.dot(x, wub_ref[...], preferred_element_type=F32)
        he = (a * jax.nn.sigmoid(a) * u).astype(BF16)
        o_ref[...] = jnp.dot(he, wdb_ref[...], preferred_element_type=F32)

    @pl.when(jnp.logical_not(used))
    def _():
        o_ref[...] = jnp.zeros_like(o_ref)


def _moe_grouped(xs, plan, wg, wu, wd, layer):
    n_tiles = plan["n_tiles"]
    D = D_MODEL
    tile = lambda j, nv: jnp.minimum(j, nv[0] - 1)
    grid_spec = pltpu.PrefetchScalarGridSpec(
        num_scalar_prefetch=2,
        grid=(n_tiles,),
        in_specs=[pl.BlockSpec((MOE_TM, D), lambda j, te, nv: (tile(j, nv), 0)),
                  pl.BlockSpec((1, 1, D, D_EXPERT), lambda j, te, nv: (layer, te[tile(j, nv)], 0, 0)),
                  pl.BlockSpec((1, 1, D, D_EXPERT), lambda j, te, nv: (layer, te[tile(j, nv)], 0, 0)),
                  pl.BlockSpec((1, 1, D_EXPERT, D), lambda j, te, nv: (layer, te[tile(j, nv)], 0, 0))],
        out_specs=pl.BlockSpec((MOE_TM, D), lambda j, te, nv: (j, 0)),
        scratch_shapes=[pltpu.VMEM((D, D_EXPERT), BF16), pltpu.VMEM((D, D_EXPERT), BF16),
                        pltpu.VMEM((D_EXPERT, D), BF16)])
    return pl.pallas_call(
        _moe_grouped_kernel,
        grid_spec=grid_spec,
        out_shape=jax.ShapeDtypeStruct((n_tiles * MOE_TM, D), F32),
        compiler_params=_cparams(("arbitrary",)),
        name="moe_grouped",
    )(plan["tile_expert"], plan["n_valid"], xs, wg, wu, wd)


def _moe_combine_kernel(final_norm, ng_ref, dst_ref, loc_ref, tot_ref, ys_ref, route_ref, x_ref, gate_ref,
                        *rest):
    if final_norm:
        fg_ref, o_ref, local_ref, sem = rest
    else:
        o_ref, local_ref, sem = rest
    i = pl.program_id(0)
    slot = i % 2
    plan_refs = (ng_ref, dst_ref, loc_ref)

    @pl.when(i == 0)
    def _():
        local_ref[...] = jnp.zeros_like(local_ref)
        _run_copies(plan_refs, 0, local_ref.at[0], ys_ref, sem.at[0], to_sorted=False)

    @pl.when(i + 1 < pl.num_programs(0))
    def _():
        _run_copies(plan_refs, i + 1, local_ref.at[1 - slot], ys_ref, sem.at[1 - slot], to_sorted=False)

    _group_waits(tot_ref[i], local_ref.at[slot], ys_ref, sem.at[slot])
    pos = lax.broadcasted_iota(jnp.int32, (MOE_TT, MOE_LOCAL), 1)
    route = route_ref[...]
    p1 = route[:, 0:1].astype(jnp.int32)
    p2 = route[:, 1:2].astype(jnp.int32)
    weigh = (jnp.where(pos == p1, route[:, 2:3], 0.0) + jnp.where(pos == p2, route[:, 3:4], 0.0)).astype(BF16)
    y = jnp.dot(weigh, local_ref[slot].astype(BF16), preferred_element_type=F32)
    o = x_ref[0] + gate_ref[0] * y
    if final_norm:
        o = o * lax.rsqrt(jnp.mean(o * o, axis=-1, keepdims=True) + EPS) * fg_ref[...]
    o_ref[0] = o


def _moe_combine(ys, route_c, plan, x, gate, final_g=None):
    B, S, D = x.shape
    nt = S // MOE_TT
    in_specs = [pl.BlockSpec(memory_space=pl.ANY),
                pl.BlockSpec((MOE_TT, LANES), lambda i, *_: (i, 0)),
                pl.BlockSpec((1, MOE_TT, D), lambda i, *_: (i // nt, i % nt, 0)),
                pl.BlockSpec((1, 1, D), lambda i, *_: (i // nt, 0, 0))]
    args = [ys, route_c, x, gate.reshape(B, 1, D)]
    if final_g is not None:
        in_specs.append(pl.BlockSpec((1, D), lambda i, *_: (0, 0)))
        args.append(final_g.reshape(1, D))
    grid_spec = pltpu.PrefetchScalarGridSpec(
        num_scalar_prefetch=4,
        grid=(B * nt,),
        in_specs=in_specs,
        out_specs=pl.BlockSpec((1, MOE_TT, D), lambda i, *_: (i // nt, i % nt, 0)),
        scratch_shapes=[pltpu.VMEM((2, MOE_LOCAL, D), F32), pltpu.SemaphoreType.DMA((2,))])
    return pl.pallas_call(
        functools.partial(_moe_combine_kernel, final_g is not None),
        grid_spec=grid_spec,
        out_shape=jax.ShapeDtypeStruct((B, S, D), F32),
        compiler_params=_cparams(("arbitrary",)),
        name="moe_combine",
    )(plan["ng"], plan["dst"], plan["local"], plan["tile_groups"], *args)


def _dft_tables():
    n1, n2, n = FFT_N1, FFT_N2, FFT_N1 * FFT_N2
    k1 = np.arange(n1)
    f1 = np.exp(-2j * np.pi * np.outer(k1, np.arange(n1)) / n1)
    tw = np.exp(-2j * np.pi * np.outer(np.arange(n2), k1) / n)
    ftw = f1[None, :, :] * tw[:, :, None]
    half = n1 // 2
    fh = ftw[:, :, :half]
    g_fwd = np.concatenate([np.concatenate([fh.real, -fh.imag], axis=2),
                            np.concatenate([fh.imag, fh.real], axis=2)], axis=1)
    back = ftw[:, :, ::-1][:, :, :half].copy()
    back[0] = np.roll(ftw[0], -1, axis=1)[:, ::-1][:, :half]
    back[0][:, 0] = 0.0
    fk = np.concatenate([fh, back], axis=2)
    g_real = np.concatenate([fk.real, fk.imag], axis=1)
    gi = np.conj(np.transpose(fh, (0, 2, 1))) / n
    g_inv = np.concatenate([np.concatenate([gi.real, -gi.imag], axis=2),
                            np.concatenate([gi.imag, gi.real], axis=2)], axis=1)
    f2 = np.exp(-2j * np.pi * np.outer(np.arange(n2), np.arange(n2)) / n2)
    f2_fwd = np.block([[f2.real, -f2.imag], [f2.imag, f2.real]])
    f2c = np.conj(f2)
    f2_inv = np.block([[f2c.real, -f2c.imag], [f2c.imag, f2c.real]])
    as_bf = lambda a: jnp.asarray(a, dtype=F32).astype(BF16)
    return as_bf(g_fwd), as_bf(g_real), as_bf(g_inv), as_bf(f2_fwd), as_bf(f2_inv)


def _fft_fast_stage(stage_ref, k1, f2):
    slab = STAGE_PITCH
    m = jnp.concatenate([stage_ref[pl.ds(k1, FFT_N2, stride=slab), :],
                         stage_ref[pl.ds(FFT_N1 + k1, FFT_N2, stride=slab), :]], axis=0)
    return jnp.dot(f2, m.astype(BF16), preferred_element_type=F32)


def _filter_fft_kernel(hf_ref, hb_ref, inv_ref, g_ref, f2_ref, h_ref, stage_ref):
    slab = STAGE_PITCH
    half = FFT_N1 // 2
    for n2 in range(FFT_N2):
        x = jnp.concatenate([hf_ref[pl.ds(n2, half, stride=FFT_N2), :],
                             hb_ref[pl.ds((FFT_N2 - n2) % FFT_N2, half, stride=FFT_N2), :]], axis=0)
        stage_ref[n2 * slab:n2 * slab + 2 * FFT_N1, :] = jnp.dot(g_ref[n2], x.astype(BF16),
                                                          preferred_element_type=F32)
    f2 = f2_ref[...]
    inv = inv_ref[...]
    for k1 in range(FFT_N1):
        h_ref[0, k1] = (_fft_fast_stage(stage_ref, k1, f2) * inv).astype(h_ref.dtype)


def _filter_fft(taps, inv_norm, g_real, f2_fwd, ct=LANES):
    L, cols = taps.shape
    C = D_CH
    n_ord = cols // (2 * C)
    nc = C // ct
    once = pl.Buffered(1)
    return pl.pallas_call(
        _filter_fft_kernel,
        grid=(n_ord, nc),
        in_specs=[pl.BlockSpec((L, ct), lambda o, c: (0, o * nc + c)),
                  pl.BlockSpec((L, ct), lambda o, c: (0, (n_ord + o) * nc + c)),
                  pl.BlockSpec((1, ct), lambda o, c: (0, o * nc + c)),
                  pl.BlockSpec(g_real.shape, lambda o, c: (0, 0, 0), pipeline_mode=once),
                  pl.BlockSpec(f2_fwd.shape, lambda o, c: (0, 0), pipeline_mode=once)],
        out_specs=pl.BlockSpec((1, FFT_N1, 2 * FFT_N2, ct), lambda o, c: (o, 0, 0, c)),
        out_shape=jax.ShapeDtypeStruct((n_ord, FFT_N1, 2 * FFT_N2, C), BF16),
        scratch_shapes=[pltpu.VMEM((FFT_N2 * STAGE_PITCH, ct), F32)],
        compiler_params=_cparams(("parallel", "parallel")),
        name="filter_fft",
    )(taps, taps, inv_norm, g_real, f2_fwd)


class _RowSets:
    def __init__(self, ref, member):
        self.ref, self.member, self.loaded = ref, member, {}

    def __call__(self, start):
        if start not in self.loaded:
            self.loaded = {k: v for k, v in self.loaded.items() if abs(k - start) <= 2}
            self.loaded[start] = self.ref[self.member, pl.ds(start, FFT_N1 // 2, stride=FFT_N2), :]
        return self.loaded[start]


def _strided_rows(at, n2, taps):
    half = FFT_N1 // 2
    cur = at(n2)
    if taps is None:
        return cur
    w_ref, b_ref = taps
    n1 = lax.broadcasted_iota(jnp.int32, cur.shape, 0)
    if n2 > 0:
        prev = at(n2 - 1)
    else:
        prev = jnp.where(n1 == 0, 0.0, pltpu.roll(at(FFT_N2 - 1), 1, axis=0))
    if n2 < FFT_N2 - 1:
        nxt = at(n2 + 1)
    else:
        nxt = jnp.where(n1 == half - 1, 0.0, pltpu.roll(at(0), half - 1, axis=0))
    return prev * w_ref[0:1, :] + cur * w_ref[1:2, :] + nxt * w_ref[2:3, :] + b_ref[...]


def _hyena_conv_kernel(conv_z, *refs):
    it = iter(refs)
    z_ref, gate_ref = next(it), next(it)
    z_taps = (next(it), next(it)) if conv_z else None
    gate_taps = (next(it), next(it))
    hb_ref, spec_ref, gf_ref, gi_ref, f2f_ref, f2i_ref, o_ref, stage_ref = it
    half = FFT_N1 // 2
    slab = STAGE_PITCH
    z_sets = [_RowSets(z_ref, m) for m in range(2)]
    for n2 in range(FFT_N2):
        x = jnp.concatenate([_strided_rows(z_sets[m], n2, z_taps) for m in range(2)], axis=0).astype(BF16)
        stage_ref[n2 * slab:n2 * slab + 2 * FFT_N1, :] = jnp.dot(gf_ref[n2], x, preferred_element_type=F32)
    f2f = f2f_ref[...]
    f2i = f2i_ref[...]
    for k1 in range(FFT_N1):
        zf = _fft_fast_stage(stage_ref, k1, f2f)
        zr, zi = zf[:FFT_N2], zf[FFT_N2:]
        hr = spec_ref[0, k1, :FFT_N2, :].astype(F32)
        hi = spec_ref[0, k1, FFT_N2:, :].astype(F32)
        p = jnp.concatenate([zr * hr - zi * hi, zr * hi + zi * hr], axis=0).astype(BF16)
        q = jnp.dot(f2i, p, preferred_element_type=F32)
        stage_ref[pl.ds(k1, FFT_N2, stride=slab), :] = q[:FFT_N2]
        stage_ref[pl.ds(FFT_N1 + k1, FFT_N2, stride=slab), :] = q[FFT_N2:]
    hb = hb_ref[...]
    z_sets = [_RowSets(z_ref, m) for m in range(2)]
    gate_sets = [_RowSets(gate_ref, m) for m in range(2)]
    for n2 in range(FFT_N2):
        y_in = stage_ref[n2 * slab:n2 * slab + 2 * FFT_N1, :].astype(BF16)
        y = jnp.dot(gi_ref[n2], y_in, preferred_element_type=F32)
        for m in range(2):
            zm = _strided_rows(z_sets[m], n2, z_taps)
            gm = _strided_rows(gate_sets[m], n2, gate_taps)
            o_ref[m, pl.ds(n2, half, stride=FFT_N2), :] = gm * (y[m * half:(m + 1) * half] + zm * hb)


def _hyena_conv(z, z_blk, gate, gate_blk, short_w, short_b, conv_z, hbias, spec, order, tabs, ct=LANES):
    B, L, _ = z.shape
    C = D_CH
    g_fwd, _, g_inv, f2_fwd, f2_inv = tabs
    once = pl.Buffered(1)
    const3 = lambda a: pl.BlockSpec(a.shape, lambda c, p: (0, 0, 0), pipeline_mode=once)
    const2 = lambda a: pl.BlockSpec(a.shape, lambda c, p: (0, 0), pipeline_mode=once)
    taps_specs = lambda blk: [pl.BlockSpec((3, ct), lambda c, p: (0, blk + c)),
                              pl.BlockSpec((1, ct), lambda c, p: (0, blk + c))]
    in_specs = [pl.BlockSpec((2, L, ct), lambda c, p: (p, 0, z_blk + c)),
                pl.BlockSpec((2, L, ct), lambda c, p: (p, 0, gate_blk + c))]
    args = [z, gate]
    if conv_z:
        in_specs += taps_specs(z_blk)
        args += [short_w, short_b]
    in_specs += taps_specs(gate_blk)
    args += [short_w, short_b]
    in_specs += [pl.BlockSpec((1, ct), lambda c, p: (0, c)),
                 pl.BlockSpec((1, FFT_N1, 2 * FFT_N2, ct), lambda c, p: (order, 0, 0, c), pipeline_mode=once),
                 const3(g_fwd), const3(g_inv), const2(f2_fwd), const2(f2_inv)]
    args += [hbias, spec, g_fwd, g_inv, f2_fwd, f2_inv]
    return pl.pallas_call(
        functools.partial(_hyena_conv_kernel, conv_z),
        grid=(C // ct, B // 2),
        in_specs=in_specs,
        out_specs=pl.BlockSpec((2, L, ct), lambda c, p: (p, 0, c)),
        out_shape=jax.ShapeDtypeStruct((B, L, C), F32),
        scratch_shapes=[pltpu.VMEM((FFT_N2 * STAGE_PITCH, ct), F32)],
        compiler_params=_cparams(("parallel", "arbitrary")),
        name="hyena_conv",
    )(*args)


def _hyena_filters(L, w1, b1, f1, w2, b2, f2, w3, b3):
    t = jnp.arange(L, dtype=F32)
    tn = t / max(L - 1, 1)
    bands = jnp.linspace(1e-4, HY_BANDS - 1, HY_BANDS, dtype=F32)
    ang = 2.0 * math.pi * t[:, None] * bands[None] / L
    feats = jnp.concatenate([tn[:, None], jnp.cos(ang), jnp.sin(ang)], axis=-1)
    h = jnp.sin(f1 * (feats @ w1 + b1))
    h = jnp.sin(f2 * (h @ w2 + b2))
    deltas = jnp.abs(jnp.linspace(HY_MIN_DECAY, HY_MAX_DECAY, D_CH, dtype=F32))
    decay = jnp.exp(-tn[:, None] * deltas[None])
    n_rep = w3.shape[1] // D_CH
    taps = (_mm_f32(h, w3, 512) + b3) * jnp.tile(decay, (1, n_rep))
    l1 = jnp.sum(jnp.abs(taps), axis=0)
    l1 = l1[:n_rep // 2 * D_CH] + l1[n_rep // 2 * D_CH:]
    return taps, (1.0 / (l1 + EPS))[None]


def _dup_heads(w):
    a, b = w[:, :HEAD_DIM], w[:, HEAD_DIM:]
    return jnp.concatenate([a, a, b, b], axis=1)


def _rope_tables(S):
    t = jnp.arange(S)
    row = (t // GRID_W).astype(F32)
    col = (t % GRID_W).astype(F32)
    half = HEAD_DIM // 2
    inv = ROPE_THETA ** (-jnp.arange(0, half, 2, dtype=F32) / half)
    ar = row[:, None] * inv[None]
    ac = col[:, None] * inv[None]
    cos = jnp.concatenate([jnp.cos(ar), jnp.cos(ar), jnp.cos(ac), jnp.cos(ac)], axis=-1)
    sin = jnp.concatenate([-jnp.sin(ar), jnp.sin(ar), -jnp.sin(ac), jnp.sin(ac)], axis=-1)
    return jnp.tile(cos, (1, 2)), jnp.tile(sin, (1, 2))


def _head_mean_matrix(width):
    blk = np.kron(np.eye(width // HEAD_DIM), np.full((HEAD_DIM, HEAD_DIM), 1.0 / HEAD_DIM))
    return jnp.asarray(blk, dtype=F32).astype(BF16)


def kernel(x, c, ctx, c_ctx, w_ada, b_ada, norm_g, final_g, w_in_even, w_out_even, a_sink, b_rpb, w_in_odd, w_out_odd, c_qnorm, c_knorm, hy_short_w, hy_short_b, hy_w1, hy_b1, hy_f1, hy_w2, hy_b2, hy_f2, hy_w3, hy_b3, hy_bias, w_router, b_router, moe_wg, moe_wu, moe_wd):
    B, S, D = x.shape
    depth = w_ada.shape[0]
    rope = _rope_tables(S)
    wr_pad = jnp.pad(w_router.astype(F32), ((0, 0), (0, LANES - N_EXPERTS)))
    wr_hi = wr_pad.astype(BF16)
    router = (b_router.astype(F32), wr_hi, (wr_pad - wr_hi.astype(F32)).astype(BF16))

    mod_in = jnp.concatenate([jax.nn.silu(c), jax.nn.silu(c_ctx)[None],
                              jnp.zeros((8 - B - 1, D), F32)], axis=0)
    xc = ctx
    for l in range(depth):
        need_ctx = l < depth - 1
        mod = _mm_f32(mod_in, w_ada[l], 1536) + b_ada[l]
        mx = mod[:B].reshape(B, 6, D)
        mc = mod[B].reshape(6, D)
        i = l // 2
        if l % 2 == 0:
            w = w_in_even[i].astype(BF16)
            w_all = jnp.concatenate([w[:, :512], _dup_heads(w[:, 512:640]), _dup_heads(w[:, 640:768]),
                                     w[:, 768:]], axis=1)
            segs_x = ((512, "rope", ATTN_SCALE), (256, "rope", 1.0), (256, "plain", 1.0),
                      (512, "plain", ATTN_SCALE), (512, "plain", 1.0), (512, "plain", 1.0))
            aq, akd, avd, bq, bk, bv = _norm_proj(x, norm_g[l, 0], mx[:, 0], mx[:, 1], w_all, segs_x,
                                                  rope=rope)
            segs_c = tuple((wd, "plain", m) for wd, _, m in segs_x)
            caq, cakd, cavd, cbq, cbk, cbv = _norm_proj(xc, norm_g[l, 0], mc[0:1], mc[1:2], w_all, segs_c)
            ya = _window_attn(aq, akd, avd, cakd, cavd, a_sink[i].astype(F32))
            yb = _nbr_attn(bq, bk, bv, cbk, cbv, _nbr_bias_table(b_rpb[i], S // GRID_W))
            ys = [ya, yb]
            w_out = w_out_even[i].astype(BF16)
            if need_ctx:
                yc = [_ctx_attn(a_sink[i].astype(F32), caq, cakd, cavd, cbq, cbk, cbv)]
        else:
            w = w_in_odd[i].astype(BF16)
            w_all = jnp.concatenate([w[:, :512], _dup_heads(w[:, 512:640]), _dup_heads(w[:, 640:768]),
                                     w[:, 768:]], axis=1)
            gains = jnp.concatenate([jnp.tile(c_qnorm[i], 8), jnp.tile(c_knorm[i], 4)])[None].astype(F32)
            norm = (_head_mean_matrix(512), gains)
            q_mult = ATTN_SCALE * math.log2(math.e)
            segs_x = ((512, "normrope", q_mult), (256, "normrope", 1.0), (256, "plain", 1.0),
                      (3 * D_CH, "plain", 1.0, F32))
            qx, kxd, vxd, ux = _norm_proj(x, norm_g[l, 0], mx[:, 0], mx[:, 1], w_all, segs_x,
                                          rope=rope, norm=norm)
            w_c = w_all[:, 512:1024]
            norm_c = (_head_mean_matrix(512), jnp.tile(c_knorm[i], 4)[None].astype(F32))
            kcd, vcd = _norm_proj(xc, norm_g[l, 0], mc[0:1], mc[1:2], w_c,
                                  ((256, "norm", 1.0), (256, "plain", 1.0)), norm=norm_c)
            logit_bound = (1.02 * HEAD_DIM * q_mult * jnp.max(jnp.abs(c_qnorm[i]))
                           * jnp.max(jnp.abs(c_knorm[i])))
            y_attn = lax.cond(logit_bound <= FULL_NOMAX_LOG2_BOUND,
                              lambda *a: _full_attn(*a, bounded=True),
                              lambda *a: _full_attn(*a, bounded=False),
                              qx, kxd, vxd, kcd, vcd)
            sw, sb = hy_short_w[i].astype(F32), hy_short_b[i].astype(F32)[None]
            tabs = _dft_tables()
            taps, inv_norm = _hyena_filters(S, hy_w1[i], hy_b1[i], hy_f1[i], hy_w2[i], hy_b2[i],
                                            hy_f2[i], hy_w3[i], hy_b3[i])
            spec = _filter_fft(taps, inv_norm, tabs[1], tabs[3])
            blocks = D_CH // LANES
            z = _hyena_conv(ux, 0, ux, blocks, sw, sb, True, hy_bias[i, 0:1], spec, 0, tabs)
            z = _hyena_conv(z, 0, ux, 2 * blocks, sw, sb, False, hy_bias[i, 1:2], spec, 1, tabs)
            ys = [y_attn, z]
            w_out = w_out_odd[i].astype(BF16)
            if need_ctx:
                raise NotImplementedError("context update of an odd layer is not needed at this depth")

        x, hx, route_t, route_c, ng = _out_proj(ys, w_out, x, mx[:, 2], norm_g[l, 1], mx[:, 3], mx[:, 4],
                                                router, sparse=True)
        if need_ctx:
            xc, hc, comb_c = _out_proj(yc, w_out, xc, mc[2:3], norm_g[l, 1], mc[3:4], mc[4:5], router,
                                       sparse=False)
            flat = lambda a: a.reshape(1, -1, a.shape[-1])
            xc = _moe(flat(hc), flat(comb_c), moe_wg, moe_wu, moe_wd, l, flat(xc), mc[5:6]).reshape(xc.shape)
        plan = _moe_plan(ng)
        sorted_rows = _moe_dispatch(hx, route_t, plan)
        x = _moe_combine(_moe_grouped(sorted_rows, plan, moe_wg, moe_wu, moe_wd, l), route_c, plan, x, mx[:, 5],
                         final_g=None if need_ctx else final_g)
    return x
```

```python
import functools
import math

import numpy as np
import jax
import jax.numpy as jnp
from jax import lax
from jax.experimental import pallas as pl
from jax.experimental.pallas import tpu as pltpu

F32 = jnp.float32
BF16 = jnp.bfloat16

D_MODEL = 1024
GRID_W = 64
CTX_LEN = 256
HEAD_DIM = 64
ROPE_THETA = 10000.0
EPS = 1e-6
ATTN_SCALE = HEAD_DIM ** -0.5
LOG2_E = math.log2(math.e)
Q_MULT = ATTN_SCALE * LOG2_E
A_WINDOW = 128
A_BLOCK = 128
B_WIN_H = 8
B_WIN_W = 16
D_CH = 512
HY_BANDS = 16
HY_MAX_DECAY = math.log(1e-2) / 0.3
HY_MIN_DECAY = math.log(1e-2) / 1.5
N_EXPERTS = 16
N_GROUPS = 4
EXPERTS_PER_GROUP = N_EXPERTS // N_GROUPS
TOP_K = 2
D_EXPERT = 512

LANES = 128
NEG = -1e30
VMEM_LIMIT = 48 * 1024 * 1024

FFT_N1 = 64
FFT_N2 = 128
STAGE_PITCH = 2 * FFT_N1 + 8


def _cparams(sem):
    return pltpu.CompilerParams(dimension_semantics=sem, vmem_limit_bytes=VMEM_LIMIT)


def _mm_f32_kernel(x_ref, w_ref, o_ref):
    o_ref[...] = jnp.dot(x_ref[...], w_ref[...], preferred_element_type=F32)


def _mm_f32(x, w, tn):
    M, K = x.shape
    N = w.shape[1]
    return pl.pallas_call(
        _mm_f32_kernel,
        grid=(N // tn,),
        in_specs=[pl.BlockSpec((M, K), lambda j: (0, 0)),
                  pl.BlockSpec((K, tn), lambda j: (0, j))],
        out_specs=pl.BlockSpec((M, tn), lambda j: (0, j)),
        out_shape=jax.ShapeDtypeStruct((M, N), F32),
        compiler_params=_cparams(("arbitrary",)),
        name="mm_f32",
    )(x, w)


def _swap16(y):
    n = y.shape[-1]
    lane = lax.broadcasted_iota(jnp.int32, y.shape, y.ndim - 1)
    up = pltpu.roll(y, n - 16, axis=y.ndim - 1)
    dn = pltpu.roll(y, 16, axis=y.ndim - 1)
    return jnp.where((lane % 32) < 16, up, dn)


def _tile_lanes(t, width):
    reps = width // t.shape[-1]
    return t if reps == 1 else jnp.concatenate([t] * reps, axis=-1)


def _norm_proj_kernel(segs, has_rope, has_norm, *refs):
    it = iter(refs)
    x_ref, g_ref, shift_ref, scale_ref, w_ref = (next(it) for _ in range(5))
    cos_ref = sin_ref = bd_ref = gain_ref = None
    if has_rope:
        cos_ref, sin_ref = next(it), next(it)
    if has_norm:
        bd_ref, gain_ref = next(it), next(it)
    out_refs = list(it)

    x = x_ref[0]
    ms = jnp.mean(x * x, axis=-1, keepdims=True)
    h = x * lax.rsqrt(ms + EPS) * g_ref[...]
    h = h * (1.0 + scale_ref[0]) + shift_ref[0]
    y = jnp.dot(h.astype(BF16), w_ref[...], preferred_element_type=F32)

    off = 0
    goff = 0
    for (width, kind, mult, *_), o_ref in zip(segs, out_refs):
        ys = y[:, off:off + width]
        if kind in ("norm", "normrope"):
            bd = bd_ref[...][:width, :width]
            hms = jnp.dot((ys * ys).astype(BF16), bd, preferred_element_type=F32)
            ys = ys * lax.rsqrt(hms + EPS) * gain_ref[:, goff:goff + width]
            goff += width
        if kind in ("rope", "normrope"):
            c = _tile_lanes(cos_ref[...], width)
            s = _tile_lanes(sin_ref[...], width)
            ys = ys * c + _swap16(ys) * s
        if mult != 1.0:
            ys = ys * mult
        o_ref[0] = ys.astype(o_ref.dtype)
        off += width


def _norm_proj(x, g, shift, scale, w, segs, rope=None, norm=None, tm=512):
    B, S, D = x.shape
    N = w.shape[1]
    tm = min(tm, S)
    bm = shift.shape[0]
    mod_map = (lambda b, i: (b, 0, 0)) if bm > 1 else (lambda b, i: (0, 0, 0))
    args = [x, g.reshape(1, D), shift.reshape(bm, 1, D), scale.reshape(bm, 1, D), w]
    in_specs = [pl.BlockSpec((1, tm, D), lambda b, i: (b, i, 0)),
                pl.BlockSpec((1, D), lambda b, i: (0, 0)),
                pl.BlockSpec((1, 1, D), mod_map),
                pl.BlockSpec((1, 1, D), mod_map),
                pl.BlockSpec((D, N), lambda b, i: (0, 0))]
    if rope is not None:
        args += [rope[0], rope[1]]
        in_specs += [pl.BlockSpec((tm, LANES), lambda b, i: (i, 0))] * 2
    if norm is not None:
        args += [norm[0], norm[1]]
        in_specs += [pl.BlockSpec(norm[0].shape, lambda b, i: (0, 0)),
                     pl.BlockSpec(norm[1].shape, lambda b, i: (0, 0))]
    out_shape = [jax.ShapeDtypeStruct((B, S, seg[0]), seg[3] if len(seg) > 3 else BF16) for seg in segs]
    out_specs = [pl.BlockSpec((1, tm, seg[0]), lambda b, i: (b, i, 0)) for seg in segs]
    return pl.pallas_call(
        functools.partial(_norm_proj_kernel, segs, rope is not None, norm is not None),
        grid=(B, S // tm),
        in_specs=in_specs,
        out_specs=out_specs,
        out_shape=out_shape,
        compiler_params=_cparams(("parallel", "parallel")),
        name="norm_proj",
    )(*args)


def _half_mask(shape):
    return lax.broadcasted_iota(jnp.int32, shape, len(shape) - 1) < HEAD_DIM


def _stack_halves(qp):
    lo = _half_mask(qp.shape)
    zero = jnp.zeros_like(qp)
    return jnp.concatenate([jnp.where(lo, qp, zero), jnp.where(lo, zero, qp)], axis=0)


def _merge_halves(o, m):
    return jnp.where(_half_mask((m, LANES)), o[:m], o[m:])


def _scores(q, k):
    return lax.dot_general(q, k, (((1,), (1,)), ((), ())), preferred_element_type=F32)


def _joint_softmax_pv(score_parts, value_parts, extra_logit=None):
    m = functools.reduce(jnp.maximum, [jnp.max(s, axis=-1, keepdims=True) for s in score_parts])
    if extra_logit is not None:
        m = jnp.maximum(m, extra_logit)
    den = jnp.exp2(extra_logit - m) if extra_logit is not None else 0.0
    acc = None
    for s, v in zip(score_parts, value_parts):
        p = jnp.exp2(s - m)
        den = den + jnp.sum(p, axis=-1, keepdims=True)
        pv = jnp.dot(p.astype(BF16), v, preferred_element_type=F32)
        acc = pv if acc is None else acc + pv
    return acc / den


def _sink_column(sink_ref, first_head, n_heads, rows_per_head):
    rows = lax.broadcasted_iota(jnp.int32, (n_heads * rows_per_head, 1), 0)
    col = jnp.zeros((n_heads * rows_per_head, 1), F32)
    for j in range(n_heads):
        in_head = (rows >= j * rows_per_head) & (rows < (j + 1) * rows_per_head)
        col = jnp.where(in_head, sink_ref[first_head + j], col)
    return col


def _window_attn_kernel(seq_len, sink_ref, q_ref, kp_ref, kc_ref, kn_ref, vp_ref, vc_ref, vn_ref,
                        ck_ref, cv_ref, o_ref):
    i = pl.program_id(1)
    blk = A_BLOCK
    q = q_ref[0]
    rows = lax.broadcasted_iota(jnp.int32, (4 * blk, 3 * blk), 0) % blk
    rel = lax.broadcasted_iota(jnp.int32, (4 * blk, 3 * blk), 1) - blk
    gpos = i * blk + rel
    valid = (jnp.abs(rows - rel) <= A_WINDOW) & (gpos >= 0) & (gpos < seq_len)
    outs = []
    for g in range(2):
        ls = slice(g * LANES, (g + 1) * LANES)
        k_loc = jnp.concatenate([kp_ref[0][:, ls], kc_ref[0][:, ls], kn_ref[0][:, ls]], axis=0)
        v_loc = jnp.concatenate([vp_ref[0][:, ls], vc_ref[0][:, ls], vn_ref[0][:, ls]], axis=0)
        qs = jnp.concatenate([_stack_halves(q[:, (2 * g + j) * LANES:(2 * g + j + 1) * LANES])
                              for j in range(2)], axis=0)
        s_loc = jnp.where(valid, _scores(qs, k_loc), NEG)
        s_ctx = _scores(qs, ck_ref[0][:, ls])
        sink = _sink_column(sink_ref, 4 * g, 4, blk)
        o = _joint_softmax_pv([s_loc, s_ctx], [v_loc, cv_ref[0][:, ls]], sink)
        outs += [_merge_halves(o[:2 * blk], blk), _merge_halves(o[2 * blk:], blk)]
    o_ref[0] = jnp.concatenate(outs, axis=-1).astype(o_ref.dtype)


def _window_attn(q, kd, vd, ckd, cvd, sink):
    B, S, _ = q.shape
    nb = S // A_BLOCK
    kv_spec = lambda f: pl.BlockSpec((1, A_BLOCK, 2 * LANES), f)
    prev_map = lambda b, i: (b, jnp.maximum(i - 1, 0), 0)
    cur_map = lambda b, i: (b, i, 0)
    next_map = lambda b, i: (b, jnp.minimum(i + 1, nb - 1), 0)
    ctx_spec = pl.BlockSpec((1, CTX_LEN, 2 * LANES), lambda b, i: (b, 0, 0))
    return pl.pallas_call(
        functools.partial(_window_attn_kernel, S),
        grid=(B, nb),
        in_specs=[pl.BlockSpec(memory_space=pltpu.SMEM),
                  pl.BlockSpec((1, A_BLOCK, 4 * LANES), cur_map),
                  kv_spec(prev_map), kv_spec(cur_map), kv_spec(next_map),
                  kv_spec(prev_map), kv_spec(cur_map), kv_spec(next_map),
                  ctx_spec, ctx_spec],
        out_specs=pl.BlockSpec((1, A_BLOCK, 4 * LANES), cur_map),
        out_shape=jax.ShapeDtypeStruct((B, S, 4 * LANES), BF16),
        compiler_params=_cparams(("parallel", "parallel")),
        name="window_attn",
    )(sink, q, kd, kd, kd, vd, vd, vd, ckd, cvd)


NBR_ROWS = 4
NBR_KROWS = 12
NBR_PAIRS = 4


def _nbr_start_row(i, n_rows):
    return jnp.clip(i * NBR_ROWS - B_WIN_H // 2, 0, n_rows - NBR_KROWS)


def _nbr_attn_kernel(n_rows, q_ref, k_ref, v_ref, ck_ref, cv_ref, bias_ref, o_ref):
    i = pl.program_id(2)
    nq = NBR_ROWS * GRID_W
    nk = NBR_KROWS * GRID_W
    start = pl.multiple_of(_nbr_start_row(i, n_rows) * GRID_W, GRID_W)
    outs = []
    for pp in range(NBR_PAIRS):
        ls = slice(pp * LANES, (pp + 1) * LANES)
        k_loc = k_ref[0, pl.ds(start, nk), ls]
        v_loc = v_ref[0, pl.ds(start, nk), ls]
        qs = _stack_halves(q_ref[0][:, ls])
        s_loc = _scores(qs, k_loc) + bias_ref[0, 2 * pp:2 * pp + 2].reshape(2 * nq, nk)
        s_ctx = _scores(qs, ck_ref[0][:, ls])
        o = _joint_softmax_pv([s_loc, s_ctx], [v_loc, cv_ref[0][:, ls]])
        outs.append(_merge_halves(o, nq))
    o_ref[0] = jnp.concatenate(outs, axis=-1).astype(o_ref.dtype)


def _nbr_bias_table(rpb, n_rows):
    kh = B_WIN_H
    n_heads = rpb.shape[0]
    col = np.arange(GRID_W)
    cs = np.clip(col - B_WIN_W // 2, 0, GRID_W - B_WIN_W)
    col_ok = (col[None, :] >= cs[:, None]) & (col[None, :] < cs[:, None] + B_WIN_W)
    dc = np.clip(col[None, :] - col[:, None], -(B_WIN_W - 1), B_WIN_W - 1) + B_WIN_W - 1
    pick_col = (dc[..., None] == np.arange(2 * B_WIN_W - 1)).astype(np.float32)
    r0 = np.array([0, NBR_ROWS, n_rows - NBR_ROWS])
    start = np.clip(r0 - kh // 2, 0, n_rows - NBR_KROWS)
    r = r0[:, None] + np.arange(NBR_ROWS)[None]
    rs = np.clip(r - kh // 2, 0, n_rows - kh)
    kr = start[:, None] + np.arange(NBR_KROWS)[None]
    row_ok = (kr[:, None, :] >= rs[:, :, None]) & (kr[:, None, :] < rs[:, :, None] + kh)
    dr = np.clip(kr[:, None, :] - r[:, :, None] + kh - 1, 0, 2 * kh - 2)
    pick_row = (dr[..., None] == np.arange(2 * kh - 1)).astype(np.float32)
    table = jnp.einsum("prkd,hdc,qjc->phrqkj", pick_row, rpb.astype(F32), pick_col,
                       precision=lax.Precision.HIGHEST)
    ok = row_ok[:, None, :, None, :, None] & col_ok[None, None, None, :, None, :]
    return jnp.where(ok, table, NEG).reshape(3, n_heads, NBR_ROWS * GRID_W, NBR_KROWS * GRID_W)


def _nbr_attn(q, k, v, ck, cv, bias):
    B, S, _ = q.shape
    n_rows = S // GRID_W
    nsteps = n_rows // NBR_ROWS
    nq = NBR_ROWS * GRID_W
    nk = NBR_KROWS * GRID_W
    pat = lambda i: jnp.where(i == 0, 0, jnp.where(i == nsteps - 1, 2, 1))
    wl = NBR_PAIRS * LANES
    return pl.pallas_call(
        functools.partial(_nbr_attn_kernel, n_rows),
        grid=(B, 4 // NBR_PAIRS, nsteps),
        in_specs=[pl.BlockSpec((1, nq, wl), lambda b, p, i: (b, i, p)),
                  pl.BlockSpec((1, S, wl), lambda b, p, i: (b, 0, p)),
                  pl.BlockSpec((1, S, wl), lambda b, p, i: (b, 0, p)),
                  pl.BlockSpec((1, CTX_LEN, wl), lambda b, p, i: (b, 0, p)),
                  pl.BlockSpec((1, CTX_LEN, wl), lambda b, p, i: (b, 0, p)),
                  pl.BlockSpec((1, 2 * NBR_PAIRS, nq, nk), lambda b, p, i: (pat(i), p, 0, 0))],
        out_specs=pl.BlockSpec((1, nq, wl), lambda b, p, i: (b, i, p)),
        out_shape=jax.ShapeDtypeStruct((B, S, 4 * LANES), BF16),
        compiler_params=_cparams(("parallel", "parallel", "arbitrary")),
        name="nbr_attn",
    )(q, k, v, ck, cv, bias)


FULL_TQ = 256
FULL_TK = 512
FULL_NOMAX_LOG2_BOUND = 60.0


def _full_attn_kernel(bounded, q_ref, k_ref, v_ref, ck_ref, cv_ref, o_ref):
    tq = FULL_TQ
    q = q_ref[0]
    groups = range(2)
    lanes = [slice(g * LANES, (g + 1) * LANES) for g in groups]
    qs = [jnp.concatenate([_stack_halves(q[:, (2 * g + j) * LANES:(2 * g + j + 1) * LANES]) for j in range(2)],
                          axis=0) for g in groups]

    def step_bounded(g, acc, k, v):
        v_ones = jnp.where(_half_mask(v.shape), v, jnp.ones_like(v))
        p = jnp.exp2(_scores(qs[g], k))
        return acc + jnp.dot(p.astype(BF16), v_ones, preferred_element_type=F32)

    def step_online(g, carry, k, v):
        m, l, acc = carry
        s = _scores(qs[g], k)
        m_new = jnp.maximum(m, jnp.max(s, axis=-1, keepdims=True))
        alpha = jnp.exp2(m - m_new)
        p = jnp.exp2(s - m_new)
        l = l * alpha + jnp.sum(p, axis=-1, keepdims=True)
        acc = acc * alpha + jnp.dot(p.astype(BF16), v, preferred_element_type=F32)
        return m_new, l, acc

    step = step_bounded if bounded else step_online
    init = jnp.zeros((4 * tq, LANES), F32)
    if not bounded:
        init = (jnp.full((4 * tq, 1), NEG, F32), jnp.zeros((4 * tq, 1), F32), init)
    carry = tuple(step(g, init, ck_ref[0][:, lanes[g]], cv_ref[0][:, lanes[g]]) for g in groups)

    def body(j, carry):
        rows = pl.ds(pl.multiple_of(j * FULL_TK, FULL_TK), FULL_TK)
        return tuple(step(g, carry[g], k_ref[0, rows, lanes[g]], v_ref[0, rows, lanes[g]]) for g in groups)

    carry = lax.fori_loop(0, k_ref.shape[1] // FULL_TK, body, carry)
    pairs = []
    for g in groups:
        if bounded:
            o = carry[g] * pltpu.roll(1.0 / carry[g], HEAD_DIM, axis=1)
            merge = lambda a, b: jnp.where(_half_mask((tq, LANES)), a, pltpu.roll(b, HEAD_DIM, axis=1))
            pairs += [merge(o[0:tq], o[tq:2 * tq]), merge(o[2 * tq:3 * tq], o[3 * tq:])]
        else:
            o = carry[g][2] / carry[g][1]
            pairs += [_merge_halves(o[:2 * tq], tq), _merge_halves(o[2 * tq:], tq)]
    o_ref[0] = jnp.concatenate(pairs, axis=-1).astype(o_ref.dtype)


def _full_attn(q, kd, vd, ckd, cvd, bounded):
    B, S, W = q.shape
    return pl.pallas_call(
        functools.partial(_full_attn_kernel, bounded),
        grid=(B, S // FULL_TQ),
        in_specs=[pl.BlockSpec((1, FULL_TQ, W), lambda b, i: (b, i, 0)),
                  pl.BlockSpec((1, S, 2 * LANES), lambda b, i: (b, 0, 0)),
                  pl.BlockSpec((1, S, 2 * LANES), lambda b, i: (b, 0, 0)),
                  pl.BlockSpec((1, CTX_LEN, 2 * LANES), lambda b, i: (b, 0, 0)),
                  pl.BlockSpec((1, CTX_LEN, 2 * LANES), lambda b, i: (b, 0, 0))],
        out_specs=pl.BlockSpec((1, FULL_TQ, W), lambda b, i: (b, i, 0)),
        out_shape=jax.ShapeDtypeStruct((B, S, W), BF16),
        compiler_params=_cparams(("parallel", "arbitrary")),
        name="full_attn_bounded" if bounded else "full_attn_online",
    )(q, kd, vd, ckd, cvd)


def _ctx_attn_kernel(sink_ref, aq_ref, akd_ref, avd_ref, bq_ref, bk_ref, bv_ref, o_ref):
    n = CTX_LEN
    aq = aq_ref[0]
    bq = bq_ref[0]
    outs = []
    for g in range(2):
        ls = slice(g * LANES, (g + 1) * LANES)
        qs = jnp.concatenate([_stack_halves(aq[:, (2 * g + j) * LANES:(2 * g + j + 1) * LANES])
                              for j in range(2)], axis=0)
        sink = _sink_column(sink_ref, 4 * g, 4, n)
        o = _joint_softmax_pv([_scores(qs, akd_ref[0][:, ls])], [avd_ref[0][:, ls]], sink)
        outs += [_merge_halves(o[:2 * n], n), _merge_halves(o[2 * n:], n)]
    for p in range(4):
        ls = slice(p * LANES, (p + 1) * LANES)
        qs = _stack_halves(bq[:, ls])
        o = _joint_softmax_pv([_scores(qs, bk_ref[0][:, ls])], [bv_ref[0][:, ls]])
        outs.append(_merge_halves(o, n))
    o_ref[0] = jnp.concatenate(outs, axis=-1).astype(o_ref.dtype)


def _ctx_attn(sink, aq, akd, avd, bq, bk, bv):
    B = aq.shape[0]
    spec = lambda a: pl.BlockSpec((1,) + a.shape[1:], lambda b: (b, 0, 0))
    args = (aq, akd, avd, bq, bk, bv)
    return pl.pallas_call(
        _ctx_attn_kernel,
        grid=(B,),
        in_specs=[pl.BlockSpec(memory_space=pltpu.SMEM)] + [spec(a) for a in args],
        out_specs=pl.BlockSpec((1, CTX_LEN, 8 * LANES), lambda b: (b, 0, 0)),
        out_shape=jax.ShapeDtypeStruct((B, CTX_LEN, 8 * LANES), BF16),
        compiler_params=_cparams(("parallel",)),
        name="ctx_attn",
    )(sink, *args)


def _pick4(idx, vals):
    return jnp.where(idx == 0, vals[0], jnp.where(idx == 1, vals[1], jnp.where(idx == 2, vals[2], vals[3])))


def _route_rows(lg_t, b_ref):
    n_tok = lg_t.shape[1]
    s = [jax.nn.sigmoid(lg_t[e:e + 1, :]) for e in range(N_EXPERTS)]
    sel = [s[e] + b_ref[e] for e in range(N_EXPERTS)]
    n = EXPERTS_PER_GROUP
    gscore = []
    for j in range(N_GROUPS):
        v = sel[n * j:n * (j + 1)]
        pair_sums = [v[a] + v[b] for a in range(n) for b in range(a + 1, n)]
        gscore.append(functools.reduce(jnp.maximum, pair_sums))
    best, gbest = gscore[0], jnp.zeros((1, n_tok), jnp.int32)
    for j in range(1, N_GROUPS):
        upd = gscore[j] > best
        best = jnp.where(upd, gscore[j], best)
        gbest = jnp.where(upd, j, gbest)
    v = [_pick4(gbest, [sel[n * j + i] for j in range(N_GROUPS)]) for i in range(n)]
    u = [_pick4(gbest, [s[n * j + i] for j in range(N_GROUPS)]) for i in range(n)]
    m1, i1 = v[0], jnp.zeros((1, n_tok), jnp.int32)
    for i in range(1, n):
        upd = v[i] > m1
        m1 = jnp.where(upd, v[i], m1)
        i1 = jnp.where(upd, i, i1)
    m2, i2 = jnp.full((1, n_tok), -jnp.inf, F32), jnp.zeros((1, n_tok), jnp.int32)
    for i in range(n):
        upd = (i1 != i) & (v[i] > m2)
        m2 = jnp.where(upd, v[i], m2)
        i2 = jnp.where(upd, i, i2)
    u1, u2 = _pick4(i1, u), _pick4(i2, u)
    tot = u1 + u2
    return n * gbest + i1, n * gbest + i2, u1 / tot, u2 / tot


def _out_proj_kernel(n_y, sparse, *refs):
    y_refs = refs[:n_y]
    br_ref, w_ref, x_ref, gate_ref, g_ref, shift_ref, scale_ref, wrh_ref, wrl_ref = refs[n_y:n_y + 9]
    n_in = n_y + 9
    if sparse:
        tri_ref, tril_ref = refs[n_in:n_in + 2]
        n_in += 2
    outs = refs[n_in:]
    xo_ref = outs[0]
    off = 0
    acc = None
    for y_ref in y_refs:
        wdt = y_ref.shape[-1]
        part = jnp.dot(y_ref[0].astype(BF16), w_ref[off:off + wdt, :], preferred_element_type=F32)
        acc = part if acc is None else acc + part
        off += wdt
    x = x_ref[0] + gate_ref[0] * acc
    xo_ref[0] = x
    ms = jnp.mean(x * x, axis=-1, keepdims=True)
    h = x * lax.rsqrt(ms + EPS) * g_ref[...]
    h = h * (1.0 + scale_ref[0]) + shift_ref[0]
    hh = h.astype(BF16)
    hl = (h - hh.astype(F32)).astype(BF16)
    lg = (jnp.dot(hh, wrh_ref[...], preferred_element_type=F32)
          + jnp.dot(hl, wrh_ref[...], preferred_element_type=F32)
          + jnp.dot(hh, wrl_ref[...], preferred_element_type=F32))
    lg_t = lg.T[:N_EXPERTS]
    e1, e2, w1, w2 = _route_rows(lg_t, br_ref)
    rows = lax.broadcasted_iota(jnp.int32, lg_t.shape, 0)
    if not sparse:
        h_ref, comb_ref = outs[1:]
        h_ref[0] = hh
        comb_t = jnp.where(rows == e1, w1, 0.0) + jnp.where(rows == e2, w2, 0.0)
        comb_ref[0] = jnp.concatenate(
            [comb_t, jnp.zeros((LANES - N_EXPERTS, comb_t.shape[1]), F32)], axis=0).T
        return

    h_ref, route_t_ref, route_c_ref, ng_ref = outs[1:]
    h_ref[0] = hh
    member = jnp.where((rows == e1) | (rows == e2), 1.0, 0.0)
    before = jnp.dot(member.astype(BF16), tri_ref[...], preferred_element_type=F32)
    groups = jnp.floor((jnp.sum(member, axis=1, keepdims=True) + (MOE_G - 1)) * (1.0 / MOE_G))
    groups = jnp.broadcast_to(groups, (N_EXPERTS, LANES))
    run_start = MOE_G * jnp.dot(tril_ref[...], groups.astype(BF16), preferred_element_type=F32)[:, 0:1]
    pos = run_start + before
    p1 = jnp.sum(jnp.where(rows == e1, pos, 0.0), axis=0, keepdims=True)
    p2 = jnp.sum(jnp.where(rows == e2, pos, 0.0), axis=0, keepdims=True)
    ng_ref[0] = groups
    field = lax.broadcasted_iota(jnp.int32, (LANES, h.shape[0]), 0)
    route = jnp.zeros((LANES, h.shape[0]), F32)
    for k, v in enumerate((p1, p2, w1, w2)):
        route = jnp.where(field == k, v, route)
    route_t_ref[...] = route[:ROUTE_FIELDS]
    route_c_ref[...] = route.T


ROUTE_FIELDS = 8
MOE_G = 8
MOE_TT = 512
MOE_LOCAL = 1152


def _out_proj(ys, w, x, gate, g, shift, scale, router, sparse, tm=512):
    B, S, D = x.shape
    tm = min(tm, S)
    bm = gate.shape[0]
    b_router, wr_hi, wr_lo = router
    nt = S // tm
    mod_map = (lambda b, i: (b, 0, 0)) if bm > 1 else (lambda b, i: (0, 0, 0))
    mod_spec = pl.BlockSpec((1, 1, D), mod_map)
    row_map = lambda b, i: (b, i, 0)
    in_specs = ([pl.BlockSpec((1, tm, y.shape[-1]), row_map) for y in ys]
                + [pl.BlockSpec(memory_space=pltpu.SMEM),
                   pl.BlockSpec(w.shape, lambda b, i: (0, 0)),
                   pl.BlockSpec((1, tm, D), row_map), mod_spec,
                   pl.BlockSpec((1, D), lambda b, i: (0, 0)), mod_spec, mod_spec,
                   pl.BlockSpec(wr_hi.shape, lambda b, i: (0, 0)),
                   pl.BlockSpec(wr_lo.shape, lambda b, i: (0, 0))])
    args = list(ys) + [b_router, w, x, gate.reshape(bm, 1, D), g.reshape(1, D), shift.reshape(bm, 1, D),
                       scale.reshape(bm, 1, D), wr_hi, wr_lo]
    out_specs = [pl.BlockSpec((1, tm, D), row_map)]
    out_shape = [jax.ShapeDtypeStruct((B, S, D), F32)]
    out_specs.append(pl.BlockSpec((1, tm, D), row_map))
    out_shape.append(jax.ShapeDtypeStruct((B, S, D), BF16))
    if sparse:
        assert tm == MOE_TT
        tri = jnp.asarray(np.triu(np.ones((tm, tm), np.float32), 1)).astype(BF16)
        tril = jnp.asarray(np.tril(np.ones((N_EXPERTS, N_EXPERTS), np.float32), -1)).astype(BF16)
        args += [tri, tril]
        in_specs += [pl.BlockSpec(tri.shape, lambda b, i: (0, 0)),
                     pl.BlockSpec(tril.shape, lambda b, i: (0, 0))]
        out_specs += [pl.BlockSpec((ROUTE_FIELDS, tm), lambda b, i: (0, b * nt + i)),
                      pl.BlockSpec((tm, LANES), lambda b, i: (b * nt + i, 0)),
                      pl.BlockSpec((1, N_EXPERTS, LANES), lambda b, i: (b * nt + i, 0, 0))]
        out_shape += [jax.ShapeDtypeStruct((ROUTE_FIELDS, B * S), F32),
                      jax.ShapeDtypeStruct((B * S, LANES), F32),
                      jax.ShapeDtypeStruct((B * nt, N_EXPERTS, LANES), F32)]
    else:
        out_specs.append(pl.BlockSpec((1, tm, LANES), row_map))
        out_shape.append(jax.ShapeDtypeStruct((B, S, LANES), F32))
    return pl.pallas_call(
        functools.partial(_out_proj_kernel, len(ys), sparse),
        grid=(B, nt),
        in_specs=in_specs,
        out_specs=out_specs,
        out_shape=out_shape,
        compiler_params=_cparams(("parallel", "parallel")),
        name="out_proj_sparse" if sparse else "out_proj",
    )(*args)


def _moe_kernel(h_ref, comb_ref, wg_ref, wu_ref, wd_ref, x_ref, gate_ref, o_ref, acc_ref):
    e = pl.program_id(2)

    @pl.when(e == 0)
    def _():
        acc_ref[...] = jnp.zeros_like(acc_ref)

    h = h_ref[0]
    a = jnp.dot(h, wg_ref[0, 0].astype(BF16), preferred_element_type=F32)
    u = jnp.dot(h, wu_ref[0, 0].astype(BF16), preferred_element_type=F32)
    he = (a * jax.nn.sigmoid(a) * u).astype(BF16)
    y = jnp.dot(he, wd_ref[0, 0].astype(BF16), preferred_element_type=F32)
    lane = lax.broadcasted_iota(jnp.int32, comb_ref.shape[1:], 1)
    c = jnp.sum(jnp.where(lane == e, comb_ref[0], 0.0), axis=-1, keepdims=True)
    acc_ref[...] += c * y

    @pl.when(e == pl.num_programs(2) - 1)
    def _():
        o_ref[0] = x_ref[0] + gate_ref[0] * acc_ref[...]


def _moe(h, comb, wg, wu, wd, layer, x, gate, tm=1024):
    B, S, D = x.shape
    tm = min(tm, S)
    bm = gate.shape[0]
    mod_map = (lambda b, i, e: (b, 0, 0)) if bm > 1 else (lambda b, i, e: (0, 0, 0))
    row_map = lambda b, i, e: (b, i, 0)
    return pl.pallas_call(
        _moe_kernel,
        grid=(B, S // tm, N_EXPERTS),
        in_specs=[pl.BlockSpec((1, tm, D), row_map),
                  pl.BlockSpec((1, tm, LANES), row_map),
                  pl.BlockSpec((1, 1, D, D_EXPERT), lambda b, i, e: (layer, e, 0, 0)),
                  pl.BlockSpec((1, 1, D, D_EXPERT), lambda b, i, e: (layer, e, 0, 0)),
                  pl.BlockSpec((1, 1, D_EXPERT, D), lambda b, i, e: (layer, e, 0, 0)),
                  pl.BlockSpec((1, tm, D), row_map),
                  pl.BlockSpec((1, 1, D), mod_map)],
        out_specs=pl.BlockSpec((1, tm, D), row_map),
        out_shape=jax.ShapeDtypeStruct((B, S, D), F32),
        scratch_shapes=[pltpu.VMEM((tm, D), F32)],
        compiler_params=_cparams(("parallel", "parallel", "arbitrary")),
        name="moe",
    )(h, comb, wg, wu, wd, x, gate.reshape(bm, 1, D))


MOE_TM = 512
MOE_TG = MOE_TM // MOE_G


def _moe_rows(n_tok):
    rows = 2 * n_tok + (n_tok // MOE_TT) * N_EXPERTS * (MOE_G - 1) + N_EXPERTS * (MOE_TM - 1)
    return (rows + MOE_TM - 1) // MOE_TM * MOE_TM


def _moe_plan(ng):
    ng = ng[:, :, 0].astype(jnp.int32)
    n_tt = ng.shape[0]
    total = jnp.sum(ng, axis=0)
    region = (total + MOE_TG - 1) // MOE_TG * MOE_TG
    region_end = jnp.cumsum(region)
    region_start = region_end - region
    dst = region_start[None, :] + jnp.cumsum(ng, axis=0) - ng
    local = jnp.cumsum(ng, axis=1) - ng
    n_tiles = _moe_rows(n_tt * MOE_TT) // MOE_TM
    tile_first = jnp.arange(n_tiles, dtype=jnp.int32) * MOE_TG
    tile_expert = jnp.minimum(jnp.sum(region_end[None, :] <= tile_first[:, None], axis=1), N_EXPERTS - 1)
    n_valid = region_end[-1:] // MOE_TG
    i32 = lambda a: a.astype(jnp.int32).reshape(-1)
    return dict(ng=i32(ng), dst=i32(dst), local=i32(local), tile_groups=i32(jnp.sum(ng, axis=1)),
                pad_first=i32(region_start + total),
                pad_count=i32(region - total), tile_expert=i32(tile_expert), n_valid=i32(n_valid),
                n_tiles=n_tiles)


def _run_copies(plan_refs, tile, local_ref, sorted_ref, sem, to_sorted):
    ng_ref, dst_ref, loc_ref = plan_refs
    for e in range(N_EXPERTS):
        k = tile * N_EXPERTS + e
        loc, dst = loc_ref[k], dst_ref[k]

        def body(g, carry):
            lrows = local_ref.at[pl.ds(pl.multiple_of((loc + g) * MOE_G, MOE_G), MOE_G)]
            srows = sorted_ref.at[pl.ds(pl.multiple_of((dst + g) * MOE_G, MOE_G), MOE_G)]
            src, tgt = (lrows, srows) if to_sorted else (srows, lrows)
            pltpu.make_async_copy(src, tgt, sem).start()
            return carry
        lax.fori_loop(0, ng_ref[k], body, 0)


def _group_waits(n_groups, local_ref, sorted_ref, sem):
    def body(g, carry):
        pltpu.make_async_copy(sorted_ref.at[pl.ds(0, MOE_G)], local_ref.at[pl.ds(0, MOE_G)], sem).wait()
        return carry
    lax.fori_loop(0, n_groups, body, 0)


def _moe_dispatch_kernel(ng_ref, dst_ref, loc_ref, tot_ref, padf_ref, padc_ref, nv_ref,
                         h_ref, route_ref, xs_ref, local_ref, zero_ref, sem):
    i = pl.program_id(0)

    @pl.when(i == 0)
    def _():
        zero_ref[...] = jnp.zeros_like(zero_ref)
        n_pad = 0
        for e in range(N_EXPERTS):
            first = padf_ref[e]

            def body(g, carry):
                rows = xs_ref.at[pl.ds(pl.multiple_of((first + g) * MOE_G, MOE_G), MOE_G)]
                pltpu.make_async_copy(zero_ref.at[pl.ds(0, MOE_G)], rows, sem.at[2]).start()
                return carry
            lax.fori_loop(0, padc_ref[e], body, 0)
            n_pad = n_pad + padc_ref[e]

        def tile_copy(j):
            rows = xs_ref.at[pl.ds(pl.multiple_of(j * MOE_TM, MOE_TM), MOE_TM)]
            return pltpu.make_async_copy(zero_ref, rows, sem.at[3])

        n_tiles = xs_ref.shape[0] // MOE_TM
        lax.fori_loop(nv_ref[0], n_tiles, lambda j, c: (tile_copy(j).start(), c)[1], 0)
        _group_waits(n_pad, zero_ref, xs_ref, sem.at[2])
        lax.fori_loop(nv_ref[0], n_tiles, lambda j, c: (tile_copy(j).wait(), c)[1], 0)

    pos = lax.broadcasted_iota(jnp.int32, (MOE_LOCAL, MOE_TT), 0)
    p1 = route_ref[0:1, :].astype(jnp.int32)
    p2 = route_ref[1:2, :].astype(jnp.int32)
    pick = jnp.where((pos == p1) | (pos == p2), 1.0, 0.0).astype(BF16)
    slot = i % 2
    last = pl.num_programs(0) - 1

    @pl.when(i >= 2)
    def _():
        _group_waits(tot_ref[i - 2], local_ref.at[slot], xs_ref, sem.at[slot])

    local_ref[slot] = jnp.dot(pick, h_ref[0], preferred_element_type=F32)
    _run_copies((ng_ref, dst_ref, loc_ref), i, local_ref.at[slot], xs_ref, sem.at[slot], to_sorted=True)

    @pl.when(i == last)
    def _():
        _group_waits(tot_ref[i], local_ref.at[slot], xs_ref, sem.at[slot])

    @pl.when((i == last) & (last >= 1))
    def _():
        _group_waits(tot_ref[i - 1], local_ref.at[1 - slot], xs_ref, sem.at[1 - slot])


def _moe_dispatch(h, route_t, plan):
    B, S, D = h.shape
    nt = S // MOE_TT
    grid_spec = pltpu.PrefetchScalarGridSpec(
        num_scalar_prefetch=7,
        grid=(B * nt,),
        in_specs=[pl.BlockSpec((1, MOE_TT, D), lambda i, *_: (i // nt, i % nt, 0)),
                  pl.BlockSpec((ROUTE_FIELDS, MOE_TT), lambda i, *_: (0, i))],
        out_specs=pl.BlockSpec(memory_space=pl.ANY),
        scratch_shapes=[pltpu.VMEM((2, MOE_LOCAL, D), F32), pltpu.VMEM((MOE_TM, D), F32),
                        pltpu.SemaphoreType.DMA((4,))])
    return pl.pallas_call(
        _moe_dispatch_kernel,
        grid_spec=grid_spec,
        out_shape=jax.ShapeDtypeStruct((plan["n_tiles"] * MOE_TM, D), F32),
        compiler_params=_cparams(("arbitrary",)),
        name="moe_dispatch",
    )(plan["ng"], plan["dst"], plan["local"], plan["tile_groups"], plan["pad_first"], plan["pad_count"],
      plan["n_valid"], h, route_t)


def _moe_grouped_kernel(te_ref, nv_ref, x_ref, wg_ref, wu_ref, wd_ref, o_ref, wgb_ref, wub_ref, wdb_ref):
    j = pl.program_id(0)
    used = j < nv_ref[0]

    @pl.when(used & ((j == 0) | (te_ref[j] != te_ref[jnp.maximum(j - 1, 0)])))
    def _():
        wgb_ref[...] = wg_ref[0, 0].astype(BF16)
        wub_ref[...] = wu_ref[0, 0].astype(BF16)
        wdb_ref[...] = wd_ref[0, 0].astype(BF16)

    @pl.when(used)
    def _():
        x = x_ref[...].astype(BF16)
        a = jnp.dot(x, wgb_ref[...], preferred_element_type=F32)
        u = jnp.dot(x, wub_ref[...], preferred_element_type=F32)
        he = (a * jax.nn.sigmoid(a) * u).astype(BF16)
        o_ref[...] = jnp.dot(he, wdb_ref[...], preferred_element_type=F32)

    @pl.when(jnp.logical_not(used))
    def _():
        o_ref[...] = jnp.zeros_like(o_ref)


def _moe_grouped(xs, plan, wg, wu, wd, layer):
    n_tiles = plan["n_tiles"]
    D = D_MODEL
    tile = lambda j, nv: jnp.minimum(j, nv[0] - 1)
    grid_spec = pltpu.PrefetchScalarGridSpec(
        num_scalar_prefetch=2,
        grid=(n_tiles,),
        in_specs=[pl.BlockSpec((MOE_TM, D), lambda j, te, nv: (tile(j, nv), 0)),
                  pl.BlockSpec((1, 1, D, D_EXPERT), lambda j, te, nv: (layer, te[tile(j, nv)], 0, 0)),
                  pl.BlockSpec((1, 1, D, D_EXPERT), lambda j, te, nv: (layer, te[tile(j, nv)], 0, 0)),
                  pl.BlockSpec((1, 1, D_EXPERT, D), lambda j, te, nv: (layer, te[tile(j, nv)], 0, 0))],
        out_specs=pl.BlockSpec((MOE_TM, D), lambda j, te, nv: (j, 0)),
        scratch_shapes=[pltpu.VMEM((D, D_EXPERT), BF16), pltpu.VMEM((D, D_EXPERT), BF16),
                        pltpu.VMEM((D_EXPERT, D), BF16)])
    return pl.pallas_call(
        _moe_grouped_kernel,
        grid_spec=grid_spec,
        out_shape=jax.ShapeDtypeStruct((n_tiles * MOE_TM, D), F32),
        compiler_params=_cparams(("arbitrary",)),
        name="moe_grouped",
    )(plan["tile_expert"], plan["n_valid"], xs, wg, wu, wd)


def _moe_combine_kernel(final_norm, ng_ref, dst_ref, loc_ref, tot_ref, ys_ref, route_ref, x_ref, gate_ref,
                        *rest):
    if final_norm:
        fg_ref, o_ref, local_ref, sem = rest
    else:
        o_ref, local_ref, sem = rest
    i = pl.program_id(0)
    slot = i % 2
    plan_refs = (ng_ref, dst_ref, loc_ref)

    @pl.when(i == 0)
    def _():
        local_ref[...] = jnp.zeros_like(local_ref)
        _run_copies(plan_refs, 0, local_ref.at[0], ys_ref, sem.at[0], to_sorted=False)

    @pl.when(i + 1 < pl.num_programs(0))
    def _():
        _run_copies(plan_refs, i + 1, local_ref.at[1 - slot], ys_ref, sem.at[1 - slot], to_sorted=False)

    _group_waits(tot_ref[i], local_ref.at[slot], ys_ref, sem.at[slot])
    pos = lax.broadcasted_iota(jnp.int32, (MOE_TT, MOE_LOCAL), 1)
    route = route_ref[...]
    p1 = route[:, 0:1].astype(jnp.int32)
    p2 = route[:, 1:2].astype(jnp.int32)
    weigh = (jnp.where(pos == p1, route[:, 2:3], 0.0) + jnp.where(pos == p2, route[:, 3:4], 0.0)).astype(BF16)
    y = jnp.dot(weigh, local_ref[slot].astype(BF16), preferred_element_type=F32)
    o = x_ref[0] + gate_ref[0] * y
    if final_norm:
        o = o * lax.rsqrt(jnp.mean(o * o, axis=-1, keepdims=True) + EPS) * fg_ref[...]
    o_ref[0] = o


def _moe_combine(ys, route_c, plan, x, gate, final_g=None):
    B, S, D = x.shape
    nt = S // MOE_TT
    in_specs = [pl.BlockSpec(memory_space=pl.ANY),
                pl.BlockSpec((MOE_TT, LANES), lambda i, *_: (i, 0)),
                pl.BlockSpec((1, MOE_TT, D), lambda i, *_: (i // nt, i % nt, 0)),
                pl.BlockSpec((1, 1, D), lambda i, *_: (i // nt, 0, 0))]
    args = [ys, route_c, x, gate.reshape(B, 1, D)]
    if final_g is not None:
        in_specs.append(pl.BlockSpec((1, D), lambda i, *_: (0, 0)))
        args.append(final_g.reshape(1, D))
    grid_spec = pltpu.PrefetchScalarGridSpec(
        num_scalar_prefetch=4,
        grid=(B * nt,),
        in_specs=in_specs,
        out_specs=pl.BlockSpec((1, MOE_TT, D), lambda i, *_: (i // nt, i % nt, 0)),
        scratch_shapes=[pltpu.VMEM((2, MOE_LOCAL, D), F32), pltpu.SemaphoreType.DMA((2,))])
    return pl.pallas_call(
        functools.partial(_moe_combine_kernel, final_g is not None),
        grid_spec=grid_spec,
        out_shape=jax.ShapeDtypeStruct((B, S, D), F32),
        compiler_params=_cparams(("arbitrary",)),
        name="moe_combine",
    )(plan["ng"], plan["dst"], plan["local"], plan["tile_groups"], *args)


def _dft_tables():
    n1, n2, n = FFT_N1, FFT_N2, FFT_N1 * FFT_N2
    k1 = np.arange(n1)
    f1 = np.exp(-2j * np.pi * np.outer(k1, np.arange(n1)) / n1)
    tw = np.exp(-2j * np.pi * np.outer(np.arange(n2), k1) / n)
    ftw = f1[None, :, :] * tw[:, :, None]
    half = n1 // 2
    fh = ftw[:, :, :half]
    g_fwd = np.concatenate([np.concatenate([fh.real, -fh.imag], axis=2),
                            np.concatenate([fh.imag, fh.real], axis=2)], axis=1)
    back = ftw[:, :, ::-1][:, :, :half].copy()
    back[0] = np.roll(ftw[0], -1, axis=1)[:, ::-1][:, :half]
    back[0][:, 0] = 0.0
    fk = np.concatenate([fh, back], axis=2)
    g_real = np.concatenate([fk.real, fk.imag], axis=1)
    gi = np.conj(np.transpose(fh, (0, 2, 1))) / n
    g_inv = np.concatenate([np.concatenate([gi.real, -gi.imag], axis=2),
                            np.concatenate([gi.imag, gi.real], axis=2)], axis=1)
    f2 = np.exp(-2j * np.pi * np.outer(np.arange(n2), np.arange(n2)) / n2)
    f2_fwd = np.block([[f2.real, -f2.imag], [f2.imag, f2.real]])
    f2c = np.conj(f2)
    f2_inv = np.block([[f2c.real, -f2c.imag], [f2c.imag, f2c.real]])
    as_bf = lambda a: jnp.asarray(a, dtype=F32).astype(BF16)
    return as_bf(g_fwd), as_bf(g_real), as_bf(g_inv), as_bf(f2_fwd), as_bf(f2_inv)


def _fft_fast_stage(stage_ref, k1, f2):
    slab = STAGE_PITCH
    m = jnp.concatenate([stage_ref[pl.ds(k1, FFT_N2, stride=slab), :],
                         stage_ref[pl.ds(FFT_N1 + k1, FFT_N2, stride=slab), :]], axis=0)
    return jnp.dot(f2, m.astype(BF16), preferred_element_type=F32)


def _filter_fft_kernel(hf_ref, hb_ref, inv_ref, g_ref, f2_ref, h_ref, stage_ref):
    slab = STAGE_PITCH
    half = FFT_N1 // 2
    for n2 in range(FFT_N2):
        x = jnp.concatenate([hf_ref[pl.ds(n2, half, stride=FFT_N2), :],
                             hb_ref[pl.ds((FFT_N2 - n2) % FFT_N2, half, stride=FFT_N2), :]], axis=0)
        stage_ref[n2 * slab:n2 * slab + 2 * FFT_N1, :] = jnp.dot(g_ref[n2], x.astype(BF16),
                                                          preferred_element_type=F32)
    f2 = f2_ref[...]
    inv = inv_ref[...]
    for k1 in range(FFT_N1):
        h_ref[0, k1] = (_fft_fast_stage(stage_ref, k1, f2) * inv).astype(h_ref.dtype)


def _filter_fft(taps, inv_norm, g_real, f2_fwd, ct=LANES):
    L, cols = taps.shape
    C = D_CH
    n_ord = cols // (2 * C)
    nc = C // ct
    once = pl.Buffered(1)
    return pl.pallas_call(
        _filter_fft_kernel,
        grid=(n_ord, nc),
        in_specs=[pl.BlockSpec((L, ct), lambda o, c: (0, o * nc + c)),
                  pl.BlockSpec((L, ct), lambda o, c: (0, (n_ord + o) * nc + c)),
                  pl.BlockSpec((1, ct), lambda o, c: (0, o * nc + c)),
                  pl.BlockSpec(g_real.shape, lambda o, c: (0, 0, 0), pipeline_mode=once),
                  pl.BlockSpec(f2_fwd.shape, lambda o, c: (0, 0), pipeline_mode=once)],
        out_specs=pl.BlockSpec((1, FFT_N1, 2 * FFT_N2, ct), lambda o, c: (o, 0, 0, c)),
        out_shape=jax.ShapeDtypeStruct((n_ord, FFT_N1, 2 * FFT_N2, C), BF16),
        scratch_shapes=[pltpu.VMEM((FFT_N2 * STAGE_PITCH, ct), F32)],
        compiler_params=_cparams(("parallel", "parallel")),
        name="filter_fft",
    )(taps, taps, inv_norm, g_real, f2_fwd)


class _RowSets:
    def __init__(self, ref, member):
        self.ref, self.member, self.loaded = ref, member, {}

    def __call__(self, start):
        if start not in self.loaded:
            self.loaded = {k: v for k, v in self.loaded.items() if abs(k - start) <= 2}
            self.loaded[start] = self.ref[self.member, pl.ds(start, FFT_N1 // 2, stride=FFT_N2), :]
        return self.loaded[start]


def _strided_rows(at, n2, taps):
    half = FFT_N1 // 2
    cur = at(n2)
    if taps is None:
        return cur
    w_ref, b_ref = taps
    n1 = lax.broadcasted_iota(jnp.int32, cur.shape, 0)
    if n2 > 0:
        prev = at(n2 - 1)
    else:
        prev = jnp.where(n1 == 0, 0.0, pltpu.roll(at(FFT_N2 - 1), 1, axis=0))
    if n2 < FFT_N2 - 1:
        nxt = at(n2 + 1)
    else:
        nxt = jnp.where(n1 == half - 1, 0.0, pltpu.roll(at(0), half - 1, axis=0))
    return prev * w_ref[0:1, :] + cur * w_ref[1:2, :] + nxt * w_ref[2:3, :] + b_ref[...]


def _hyena_conv_kernel(conv_z, *refs):
    it = iter(refs)
    z_ref, gate_ref = next(it), next(it)
    z_taps = (next(it), next(it)) if conv_z else None
    gate_taps = (next(it), next(it))
    hb_ref, spec_ref, gf_ref, gi_ref, f2f_ref, f2i_ref, o_ref, stage_ref = it
    half = FFT_N1 // 2
    slab = STAGE_PITCH
    z_sets = [_RowSets(z_ref, m) for m in range(2)]
    for n2 in range(FFT_N2):
        x = jnp.concatenate([_strided_rows(z_sets[m], n2, z_taps) for m in range(2)], axis=0).astype(BF16)
        stage_ref[n2 * slab:n2 * slab + 2 * FFT_N1, :] = jnp.dot(gf_ref[n2], x, preferred_element_type=F32)
    f2f = f2f_ref[...]
    f2i = f2i_ref[...]
    for k1 in range(FFT_N1):
        zf = _fft_fast_stage(stage_ref, k1, f2f)
        zr, zi = zf[:FFT_N2], zf[FFT_N2:]
        hr = spec_ref[0, k1, :FFT_N2, :].astype(F32)
        hi = spec_ref[0, k1, FFT_N2:, :].astype(F32)
        p = jnp.concatenate([zr * hr - zi * hi, zr * hi + zi * hr], axis=0).astype(BF16)
        q = jnp.dot(f2i, p, preferred_element_type=F32)
        stage_ref[pl.ds(k1, FFT_N2, stride=slab), :] = q[:FFT_N2]
        stage_ref[pl.ds(FFT_N1 + k1, FFT_N2, stride=slab), :] = q[FFT_N2:]
    hb = hb_ref[...]
    z_sets = [_RowSets(z_ref, m) for m in range(2)]
    gate_sets = [_RowSets(gate_ref, m) for m in range(2)]
    for n2 in range(FFT_N2):
        y_in = stage_ref[n2 * slab:n2 * slab + 2 * FFT_N1, :].astype(BF16)
        y = jnp.dot(gi_ref[n2], y_in, preferred_element_type=F32)
        for m in range(2):
            zm = _strided_rows(z_sets[m], n2, z_taps)
            gm = _strided_rows(gate_sets[m], n2, gate_taps)
            o_ref[m, pl.ds(n2, half, stride=FFT_N2), :] = gm * (y[m * half:(m + 1) * half] + zm * hb)


def _hyena_conv(z, z_blk, gate, gate_blk, short_w, short_b, conv_z, hbias, spec, order, tabs, ct=LANES):
    B, L, _ = z.shape
    C = D_CH
    g_fwd, _, g_inv, f2_fwd, f2_inv = tabs
    once = pl.Buffered(1)
    const3 = lambda a: pl.BlockSpec(a.shape, lambda c, p: (0, 0, 0), pipeline_mode=once)
    const2 = lambda a: pl.BlockSpec(a.shape, lambda c, p: (0, 0), pipeline_mode=once)
    taps_specs = lambda blk: [pl.BlockSpec((3, ct), lambda c, p: (0, blk + c)),
                              pl.BlockSpec((1, ct), lambda c, p: (0, blk + c))]
    in_specs = [pl.BlockSpec((2, L, ct), lambda c, p: (p, 0, z_blk + c)),
                pl.BlockSpec((2, L, ct), lambda c, p: (p, 0, gate_blk + c))]
    args = [z, gate]
    if conv_z:
        in_specs += taps_specs(z_blk)
        args += [short_w, short_b]
    in_specs += taps_specs(gate_blk)
    args += [short_w, short_b]
    in_specs += [pl.BlockSpec((1, ct), lambda c, p: (0, c)),
                 pl.BlockSpec((1, FFT_N1, 2 * FFT_N2, ct), lambda c, p: (order, 0, 0, c), pipeline_mode=once),
                 const3(g_fwd), const3(g_inv), const2(f2_fwd), const2(f2_inv)]
    args += [hbias, spec, g_fwd, g_inv, f2_fwd, f2_inv]
    return pl.pallas_call(
        functools.partial(_hyena_conv_kernel, conv_z),
        grid=(C // ct, B // 2),
        in_specs=in_specs,
        out_specs=pl.BlockSpec((2, L, ct), lambda c, p: (p, 0, c)),
        out_shape=jax.ShapeDtypeStruct((B, L, C), F32),
        scratch_shapes=[pltpu.VMEM((FFT_N2 * STAGE_PITCH, ct), F32)],
        compiler_params=_cparams(("parallel", "arbitrary")),
        name="hyena_conv",
    )(*args)


def _hyena_filters(L, w1, b1, f1, w2, b2, f2, w3, b3):
    t = jnp.arange(L, dtype=F32)
    tn = t / max(L - 1, 1)
    bands = jnp.linspace(1e-4, HY_BANDS - 1, HY_BANDS, dtype=F32)
    ang = 2.0 * math.pi * t[:, None] * bands[None] / L
    feats = jnp.concatenate([tn[:, None], jnp.cos(ang), jnp.sin(ang)], axis=-1)
    h = jnp.sin(f1 * (feats @ w1 + b1))
    h = jnp.sin(f2 * (h @ w2 + b2))
    deltas = jnp.abs(jnp.linspace(HY_MIN_DECAY, HY_MAX_DECAY, D_CH, dtype=F32))
    decay = jnp.exp(-tn[:, None] * deltas[None])
    n_rep = w3.shape[1] // D_CH
    taps = (_mm_f32(h, w3, 512) + b3) * jnp.tile(decay, (1, n_rep))
    l1 = jnp.sum(jnp.abs(taps), axis=0)
    l1 = l1[:n_rep // 2 * D_CH] + l1[n_rep // 2 * D_CH:]
    return taps, (1.0 / (l1 + EPS))[None]


def _dup_heads(w):
    a, b = w[:, :HEAD_DIM], w[:, HEAD_DIM:]
    return jnp.concatenate([a, a, b, b], axis=1)


def _rope_tables(S):
    t = jnp.arange(S)
    row = (t // GRID_W).astype(F32)
    col = (t % GRID_W).astype(F32)
    half = HEAD_DIM // 2
    inv = ROPE_THETA ** (-jnp.arange(0, half, 2, dtype=F32) / half)
    ar = row[:, None] * inv[None]
    ac = col[:, None] * inv[None]
    cos = jnp.concatenate([jnp.cos(ar), jnp.cos(ar), jnp.cos(ac), jnp.cos(ac)], axis=-1)
    sin = jnp.concatenate([-jnp.sin(ar), jnp.sin(ar), -jnp.sin(ac), jnp.sin(ac)], axis=-1)
    return jnp.tile(cos, (1, 2)), jnp.tile(sin, (1, 2))


def _head_mean_matrix(width):
    blk = np.kron(np.eye(width // HEAD_DIM), np.full((HEAD_DIM, HEAD_DIM), 1.0 / HEAD_DIM))
    return jnp.asarray(blk, dtype=F32).astype(BF16)


def kernel(x, c, ctx, c_ctx, w_ada, b_ada, norm_g, final_g, w_in_even, w_out_even, a_sink, b_rpb, w_in_odd, w_out_odd, c_qnorm, c_knorm, hy_short_w, hy_short_b, hy_w1, hy_b1, hy_f1, hy_w2, hy_b2, hy_f2, hy_w3, hy_b3, hy_bias, w_router, b_router, moe_wg, moe_wu, moe_wd):
    B, S, D = x.shape
    depth = w_ada.shape[0]
    rope = _rope_tables(S)
    wr_pad = jnp.pad(w_router.astype(F32), ((0, 0), (0, LANES - N_EXPERTS)))
    wr_hi = wr_pad.astype(BF16)
    router = (b_router.astype(F32), wr_hi, (wr_pad - wr_hi.astype(F32)).astype(BF16))

    mod_in = jnp.concatenate([jax.nn.silu(c), jax.nn.silu(c_ctx)[None],
                              jnp.zeros((8 - B - 1, D), F32)], axis=0)
    xc = ctx
    for l in range(depth):
        need_ctx = l < depth - 1
        mod = _mm_f32(mod_in, w_ada[l], 1536) + b_ada[l]
        mx = mod[:B].reshape(B, 6, D)
        mc = mod[B].reshape(6, D)
        i = l // 2
        if l % 2 == 0:
            w = w_in_even[i].astype(BF16)
            w_all = jnp.concatenate([w[:, :512], _dup_heads(w[:, 512:640]), _dup_heads(w[:, 640:768]),
                                     w[:, 768:]], axis=1)
            segs_x = ((512, "rope", Q_MULT), (256, "rope", 1.0), (256, "plain", 1.0),
                      (512, "plain", Q_MULT), (512, "plain", 1.0), (512, "plain", 1.0))
            sink = a_sink[i].astype(F32) * LOG2_E
            bias = _nbr_bias_table(b_rpb[i].astype(F32) * LOG2_E, S // GRID_W)
            aq, akd, avd, bq, bk, bv = _norm_proj(x, norm_g[l, 0], mx[:, 0], mx[:, 1], w_all, segs_x,
                                                  rope=rope)
            segs_c = tuple((wd, "plain", m) for wd, _, m in segs_x)
            caq, cakd, cavd, cbq, cbk, cbv = _norm_proj(xc, norm_g[l, 0], mc[0:1], mc[1:2], w_all, segs_c)
            ya = _window_attn(aq, akd, avd, cakd, cavd, sink)
            yb = _nbr_attn(bq, bk, bv, cbk, cbv, bias)
            ys = [ya, yb]
            w_out = w_out_even[i].astype(BF16)
            if need_ctx:
                yc = [_ctx_attn(sink, caq, cakd, cavd, cbq, cbk, cbv)]
        else:
            w = w_in_odd[i].astype(BF16)
            w_all = jnp.concatenate([w[:, :512], _dup_heads(w[:, 512:640]), _dup_heads(w[:, 640:768]),
                                     w[:, 768:]], axis=1)
            gains = jnp.concatenate([jnp.tile(c_qnorm[i], 8), jnp.tile(c_knorm[i], 4)])[None].astype(F32)
            norm = (_head_mean_matrix(512), gains)
            segs_x = ((512, "normrope", Q_MULT), (256, "normrope", 1.0), (256, "plain", 1.0),
                      (3 * D_CH, "plain", 1.0, F32))
            qx, kxd, vxd, ux = _norm_proj(x, norm_g[l, 0], mx[:, 0], mx[:, 1], w_all, segs_x,
                                          rope=rope, norm=norm)
            w_c = w_all[:, 512:1024]
            norm_c = (_head_mean_matrix(512), jnp.tile(c_knorm[i], 4)[None].astype(F32))
            kcd, vcd = _norm_proj(xc, norm_g[l, 0], mc[0:1], mc[1:2], w_c,
                                  ((256, "norm", 1.0), (256, "plain", 1.0)), norm=norm_c)
            logit_bound = (1.02 * HEAD_DIM * Q_MULT * jnp.max(jnp.abs(c_qnorm[i]))
                           * jnp.max(jnp.abs(c_knorm[i])))
            y_attn = lax.cond(logit_bound <= FULL_NOMAX_LOG2_BOUND,
                              lambda *a: _full_attn(*a, bounded=True),
                              lambda *a: _full_attn(*a, bounded=False),
                              qx, kxd, vxd, kcd, vcd)
            sw, sb = hy_short_w[i].astype(F32), hy_short_b[i].astype(F32)[None]
            tabs = _dft_tables()
            taps, inv_norm = _hyena_filters(S, hy_w1[i], hy_b1[i], hy_f1[i], hy_w2[i], hy_b2[i],
                                            hy_f2[i], hy_w3[i], hy_b3[i])
            spec = _filter_fft(taps, inv_norm, tabs[1], tabs[3])
            blocks = D_CH // LANES
            z = _hyena_conv(ux, 0, ux, blocks, sw, sb, True, hy_bias[i, 0:1], spec, 0, tabs)
            z = _hyena_conv(z, 0, ux, 2 * blocks, sw, sb, False, hy_bias[i, 1:2], spec, 1, tabs)
            ys = [y_attn, z]
            w_out = w_out_odd[i].astype(BF16)
            if need_ctx:
                raise NotImplementedError("context update of an odd layer is not needed at this depth")

        x, hx, route_t, route_c, ng = _out_proj(ys, w_out, x, mx[:, 2], norm_g[l, 1], mx[:, 3], mx[:, 4],
                                                router, sparse=True)
        if need_ctx:
            xc, hc, comb_c = _out_proj(yc, w_out, xc, mc[2:3], norm_g[l, 1], mc[3:4], mc[4:5], router,
                                       sparse=False)
            flat = lambda a: a.reshape(1, -1, a.shape[-1])
            xc = _moe(flat(hc), flat(comb_c), moe_wg, moe_wu, moe_wd, l, flat(xc), mc[5:6]).reshape(xc.shape)
        plan = _moe_plan(ng)
        sorted_rows = _moe_dispatch(hx, route_t, plan)
        x = _moe_combine(_moe_grouped(sorted_rows, plan, moe_wg, moe_wu, moe_wd, l), route_c, plan, x, mx[:, 5],
                         final_g=None if need_ctx else final_g)
    return x
```

```python
import functools
import math

import numpy as np
import jax
import jax.numpy as jnp
from jax import lax
from jax.experimental import pallas as pl
from jax.experimental.pallas import tpu as pltpu

F32 = jnp.float32
BF16 = jnp.bfloat16

D_MODEL = 1024
GRID_W = 64
CTX_LEN = 256
HEAD_DIM = 64
ROPE_THETA = 10000.0
EPS = 1e-6
ATTN_SCALE = HEAD_DIM ** -0.5
LOG2_E = math.log2(math.e)
Q_MULT = ATTN_SCALE * LOG2_E
A_WINDOW = 128
A_BLOCK = 128
B_WIN_H = 8
B_WIN_W = 16
D_CH = 512
HY_BANDS = 16
HY_MAX_DECAY = math.log(1e-2) / 0.3
HY_MIN_DECAY = math.log(1e-2) / 1.5
N_EXPERTS = 16
N_GROUPS = 4
EXPERTS_PER_GROUP = N_EXPERTS // N_GROUPS
TOP_K = 2
D_EXPERT = 512

LANES = 128
NEG = -1e30
VMEM_LIMIT = 48 * 1024 * 1024

FFT_N1 = 64
FFT_N2 = 128
STAGE_PITCH = 2 * FFT_N1 + 8


def _cparams(sem):
    return pltpu.CompilerParams(dimension_semantics=sem, vmem_limit_bytes=VMEM_LIMIT)


def _mm_f32_kernel(x_ref, w_ref, o_ref):
    o_ref[...] = jnp.dot(x_ref[...], w_ref[...], preferred_element_type=F32)


def _mm_f32(x, w, tn):
    M, K = x.shape
    N = w.shape[1]
    return pl.pallas_call(
        _mm_f32_kernel,
        grid=(N // tn,),
        in_specs=[pl.BlockSpec((M, K), lambda j: (0, 0)),
                  pl.BlockSpec((K, tn), lambda j: (0, j))],
        out_specs=pl.BlockSpec((M, tn), lambda j: (0, j)),
        out_shape=jax.ShapeDtypeStruct((M, N), F32),
        compiler_params=_cparams(("arbitrary",)),
        name="mm_f32",
    )(x, w)


def _swap16(y):
    n = y.shape[-1]
    lane = lax.broadcasted_iota(jnp.int32, y.shape, y.ndim - 1)
    up = pltpu.roll(y, n - 16, axis=y.ndim - 1)
    dn = pltpu.roll(y, 16, axis=y.ndim - 1)
    return jnp.where((lane % 32) < 16, up, dn)


def _tile_lanes(t, width):
    reps = width // t.shape[-1]
    return t if reps == 1 else jnp.concatenate([t] * reps, axis=-1)


def _norm_proj_kernel(segs, has_rope, has_norm, *refs):
    it = iter(refs)
    x_ref, g_ref, shift_ref, scale_ref, w_ref = (next(it) for _ in range(5))
    cos_ref = sin_ref = bd_ref = gain_ref = None
    if has_rope:
        cos_ref, sin_ref = next(it), next(it)
    if has_norm:
        bd_ref, gain_ref = next(it), next(it)
    out_refs = list(it)

    x = x_ref[0]
    ms = jnp.mean(x * x, axis=-1, keepdims=True)
    h = x * lax.rsqrt(ms + EPS) * g_ref[...]
    h = h * (1.0 + scale_ref[0]) + shift_ref[0]
    y = jnp.dot(h.astype(BF16), w_ref[...], preferred_element_type=F32)

    off = 0
    goff = 0
    for (width, kind, mult, *_), o_ref in zip(segs, out_refs):
        ys = y[:, off:off + width]
        if kind in ("norm", "normrope"):
            bd = bd_ref[...][:width, :width]
            hms = jnp.dot((ys * ys).astype(BF16), bd, preferred_element_type=F32)
            ys = ys * lax.rsqrt(hms + EPS) * gain_ref[:, goff:goff + width]
            goff += width
        if kind in ("rope", "normrope"):
            c = _tile_lanes(cos_ref[...], width)
            s = _tile_lanes(sin_ref[...], width)
            ys = ys * c + _swap16(ys) * s
        if mult != 1.0:
            ys = ys * mult
        o_ref[0] = ys.astype(o_ref.dtype)
        off += width


def _norm_proj(x, g, shift, scale, w, segs, rope=None, norm=None, tm=512):
    B, S, D = x.shape
    N = w.shape[1]
    tm = min(tm, S)
    bm = shift.shape[0]
    mod_map = (lambda b, i: (b, 0, 0)) if bm > 1 else (lambda b, i: (0, 0, 0))
    args = [x, g.reshape(1, D), shift.reshape(bm, 1, D), scale.reshape(bm, 1, D), w]
    in_specs = [pl.BlockSpec((1, tm, D), lambda b, i: (b, i, 0)),
                pl.BlockSpec((1, D), lambda b, i: (0, 0)),
                pl.BlockSpec((1, 1, D), mod_map),
                pl.BlockSpec((1, 1, D), mod_map),
                pl.BlockSpec((D, N), lambda b, i: (0, 0))]
    if rope is not None:
        args += [rope[0], rope[1]]
        in_specs += [pl.BlockSpec((tm, LANES), lambda b, i: (i, 0))] * 2
    if norm is not None:
        args += [norm[0], norm[1]]
        in_specs += [pl.BlockSpec(norm[0].shape, lambda b, i: (0, 0)),
                     pl.BlockSpec(norm[1].shape, lambda b, i: (0, 0))]
    out_shape = [jax.ShapeDtypeStruct((B, S, seg[0]), seg[3] if len(seg) > 3 else BF16) for seg in segs]
    out_specs = [pl.BlockSpec((1, tm, seg[0]), lambda b, i: (b, i, 0)) for seg in segs]
    return pl.pallas_call(
        functools.partial(_norm_proj_kernel, segs, rope is not None, norm is not None),
        grid=(B, S // tm),
        in_specs=in_specs,
        out_specs=out_specs,
        out_shape=out_shape,
        compiler_params=_cparams(("parallel", "parallel")),
        name="norm_proj",
    )(*args)


def _half_mask(shape):
    return lax.broadcasted_iota(jnp.int32, shape, len(shape) - 1) < HEAD_DIM


def _stack_halves(qp):
    lo = _half_mask(qp.shape)
    zero = jnp.zeros_like(qp)
    return jnp.concatenate([jnp.where(lo, qp, zero), jnp.where(lo, zero, qp)], axis=0)


def _merge_halves(o, m):
    return jnp.where(_half_mask((m, LANES)), o[:m], o[m:])


def _scores(q, k):
    return lax.dot_general(q, k, (((1,), (1,)), ((), ())), preferred_element_type=F32)


def _joint_softmax_pv(score_parts, value_parts, extra_logit=None):
    m = functools.reduce(jnp.maximum, [jnp.max(s, axis=-1, keepdims=True) for s in score_parts])
    if extra_logit is not None:
        m = jnp.maximum(m, extra_logit)
    den = jnp.exp2(extra_logit - m) if extra_logit is not None else 0.0
    acc = None
    for s, v in zip(score_parts, value_parts):
        p = jnp.exp2(s - m)
        den = den + jnp.sum(p, axis=-1, keepdims=True)
        pv = jnp.dot(p.astype(BF16), v, preferred_element_type=F32)
        acc = pv if acc is None else acc + pv
    return acc / den


def _sink_column(sink_ref, first_head, n_heads, rows_per_head):
    rows = lax.broadcasted_iota(jnp.int32, (n_heads * rows_per_head, 1), 0)
    col = jnp.zeros((n_heads * rows_per_head, 1), F32)
    for j in range(n_heads):
        in_head = (rows >= j * rows_per_head) & (rows < (j + 1) * rows_per_head)
        col = jnp.where(in_head, sink_ref[first_head + j], col)
    return col


def _window_attn_kernel(sink_ref, q_ref, kp_ref, kc_ref, kn_ref, vp_ref, vc_ref, vn_ref,
                        ck_ref, cv_ref, mask_ref, o_ref):
    blk = A_BLOCK
    q = q_ref[0]
    mask = mask_ref[0]
    outs = []
    for g in range(2):
        ls = slice(g * LANES, (g + 1) * LANES)
        k_loc = jnp.concatenate([kp_ref[0][:, ls], kc_ref[0][:, ls], kn_ref[0][:, ls]], axis=0)
        v_loc = jnp.concatenate([vp_ref[0][:, ls], vc_ref[0][:, ls], vn_ref[0][:, ls]], axis=0)
        qs = jnp.concatenate([_stack_halves(q[:, (2 * g + j) * LANES:(2 * g + j + 1) * LANES])
                              for j in range(2)], axis=0)
        s_loc = _scores(qs, k_loc) + mask
        s_ctx = _scores(qs, ck_ref[0][:, ls])
        sink = _sink_column(sink_ref, 4 * g, 4, blk)
        o = _joint_softmax_pv([s_loc, s_ctx], [v_loc, cv_ref[0][:, ls]], sink)
        outs += [_merge_halves(o[:2 * blk], blk), _merge_halves(o[2 * blk:], blk)]
    o_ref[0] = jnp.concatenate(outs, axis=-1).astype(o_ref.dtype)


def _window_mask_table(seq_len):
    blk = A_BLOCK
    rows = np.arange(4 * blk)[:, None] % blk
    rel = np.arange(3 * blk)[None, :] - blk
    near = np.abs(rows - rel) <= A_WINDOW
    tabs = [near & (rel >= 0), near, near & (rel < blk)]
    return jnp.asarray(np.where(np.stack(tabs), 0.0, NEG).astype(np.float32))


def _window_attn(q, kd, vd, ckd, cvd, sink):
    B, S, _ = q.shape
    nb = S // A_BLOCK
    mask = _window_mask_table(S)
    pat = lambda i: jnp.where(i == 0, 0, jnp.where(i == nb - 1, 2, 1))
    kv_spec = lambda f: pl.BlockSpec((1, A_BLOCK, 2 * LANES), f)
    prev_map = lambda b, i: (b, jnp.maximum(i - 1, 0), 0)
    cur_map = lambda b, i: (b, i, 0)
    next_map = lambda b, i: (b, jnp.minimum(i + 1, nb - 1), 0)
    ctx_spec = pl.BlockSpec((1, CTX_LEN, 2 * LANES), lambda b, i: (b, 0, 0))
    return pl.pallas_call(
        _window_attn_kernel,
        grid=(B, nb),
        in_specs=[pl.BlockSpec(memory_space=pltpu.SMEM),
                  pl.BlockSpec((1, A_BLOCK, 4 * LANES), cur_map),
                  kv_spec(prev_map), kv_spec(cur_map), kv_spec(next_map),
                  kv_spec(prev_map), kv_spec(cur_map), kv_spec(next_map),
                  ctx_spec, ctx_spec,
                  pl.BlockSpec((1,) + mask.shape[1:], lambda b, i: (pat(i), 0, 0))],
        out_specs=pl.BlockSpec((1, A_BLOCK, 4 * LANES), cur_map),
        out_shape=jax.ShapeDtypeStruct((B, S, 4 * LANES), BF16),
        compiler_params=_cparams(("parallel", "parallel")),
        name="window_attn",
    )(sink, q, kd, kd, kd, vd, vd, vd, ckd, cvd, mask)


NBR_ROWS = 4
NBR_KROWS = 12
NBR_PAIRS = 4


def _nbr_start_row(i, n_rows):
    return jnp.clip(i * NBR_ROWS - B_WIN_H // 2, 0, n_rows - NBR_KROWS)


def _nbr_attn_kernel(n_rows, q_ref, k_ref, v_ref, ck_ref, cv_ref, bias_ref, o_ref):
    i = pl.program_id(2)
    nq = NBR_ROWS * GRID_W
    nk = NBR_KROWS * GRID_W
    start = pl.multiple_of(_nbr_start_row(i, n_rows) * GRID_W, GRID_W)
    outs = []
    for pp in range(NBR_PAIRS):
        ls = slice(pp * LANES, (pp + 1) * LANES)
        k_loc = k_ref[0, pl.ds(start, nk), ls]
        v_loc = v_ref[0, pl.ds(start, nk), ls]
        qs = _stack_halves(q_ref[0][:, ls])
        s_loc = _scores(qs, k_loc) + bias_ref[0, 2 * pp:2 * pp + 2].reshape(2 * nq, nk)
        s_ctx = _scores(qs, ck_ref[0][:, ls])
        o = _joint_softmax_pv([s_loc, s_ctx], [v_loc, cv_ref[0][:, ls]])
        outs.append(_merge_halves(o, nq))
    o_ref[0] = jnp.concatenate(outs, axis=-1).astype(o_ref.dtype)


def _nbr_bias_table(rpb, n_rows):
    kh = B_WIN_H
    n_heads = rpb.shape[0]
    col = np.arange(GRID_W)
    cs = np.clip(col - B_WIN_W // 2, 0, GRID_W - B_WIN_W)
    col_ok = (col[None, :] >= cs[:, None]) & (col[None, :] < cs[:, None] + B_WIN_W)
    dc = np.clip(col[None, :] - col[:, None], -(B_WIN_W - 1), B_WIN_W - 1) + B_WIN_W - 1
    pick_col = (dc[..., None] == np.arange(2 * B_WIN_W - 1)).astype(np.float32)
    r0 = np.array([0, NBR_ROWS, n_rows - NBR_ROWS])
    start = np.clip(r0 - kh // 2, 0, n_rows - NBR_KROWS)
    r = r0[:, None] + np.arange(NBR_ROWS)[None]
    rs = np.clip(r - kh // 2, 0, n_rows - kh)
    kr = start[:, None] + np.arange(NBR_KROWS)[None]
    row_ok = (kr[:, None, :] >= rs[:, :, None]) & (kr[:, None, :] < rs[:, :, None] + kh)
    dr = np.clip(kr[:, None, :] - r[:, :, None] + kh - 1, 0, 2 * kh - 2)
    pick_row = (dr[..., None] == np.arange(2 * kh - 1)).astype(np.float32)
    table = jnp.einsum("prkd,hdc,qjc->phrqkj", pick_row, rpb.astype(F32), pick_col,
                       precision=lax.Precision.HIGHEST)
    ok = row_ok[:, None, :, None, :, None] & col_ok[None, None, None, :, None, :]
    return jnp.where(ok, table, NEG).reshape(3, n_heads, NBR_ROWS * GRID_W, NBR_KROWS * GRID_W)


def _nbr_attn(q, k, v, ck, cv, bias):
    B, S, _ = q.shape
    n_rows = S // GRID_W
    nsteps = n_rows // NBR_ROWS
    nq = NBR_ROWS * GRID_W
    nk = NBR_KROWS * GRID_W
    pat = lambda i: jnp.where(i == 0, 0, jnp.where(i == nsteps - 1, 2, 1))
    wl = NBR_PAIRS * LANES
    return pl.pallas_call(
        functools.partial(_nbr_attn_kernel, n_rows),
        grid=(B, 4 // NBR_PAIRS, nsteps),
        in_specs=[pl.BlockSpec((1, nq, wl), lambda b, p, i: (b, i, p)),
                  pl.BlockSpec((1, S, wl), lambda b, p, i: (b, 0, p)),
                  pl.BlockSpec((1, S, wl), lambda b, p, i: (b, 0, p)),
                  pl.BlockSpec((1, CTX_LEN, wl), lambda b, p, i: (b, 0, p)),
                  pl.BlockSpec((1, CTX_LEN, wl), lambda b, p, i: (b, 0, p)),
                  pl.BlockSpec((1, 2 * NBR_PAIRS, nq, nk), lambda b, p, i: (pat(i), p, 0, 0))],
        out_specs=pl.BlockSpec((1, nq, wl), lambda b, p, i: (b, i, p)),
        out_shape=jax.ShapeDtypeStruct((B, S, 4 * LANES), BF16),
        compiler_params=_cparams(("parallel", "parallel", "arbitrary")),
        name="nbr_attn",
    )(q, k, v, ck, cv, bias)


FULL_TQ = 512
FULL_TK = 512
FULL_NOMAX_LOG2_BOUND = 60.0


def _full_attn_kernel(bounded, q_ref, k_ref, v_ref, ck_ref, cv_ref, o_ref):
    tq = FULL_TQ
    q = q_ref[0]
    groups = range(2)
    lanes = [slice(g * LANES, (g + 1) * LANES) for g in groups]
    qs = [jnp.concatenate([_stack_halves(q[:, (2 * g + j) * LANES:(2 * g + j + 1) * LANES]) for j in range(2)],
                          axis=0) for g in groups]

    def step_bounded(g, acc, k, v):
        v_ones = jnp.where(_half_mask(v.shape), v, jnp.ones_like(v))
        p = jnp.exp2(_scores(qs[g], k))
        return acc + jnp.dot(p.astype(BF16), v_ones, preferred_element_type=F32)

    def step_online(g, carry, k, v):
        m, l, acc = carry
        s = _scores(qs[g], k)
        m_new = jnp.maximum(m, jnp.max(s, axis=-1, keepdims=True))
        alpha = jnp.exp2(m - m_new)
        p = jnp.exp2(s - m_new)
        l = l * alpha + jnp.sum(p, axis=-1, keepdims=True)
        acc = acc * alpha + jnp.dot(p.astype(BF16), v, preferred_element_type=F32)
        return m_new, l, acc

    step = step_bounded if bounded else step_online
    init = jnp.zeros((4 * tq, LANES), F32)
    if not bounded:
        init = (jnp.full((4 * tq, 1), NEG, F32), jnp.zeros((4 * tq, 1), F32), init)
    carry = tuple(step(g, init, ck_ref[0][:, lanes[g]], cv_ref[0][:, lanes[g]]) for g in groups)

    def body(j, carry):
        rows = pl.ds(pl.multiple_of(j * FULL_TK, FULL_TK), FULL_TK)
        return tuple(step(g, carry[g], k_ref[0, rows, lanes[g]], v_ref[0, rows, lanes[g]]) for g in groups)

    carry = lax.fori_loop(0, k_ref.shape[1] // FULL_TK, body, carry)
    pairs = []
    for g in groups:
        if bounded:
            o = carry[g] * pltpu.roll(1.0 / carry[g], HEAD_DIM, axis=1)
            merge = lambda a, b: jnp.where(_half_mask((tq, LANES)), a, pltpu.roll(b, HEAD_DIM, axis=1))
            pairs += [merge(o[0:tq], o[tq:2 * tq]), merge(o[2 * tq:3 * tq], o[3 * tq:])]
        else:
            o = carry[g][2] / carry[g][1]
            pairs += [_merge_halves(o[:2 * tq], tq), _merge_halves(o[2 * tq:], tq)]
    o_ref[0] = jnp.concatenate(pairs, axis=-1).astype(o_ref.dtype)


def _full_attn(q, kd, vd, ckd, cvd, bounded):
    B, S, W = q.shape
    return pl.pallas_call(
        functools.partial(_full_attn_kernel, bounded),
        grid=(B, S // FULL_TQ),
        in_specs=[pl.BlockSpec((1, FULL_TQ, W), lambda b, i: (b, i, 0)),
                  pl.BlockSpec((1, S, 2 * LANES), lambda b, i: (b, 0, 0)),
                  pl.BlockSpec((1, S, 2 * LANES), lambda b, i: (b, 0, 0)),
                  pl.BlockSpec((1, CTX_LEN, 2 * LANES), lambda b, i: (b, 0, 0)),
                  pl.BlockSpec((1, CTX_LEN, 2 * LANES), lambda b, i: (b, 0, 0))],
        out_specs=pl.BlockSpec((1, FULL_TQ, W), lambda b, i: (b, i, 0)),
        out_shape=jax.ShapeDtypeStruct((B, S, W), BF16),
        compiler_params=_cparams(("parallel", "arbitrary")),
        name="full_attn_bounded" if bounded else "full_attn_online",
    )(q, kd, vd, ckd, cvd)


def _ctx_attn_kernel(sink_ref, aq_ref, akd_ref, avd_ref, bq_ref, bk_ref, bv_ref, o_ref):
    n = CTX_LEN
    aq = aq_ref[0]
    bq = bq_ref[0]
    outs = []
    for g in range(2):
        ls = slice(g * LANES, (g + 1) * LANES)
        qs = jnp.concatenate([_stack_halves(aq[:, (2 * g + j) * LANES:(2 * g + j + 1) * LANES])
                              for j in range(2)], axis=0)
        sink = _sink_column(sink_ref, 4 * g, 4, n)
        o = _joint_softmax_pv([_scores(qs, akd_ref[0][:, ls])], [avd_ref[0][:, ls]], sink)
        outs += [_merge_halves(o[:2 * n], n), _merge_halves(o[2 * n:], n)]
    for p in range(4):
        ls = slice(p * LANES, (p + 1) * LANES)
        qs = _stack_halves(bq[:, ls])
        o = _joint_softmax_pv([_scores(qs, bk_ref[0][:, ls])], [bv_ref[0][:, ls]])
        outs.append(_merge_halves(o, n))
    o_ref[0] = jnp.concatenate(outs, axis=-1).astype(o_ref.dtype)


def _ctx_attn(sink, aq, akd, avd, bq, bk, bv):
    B = aq.shape[0]
    spec = lambda a: pl.BlockSpec((1,) + a.shape[1:], lambda b: (b, 0, 0))
    args = (aq, akd, avd, bq, bk, bv)
    return pl.pallas_call(
        _ctx_attn_kernel,
        grid=(B,),
        in_specs=[pl.BlockSpec(memory_space=pltpu.SMEM)] + [spec(a) for a in args],
        out_specs=pl.BlockSpec((1, CTX_LEN, 8 * LANES), lambda b: (b, 0, 0)),
        out_shape=jax.ShapeDtypeStruct((B, CTX_LEN, 8 * LANES), BF16),
        compiler_params=_cparams(("parallel",)),
        name="ctx_attn",
    )(sink, *args)


def _pick4(idx, vals):
    return jnp.where(idx == 0, vals[0], jnp.where(idx == 1, vals[1], jnp.where(idx == 2, vals[2], vals[3])))


def _route_rows(lg_t, b_ref):
    n_tok = lg_t.shape[1]
    s = [jax.nn.sigmoid(lg_t[e:e + 1, :]) for e in range(N_EXPERTS)]
    sel = [s[e] + b_ref[e] for e in range(N_EXPERTS)]
    n = EXPERTS_PER_GROUP
    gscore = []
    for j in range(N_GROUPS):
        v = sel[n * j:n * (j + 1)]
        pair_sums = [v[a] + v[b] for a in range(n) for b in range(a + 1, n)]
        gscore.append(functools.reduce(jnp.maximum, pair_sums))
    best, gbest = gscore[0], jnp.zeros((1, n_tok), jnp.int32)
    for j in range(1, N_GROUPS):
        upd = gscore[j] > best
        best = jnp.where(upd, gscore[j], best)
        gbest = jnp.where(upd, j, gbest)
    v = [_pick4(gbest, [sel[n * j + i] for j in range(N_GROUPS)]) for i in range(n)]
    u = [_pick4(gbest, [s[n * j + i] for j in range(N_GROUPS)]) for i in range(n)]
    m1, i1 = v[0], jnp.zeros((1, n_tok), jnp.int32)
    for i in range(1, n):
        upd = v[i] > m1
        m1 = jnp.where(upd, v[i], m1)
        i1 = jnp.where(upd, i, i1)
    m2, i2 = jnp.full((1, n_tok), -jnp.inf, F32), jnp.zeros((1, n_tok), jnp.int32)
    for i in range(n):
        upd = (i1 != i) & (v[i] > m2)
        m2 = jnp.where(upd, v[i], m2)
        i2 = jnp.where(upd, i, i2)
    u1, u2 = _pick4(i1, u), _pick4(i2, u)
    tot = u1 + u2
    return n * gbest + i1, n * gbest + i2, u1 / tot, u2 / tot


def _out_proj_kernel(n_y, sparse, *refs):
    y_refs = refs[:n_y]
    br_ref, w_ref, x_ref, gate_ref, g_ref, shift_ref, scale_ref, wrh_ref, wrl_ref = refs[n_y:n_y + 9]
    n_in = n_y + 9
    if sparse:
        tri_ref, tril_ref = refs[n_in:n_in + 2]
        n_in += 2
    outs = refs[n_in:]
    xo_ref = outs[0]
    off = 0
    acc = None
    for y_ref in y_refs:
        wdt = y_ref.shape[-1]
        part = jnp.dot(y_ref[0].astype(BF16), w_ref[off:off + wdt, :], preferred_element_type=F32)
        acc = part if acc is None else acc + part
        off += wdt
    x = x_ref[0] + gate_ref[0] * acc
    xo_ref[0] = x
    ms = jnp.mean(x * x, axis=-1, keepdims=True)
    h = x * lax.rsqrt(ms + EPS) * g_ref[...]
    h = h * (1.0 + scale_ref[0]) + shift_ref[0]
    hh = h.astype(BF16)
    hl = (h - hh.astype(F32)).astype(BF16)
    lg = (jnp.dot(hh, wrh_ref[...], preferred_element_type=F32)
          + jnp.dot(hl, wrh_ref[...], preferred_element_type=F32)
          + jnp.dot(hh, wrl_ref[...], preferred_element_type=F32))
    lg_t = lg.T[:N_EXPERTS]
    e1, e2, w1, w2 = _route_rows(lg_t, br_ref)
    rows = lax.broadcasted_iota(jnp.int32, lg_t.shape, 0)
    if not sparse:
        h_ref, comb_ref = outs[1:]
        h_ref[0] = hh
        comb_t = jnp.where(rows == e1, w1, 0.0) + jnp.where(rows == e2, w2, 0.0)
        comb_ref[0] = jnp.concatenate(
            [comb_t, jnp.zeros((LANES - N_EXPERTS, comb_t.shape[1]), F32)], axis=0).T
        return

    h_ref, route_t_ref, route_c_ref, ng_ref = outs[1:]
    h_ref[0] = hh
    member = jnp.where((rows == e1) | (rows == e2), 1.0, 0.0)
    before = jnp.dot(member.astype(BF16), tri_ref[...], preferred_element_type=F32)
    groups = jnp.floor((jnp.sum(member, axis=1, keepdims=True) + (MOE_G - 1)) * (1.0 / MOE_G))
    groups = jnp.broadcast_to(groups, (N_EXPERTS, LANES))
    run_start = MOE_G * jnp.dot(tril_ref[...], groups.astype(BF16), preferred_element_type=F32)[:, 0:1]
    pos = run_start + before
    p1 = jnp.sum(jnp.where(rows == e1, pos, 0.0), axis=0, keepdims=True)
    p2 = jnp.sum(jnp.where(rows == e2, pos, 0.0), axis=0, keepdims=True)
    ng_ref[0] = groups
    field = lax.broadcasted_iota(jnp.int32, (LANES, h.shape[0]), 0)
    route = jnp.zeros((LANES, h.shape[0]), F32)
    for k, v in enumerate((p1, p2, w1, w2)):
        route = jnp.where(field == k, v, route)
    route_t_ref[...] = route[:ROUTE_FIELDS]
    route_c_ref[...] = route.T


ROUTE_FIELDS = 8
MOE_G = 8
MOE_TT = 512
MOE_LOCAL = 1152


def _out_proj(ys, w, x, gate, g, shift, scale, router, sparse, tm=512):
    B, S, D = x.shape
    tm = min(tm, S)
    bm = gate.shape[0]
    b_router, wr_hi, wr_lo = router
    nt = S // tm
    mod_map = (lambda b, i: (b, 0, 0)) if bm > 1 else (lambda b, i: (0, 0, 0))
    mod_spec = pl.BlockSpec((1, 1, D), mod_map)
    row_map = lambda b, i: (b, i, 0)
    in_specs = ([pl.BlockSpec((1, tm, y.shape[-1]), row_map) for y in ys]
                + [pl.BlockSpec(memory_space=pltpu.SMEM),
                   pl.BlockSpec(w.shape, lambda b, i: (0, 0)),
                   pl.BlockSpec((1, tm, D), row_map), mod_spec,
                   pl.BlockSpec((1, D), lambda b, i: (0, 0)), mod_spec, mod_spec,
                   pl.BlockSpec(wr_hi.shape, lambda b, i: (0, 0)),
                   pl.BlockSpec(wr_lo.shape, lambda b, i: (0, 0))])
    args = list(ys) + [b_router, w, x, gate.reshape(bm, 1, D), g.reshape(1, D), shift.reshape(bm, 1, D),
                       scale.reshape(bm, 1, D), wr_hi, wr_lo]
    out_specs = [pl.BlockSpec((1, tm, D), row_map)]
    out_shape = [jax.ShapeDtypeStruct((B, S, D), F32)]
    out_specs.append(pl.BlockSpec((1, tm, D), row_map))
    out_shape.append(jax.ShapeDtypeStruct((B, S, D), BF16))
    if sparse:
        assert tm == MOE_TT
        tri = jnp.asarray(np.triu(np.ones((tm, tm), np.float32), 1)).astype(BF16)
        tril = jnp.asarray(np.tril(np.ones((N_EXPERTS, N_EXPERTS), np.float32), -1)).astype(BF16)
        args += [tri, tril]
        in_specs += [pl.BlockSpec(tri.shape, lambda b, i: (0, 0)),
                     pl.BlockSpec(tril.shape, lambda b, i: (0, 0))]
        out_specs += [pl.BlockSpec((ROUTE_FIELDS, tm), lambda b, i: (0, b * nt + i)),
                      pl.BlockSpec((tm, LANES), lambda b, i: (b * nt + i, 0)),
                      pl.BlockSpec((1, N_EXPERTS, LANES), lambda b, i: (b * nt + i, 0, 0))]
        out_shape += [jax.ShapeDtypeStruct((ROUTE_FIELDS, B * S), F32),
                      jax.ShapeDtypeStruct((B * S, LANES), F32),
                      jax.ShapeDtypeStruct((B * nt, N_EXPERTS, LANES), F32)]
    else:
        out_specs.append(pl.BlockSpec((1, tm, LANES), row_map))
        out_shape.append(jax.ShapeDtypeStruct((B, S, LANES), F32))
    return pl.pallas_call(
        functools.partial(_out_proj_kernel, len(ys), sparse),
        grid=(B, nt),
        in_specs=in_specs,
        out_specs=out_specs,
        out_shape=out_shape,
        compiler_params=_cparams(("parallel", "parallel")),
        name="out_proj_sparse" if sparse else "out_proj",
    )(*args)


def _moe_kernel(h_ref, comb_ref, wg_ref, wu_ref, wd_ref, x_ref, gate_ref, o_ref, acc_ref):
    e = pl.program_id(2)

    @pl.when(e == 0)
    def _():
        acc_ref[...] = jnp.zeros_like(acc_ref)

    h = h_ref[0]
    a = jnp.dot(h, wg_ref[0, 0].astype(BF16), preferred_element_type=F32)
    u = jnp.dot(h, wu_ref[0, 0].astype(BF16), preferred_element_type=F32)
    he = (a * jax.nn.sigmoid(a) * u).astype(BF16)
    y = jnp.dot(he, wd_ref[0, 0].astype(BF16), preferred_element_type=F32)
    lane = lax.broadcasted_iota(jnp.int32, comb_ref.shape[1:], 1)
    c = jnp.sum(jnp.where(lane == e, comb_ref[0], 0.0), axis=-1, keepdims=True)
    acc_ref[...] += c * y

    @pl.when(e == pl.num_programs(2) - 1)
    def _():
        o_ref[0] = x_ref[0] + gate_ref[0] * acc_ref[...]


def _moe(h, comb, wg, wu, wd, layer, x, gate, tm=1024):
    B, S, D = x.shape
    tm = min(tm, S)
    bm = gate.shape[0]
    mod_map = (lambda b, i, e: (b, 0, 0)) if bm > 1 else (lambda b, i, e: (0, 0, 0))
    row_map = lambda b, i, e: (b, i, 0)
    return pl.pallas_call(
        _moe_kernel,
        grid=(B, S // tm, N_EXPERTS),
        in_specs=[pl.BlockSpec((1, tm, D), row_map),
                  pl.BlockSpec((1, tm, LANES), row_map),
                  pl.BlockSpec((1, 1, D, D_EXPERT), lambda b, i, e: (layer, e, 0, 0)),
                  pl.BlockSpec((1, 1, D, D_EXPERT), lambda b, i, e: (layer, e, 0, 0)),
                  pl.BlockSpec((1, 1, D_EXPERT, D), lambda b, i, e: (layer, e, 0, 0)),
                  pl.BlockSpec((1, tm, D), row_map),
                  pl.BlockSpec((1, 1, D), mod_map)],
        out_specs=pl.BlockSpec((1, tm, D), row_map),
        out_shape=jax.ShapeDtypeStruct((B, S, D), F32),
        scratch_shapes=[pltpu.VMEM((tm, D), F32)],
        compiler_params=_cparams(("parallel", "parallel", "arbitrary")),
        name="moe",
    )(h, comb, wg, wu, wd, x, gate.reshape(bm, 1, D))


MOE_TM = 1024
MOE_TG = MOE_TM // MOE_G


def _moe_rows(n_tok):
    rows = 2 * n_tok + (n_tok // MOE_TT) * N_EXPERTS * (MOE_G - 1) + N_EXPERTS * (MOE_TM - 1)
    return (rows + MOE_TM - 1) // MOE_TM * MOE_TM


def _moe_plan(ng):
    ng = ng[:, :, 0].astype(jnp.int32)
    n_tt = ng.shape[0]
    total = jnp.sum(ng, axis=0)
    region = (total + MOE_TG - 1) // MOE_TG * MOE_TG
    region_end = jnp.cumsum(region)
    region_start = region_end - region
    dst = region_start[None, :] + jnp.cumsum(ng, axis=0) - ng
    local = jnp.cumsum(ng, axis=1) - ng
    n_tiles = _moe_rows(n_tt * MOE_TT) // MOE_TM
    tile_first = jnp.arange(n_tiles, dtype=jnp.int32) * MOE_TG
    tile_expert = jnp.minimum(jnp.sum(region_end[None, :] <= tile_first[:, None], axis=1), N_EXPERTS - 1)
    n_valid = region_end[-1:] // MOE_TG
    i32 = lambda a: a.astype(jnp.int32).reshape(-1)
    return dict(ng=i32(ng), dst=i32(dst), local=i32(local), tile_groups=i32(jnp.sum(ng, axis=1)),
                pad_first=i32(region_start + total),
                pad_count=i32(region - total), tile_expert=i32(tile_expert), n_valid=i32(n_valid),
                n_tiles=n_tiles)


def _run_copies(plan_refs, tile, local_ref, sorted_ref, sem, to_sorted):
    ng_ref, dst_ref, loc_ref = plan_refs
    for e in range(N_EXPERTS):
        k = tile * N_EXPERTS + e
        loc, dst = loc_ref[k], dst_ref[k]

        def body(g, carry):
            lrows = local_ref.at[pl.ds(pl.multiple_of((loc + g) * MOE_G, MOE_G), MOE_G)]
            srows = sorted_ref.at[pl.ds(pl.multiple_of((dst + g) * MOE_G, MOE_G), MOE_G)]
            src, tgt = (lrows, srows) if to_sorted else (srows, lrows)
            pltpu.make_async_copy(src, tgt, sem).start()
            return carry
        lax.fori_loop(0, ng_ref[k], body, 0)


def _group_waits(n_groups, local_ref, sorted_ref, sem):
    def body(g, carry):
        pltpu.make_async_copy(sorted_ref.at[pl.ds(0, MOE_G)], local_ref.at[pl.ds(0, MOE_G)], sem).wait()
        return carry
    lax.fori_loop(0, n_groups, body, 0)


def _moe_dispatch_kernel(ng_ref, dst_ref, loc_ref, tot_ref, padf_ref, padc_ref, nv_ref,
                         h_ref, route_ref, xs_ref, local_ref, zero_ref, sem):
    i = pl.program_id(0)

    @pl.when(i == 0)
    def _():
        zero_ref[...] = jnp.zeros_like(zero_ref)
        n_pad = 0
        for e in range(N_EXPERTS):
            first = padf_ref[e]

            def body(g, carry):
                rows = xs_ref.at[pl.ds(pl.multiple_of((first + g) * MOE_G, MOE_G), MOE_G)]
                pltpu.make_async_copy(zero_ref.at[pl.ds(0, MOE_G)], rows, sem.at[2]).start()
                return carry
            lax.fori_loop(0, padc_ref[e], body, 0)
            n_pad = n_pad + padc_ref[e]

        def tile_copy(j):
            rows = xs_ref.at[pl.ds(pl.multiple_of(j * MOE_TM, MOE_TM), MOE_TM)]
            return pltpu.make_async_copy(zero_ref, rows, sem.at[3])

        n_tiles = xs_ref.shape[0] // MOE_TM
        lax.fori_loop(nv_ref[0], n_tiles, lambda j, c: (tile_copy(j).start(), c)[1], 0)
        _group_waits(n_pad, zero_ref, xs_ref, sem.at[2])
        lax.fori_loop(nv_ref[0], n_tiles, lambda j, c: (tile_copy(j).wait(), c)[1], 0)

    pos = lax.broadcasted_iota(jnp.int32, (MOE_LOCAL, MOE_TT), 0)
    p1 = route_ref[0:1, :].astype(jnp.int32)
    p2 = route_ref[1:2, :].astype(jnp.int32)
    pick = jnp.where((pos == p1) | (pos == p2), 1.0, 0.0).astype(BF16)
    slot = i % 2
    last = pl.num_programs(0) - 1

    @pl.when(i >= 2)
    def _():
        _group_waits(tot_ref[i - 2], local_ref.at[slot], xs_ref, sem.at[slot])

    local_ref[slot] = jnp.dot(pick, h_ref[0], preferred_element_type=F32)
    _run_copies((ng_ref, dst_ref, loc_ref), i, local_ref.at[slot], xs_ref, sem.at[slot], to_sorted=True)

    @pl.when(i == last)
    def _():
        _group_waits(tot_ref[i], local_ref.at[slot], xs_ref, sem.at[slot])

    @pl.when((i == last) & (last >= 1))
    def _():
        _group_waits(tot_ref[i - 1], local_ref.at[1 - slot], xs_ref, sem.at[1 - slot])


def _moe_dispatch(h, route_t, plan):
    B, S, D = h.shape
    nt = S // MOE_TT
    grid_spec = pltpu.PrefetchScalarGridSpec(
        num_scalar_prefetch=7,
        grid=(B * nt,),
        in_specs=[pl.BlockSpec((1, MOE_TT, D), lambda i, *_: (i // nt, i % nt, 0)),
                  pl.BlockSpec((ROUTE_FIELDS, MOE_TT), lambda i, *_: (0, i))],
        out_specs=pl.BlockSpec(memory_space=pl.ANY),
        scratch_shapes=[pltpu.VMEM((2, MOE_LOCAL, D), F32), pltpu.VMEM((MOE_TM, D), F32),
                        pltpu.SemaphoreType.DMA((4,))])
    return pl.pallas_call(
        _moe_dispatch_kernel,
        grid_spec=grid_spec,
        out_shape=jax.ShapeDtypeStruct((plan["n_tiles"] * MOE_TM, D), F32),
        compiler_params=_cparams(("arbitrary",)),
        name="moe_dispatch",
    )(plan["ng"], plan["dst"], plan["local"], plan["tile_groups"], plan["pad_first"], plan["pad_count"],
      plan["n_valid"], h, route_t)


def _moe_grouped_kernel(te_ref, nv_ref, x_ref, wg_ref, wu_ref, wd_ref, o_ref, wgb_ref, wub_ref, wdb_ref):
    j = pl.program_id(0)
    used = j < nv_ref[0]

    @pl.when(used & ((j == 0) | (te_ref[j] != te_ref[jnp.maximum(j - 1, 0)])))
    def _():
        wgb_ref[...] = wg_ref[0, 0].astype(BF16)
        wub_ref[...] = wu_ref[0, 0].astype(BF16)
        wdb_ref[...] = wd_ref[0, 0].astype(BF16)

    @pl.when(used)
    def _():
        x = x_ref[...].astype(BF16)
        a = jnp.dot(x, wgb_ref[...], preferred_element_type=F32)
        u = jnp.dot(x, wub_ref[...], preferred_element_type=F32)
        he = (a * jax.nn.sigmoid(a) * u).astype(BF16)
        o_ref[...] = jnp.dot(he, wdb_ref[...], preferred_element_type=F32)

    @pl.when(jnp.logical_not(used))
    def _():
        o_ref[...] = jnp.zeros_like(o_ref)


def _moe_grouped(xs, plan, wg, wu, wd, layer):
    n_tiles = plan["n_tiles"]
    D = D_MODEL
    tile = lambda j, nv: jnp.minimum(j, nv[0] - 1)
    grid_spec = pltpu.PrefetchScalarGridSpec(
        num_scalar_prefetch=2,
        grid=(n_tiles,),
        in_specs=[pl.BlockSpec((MOE_TM, D), lambda j, te, nv: (tile(j, nv), 0)),
                  pl.BlockSpec((1, 1, D, D_EXPERT), lambda j, te, nv: (layer, te[tile(j, nv)], 0, 0)),
                  pl.BlockSpec((1, 1, D, D_EXPERT), lambda j, te, nv: (layer, te[tile(j, nv)], 0, 0)),
                  pl.BlockSpec((1, 1, D_EXPERT, D), lambda j, te, nv: (layer, te[tile(j, nv)], 0, 0))],
        out_specs=pl.BlockSpec((MOE_TM, D), lambda j, te, nv: (j, 0)),
        scratch_shapes=[pltpu.VMEM((D, D_EXPERT), BF16), pltpu.VMEM((D, D_EXPERT), BF16),
                        pltpu.VMEM((D_EXPERT, D), BF16)])
    return pl.pallas_call(
        _moe_grouped_kernel,
        grid_spec=grid_spec,
        out_shape=jax.ShapeDtypeStruct((n_tiles * MOE_TM, D), F32),
        compiler_params=_cparams(("arbitrary",)),
        name="moe_grouped",
    )(plan["tile_expert"], plan["n_valid"], xs, wg, wu, wd)


def _moe_combine_kernel(final_norm, ng_ref, dst_ref, loc_ref, tot_ref, ys_ref, route_ref, x_ref, gate_ref,
                        *rest):
    if final_norm:
        fg_ref, o_ref, local_ref, sem = rest
    else:
        o_ref, local_ref, sem = rest
    i = pl.program_id(0)
    slot = i % 2
    plan_refs = (ng_ref, dst_ref, loc_ref)

    @pl.when(i == 0)
    def _():
        local_ref[...] = jnp.zeros_like(local_ref)
        _run_copies(plan_refs, 0, local_ref.at[0], ys_ref, sem.at[0], to_sorted=False)

    @pl.when(i + 1 < pl.num_programs(0))
    def _():
        _run_copies(plan_refs, i + 1, local_ref.at[1 - slot], ys_ref, sem.at[1 - slot], to_sorted=False)

    _group_waits(tot_ref[i], local_ref.at[slot], ys_ref, sem.at[slot])
    pos = lax.broadcasted_iota(jnp.int32, (MOE_TT, MOE_LOCAL), 1)
    route = route_ref[...]
    p1 = route[:, 0:1].astype(jnp.int32)
    p2 = route[:, 1:2].astype(jnp.int32)
    weigh = (jnp.where(pos == p1, route[:, 2:3], 0.0) + jnp.where(pos == p2, route[:, 3:4], 0.0)).astype(BF16)
    y = jnp.dot(weigh, local_ref[slot].astype(BF16), preferred_element_type=F32)
    o = x_ref[0] + gate_ref[0] * y
    if final_norm:
        o = o * lax.rsqrt(jnp.mean(o * o, axis=-1, keepdims=True) + EPS) * fg_ref[...]
    o_ref[0] = o


def _moe_combine(ys, route_c, plan, x, gate, final_g=None):
    B, S, D = x.shape
    nt = S // MOE_TT
    in_specs = [pl.BlockSpec(memory_space=pl.ANY),
                pl.BlockSpec((MOE_TT, LANES), lambda i, *_: (i, 0)),
                pl.BlockSpec((1, MOE_TT, D), lambda i, *_: (i // nt, i % nt, 0)),
                pl.BlockSpec((1, 1, D), lambda i, *_: (i // nt, 0, 0))]
    args = [ys, route_c, x, gate.reshape(B, 1, D)]
    if final_g is not None:
        in_specs.append(pl.BlockSpec((1, D), lambda i, *_: (0, 0)))
        args.append(final_g.reshape(1, D))
    grid_spec = pltpu.PrefetchScalarGridSpec(
        num_scalar_prefetch=4,
        grid=(B * nt,),
        in_specs=in_specs,
        out_specs=pl.BlockSpec((1, MOE_TT, D), lambda i, *_: (i // nt, i % nt, 0)),
        scratch_shapes=[pltpu.VMEM((2, MOE_LOCAL, D), F32), pltpu.SemaphoreType.DMA((2,))])
    return pl.pallas_call(
        functools.partial(_moe_combine_kernel, final_g is not None),
        grid_spec=grid_spec,
        out_shape=jax.ShapeDtypeStruct((B, S, D), F32),
        compiler_params=_cparams(("arbitrary",)),
        name="moe_combine",
    )(plan["ng"], plan["dst"], plan["local"], plan["tile_groups"], *args)


def _dft_tables():
    n1, n2, n = FFT_N1, FFT_N2, FFT_N1 * FFT_N2
    k1 = np.arange(n1)
    f1 = np.exp(-2j * np.pi * np.outer(k1, np.arange(n1)) / n1)
    tw = np.exp(-2j * np.pi * np.outer(np.arange(n2), k1) / n)
    ftw = f1[None, :, :] * tw[:, :, None]
    half = n1 // 2
    fh = ftw[:, :, :half]
    g_fwd = np.concatenate([np.concatenate([fh.real, -fh.imag], axis=2),
                            np.concatenate([fh.imag, fh.real], axis=2)], axis=1)
    back = ftw[:, :, ::-1][:, :, :half].copy()
    back[0] = np.roll(ftw[0], -1, axis=1)[:, ::-1][:, :half]
    back[0][:, 0] = 0.0
    fk = np.concatenate([fh, back], axis=2)
    g_real = np.concatenate([fk.real, fk.imag], axis=1)
    gi = np.conj(np.transpose(fh, (0, 2, 1))) / n
    g_inv = np.concatenate([np.concatenate([gi.real, -gi.imag], axis=2),
                            np.concatenate([gi.imag, gi.real], axis=2)], axis=1)
    f2 = np.exp(-2j * np.pi * np.outer(np.arange(n2), np.arange(n2)) / n2)
    f2_fwd = np.block([[f2.real, -f2.imag], [f2.imag, f2.real]])
    f2c = np.conj(f2)
    f2_inv = np.block([[f2c.real, -f2c.imag], [f2c.imag, f2c.real]])
    as_bf = lambda a: jnp.asarray(a, dtype=F32).astype(BF16)
    return as_bf(g_fwd), as_bf(g_real), as_bf(g_inv), as_bf(f2_fwd), as_bf(f2_inv)


def _fft_fast_stage(stage_ref, k1, f2):
    slab = STAGE_PITCH
    m = jnp.concatenate([stage_ref[pl.ds(k1, FFT_N2, stride=slab), :],
                         stage_ref[pl.ds(FFT_N1 + k1, FFT_N2, stride=slab), :]], axis=0)
    return jnp.dot(f2, m.astype(BF16), preferred_element_type=F32)


def _filter_fft_kernel(hf_ref, hb_ref, inv_ref, g_ref, f2_ref, h_ref, stage_ref):
    slab = STAGE_PITCH
    half = FFT_N1 // 2
    for n2 in range(FFT_N2):
        x = jnp.concatenate([hf_ref[pl.ds(n2, half, stride=FFT_N2), :],
                             hb_ref[pl.ds((FFT_N2 - n2) % FFT_N2, half, stride=FFT_N2), :]], axis=0)
        stage_ref[n2 * slab:n2 * slab + 2 * FFT_N1, :] = jnp.dot(g_ref[n2], x.astype(BF16),
                                                          preferred_element_type=F32)
    f2 = f2_ref[...]
    inv = inv_ref[...]
    for k1 in range(FFT_N1):
        h_ref[0, k1] = (_fft_fast_stage(stage_ref, k1, f2) * inv).astype(h_ref.dtype)


def _filter_fft(taps, inv_norm, g_real, f2_fwd, ct=LANES):
    L, cols = taps.shape
    C = D_CH
    n_ord = cols // (2 * C)
    nc = C // ct
    once = pl.Buffered(1)
    return pl.pallas_call(
        _filter_fft_kernel,
        grid=(n_ord, nc),
        in_specs=[pl.BlockSpec((L, ct), lambda o, c: (0, o * nc + c)),
                  pl.BlockSpec((L, ct), lambda o, c: (0, (n_ord + o) * nc + c)),
                  pl.BlockSpec((1, ct), lambda o, c: (0, o * nc + c)),
                  pl.BlockSpec(g_real.shape, lambda o, c: (0, 0, 0), pipeline_mode=once),
                  pl.BlockSpec(f2_fwd.shape, lambda o, c: (0, 0), pipeline_mode=once)],
        out_specs=pl.BlockSpec((1, FFT_N1, 2 * FFT_N2, ct), lambda o, c: (o, 0, 0, c)),
        out_shape=jax.ShapeDtypeStruct((n_ord, FFT_N1, 2 * FFT_N2, C), BF16),
        scratch_shapes=[pltpu.VMEM((FFT_N2 * STAGE_PITCH, ct), F32)],
        compiler_params=_cparams(("parallel", "parallel")),
        name="filter_fft",
    )(taps, taps, inv_norm, g_real, f2_fwd)


class _RowSets:
    def __init__(self, ref, member):
        self.ref, self.member, self.loaded = ref, member, {}

    def __call__(self, start):
        if start not in self.loaded:
            self.loaded = {k: v for k, v in self.loaded.items() if abs(k - start) <= 2}
            self.loaded[start] = self.ref[self.member, pl.ds(start, FFT_N1 // 2, stride=FFT_N2), :]
        return self.loaded[start]


def _strided_rows(at, n2, taps):
    half = FFT_N1 // 2
    cur = at(n2)
    if taps is None:
        return cur
    w_ref, b_ref = taps
    n1 = lax.broadcasted_iota(jnp.int32, cur.shape, 0)
    if n2 > 0:
        prev = at(n2 - 1)
    else:
        prev = jnp.where(n1 == 0, 0.0, pltpu.roll(at(FFT_N2 - 1), 1, axis=0))
    if n2 < FFT_N2 - 1:
        nxt = at(n2 + 1)
    else:
        nxt = jnp.where(n1 == half - 1, 0.0, pltpu.roll(at(0), half - 1, axis=0))
    return prev * w_ref[0:1, :] + cur * w_ref[1:2, :] + nxt * w_ref[2:3, :] + b_ref[...]


def _hyena_conv_kernel(conv_z, *refs):
    it = iter(refs)
    z_ref, gate_ref = next(it), next(it)
    z_taps = (next(it), next(it)) if conv_z else None
    gate_taps = (next(it), next(it))
    hb_ref, spec_ref, gf_ref, gi_ref, f2f_ref, f2i_ref, o_ref, stage_ref = it
    half = FFT_N1 // 2
    slab = STAGE_PITCH
    z_sets = [_RowSets(z_ref, m) for m in range(2)]
    for n2 in range(FFT_N2):
        x = jnp.concatenate([_strided_rows(z_sets[m], n2, z_taps) for m in range(2)], axis=0).astype(BF16)
        stage_ref[n2 * slab:n2 * slab + 2 * FFT_N1, :] = jnp.dot(gf_ref[n2], x, preferred_element_type=F32)
    f2f = f2f_ref[...]
    f2i = f2i_ref[...]
    for k1 in range(FFT_N1):
        zf = _fft_fast_stage(stage_ref, k1, f2f)
        zr, zi = zf[:FFT_N2], zf[FFT_N2:]
        hr = spec_ref[0, k1, :FFT_N2, :].astype(F32)
        hi = spec_ref[0, k1, FFT_N2:, :].astype(F32)
        p = jnp.concatenate([zr * hr - zi * hi, zr * hi + zi * hr], axis=0).astype(BF16)
        q = jnp.dot(f2i, p, preferred_element_type=F32)
        stage_ref[pl.ds(k1, FFT_N2, stride=slab), :] = q[:FFT_N2]
        stage_ref[pl.ds(FFT_N1 + k1, FFT_N2, stride=slab), :] = q[FFT_N2:]
    hb = hb_ref[...]
    z_sets = [_RowSets(z_ref, m) for m in range(2)]
    gate_sets = [_RowSets(gate_ref, m) for m in range(2)]
    for n2 in range(FFT_N2):
        y_in = stage_ref[n2 * slab:n2 * slab + 2 * FFT_N1, :].astype(BF16)
        y = jnp.dot(gi_ref[n2], y_in, preferred_element_type=F32)
        for m in range(2):
            zm = _strided_rows(z_sets[m], n2, z_taps)
            gm = _strided_rows(gate_sets[m], n2, gate_taps)
            o_ref[m, pl.ds(n2, half, stride=FFT_N2), :] = gm * (y[m * half:(m + 1) * half] + zm * hb)


def _hyena_conv(z, z_blk, gate, gate_blk, short_w, short_b, conv_z, hbias, spec, order, tabs, ct=LANES):
    B, L, _ = z.shape
    C = D_CH
    g_fwd, _, g_inv, f2_fwd, f2_inv = tabs
    once = pl.Buffered(1)
    const3 = lambda a: pl.BlockSpec(a.shape, lambda c, p: (0, 0, 0), pipeline_mode=once)
    const2 = lambda a: pl.BlockSpec(a.shape, lambda c, p: (0, 0), pipeline_mode=once)
    taps_specs = lambda blk: [pl.BlockSpec((3, ct), lambda c, p: (0, blk + c)),
                              pl.BlockSpec((1, ct), lambda c, p: (0, blk + c))]
    in_specs = [pl.BlockSpec((2, L, ct), lambda c, p: (p, 0, z_blk + c)),
                pl.BlockSpec((2, L, ct), lambda c, p: (p, 0, gate_blk + c))]
    args = [z, gate]
    if conv_z:
        in_specs += taps_specs(z_blk)
        args += [short_w, short_b]
    in_specs += taps_specs(gate_blk)
    args += [short_w, short_b]
    in_specs += [pl.BlockSpec((1, ct), lambda c, p: (0, c)),
                 pl.BlockSpec((1, FFT_N1, 2 * FFT_N2, ct), lambda c, p: (order, 0, 0, c), pipeline_mode=once),
                 const3(g_fwd), const3(g_inv), const2(f2_fwd), const2(f2_inv)]
    args += [hbias, spec, g_fwd, g_inv, f2_fwd, f2_inv]
    return pl.pallas_call(
        functools.partial(_hyena_conv_kernel, conv_z),
        grid=(C // ct, B // 2),
        in_specs=in_specs,
        out_specs=pl.BlockSpec((2, L, ct), lambda c, p: (p, 0, c)),
        out_shape=jax.ShapeDtypeStruct((B, L, C), F32),
        scratch_shapes=[pltpu.VMEM((FFT_N2 * STAGE_PITCH, ct), F32)],
        compiler_params=_cparams(("parallel", "arbitrary")),
        name="hyena_conv",
    )(*args)


def _hyena_filters(L, w1, b1, f1, w2, b2, f2, w3, b3):
    t = jnp.arange(L, dtype=F32)
    tn = t / max(L - 1, 1)
    bands = jnp.linspace(1e-4, HY_BANDS - 1, HY_BANDS, dtype=F32)
    ang = 2.0 * math.pi * t[:, None] * bands[None] / L
    feats = jnp.concatenate([tn[:, None], jnp.cos(ang), jnp.sin(ang)], axis=-1)
    h = jnp.sin(f1 * (feats @ w1 + b1))
    h = jnp.sin(f2 * (h @ w2 + b2))
    deltas = jnp.abs(jnp.linspace(HY_MIN_DECAY, HY_MAX_DECAY, D_CH, dtype=F32))
    decay = jnp.exp(-tn[:, None] * deltas[None])
    n_rep = w3.shape[1] // D_CH
    taps = (_mm_f32(h, w3, 512) + b3) * jnp.tile(decay, (1, n_rep))
    l1 = jnp.sum(jnp.abs(taps), axis=0)
    l1 = l1[:n_rep // 2 * D_CH] + l1[n_rep // 2 * D_CH:]
    return taps, (1.0 / (l1 + EPS))[None]


def _dup_heads(w):
    a, b = w[:, :HEAD_DIM], w[:, HEAD_DIM:]
    return jnp.concatenate([a, a, b, b], axis=1)


def _rope_tables(S):
    t = jnp.arange(S)
    row = (t // GRID_W).astype(F32)
    col = (t % GRID_W).astype(F32)
    half = HEAD_DIM // 2
    inv = ROPE_THETA ** (-jnp.arange(0, half, 2, dtype=F32) / half)
    ar = row[:, None] * inv[None]
    ac = col[:, None] * inv[None]
    cos = jnp.concatenate([jnp.cos(ar), jnp.cos(ar), jnp.cos(ac), jnp.cos(ac)], axis=-1)
    sin = jnp.concatenate([-jnp.sin(ar), jnp.sin(ar), -jnp.sin(ac), jnp.sin(ac)], axis=-1)
    return jnp.tile(cos, (1, 2)), jnp.tile(sin, (1, 2))


def _head_mean_matrix(width):
    blk = np.kron(np.eye(width // HEAD_DIM), np.full((HEAD_DIM, HEAD_DIM), 1.0 / HEAD_DIM))
    return jnp.asarray(blk, dtype=F32).astype(BF16)


def kernel(x, c, ctx, c_ctx, w_ada, b_ada, norm_g, final_g, w_in_even, w_out_even, a_sink, b_rpb, w_in_odd, w_out_odd, c_qnorm, c_knorm, hy_short_w, hy_short_b, hy_w1, hy_b1, hy_f1, hy_w2, hy_b2, hy_f2, hy_w3, hy_b3, hy_bias, w_router, b_router, moe_wg, moe_wu, moe_wd):
    B, S, D = x.shape
    depth = w_ada.shape[0]
    rope = _rope_tables(S)
    wr_pad = jnp.pad(w_router.astype(F32), ((0, 0), (0, LANES - N_EXPERTS)))
    wr_hi = wr_pad.astype(BF16)
    router = (b_router.astype(F32), wr_hi, (wr_pad - wr_hi.astype(F32)).astype(BF16))

    mod_in = jnp.concatenate([jax.nn.silu(c), jax.nn.silu(c_ctx)[None],
                              jnp.zeros((8 - B - 1, D), F32)], axis=0)
    xc = ctx
    for l in range(depth):
        need_ctx = l < depth - 1
        mod = _mm_f32(mod_in, w_ada[l], 1536) + b_ada[l]
        mx = mod[:B].reshape(B, 6, D)
        mc = mod[B].reshape(6, D)
        i = l // 2
        if l % 2 == 0:
            w = w_in_even[i].astype(BF16)
            w_all = jnp.concatenate([w[:, :512], _dup_heads(w[:, 512:640]), _dup_heads(w[:, 640:768]),
                                     w[:, 768:]], axis=1)
            segs_x = ((512, "rope", Q_MULT), (256, "rope", 1.0), (256, "plain", 1.0),
                      (512, "plain", Q_MULT), (512, "plain", 1.0), (512, "plain", 1.0))
            sink = a_sink[i].astype(F32) * LOG2_E
            bias = _nbr_bias_table(b_rpb[i].astype(F32) * LOG2_E, S // GRID_W)
            aq, akd, avd, bq, bk, bv = _norm_proj(x, norm_g[l, 0], mx[:, 0], mx[:, 1], w_all, segs_x,
                                                  rope=rope)
            segs_c = tuple((wd, "plain", m) for wd, _, m in segs_x)
            caq, cakd, cavd, cbq, cbk, cbv = _norm_proj(xc, norm_g[l, 0], mc[0:1], mc[1:2], w_all, segs_c)
            ya = _window_attn(aq, akd, avd, cakd, cavd, sink)
            yb = _nbr_attn(bq, bk, bv, cbk, cbv, bias)
            ys = [ya, yb]
            w_out = w_out_even[i].astype(BF16)
            if need_ctx:
                yc = [_ctx_attn(sink, caq, cakd, cavd, cbq, cbk, cbv)]
        else:
            w = w_in_odd[i].astype(BF16)
            w_all = jnp.concatenate([w[:, :512], _dup_heads(w[:, 512:640]), _dup_heads(w[:, 640:768]),
                                     w[:, 768:]], axis=1)
            gains = jnp.concatenate([jnp.tile(c_qnorm[i], 8), jnp.tile(c_knorm[i], 4)])[None].astype(F32)
            norm = (_head_mean_matrix(512), gains)
            segs_x = ((512, "normrope", Q_MULT), (256, "normrope", 1.0), (256, "plain", 1.0),
                      (3 * D_CH, "plain", 1.0, F32))
            qx, kxd, vxd, ux = _norm_proj(x, norm_g[l, 0], mx[:, 0], mx[:, 1], w_all, segs_x,
                                          rope=rope, norm=norm)
            w_c = w_all[:, 512:1024]
            norm_c = (_head_mean_matrix(512), jnp.tile(c_knorm[i], 4)[None].astype(F32))
            kcd, vcd = _norm_proj(xc, norm_g[l, 0], mc[0:1], mc[1:2], w_c,
                                  ((256, "norm", 1.0), (256, "plain", 1.0)), norm=norm_c)
            logit_bound = (1.02 * HEAD_DIM * Q_MULT * jnp.max(jnp.abs(c_qnorm[i]))
                           * jnp.max(jnp.abs(c_knorm[i])))
            y_attn = lax.cond(logit_bound <= FULL_NOMAX_LOG2_BOUND,
                              lambda *a: _full_attn(*a, bounded=True),
                              lambda *a: _full_attn(*a, bounded=False),
                              qx, kxd, vxd, kcd, vcd)
            sw, sb = hy_short_w[i].astype(F32), hy_short_b[i].astype(F32)[None]
            tabs = _dft_tables()
            taps, inv_norm = _hyena_filters(S, hy_w1[i], hy_b1[i], hy_f1[i], hy_w2[i], hy_b2[i],
                                            hy_f2[i], hy_w3[i], hy_b3[i])
            spec = _filter_fft(taps, inv_norm, tabs[1], tabs[3])
            blocks = D_CH // LANES
            z = _hyena_conv(ux, 0, ux, blocks, sw, sb, True, hy_bias[i, 0:1], spec, 0, tabs)
            z = _hyena_conv(z, 0, ux, 2 * blocks, sw, sb, False, hy_bias[i, 1:2], spec, 1, tabs)
            ys = [y_attn, z]
            w_out = w_out_odd[i].astype(BF16)
            if need_ctx:
                raise NotImplementedError("context update of an odd layer is not needed at this depth")

        x, hx, route_t, route_c, ng = _out_proj(ys, w_out, x, mx[:, 2], norm_g[l, 1], mx[:, 3], mx[:, 4],
                                                router, sparse=True)
        if need_ctx:
            xc, hc, comb_c = _out_proj(yc, w_out, xc, mc[2:3], norm_g[l, 1], mc[3:4], mc[4:5], router,
                                       sparse=False)
            flat = lambda a: a.reshape(1, -1, a.shape[-1])
            xc = _moe(flat(hc), flat(comb_c), moe_wg, moe_wu, moe_wd, l, flat(xc), mc[5:6]).reshape(xc.shape)
        plan = _moe_plan(ng)
        sorted_rows = _moe_dispatch(hx, route_t, plan)
        x = _moe_combine(_moe_grouped(sorted_rows, plan, moe_wg, moe_wu, moe_wd, l), route_c, plan, x, mx[:, 5],
                         final_g=None if need_ctx else final_g)
    return x
```

```python
import functools
import math

import numpy as np
import jax
import jax.numpy as jnp
from jax import lax
from jax.experimental import pallas as pl
from jax.experimental.pallas import tpu as pltpu

F32 = jnp.float32
BF16 = jnp.bfloat16

D_MODEL = 1024
GRID_W = 64
CTX_LEN = 256
HEAD_DIM = 64
ROPE_THETA = 10000.0
EPS = 1e-6
ATTN_SCALE = HEAD_DIM ** -0.5
LOG2_E = math.log2(math.e)
Q_MULT = ATTN_SCALE * LOG2_E
A_WINDOW = 128
A_BLOCK = 128
B_WIN_H = 8
B_WIN_W = 16
D_CH = 512
HY_BANDS = 16
HY_MAX_DECAY = math.log(1e-2) / 0.3
HY_MIN_DECAY = math.log(1e-2) / 1.5
N_EXPERTS = 16
N_GROUPS = 4
EXPERTS_PER_GROUP = N_EXPERTS // N_GROUPS
TOP_K = 2
D_EXPERT = 512

LANES = 128
NEG = -1e30
VMEM_LIMIT = 48 * 1024 * 1024

FFT_N1 = 64
FFT_N2 = 128
STAGE_PITCH = 2 * FFT_N1 + 8


def _cparams(sem):
    return pltpu.CompilerParams(dimension_semantics=sem, vmem_limit_bytes=VMEM_LIMIT)


def _mm_f32_kernel(x_ref, w_ref, o_ref):
    o_ref[...] = jnp.dot(x_ref[...], w_ref[...], preferred_element_type=F32)


def _mm_f32(x, w, tn):
    M, K = x.shape
    N = w.shape[1]
    return pl.pallas_call(
        _mm_f32_kernel,
        grid=(N // tn,),
        in_specs=[pl.BlockSpec((M, K), lambda j: (0, 0)),
                  pl.BlockSpec((K, tn), lambda j: (0, j))],
        out_specs=pl.BlockSpec((M, tn), lambda j: (0, j)),
        out_shape=jax.ShapeDtypeStruct((M, N), F32),
        compiler_params=_cparams(("arbitrary",)),
        name="mm_f32",
    )(x, w)


def _swap16(y):
    n = y.shape[-1]
    lane = lax.broadcasted_iota(jnp.int32, y.shape, y.ndim - 1)
    up = pltpu.roll(y, n - 16, axis=y.ndim - 1)
    dn = pltpu.roll(y, 16, axis=y.ndim - 1)
    return jnp.where((lane % 32) < 16, up, dn)


def _tile_lanes(t, width):
    reps = width // t.shape[-1]
    return t if reps == 1 else jnp.concatenate([t] * reps, axis=-1)


def _norm_proj_kernel(segs, has_rope, has_norm, *refs):
    it = iter(refs)
    x_ref, g_ref, shift_ref, scale_ref, w_ref = (next(it) for _ in range(5))
    cos_ref = sin_ref = bd_ref = gain_ref = None
    if has_rope:
        cos_ref, sin_ref = next(it), next(it)
    if has_norm:
        bd_ref, gain_ref = next(it), next(it)
    out_refs = list(it)

    x = x_ref[0]
    ms = jnp.mean(x * x, axis=-1, keepdims=True)
    h = x * lax.rsqrt(ms + EPS) * g_ref[...]
    h = h * (1.0 + scale_ref[0]) + shift_ref[0]
    y = jnp.dot(h.astype(BF16), w_ref[...], preferred_element_type=F32)

    off = 0
    goff = 0
    for (width, kind, mult, *_), o_ref in zip(segs, out_refs):
        ys = y[:, off:off + width]
        if kind in ("norm", "normrope"):
            bd = bd_ref[...][:width, :width]
            hms = jnp.dot((ys * ys).astype(BF16), bd, preferred_element_type=F32)
            ys = ys * lax.rsqrt(hms + EPS) * gain_ref[:, goff:goff + width]
            goff += width
        if kind in ("rope", "normrope"):
            c = _tile_lanes(cos_ref[...], width)
            s = _tile_lanes(sin_ref[...], width)
            ys = ys * c + _swap16(ys) * s
        if mult != 1.0:
            ys = ys * mult
        o_ref[0] = ys.astype(o_ref.dtype)
        off += width


def _norm_proj(x, g, shift, scale, w, segs, rope=None, norm=None, tm=512):
    B, S, D = x.shape
    N = w.shape[1]
    tm = min(tm, S)
    bm = shift.shape[0]
    mod_map = (lambda b, i: (b, 0, 0)) if bm > 1 else (lambda b, i: (0, 0, 0))
    args = [x, g.reshape(1, D), shift.reshape(bm, 1, D), scale.reshape(bm, 1, D), w]
    in_specs = [pl.BlockSpec((1, tm, D), lambda b, i: (b, i, 0)),
                pl.BlockSpec((1, D), lambda b, i: (0, 0)),
                pl.BlockSpec((1, 1, D), mod_map),
                pl.BlockSpec((1, 1, D), mod_map),
                pl.BlockSpec((D, N), lambda b, i: (0, 0))]
    if rope is not None:
        args += [rope[0], rope[1]]
        in_specs += [pl.BlockSpec((tm, LANES), lambda b, i: (i, 0))] * 2
    if norm is not None:
        args += [norm[0], norm[1]]
        in_specs += [pl.BlockSpec(norm[0].shape, lambda b, i: (0, 0)),
                     pl.BlockSpec(norm[1].shape, lambda b, i: (0, 0))]
    out_shape = [jax.ShapeDtypeStruct((B, S, seg[0]), seg[3] if len(seg) > 3 else BF16) for seg in segs]
    out_specs = [pl.BlockSpec((1, tm, seg[0]), lambda b, i: (b, i, 0)) for seg in segs]
    return pl.pallas_call(
        functools.partial(_norm_proj_kernel, segs, rope is not None, norm is not None),
        grid=(B, S // tm),
        in_specs=in_specs,
        out_specs=out_specs,
        out_shape=out_shape,
        compiler_params=_cparams(("parallel", "parallel")),
        name="norm_proj",
    )(*args)


def _half_mask(shape):
    return lax.broadcasted_iota(jnp.int32, shape, len(shape) - 1) < HEAD_DIM


def _stack_halves(qp):
    lo = _half_mask(qp.shape)
    zero = jnp.zeros_like(qp)
    return jnp.concatenate([jnp.where(lo, qp, zero), jnp.where(lo, zero, qp)], axis=0)


def _merge_halves(o, m):
    return jnp.where(_half_mask((m, LANES)), o[:m], o[m:])


def _scores(q, k):
    return lax.dot_general(q, k, (((1,), (1,)), ((), ())), preferred_element_type=F32)


def _joint_softmax_pv(score_parts, value_parts, extra_logit=None):
    m = functools.reduce(jnp.maximum, [jnp.max(s, axis=-1, keepdims=True) for s in score_parts])
    if extra_logit is not None:
        m = jnp.maximum(m, extra_logit)
    den = jnp.exp2(extra_logit - m) if extra_logit is not None else 0.0
    acc = None
    for s, v in zip(score_parts, value_parts):
        p = jnp.exp2(s - m)
        den = den + jnp.sum(p, axis=-1, keepdims=True)
        pv = jnp.dot(p.astype(BF16), v, preferred_element_type=F32)
        acc = pv if acc is None else acc + pv
    return acc / den


def _sink_column(sink_ref, first_head, n_heads, rows_per_head):
    rows = lax.broadcasted_iota(jnp.int32, (n_heads * rows_per_head, 1), 0)
    col = jnp.zeros((n_heads * rows_per_head, 1), F32)
    for j in range(n_heads):
        in_head = (rows >= j * rows_per_head) & (rows < (j + 1) * rows_per_head)
        col = jnp.where(in_head, sink_ref[first_head + j], col)
    return col


def _window_attn_kernel(sink_ref, q_ref, kp_ref, kc_ref, kn_ref, vp_ref, vc_ref, vn_ref,
                        ck_ref, cv_ref, mask_ref, o_ref):
    blk = A_BLOCK
    q = q_ref[0]
    mask = mask_ref[0]
    outs = []
    for g in range(2):
        ls = slice(g * LANES, (g + 1) * LANES)
        k_loc = jnp.concatenate([kp_ref[0][:, ls], kc_ref[0][:, ls], kn_ref[0][:, ls]], axis=0)
        v_loc = jnp.concatenate([vp_ref[0][:, ls], vc_ref[0][:, ls], vn_ref[0][:, ls]], axis=0)
        qs = jnp.concatenate([_stack_halves(q[:, (2 * g + j) * LANES:(2 * g + j + 1) * LANES])
                              for j in range(2)], axis=0)
        s_loc = _scores(qs, k_loc) + mask
        s_ctx = _scores(qs, ck_ref[0][:, ls])
        sink = _sink_column(sink_ref, 4 * g, 4, blk)
        o = _joint_softmax_pv([s_loc, s_ctx], [v_loc, cv_ref[0][:, ls]], sink)
        outs += [_merge_halves(o[:2 * blk], blk), _merge_halves(o[2 * blk:], blk)]
    o_ref[0] = jnp.concatenate(outs, axis=-1).astype(o_ref.dtype)


def _window_mask_table(seq_len):
    blk = A_BLOCK
    rows = np.arange(4 * blk)[:, None] % blk
    rel = np.arange(3 * blk)[None, :] - blk
    near = np.abs(rows - rel) <= A_WINDOW
    tabs = [near & (rel >= 0), near, near & (rel < blk)]
    return jnp.asarray(np.where(np.stack(tabs), 0.0, NEG).astype(np.float32))


def _window_attn(q, kd, vd, ckd, cvd, sink):
    B, S, _ = q.shape
    nb = S // A_BLOCK
    mask = _window_mask_table(S)
    pat = lambda i: jnp.where(i == 0, 0, jnp.where(i == nb - 1, 2, 1))
    kv_spec = lambda f: pl.BlockSpec((1, A_BLOCK, 2 * LANES), f)
    prev_map = lambda b, i: (b, jnp.maximum(i - 1, 0), 0)
    cur_map = lambda b, i: (b, i, 0)
    next_map = lambda b, i: (b, jnp.minimum(i + 1, nb - 1), 0)
    ctx_spec = pl.BlockSpec((1, CTX_LEN, 2 * LANES), lambda b, i: (b, 0, 0))
    return pl.pallas_call(
        _window_attn_kernel,
        grid=(B, nb),
        in_specs=[pl.BlockSpec(memory_space=pltpu.SMEM),
                  pl.BlockSpec((1, A_BLOCK, 4 * LANES), cur_map),
                  kv_spec(prev_map), kv_spec(cur_map), kv_spec(next_map),
                  kv_spec(prev_map), kv_spec(cur_map), kv_spec(next_map),
                  ctx_spec, ctx_spec,
                  pl.BlockSpec((1,) + mask.shape[1:], lambda b, i: (pat(i), 0, 0))],
        out_specs=pl.BlockSpec((1, A_BLOCK, 4 * LANES), cur_map),
        out_shape=jax.ShapeDtypeStruct((B, S, 4 * LANES), BF16),
        compiler_params=_cparams(("parallel", "parallel")),
        name="window_attn",
    )(sink, q, kd, kd, kd, vd, vd, vd, ckd, cvd, mask)


NBR_ROWS = 4
NBR_KROWS = 12
NBR_PAIRS = 4


def _nbr_start_row(i, n_rows):
    return jnp.clip(i * NBR_ROWS - B_WIN_H // 2, 0, n_rows - NBR_KROWS)


def _nbr_attn_kernel(n_rows, q_ref, k_ref, v_ref, ck_ref, cv_ref, bias_ref, o_ref):
    i = pl.program_id(2)
    nq = NBR_ROWS * GRID_W
    nk = NBR_KROWS * GRID_W
    start = pl.multiple_of(_nbr_start_row(i, n_rows) * GRID_W, GRID_W)
    outs = []
    for pp in range(NBR_PAIRS):
        ls = slice(pp * LANES, (pp + 1) * LANES)
        k_loc = k_ref[0, pl.ds(start, nk), ls]
        v_loc = v_ref[0, pl.ds(start, nk), ls]
        qs = _stack_halves(q_ref[0][:, ls])
        s_loc = _scores(qs, k_loc) + bias_ref[0, 2 * pp:2 * pp + 2].reshape(2 * nq, nk)
        s_ctx = _scores(qs, ck_ref[0][:, ls])
        o = _joint_softmax_pv([s_loc, s_ctx], [v_loc, cv_ref[0][:, ls]])
        outs.append(_merge_halves(o, nq))
    o_ref[0] = jnp.concatenate(outs, axis=-1).astype(o_ref.dtype)


def _nbr_bias_table(rpb, n_rows):
    kh = B_WIN_H
    n_heads = rpb.shape[0]
    col = np.arange(GRID_W)
    cs = np.clip(col - B_WIN_W // 2, 0, GRID_W - B_WIN_W)
    col_ok = (col[None, :] >= cs[:, None]) & (col[None, :] < cs[:, None] + B_WIN_W)
    dc = np.clip(col[None, :] - col[:, None], -(B_WIN_W - 1), B_WIN_W - 1) + B_WIN_W - 1
    pick_col = (dc[..., None] == np.arange(2 * B_WIN_W - 1)).astype(np.float32)
    r0 = np.array([0, NBR_ROWS, n_rows - NBR_ROWS])
    start = np.clip(r0 - kh // 2, 0, n_rows - NBR_KROWS)
    r = r0[:, None] + np.arange(NBR_ROWS)[None]
    rs = np.clip(r - kh // 2, 0, n_rows - kh)
    kr = start[:, None] + np.arange(NBR_KROWS)[None]
    row_ok = (kr[:, None, :] >= rs[:, :, None]) & (kr[:, None, :] < rs[:, :, None] + kh)
    dr = np.clip(kr[:, None, :] - r[:, :, None] + kh - 1, 0, 2 * kh - 2)
    pick_row = (dr[..., None] == np.arange(2 * kh - 1)).astype(np.float32)
    table = jnp.einsum("prkd,hdc,qjc->phrqkj", pick_row, rpb.astype(F32), pick_col,
                       precision=lax.Precision.HIGHEST)
    ok = row_ok[:, None, :, None, :, None] & col_ok[None, None, None, :, None, :]
    return jnp.where(ok, table, NEG).reshape(3, n_heads, NBR_ROWS * GRID_W, NBR_KROWS * GRID_W)


def _nbr_attn(q, k, v, ck, cv, bias):
    B, S, _ = q.shape
    n_rows = S // GRID_W
    nsteps = n_rows // NBR_ROWS
    nq = NBR_ROWS * GRID_W
    nk = NBR_KROWS * GRID_W
    pat = lambda i: jnp.where(i == 0, 0, jnp.where(i == nsteps - 1, 2, 1))
    wl = NBR_PAIRS * LANES
    return pl.pallas_call(
        functools.partial(_nbr_attn_kernel, n_rows),
        grid=(B, 4 // NBR_PAIRS, nsteps),
        in_specs=[pl.BlockSpec((1, nq, wl), lambda b, p, i: (b, i, p)),
                  pl.BlockSpec((1, S, wl), lambda b, p, i: (b, 0, p)),
                  pl.BlockSpec((1, S, wl), lambda b, p, i: (b, 0, p)),
                  pl.BlockSpec((1, CTX_LEN, wl), lambda b, p, i: (b, 0, p)),
                  pl.BlockSpec((1, CTX_LEN, wl), lambda b, p, i: (b, 0, p)),
                  pl.BlockSpec((1, 2 * NBR_PAIRS, nq, nk), lambda b, p, i: (pat(i), p, 0, 0))],
        out_specs=pl.BlockSpec((1, nq, wl), lambda b, p, i: (b, i, p)),
        out_shape=jax.ShapeDtypeStruct((B, S, 4 * LANES), BF16),
        compiler_params=_cparams(("parallel", "parallel", "arbitrary")),
        name="nbr_attn",
    )(q, k, v, ck, cv, bias)


FULL_TQ = 512
FULL_TK = 512
FULL_NOMAX_LOG2_BOUND = 60.0


def _full_attn_kernel(bounded, q_ref, k_ref, v_ref, ck_ref, cv_ref, o_ref):
    tq = FULL_TQ
    q = q_ref[0]
    groups = range(2)
    lanes = [slice(g * LANES, (g + 1) * LANES) for g in groups]
    qs = [jnp.concatenate([_stack_halves(q[:, (2 * g + j) * LANES:(2 * g + j + 1) * LANES]) for j in range(2)],
                          axis=0) for g in groups]

    def step_bounded(g, acc, k, v):
        v_ones = jnp.where(_half_mask(v.shape), v, jnp.ones_like(v))
        p = jnp.exp2(_scores(qs[g], k))
        return acc + jnp.dot(p.astype(BF16), v_ones, preferred_element_type=F32)

    def step_online(g, carry, k, v):
        m, l, acc = carry
        s = _scores(qs[g], k)
        m_new = jnp.maximum(m, jnp.max(s, axis=-1, keepdims=True))
        alpha = jnp.exp2(m - m_new)
        p = jnp.exp2(s - m_new)
        l = l * alpha + jnp.sum(p, axis=-1, keepdims=True)
        acc = acc * alpha + jnp.dot(p.astype(BF16), v, preferred_element_type=F32)
        return m_new, l, acc

    step = step_bounded if bounded else step_online
    init = jnp.zeros((4 * tq, LANES), F32)
    if not bounded:
        init = (jnp.full((4 * tq, 1), NEG, F32), jnp.zeros((4 * tq, 1), F32), init)
    carry = tuple(step(g, init, ck_ref[0][:, lanes[g]], cv_ref[0][:, lanes[g]]) for g in groups)

    def body(j, carry):
        rows = pl.ds(pl.multiple_of(j * FULL_TK, FULL_TK), FULL_TK)
        return tuple(step(g, carry[g], k_ref[0, rows, lanes[g]], v_ref[0, rows, lanes[g]]) for g in groups)

    carry = lax.fori_loop(0, k_ref.shape[1] // FULL_TK, body, carry)
    pairs = []
    for g in groups:
        if bounded:
            o = carry[g] * pltpu.roll(1.0 / carry[g], HEAD_DIM, axis=1)
            merge = lambda a, b: jnp.where(_half_mask((tq, LANES)), a, pltpu.roll(b, HEAD_DIM, axis=1))
            pairs += [merge(o[0:tq], o[tq:2 * tq]), merge(o[2 * tq:3 * tq], o[3 * tq:])]
        else:
            o = carry[g][2] / carry[g][1]
            pairs += [_merge_halves(o[:2 * tq], tq), _merge_halves(o[2 * tq:], tq)]
    o_ref[0] = jnp.concatenate(pairs, axis=-1).astype(o_ref.dtype)


def _full_attn(q, kd, vd, ckd, cvd, bounded):
    B, S, W = q.shape
    return pl.pallas_call(
        functools.partial(_full_attn_kernel, bounded),
        grid=(B, S // FULL_TQ),
        in_specs=[pl.BlockSpec((1, FULL_TQ, W), lambda b, i: (b, i, 0)),
                  pl.BlockSpec((1, S, 2 * LANES), lambda b, i: (b, 0, 0)),
                  pl.BlockSpec((1, S, 2 * LANES), lambda b, i: (b, 0, 0)),
                  pl.BlockSpec((1, CTX_LEN, 2 * LANES), lambda b, i: (b, 0, 0)),
                  pl.BlockSpec((1, CTX_LEN, 2 * LANES), lambda b, i: (b, 0, 0))],
        out_specs=pl.BlockSpec((1, FULL_TQ, W), lambda b, i: (b, i, 0)),
        out_shape=jax.ShapeDtypeStruct((B, S, W), BF16),
        compiler_params=_cparams(("parallel", "arbitrary")),
        name="full_attn_bounded" if bounded else "full_attn_online",
    )(q, kd, vd, ckd, cvd)


def _ctx_attn_kernel(sink_ref, aq_ref, akd_ref, avd_ref, bq_ref, bk_ref, bv_ref, o_ref):
    n = CTX_LEN
    aq = aq_ref[0]
    bq = bq_ref[0]
    outs = []
    for g in range(2):
        ls = slice(g * LANES, (g + 1) * LANES)
        qs = jnp.concatenate([_stack_halves(aq[:, (2 * g + j) * LANES:(2 * g + j + 1) * LANES])
                              for j in range(2)], axis=0)
        sink = _sink_column(sink_ref, 4 * g, 4, n)
        o = _joint_softmax_pv([_scores(qs, akd_ref[0][:, ls])], [avd_ref[0][:, ls]], sink)
        outs += [_merge_halves(o[:2 * n], n), _merge_halves(o[2 * n:], n)]
    for p in range(4):
        ls = slice(p * LANES, (p + 1) * LANES)
        qs = _stack_halves(bq[:, ls])
        o = _joint_softmax_pv([_scores(qs, bk_ref[0][:, ls])], [bv_ref[0][:, ls]])
        outs.append(_merge_halves(o, n))
    o_ref[0] = jnp.concatenate(outs, axis=-1).astype(o_ref.dtype)


def _ctx_attn(sink, aq, akd, avd, bq, bk, bv):
    B = aq.shape[0]
    spec = lambda a: pl.BlockSpec((1,) + a.shape[1:], lambda b: (b, 0, 0))
    args = (aq, akd, avd, bq, bk, bv)
    return pl.pallas_call(
        _ctx_attn_kernel,
        grid=(B,),
        in_specs=[pl.BlockSpec(memory_space=pltpu.SMEM)] + [spec(a) for a in args],
        out_specs=pl.BlockSpec((1, CTX_LEN, 8 * LANES), lambda b: (b, 0, 0)),
        out_shape=jax.ShapeDtypeStruct((B, CTX_LEN, 8 * LANES), BF16),
        compiler_params=_cparams(("parallel",)),
        name="ctx_attn",
    )(sink, *args)


def _pick4(idx, vals):
    return jnp.where(idx == 0, vals[0], jnp.where(idx == 1, vals[1], jnp.where(idx == 2, vals[2], vals[3])))


def _route_rows(lg_t, b_ref):
    n_tok = lg_t.shape[1]
    s = [jax.nn.sigmoid(lg_t[e:e + 1, :]) for e in range(N_EXPERTS)]
    sel = [s[e] + b_ref[e] for e in range(N_EXPERTS)]
    n = EXPERTS_PER_GROUP
    gscore = []
    for j in range(N_GROUPS):
        v = sel[n * j:n * (j + 1)]
        pair_sums = [v[a] + v[b] for a in range(n) for b in range(a + 1, n)]
        gscore.append(functools.reduce(jnp.maximum, pair_sums))
    best, gbest = gscore[0], jnp.zeros((1, n_tok), jnp.int32)
    for j in range(1, N_GROUPS):
        upd = gscore[j] > best
        best = jnp.where(upd, gscore[j], best)
        gbest = jnp.where(upd, j, gbest)
    v = [_pick4(gbest, [sel[n * j + i] for j in range(N_GROUPS)]) for i in range(n)]
    u = [_pick4(gbest, [s[n * j + i] for j in range(N_GROUPS)]) for i in range(n)]
    m1, i1 = v[0], jnp.zeros((1, n_tok), jnp.int32)
    for i in range(1, n):
        upd = v[i] > m1
        m1 = jnp.where(upd, v[i], m1)
        i1 = jnp.where(upd, i, i1)
    m2, i2 = jnp.full((1, n_tok), -jnp.inf, F32), jnp.zeros((1, n_tok), jnp.int32)
    for i in range(n):
        upd = (i1 != i) & (v[i] > m2)
        m2 = jnp.where(upd, v[i], m2)
        i2 = jnp.where(upd, i, i2)
    u1, u2 = _pick4(i1, u), _pick4(i2, u)
    tot = u1 + u2
    return n * gbest + i1, n * gbest + i2, u1 / tot, u2 / tot


def _out_proj_kernel(n_y, sparse, *refs):
    y_refs = refs[:n_y]
    br_ref, w_ref, x_ref, gate_ref, g_ref, shift_ref, scale_ref, wrh_ref, wrl_ref = refs[n_y:n_y + 9]
    n_in = n_y + 9
    if sparse:
        tri_ref, tril_ref = refs[n_in:n_in + 2]
        n_in += 2
    outs = refs[n_in:]
    xo_ref = outs[0]
    off = 0
    acc = None
    for y_ref in y_refs:
        wdt = y_ref.shape[-1]
        part = jnp.dot(y_ref[0].astype(BF16), w_ref[off:off + wdt, :], preferred_element_type=F32)
        acc = part if acc is None else acc + part
        off += wdt
    x = x_ref[0] + gate_ref[0] * acc
    xo_ref[0] = x
    ms = jnp.mean(x * x, axis=-1, keepdims=True)
    h = x * lax.rsqrt(ms + EPS) * g_ref[...]
    h = h * (1.0 + scale_ref[0]) + shift_ref[0]
    hh = h.astype(BF16)
    hl = (h - hh.astype(F32)).astype(BF16)
    lg = (jnp.dot(hh, wrh_ref[...], preferred_element_type=F32)
          + jnp.dot(hl, wrh_ref[...], preferred_element_type=F32)
          + jnp.dot(hh, wrl_ref[...], preferred_element_type=F32))
    lg_t = lg.T[:N_EXPERTS]
    e1, e2, w1, w2 = _route_rows(lg_t, br_ref)
    rows = lax.broadcasted_iota(jnp.int32, lg_t.shape, 0)
    if not sparse:
        h_ref, comb_ref = outs[1:]
        h_ref[0] = hh
        comb_t = jnp.where(rows == e1, w1, 0.0) + jnp.where(rows == e2, w2, 0.0)
        comb_ref[0] = jnp.concatenate(
            [comb_t, jnp.zeros((LANES - N_EXPERTS, comb_t.shape[1]), F32)], axis=0).T
        return

    h_ref, route_t_ref, route_c_ref, ng_ref = outs[1:]
    h_ref[0] = hh
    member = jnp.where((rows == e1) | (rows == e2), 1.0, 0.0)
    before = jnp.dot(member.astype(BF16), tri_ref[...], preferred_element_type=F32)
    groups = jnp.floor((jnp.sum(member, axis=1, keepdims=True) + (MOE_G - 1)) * (1.0 / MOE_G))
    groups = jnp.broadcast_to(groups, (N_EXPERTS, LANES))
    run_start = MOE_G * jnp.dot(tril_ref[...], groups.astype(BF16), preferred_element_type=F32)[:, 0:1]
    pos = run_start + before
    p1 = jnp.sum(jnp.where(rows == e1, pos, 0.0), axis=0, keepdims=True)
    p2 = jnp.sum(jnp.where(rows == e2, pos, 0.0), axis=0, keepdims=True)
    ng_ref[0] = groups
    field = lax.broadcasted_iota(jnp.int32, (LANES, h.shape[0]), 0)
    route = jnp.zeros((LANES, h.shape[0]), F32)
    for k, v in enumerate((p1, p2, w1, w2)):
        route = jnp.where(field == k, v, route)
    route_t_ref[...] = route[:ROUTE_FIELDS]
    route_c_ref[...] = route.T


ROUTE_FIELDS = 8
MOE_G = 16
MOE_TT = 512
MOE_LOCAL = 1280


def _out_proj(ys, w, x, gate, g, shift, scale, router, sparse, tm=512):
    B, S, D = x.shape
    tm = min(tm, S)
    bm = gate.shape[0]
    b_router, wr_hi, wr_lo = router
    nt = S // tm
    mod_map = (lambda b, i: (b, 0, 0)) if bm > 1 else (lambda b, i: (0, 0, 0))
    mod_spec = pl.BlockSpec((1, 1, D), mod_map)
    row_map = lambda b, i: (b, i, 0)
    in_specs = ([pl.BlockSpec((1, tm, y.shape[-1]), row_map) for y in ys]
                + [pl.BlockSpec(memory_space=pltpu.SMEM),
                   pl.BlockSpec(w.shape, lambda b, i: (0, 0)),
                   pl.BlockSpec((1, tm, D), row_map), mod_spec,
                   pl.BlockSpec((1, D), lambda b, i: (0, 0)), mod_spec, mod_spec,
                   pl.BlockSpec(wr_hi.shape, lambda b, i: (0, 0)),
                   pl.BlockSpec(wr_lo.shape, lambda b, i: (0, 0))])
    args = list(ys) + [b_router, w, x, gate.reshape(bm, 1, D), g.reshape(1, D), shift.reshape(bm, 1, D),
                       scale.reshape(bm, 1, D), wr_hi, wr_lo]
    out_specs = [pl.BlockSpec((1, tm, D), row_map)]
    out_shape = [jax.ShapeDtypeStruct((B, S, D), F32)]
    out_specs.append(pl.BlockSpec((1, tm, D), row_map))
    out_shape.append(jax.ShapeDtypeStruct((B, S, D), BF16))
    if sparse:
        assert tm == MOE_TT
        tri = jnp.asarray(np.triu(np.ones((tm, tm), np.float32), 1)).astype(BF16)
        tril = jnp.asarray(np.tril(np.ones((N_EXPERTS, N_EXPERTS), np.float32), -1)).astype(BF16)
        args += [tri, tril]
        in_specs += [pl.BlockSpec(tri.shape, lambda b, i: (0, 0)),
                     pl.BlockSpec(tril.shape, lambda b, i: (0, 0))]
        out_specs += [pl.BlockSpec((ROUTE_FIELDS, tm), lambda b, i: (0, b * nt + i)),
                      pl.BlockSpec((tm, LANES), lambda b, i: (b * nt + i, 0)),
                      pl.BlockSpec((1, N_EXPERTS, LANES), lambda b, i: (b * nt + i, 0, 0))]
        out_shape += [jax.ShapeDtypeStruct((ROUTE_FIELDS, B * S), F32),
                      jax.ShapeDtypeStruct((B * S, LANES), F32),
                      jax.ShapeDtypeStruct((B * nt, N_EXPERTS, LANES), F32)]
    else:
        out_specs.append(pl.BlockSpec((1, tm, LANES), row_map))
        out_shape.append(jax.ShapeDtypeStruct((B, S, LANES), F32))
    return pl.pallas_call(
        functools.partial(_out_proj_kernel, len(ys), sparse),
        grid=(B, nt),
        in_specs=in_specs,
        out_specs=out_specs,
        out_shape=out_shape,
        compiler_params=_cparams(("parallel", "parallel")),
        name="out_proj_sparse" if sparse else "out_proj",
    )(*args)


def _moe_kernel(h_ref, comb_ref, wg_ref, wu_ref, wd_ref, x_ref, gate_ref, o_ref, acc_ref):
    e = pl.program_id(2)

    @pl.when(e == 0)
    def _():
        acc_ref[...] = jnp.zeros_like(acc_ref)

    h = h_ref[0]
    a = jnp.dot(h, wg_ref[0, 0].astype(BF16), preferred_element_type=F32)
    u = jnp.dot(h, wu_ref[0, 0].astype(BF16), preferred_element_type=F32)
    he = (a * jax.nn.sigmoid(a) * u).astype(BF16)
    y = jnp.dot(he, wd_ref[0, 0].astype(BF16), preferred_element_type=F32)
    lane = lax.broadcasted_iota(jnp.int32, comb_ref.shape[1:], 1)
    c = jnp.sum(jnp.where(lane == e, comb_ref[0], 0.0), axis=-1, keepdims=True)
    acc_ref[...] += c * y

    @pl.when(e == pl.num_programs(2) - 1)
    def _():
        o_ref[0] = x_ref[0] + gate_ref[0] * acc_ref[...]


def _moe(h, comb, wg, wu, wd, layer, x, gate, tm=1024):
    B, S, D = x.shape
    tm = min(tm, S)
    bm = gate.shape[0]
    mod_map = (lambda b, i, e: (b, 0, 0)) if bm > 1 else (lambda b, i, e: (0, 0, 0))
    row_map = lambda b, i, e: (b, i, 0)
    return pl.pallas_call(
        _moe_kernel,
        grid=(B, S // tm, N_EXPERTS),
        in_specs=[pl.BlockSpec((1, tm, D), row_map),
                  pl.BlockSpec((1, tm, LANES), row_map),
                  pl.BlockSpec((1, 1, D, D_EXPERT), lambda b, i, e: (layer, e, 0, 0)),
                  pl.BlockSpec((1, 1, D, D_EXPERT), lambda b, i, e: (layer, e, 0, 0)),
                  pl.BlockSpec((1, 1, D_EXPERT, D), lambda b, i, e: (layer, e, 0, 0)),
                  pl.BlockSpec((1, tm, D), row_map),
                  pl.BlockSpec((1, 1, D), mod_map)],
        out_specs=pl.BlockSpec((1, tm, D), row_map),
        out_shape=jax.ShapeDtypeStruct((B, S, D), F32),
        scratch_shapes=[pltpu.VMEM((tm, D), F32)],
        compiler_params=_cparams(("parallel", "parallel", "arbitrary")),
        name="moe",
    )(h, comb, wg, wu, wd, x, gate.reshape(bm, 1, D))


MOE_TM = 1024
MOE_TG = MOE_TM // MOE_G


def _moe_rows(n_tok):
    rows = 2 * n_tok + (n_tok // MOE_TT) * N_EXPERTS * (MOE_G - 1) + N_EXPERTS * (MOE_TM - 1)
    return (rows + MOE_TM - 1) // MOE_TM * MOE_TM


def _moe_plan(ng):
    ng = ng[:, :, 0].astype(jnp.int32)
    n_tt = ng.shape[0]
    total = jnp.sum(ng, axis=0)
    region = (total + MOE_TG - 1) // MOE_TG * MOE_TG
    region_end = jnp.cumsum(region)
    region_start = region_end - region
    dst = region_start[None, :] + jnp.cumsum(ng, axis=0) - ng
    local = jnp.cumsum(ng, axis=1) - ng
    n_tiles = _moe_rows(n_tt * MOE_TT) // MOE_TM
    tile_first = jnp.arange(n_tiles, dtype=jnp.int32) * MOE_TG
    tile_expert = jnp.minimum(jnp.sum(region_end[None, :] <= tile_first[:, None], axis=1), N_EXPERTS - 1)
    n_valid = region_end[-1:] // MOE_TG
    i32 = lambda a: a.astype(jnp.int32).reshape(-1)
    return dict(ng=i32(ng), dst=i32(dst), local=i32(local), tile_groups=i32(jnp.sum(ng, axis=1)),
                pad_first=i32(region_start + total),
                pad_count=i32(region - total), tile_expert=i32(tile_expert), n_valid=i32(n_valid),
                n_tiles=n_tiles)


def _run_copies(plan_refs, tile, local_ref, sorted_ref, sem, to_sorted):
    ng_ref, dst_ref, loc_ref = plan_refs
    for e in range(N_EXPERTS):
        k = tile * N_EXPERTS + e
        loc, dst = loc_ref[k], dst_ref[k]

        def body(g, carry):
            lrows = local_ref.at[pl.ds(pl.multiple_of((loc + g) * MOE_G, MOE_G), MOE_G)]
            srows = sorted_ref.at[pl.ds(pl.multiple_of((dst + g) * MOE_G, MOE_G), MOE_G)]
            src, tgt = (lrows, srows) if to_sorted else (srows, lrows)
            pltpu.make_async_copy(src, tgt, sem).start()
            return carry
        lax.fori_loop(0, ng_ref[k], body, 0)


def _group_waits(n_groups, local_ref, sorted_ref, sem):
    def body(g, carry):
        pltpu.make_async_copy(sorted_ref.at[pl.ds(0, MOE_G)], local_ref.at[pl.ds(0, MOE_G)], sem).wait()
        return carry
    lax.fori_loop(0, n_groups, body, 0)


def _moe_dispatch_kernel(ng_ref, dst_ref, loc_ref, tot_ref, padf_ref, padc_ref, nv_ref,
                         h_ref, route_ref, xs_ref, local_ref, zero_ref, sem):
    i = pl.program_id(0)

    @pl.when(i == 0)
    def _():
        zero_ref[...] = jnp.zeros_like(zero_ref)
        n_pad = 0
        for e in range(N_EXPERTS):
            first = padf_ref[e]

            def body(g, carry):
                rows = xs_ref.at[pl.ds(pl.multiple_of((first + g) * MOE_G, MOE_G), MOE_G)]
                pltpu.make_async_copy(zero_ref.at[pl.ds(0, MOE_G)], rows, sem.at[2]).start()
                return carry
            lax.fori_loop(0, padc_ref[e], body, 0)
            n_pad = n_pad + padc_ref[e]

        def tile_copy(j):
            rows = xs_ref.at[pl.ds(pl.multiple_of(j * MOE_TM, MOE_TM), MOE_TM)]
            return pltpu.make_async_copy(zero_ref, rows, sem.at[3])

        n_tiles = xs_ref.shape[0] // MOE_TM
        lax.fori_loop(nv_ref[0], n_tiles, lambda j, c: (tile_copy(j).start(), c)[1], 0)
        _group_waits(n_pad, zero_ref, xs_ref, sem.at[2])
        lax.fori_loop(nv_ref[0], n_tiles, lambda j, c: (tile_copy(j).wait(), c)[1], 0)

    pos = lax.broadcasted_iota(jnp.int32, (MOE_LOCAL, MOE_TT), 0)
    p1 = route_ref[0:1, :].astype(jnp.int32)
    p2 = route_ref[1:2, :].astype(jnp.int32)
    pick = jnp.where((pos == p1) | (pos == p2), 1.0, 0.0).astype(BF16)
    slot = i % 2
    last = pl.num_programs(0) - 1

    @pl.when(i >= 2)
    def _():
        _group_waits(tot_ref[i - 2], local_ref.at[slot], xs_ref, sem.at[slot])

    local_ref[slot] = jnp.dot(pick, h_ref[0], preferred_element_type=F32).astype(BF16)
    _run_copies((ng_ref, dst_ref, loc_ref), i, local_ref.at[slot], xs_ref, sem.at[slot], to_sorted=True)

    @pl.when(i == last)
    def _():
        _group_waits(tot_ref[i], local_ref.at[slot], xs_ref, sem.at[slot])

    @pl.when((i == last) & (last >= 1))
    def _():
        _group_waits(tot_ref[i - 1], local_ref.at[1 - slot], xs_ref, sem.at[1 - slot])


def _moe_dispatch(h, route_t, plan):
    B, S, D = h.shape
    nt = S // MOE_TT
    grid_spec = pltpu.PrefetchScalarGridSpec(
        num_scalar_prefetch=7,
        grid=(B * nt,),
        in_specs=[pl.BlockSpec((1, MOE_TT, D), lambda i, *_: (i // nt, i % nt, 0)),
                  pl.BlockSpec((ROUTE_FIELDS, MOE_TT), lambda i, *_: (0, i))],
        out_specs=pl.BlockSpec(memory_space=pl.ANY),
        scratch_shapes=[pltpu.VMEM((2, MOE_LOCAL, D), BF16), pltpu.VMEM((MOE_TM, D), BF16),
                        pltpu.SemaphoreType.DMA((4,))])
    return pl.pallas_call(
        _moe_dispatch_kernel,
        grid_spec=grid_spec,
        out_shape=jax.ShapeDtypeStruct((plan["n_tiles"] * MOE_TM, D), BF16),
        compiler_params=_cparams(("arbitrary",)),
        name="moe_dispatch",
    )(plan["ng"], plan["dst"], plan["local"], plan["tile_groups"], plan["pad_first"], plan["pad_count"],
      plan["n_valid"], h, route_t)


def _moe_grouped_kernel(te_ref, nv_ref, x_ref, wg_ref, wu_ref, wd_ref, o_ref, wgb_ref, wub_ref, wdb_ref):
    j = pl.program_id(0)
    used = j < nv_ref[0]

    @pl.when(used & ((j == 0) | (te_ref[j] != te_ref[jnp.maximum(j - 1, 0)])))
    def _():
        wgb_ref[...] = wg_ref[0, 0].astype(BF16)
        wub_ref[...] = wu_ref[0, 0].astype(BF16)
        wdb_ref[...] = wd_ref[0, 0].astype(BF16)

    @pl.when(used)
    def _():
        x = x_ref[...]
        a = jnp.dot(x, wgb_ref[...], preferred_element_type=F32)
        u = jnp.dot(x, wub_ref[...], preferred_element_type=F32)
        he = (a * jax.nn.sigmoid(a) * u).astype(BF16)
        o_ref[...] = jnp.dot(he, wdb_ref[...], preferred_element_type=F32).astype(o_ref.dtype)

    @pl.when(jnp.logical_not(used))
    def _():
        o_ref[...] = jnp.zeros_like(o_ref)


def _moe_grouped(xs, plan, wg, wu, wd, layer):
    n_tiles = plan["n_tiles"]
    D = D_MODEL
    tile = lambda j, nv: jnp.minimum(j, nv[0] - 1)
    grid_spec = pltpu.PrefetchScalarGridSpec(
        num_scalar_prefetch=2,
        grid=(n_tiles,),
        in_specs=[pl.BlockSpec((MOE_TM, D), lambda j, te, nv: (tile(j, nv), 0)),
                  pl.BlockSpec((1, 1, D, D_EXPERT), lambda j, te, nv: (layer, te[tile(j, nv)], 0, 0)),
                  pl.BlockSpec((1, 1, D, D_EXPERT), lambda j, te, nv: (layer, te[tile(j, nv)], 0, 0)),
                  pl.BlockSpec((1, 1, D_EXPERT, D), lambda j, te, nv: (layer, te[tile(j, nv)], 0, 0))],
        out_specs=pl.BlockSpec((MOE_TM, D), lambda j, te, nv: (j, 0)),
        scratch_shapes=[pltpu.VMEM((D, D_EXPERT), BF16), pltpu.VMEM((D, D_EXPERT), BF16),
                        pltpu.VMEM((D_EXPERT, D), BF16)])
    return pl.pallas_call(
        _moe_grouped_kernel,
        grid_spec=grid_spec,
        out_shape=jax.ShapeDtypeStruct((n_tiles * MOE_TM, D), BF16),
        compiler_params=_cparams(("arbitrary",)),
        name="moe_grouped",
    )(plan["tile_expert"], plan["n_valid"], xs, wg, wu, wd)


def _moe_combine_kernel(final_norm, ng_ref, dst_ref, loc_ref, tot_ref, ys_ref, route_ref, x_ref, gate_ref,
                        *rest):
    if final_norm:
        fg_ref, o_ref, local_ref, sem = rest
    else:
        o_ref, local_ref, sem = rest
    i = pl.program_id(0)
    slot = i % 2
    plan_refs = (ng_ref, dst_ref, loc_ref)

    @pl.when(i == 0)
    def _():
        local_ref[...] = jnp.zeros_like(local_ref)
        _run_copies(plan_refs, 0, local_ref.at[0], ys_ref, sem.at[0], to_sorted=False)

    @pl.when(i + 1 < pl.num_programs(0))
    def _():
        _run_copies(plan_refs, i + 1, local_ref.at[1 - slot], ys_ref, sem.at[1 - slot], to_sorted=False)

    _group_waits(tot_ref[i], local_ref.at[slot], ys_ref, sem.at[slot])
    pos = lax.broadcasted_iota(jnp.int32, (MOE_TT, MOE_LOCAL), 1)
    route = route_ref[...]
    p1 = route[:, 0:1].astype(jnp.int32)
    p2 = route[:, 1:2].astype(jnp.int32)
    weigh = (jnp.where(pos == p1, route[:, 2:3], 0.0) + jnp.where(pos == p2, route[:, 3:4], 0.0)).astype(BF16)
    y = jnp.dot(weigh, local_ref[slot], preferred_element_type=F32)
    o = x_ref[0] + gate_ref[0] * y
    if final_norm:
        o = o * lax.rsqrt(jnp.mean(o * o, axis=-1, keepdims=True) + EPS) * fg_ref[...]
    o_ref[0] = o


def _moe_combine(ys, route_c, plan, x, gate, final_g=None):
    B, S, D = x.shape
    nt = S // MOE_TT
    in_specs = [pl.BlockSpec(memory_space=pl.ANY),
                pl.BlockSpec((MOE_TT, LANES), lambda i, *_: (i, 0)),
                pl.BlockSpec((1, MOE_TT, D), lambda i, *_: (i // nt, i % nt, 0)),
                pl.BlockSpec((1, 1, D), lambda i, *_: (i // nt, 0, 0))]
    args = [ys, route_c, x, gate.reshape(B, 1, D)]
    if final_g is not None:
        in_specs.append(pl.BlockSpec((1, D), lambda i, *_: (0, 0)))
        args.append(final_g.reshape(1, D))
    grid_spec = pltpu.PrefetchScalarGridSpec(
        num_scalar_prefetch=4,
        grid=(B * nt,),
        in_specs=in_specs,
        out_specs=pl.BlockSpec((1, MOE_TT, D), lambda i, *_: (i // nt, i % nt, 0)),
        scratch_shapes=[pltpu.VMEM((2, MOE_LOCAL, D), BF16), pltpu.SemaphoreType.DMA((2,))])
    return pl.pallas_call(
        functools.partial(_moe_combine_kernel, final_g is not None),
        grid_spec=grid_spec,
        out_shape=jax.ShapeDtypeStruct((B, S, D), F32),
        compiler_params=_cparams(("arbitrary",)),
        name="moe_combine",
    )(plan["ng"], plan["dst"], plan["local"], plan["tile_groups"], *args)


def _dft_tables():
    n1, n2, n = FFT_N1, FFT_N2, FFT_N1 * FFT_N2
    k1 = np.arange(n1)
    f1 = np.exp(-2j * np.pi * np.outer(k1, np.arange(n1)) / n1)
    tw = np.exp(-2j * np.pi * np.outer(np.arange(n2), k1) / n)
    ftw = f1[None, :, :] * tw[:, :, None]
    half = n1 // 2
    fh = ftw[:, :, :half]
    g_fwd = np.concatenate([np.concatenate([fh.real, -fh.imag], axis=2),
                            np.concatenate([fh.imag, fh.real], axis=2)], axis=1)
    back = ftw[:, :, ::-1][:, :, :half].copy()
    back[0] = np.roll(ftw[0], -1, axis=1)[:, ::-1][:, :half]
    back[0][:, 0] = 0.0
    fk = np.concatenate([fh, back], axis=2)
    g_real = np.concatenate([fk.real, fk.imag], axis=1)
    gi = np.conj(np.transpose(fh, (0, 2, 1))) / n
    g_inv = np.concatenate([np.concatenate([gi.real, -gi.imag], axis=2),
                            np.concatenate([gi.imag, gi.real], axis=2)], axis=1)
    f2 = np.exp(-2j * np.pi * np.outer(np.arange(n2), np.arange(n2)) / n2)
    f2_fwd = np.block([[f2.real, -f2.imag], [f2.imag, f2.real]])
    f2c = np.conj(f2)
    f2_inv = np.block([[f2c.real, -f2c.imag], [f2c.imag, f2c.real]])
    as_bf = lambda a: jnp.asarray(a, dtype=F32).astype(BF16)
    return as_bf(g_fwd), as_bf(g_real), as_bf(g_inv), as_bf(f2_fwd), as_bf(f2_inv)


def _fft_fast_stage(stage_ref, k1, f2):
    slab = STAGE_PITCH
    m = jnp.concatenate([stage_ref[pl.ds(k1, FFT_N2, stride=slab), :],
                         stage_ref[pl.ds(FFT_N1 + k1, FFT_N2, stride=slab), :]], axis=0)
    return jnp.dot(f2, m.astype(BF16), preferred_element_type=F32)


def _filter_fft_kernel(hf_ref, hb_ref, inv_ref, g_ref, f2_ref, h_ref, stage_ref):
    slab = STAGE_PITCH
    half = FFT_N1 // 2
    for n2 in range(FFT_N2):
        x = jnp.concatenate([hf_ref[pl.ds(n2, half, stride=FFT_N2), :],
                             hb_ref[pl.ds((FFT_N2 - n2) % FFT_N2, half, stride=FFT_N2), :]], axis=0)
        stage_ref[n2 * slab:n2 * slab + 2 * FFT_N1, :] = jnp.dot(g_ref[n2], x.astype(BF16),
                                                          preferred_element_type=F32)
    f2 = f2_ref[...]
    inv = inv_ref[...]
    for k1 in range(FFT_N1):
        h_ref[0, k1] = (_fft_fast_stage(stage_ref, k1, f2) * inv).astype(h_ref.dtype)


def _filter_fft(taps, inv_norm, g_real, f2_fwd, ct=LANES):
    L, cols = taps.shape
    C = D_CH
    n_ord = cols // (2 * C)
    nc = C // ct
    once = pl.Buffered(1)
    return pl.pallas_call(
        _filter_fft_kernel,
        grid=(n_ord, nc),
        in_specs=[pl.BlockSpec((L, ct), lambda o, c: (0, o * nc + c)),
                  pl.BlockSpec((L, ct), lambda o, c: (0, (n_ord + o) * nc + c)),
                  pl.BlockSpec((1, ct), lambda o, c: (0, o * nc + c)),
                  pl.BlockSpec(g_real.shape, lambda o, c: (0, 0, 0), pipeline_mode=once),
                  pl.BlockSpec(f2_fwd.shape, lambda o, c: (0, 0), pipeline_mode=once)],
        out_specs=pl.BlockSpec((1, FFT_N1, 2 * FFT_N2, ct), lambda o, c: (o, 0, 0, c)),
        out_shape=jax.ShapeDtypeStruct((n_ord, FFT_N1, 2 * FFT_N2, C), BF16),
        scratch_shapes=[pltpu.VMEM((FFT_N2 * STAGE_PITCH, ct), F32)],
        compiler_params=_cparams(("parallel", "parallel")),
        name="filter_fft",
    )(taps, taps, inv_norm, g_real, f2_fwd)


class _RowSets:
    def __init__(self, ref, member):
        self.ref, self.member, self.loaded = ref, member, {}

    def __call__(self, start):
        if start not in self.loaded:
            self.loaded = {k: v for k, v in self.loaded.items() if abs(k - start) <= 2}
            self.loaded[start] = self.ref[self.member, pl.ds(start, FFT_N1 // 2, stride=FFT_N2), :]
        return self.loaded[start]


def _strided_rows(at, n2, taps):
    half = FFT_N1 // 2
    cur = at(n2)
    if taps is None:
        return cur
    w_ref, b_ref = taps
    n1 = lax.broadcasted_iota(jnp.int32, cur.shape, 0)
    if n2 > 0:
        prev = at(n2 - 1)
    else:
        prev = jnp.where(n1 == 0, 0.0, pltpu.roll(at(FFT_N2 - 1), 1, axis=0))
    if n2 < FFT_N2 - 1:
        nxt = at(n2 + 1)
    else:
        nxt = jnp.where(n1 == half - 1, 0.0, pltpu.roll(at(0), half - 1, axis=0))
    return prev * w_ref[0:1, :] + cur * w_ref[1:2, :] + nxt * w_ref[2:3, :] + b_ref[...]


def _hyena_conv_kernel(conv_z, *refs):
    it = iter(refs)
    z_ref, gate_ref = next(it), next(it)
    z_taps = (next(it), next(it)) if conv_z else None
    gate_taps = (next(it), next(it))
    hb_ref, spec_ref, gf_ref, gi_ref, f2f_ref, f2i_ref, o_ref, stage_ref = it
    half = FFT_N1 // 2
    slab = STAGE_PITCH
    z_sets = [_RowSets(z_ref, m) for m in range(2)]
    for n2 in range(FFT_N2):
        x = jnp.concatenate([_strided_rows(z_sets[m], n2, z_taps) for m in range(2)], axis=0).astype(BF16)
        stage_ref[n2 * slab:n2 * slab + 2 * FFT_N1, :] = jnp.dot(gf_ref[n2], x, preferred_element_type=F32)
    f2f = f2f_ref[...]
    f2i = f2i_ref[...]
    for k1 in range(FFT_N1):
        zf = _fft_fast_stage(stage_ref, k1, f2f)
        zr, zi = zf[:FFT_N2], zf[FFT_N2:]
        hr = spec_ref[0, k1, :FFT_N2, :].astype(F32)
        hi = spec_ref[0, k1, FFT_N2:, :].astype(F32)
        p = jnp.concatenate([zr * hr - zi * hi, zr * hi + zi * hr], axis=0).astype(BF16)
        q = jnp.dot(f2i, p, preferred_element_type=F32)
        stage_ref[pl.ds(k1, FFT_N2, stride=slab), :] = q[:FFT_N2]
        stage_ref[pl.ds(FFT_N1 + k1, FFT_N2, stride=slab), :] = q[FFT_N2:]
    hb = hb_ref[...]
    z_sets = [_RowSets(z_ref, m) for m in range(2)]
    gate_sets = [_RowSets(gate_ref, m) for m in range(2)]
    for n2 in range(FFT_N2):
        y_in = stage_ref[n2 * slab:n2 * slab + 2 * FFT_N1, :].astype(BF16)
        y = jnp.dot(gi_ref[n2], y_in, preferred_element_type=F32)
        for m in range(2):
            zm = _strided_rows(z_sets[m], n2, z_taps)
            gm = _strided_rows(gate_sets[m], n2, gate_taps)
            o_ref[m, pl.ds(n2, half, stride=FFT_N2), :] = gm * (y[m * half:(m + 1) * half] + zm * hb)


def _hyena_conv(z, z_blk, gate, gate_blk, short_w, short_b, conv_z, hbias, spec, order, tabs, ct=LANES):
    B, L, _ = z.shape
    C = D_CH
    g_fwd, _, g_inv, f2_fwd, f2_inv = tabs
    once = pl.Buffered(1)
    const3 = lambda a: pl.BlockSpec(a.shape, lambda c, p: (0, 0, 0), pipeline_mode=once)
    const2 = lambda a: pl.BlockSpec(a.shape, lambda c, p: (0, 0), pipeline_mode=once)
    taps_specs = lambda blk: [pl.BlockSpec((3, ct), lambda c, p: (0, blk + c)),
                              pl.BlockSpec((1, ct), lambda c, p: (0, blk + c))]
    in_specs = [pl.BlockSpec((2, L, ct), lambda c, p: (p, 0, z_blk + c)),
                pl.BlockSpec((2, L, ct), lambda c, p: (p, 0, gate_blk + c))]
    args = [z, gate]
    if conv_z:
        in_specs += taps_specs(z_blk)
        args += [short_w, short_b]
    in_specs += taps_specs(gate_blk)
    args += [short_w, short_b]
    in_specs += [pl.BlockSpec((1, ct), lambda c, p: (0, c)),
                 pl.BlockSpec((1, FFT_N1, 2 * FFT_N2, ct), lambda c, p: (order, 0, 0, c), pipeline_mode=once),
                 const3(g_fwd), const3(g_inv), const2(f2_fwd), const2(f2_inv)]
    args += [hbias, spec, g_fwd, g_inv, f2_fwd, f2_inv]
    return pl.pallas_call(
        functools.partial(_hyena_conv_kernel, conv_z),
        grid=(C // ct, B // 2),
        in_specs=in_specs,
        out_specs=pl.BlockSpec((2, L, ct), lambda c, p: (p, 0, c)),
        out_shape=jax.ShapeDtypeStruct((B, L, C), F32),
        scratch_shapes=[pltpu.VMEM((FFT_N2 * STAGE_PITCH, ct), F32)],
        compiler_params=_cparams(("parallel", "arbitrary")),
        name="hyena_conv",
    )(*args)


def _hyena_filters(L, w1, b1, f1, w2, b2, f2, w3, b3):
    t = jnp.arange(L, dtype=F32)
    tn = t / max(L - 1, 1)
    bands = jnp.linspace(1e-4, HY_BANDS - 1, HY_BANDS, dtype=F32)
    ang = 2.0 * math.pi * t[:, None] * bands[None] / L
    feats = jnp.concatenate([tn[:, None], jnp.cos(ang), jnp.sin(ang)], axis=-1)
    h = jnp.sin(f1 * (feats @ w1 + b1))
    h = jnp.sin(f2 * (h @ w2 + b2))
    deltas = jnp.abs(jnp.linspace(HY_MIN_DECAY, HY_MAX_DECAY, D_CH, dtype=F32))
    decay = jnp.exp(-tn[:, None] * deltas[None])
    n_rep = w3.shape[1] // D_CH
    taps = (_mm_f32(h, w3, 512) + b3) * jnp.tile(decay, (1, n_rep))
    l1 = jnp.sum(jnp.abs(taps), axis=0)
    l1 = l1[:n_rep // 2 * D_CH] + l1[n_rep // 2 * D_CH:]
    return taps, (1.0 / (l1 + EPS))[None]


def _dup_heads(w):
    a, b = w[:, :HEAD_DIM], w[:, HEAD_DIM:]
    return jnp.concatenate([a, a, b, b], axis=1)


def _rope_tables(S):
    t = jnp.arange(S)
    row = (t // GRID_W).astype(F32)
    col = (t % GRID_W).astype(F32)
    half = HEAD_DIM // 2
    inv = ROPE_THETA ** (-jnp.arange(0, half, 2, dtype=F32) / half)
    ar = row[:, None] * inv[None]
    ac = col[:, None] * inv[None]
    cos = jnp.concatenate([jnp.cos(ar), jnp.cos(ar), jnp.cos(ac), jnp.cos(ac)], axis=-1)
    sin = jnp.concatenate([-jnp.sin(ar), jnp.sin(ar), -jnp.sin(ac), jnp.sin(ac)], axis=-1)
    return jnp.tile(cos, (1, 2)), jnp.tile(sin, (1, 2))


def _head_mean_matrix(width):
    blk = np.kron(np.eye(width // HEAD_DIM), np.full((HEAD_DIM, HEAD_DIM), 1.0 / HEAD_DIM))
    return jnp.asarray(blk, dtype=F32).astype(BF16)


def kernel(x, c, ctx, c_ctx, w_ada, b_ada, norm_g, final_g, w_in_even, w_out_even, a_sink, b_rpb, w_in_odd, w_out_odd, c_qnorm, c_knorm, hy_short_w, hy_short_b, hy_w1, hy_b1, hy_f1, hy_w2, hy_b2, hy_f2, hy_w3, hy_b3, hy_bias, w_router, b_router, moe_wg, moe_wu, moe_wd):
    B, S, D = x.shape
    depth = w_ada.shape[0]
    rope = _rope_tables(S)
    wr_pad = jnp.pad(w_router.astype(F32), ((0, 0), (0, LANES - N_EXPERTS)))
    wr_hi = wr_pad.astype(BF16)
    router = (b_router.astype(F32), wr_hi, (wr_pad - wr_hi.astype(F32)).astype(BF16))

    mod_in = jnp.concatenate([jax.nn.silu(c), jax.nn.silu(c_ctx)[None],
                              jnp.zeros((8 - B - 1, D), F32)], axis=0)
    xc = ctx
    for l in range(depth):
        need_ctx = l < depth - 1
        mod = _mm_f32(mod_in, w_ada[l], 1536) + b_ada[l]
        mx = mod[:B].reshape(B, 6, D)
        mc = mod[B].reshape(6, D)
        i = l // 2
        if l % 2 == 0:
            w = w_in_even[i].astype(BF16)
            w_all = jnp.concatenate([w[:, :512], _dup_heads(w[:, 512:640]), _dup_heads(w[:, 640:768]),
                                     w[:, 768:]], axis=1)
            segs_x = ((512, "rope", Q_MULT), (256, "rope", 1.0), (256, "plain", 1.0),
                      (512, "plain", Q_MULT), (512, "plain", 1.0), (512, "plain", 1.0))
            sink = a_sink[i].astype(F32) * LOG2_E
            bias = _nbr_bias_table(b_rpb[i].astype(F32) * LOG2_E, S // GRID_W)
            aq, akd, avd, bq, bk, bv = _norm_proj(x, norm_g[l, 0], mx[:, 0], mx[:, 1], w_all, segs_x,
                                                  rope=rope)
            segs_c = tuple((wd, "plain", m) for wd, _, m in segs_x)
            caq, cakd, cavd, cbq, cbk, cbv = _norm_proj(xc, norm_g[l, 0], mc[0:1], mc[1:2], w_all, segs_c)
            ya = _window_attn(aq, akd, avd, cakd, cavd, sink)
            yb = _nbr_attn(bq, bk, bv, cbk, cbv, bias)
            ys = [ya, yb]
            w_out = w_out_even[i].astype(BF16)
            if need_ctx:
                yc = [_ctx_attn(sink, caq, cakd, cavd, cbq, cbk, cbv)]
        else:
            w = w_in_odd[i].astype(BF16)
            w_all = jnp.concatenate([w[:, :512], _dup_heads(w[:, 512:640]), _dup_heads(w[:, 640:768]),
                                     w[:, 768:]], axis=1)
            gains = jnp.concatenate([jnp.tile(c_qnorm[i], 8), jnp.tile(c_knorm[i], 4)])[None].astype(F32)
            norm = (_head_mean_matrix(512), gains)
            segs_x = ((512, "normrope", Q_MULT), (256, "normrope", 1.0), (256, "plain", 1.0),
                      (3 * D_CH, "plain", 1.0, F32))
            qx, kxd, vxd, ux = _norm_proj(x, norm_g[l, 0], mx[:, 0], mx[:, 1], w_all, segs_x,
                                          rope=rope, norm=norm)
            w_c = w_all[:, 512:1024]
            norm_c = (_head_mean_matrix(512), jnp.tile(c_knorm[i], 4)[None].astype(F32))
            kcd, vcd = _norm_proj(xc, norm_g[l, 0], mc[0:1], mc[1:2], w_c,
                                  ((256, "norm", 1.0), (256, "plain", 1.0)), norm=norm_c)
            logit_bound = (1.02 * HEAD_DIM * Q_MULT * jnp.max(jnp.abs(c_qnorm[i]))
                           * jnp.max(jnp.abs(c_knorm[i])))
            y_attn = lax.cond(logit_bound <= FULL_NOMAX_LOG2_BOUND,
                              lambda *a: _full_attn(*a, bounded=True),
                              lambda *a: _full_attn(*a, bounded=False),
                              qx, kxd, vxd, kcd, vcd)
            sw, sb = hy_short_w[i].astype(F32), hy_short_b[i].astype(F32)[None]
            tabs = _dft_tables()
            taps, inv_norm = _hyena_filters(S, hy_w1[i], hy_b1[i], hy_f1[i], hy_w2[i], hy_b2[i],
                                            hy_f2[i], hy_w3[i], hy_b3[i])
            spec = _filter_fft(taps, inv_norm, tabs[1], tabs[3])
            blocks = D_CH // LANES
            z = _hyena_conv(ux, 0, ux, blocks, sw, sb, True, hy_bias[i, 0:1], spec, 0, tabs)
            z = _hyena_conv(z, 0, ux, 2 * blocks, sw, sb, False, hy_bias[i, 1:2], spec, 1, tabs)
            ys = [y_attn, z]
            w_out = w_out_odd[i].astype(BF16)
            if need_ctx:
                raise NotImplementedError("context update of an odd layer is not needed at this depth")

        x, hx, route_t, route_c, ng = _out_proj(ys, w_out, x, mx[:, 2], norm_g[l, 1], mx[:, 3], mx[:, 4],
                                                router, sparse=True)
        if need_ctx:
            xc, hc, comb_c = _out_proj(yc, w_out, xc, mc[2:3], norm_g[l, 1], mc[3:4], mc[4:5], router,
                                       sparse=False)
            flat = lambda a: a.reshape(1, -1, a.shape[-1])
            xc = _moe(flat(hc), flat(comb_c), moe_wg, moe_wu, moe_wd, l, flat(xc), mc[5:6]).reshape(xc.shape)
        plan = _moe_plan(ng)
        sorted_rows = _moe_dispatch(hx, route_t, plan)
        x = _moe_combine(_moe_grouped(sorted_rows, plan, moe_wg, moe_wu, moe_wd, l), route_c, plan, x, mx[:, 5],
                         final_g=None if need_ctx else final_g)
    return x
```

```python
import functools
import math

import numpy as np
import jax
import jax.numpy as jnp
from jax import lax
from jax.experimental import pallas as pl
from jax.experimental.pallas import tpu as pltpu

F32 = jnp.float32
BF16 = jnp.bfloat16

D_MODEL = 1024
GRID_W = 64
CTX_LEN = 256
HEAD_DIM = 64
ROPE_THETA = 10000.0
EPS = 1e-6
ATTN_SCALE = HEAD_DIM ** -0.5
LOG2_E = math.log2(math.e)
Q_MULT = ATTN_SCALE * LOG2_E
A_WINDOW = 128
A_BLOCK = 128
B_WIN_H = 8
B_WIN_W = 16
D_CH = 512
HY_BANDS = 16
HY_MAX_DECAY = math.log(1e-2) / 0.3
HY_MIN_DECAY = math.log(1e-2) / 1.5
N_EXPERTS = 16
N_GROUPS = 4
EXPERTS_PER_GROUP = N_EXPERTS // N_GROUPS
TOP_K = 2
D_EXPERT = 512

LANES = 128
NEG = -1e30
VMEM_LIMIT = 48 * 1024 * 1024

FFT_N1 = 64
FFT_N2 = 128
STAGE_PITCH = 2 * FFT_N1 + 8


def _cparams(sem):
    return pltpu.CompilerParams(dimension_semantics=sem, vmem_limit_bytes=VMEM_LIMIT)


def _mm_f32_kernel(x_ref, w_ref, o_ref):
    o_ref[...] = jnp.dot(x_ref[...], w_ref[...], preferred_element_type=F32)


def _mm_f32(x, w, tn):
    M, K = x.shape
    N = w.shape[1]
    return pl.pallas_call(
        _mm_f32_kernel,
        grid=(N // tn,),
        in_specs=[pl.BlockSpec((M, K), lambda j: (0, 0)),
                  pl.BlockSpec((K, tn), lambda j: (0, j))],
        out_specs=pl.BlockSpec((M, tn), lambda j: (0, j)),
        out_shape=jax.ShapeDtypeStruct((M, N), F32),
        compiler_params=_cparams(("arbitrary",)),
        name="mm_f32",
    )(x, w)


def _swap16(y):
    n = y.shape[-1]
    lane = lax.broadcasted_iota(jnp.int32, y.shape, y.ndim - 1)
    up = pltpu.roll(y, n - 16, axis=y.ndim - 1)
    dn = pltpu.roll(y, 16, axis=y.ndim - 1)
    return jnp.where((lane % 32) < 16, up, dn)


def _tile_lanes(t, width):
    reps = width // t.shape[-1]
    return t if reps == 1 else jnp.concatenate([t] * reps, axis=-1)


def _norm_proj_kernel(segs, has_rope, has_norm, *refs):
    it = iter(refs)
    x_ref, g_ref, shift_ref, scale_ref, w_ref = (next(it) for _ in range(5))
    cos_ref = sin_ref = bd_ref = gain_ref = None
    if has_rope:
        cos_ref, sin_ref = next(it), next(it)
    if has_norm:
        bd_ref, gain_ref = next(it), next(it)
    out_refs = list(it)

    x = x_ref[0]
    ms = jnp.mean(x * x, axis=-1, keepdims=True)
    h = x * lax.rsqrt(ms + EPS) * g_ref[...]
    h = h * (1.0 + scale_ref[0]) + shift_ref[0]
    y = jnp.dot(h.astype(BF16), w_ref[...], preferred_element_type=F32)

    off = 0
    goff = 0
    for (width, kind, mult, *_), o_ref in zip(segs, out_refs):
        ys = y[:, off:off + width]
        if kind in ("norm", "normrope"):
            bd = bd_ref[...][:width, :width]
            hms = jnp.dot((ys * ys).astype(BF16), bd, preferred_element_type=F32)
            ys = ys * lax.rsqrt(hms + EPS) * gain_ref[:, goff:goff + width]
            goff += width
        if kind in ("rope", "normrope"):
            c = _tile_lanes(cos_ref[...], width)
            s = _tile_lanes(sin_ref[...], width)
            ys = ys * c + _swap16(ys) * s
        if mult != 1.0:
            ys = ys * mult
        o_ref[0] = ys.astype(o_ref.dtype)
        off += width


def _norm_proj(x, g, shift, scale, w, segs, rope=None, norm=None, tm=512):
    B, S, D = x.shape
    N = w.shape[1]
    tm = min(tm, S)
    bm = shift.shape[0]
    mod_map = (lambda b, i: (b, 0, 0)) if bm > 1 else (lambda b, i: (0, 0, 0))
    args = [x, g.reshape(1, D), shift.reshape(bm, 1, D), scale.reshape(bm, 1, D), w]
    in_specs = [pl.BlockSpec((1, tm, D), lambda b, i: (b, i, 0)),
                pl.BlockSpec((1, D), lambda b, i: (0, 0)),
                pl.BlockSpec((1, 1, D), mod_map),
                pl.BlockSpec((1, 1, D), mod_map),
                pl.BlockSpec((D, N), lambda b, i: (0, 0))]
    if rope is not None:
        args += [rope[0], rope[1]]
        in_specs += [pl.BlockSpec((tm, LANES), lambda b, i: (i, 0))] * 2
    if norm is not None:
        args += [norm[0], norm[1]]
        in_specs += [pl.BlockSpec(norm[0].shape, lambda b, i: (0, 0)),
                     pl.BlockSpec(norm[1].shape, lambda b, i: (0, 0))]
    out_shape = [jax.ShapeDtypeStruct((B, S, seg[0]), seg[3] if len(seg) > 3 else BF16) for seg in segs]
    out_specs = [pl.BlockSpec((1, tm, seg[0]), lambda b, i: (b, i, 0)) for seg in segs]
    return pl.pallas_call(
        functools.partial(_norm_proj_kernel, segs, rope is not None, norm is not None),
        grid=(B, S // tm),
        in_specs=in_specs,
        out_specs=out_specs,
        out_shape=out_shape,
        compiler_params=_cparams(("parallel", "parallel")),
        name="norm_proj",
    )(*args)


def _half_mask(shape):
    return lax.broadcasted_iota(jnp.int32, shape, len(shape) - 1) < HEAD_DIM


def _stack_halves(qp):
    lo = _half_mask(qp.shape)
    zero = jnp.zeros_like(qp)
    return jnp.concatenate([jnp.where(lo, qp, zero), jnp.where(lo, zero, qp)], axis=0)


def _merge_halves(o, m):
    return jnp.where(_half_mask((m, LANES)), o[:m], o[m:])


def _scores(q, k):
    return lax.dot_general(q, k, (((1,), (1,)), ((), ())), preferred_element_type=F32)


def _joint_softmax_pv(score_parts, value_parts, extra_logit=None):
    m = functools.reduce(jnp.maximum, [jnp.max(s, axis=-1, keepdims=True) for s in score_parts])
    if extra_logit is not None:
        m = jnp.maximum(m, extra_logit)
    den = jnp.exp2(extra_logit - m) if extra_logit is not None else 0.0
    acc = None
    for s, v in zip(score_parts, value_parts):
        p = jnp.exp2(s - m)
        den = den + jnp.sum(p, axis=-1, keepdims=True)
        pv = jnp.dot(p.astype(BF16), v, preferred_element_type=F32)
        acc = pv if acc is None else acc + pv
    return acc / den


def _sink_column(sink_ref, first_head, n_heads, rows_per_head):
    rows = lax.broadcasted_iota(jnp.int32, (n_heads * rows_per_head, 1), 0)
    col = jnp.zeros((n_heads * rows_per_head, 1), F32)
    for j in range(n_heads):
        in_head = (rows >= j * rows_per_head) & (rows < (j + 1) * rows_per_head)
        col = jnp.where(in_head, sink_ref[first_head + j], col)
    return col


def _window_attn_kernel(sink_ref, q_ref, kp_ref, kc_ref, kn_ref, vp_ref, vc_ref, vn_ref,
                        ck_ref, cv_ref, mask_ref, o_ref):
    blk = A_BLOCK
    q = q_ref[0]
    mask = mask_ref[0]
    outs = []
    for g in range(2):
        ls = slice(g * LANES, (g + 1) * LANES)
        k_loc = jnp.concatenate([kp_ref[0][:, ls], kc_ref[0][:, ls], kn_ref[0][:, ls]], axis=0)
        v_loc = jnp.concatenate([vp_ref[0][:, ls], vc_ref[0][:, ls], vn_ref[0][:, ls]], axis=0)
        qs = jnp.concatenate([_stack_halves(q[:, (2 * g + j) * LANES:(2 * g + j + 1) * LANES])
                              for j in range(2)], axis=0)
        s_loc = _scores(qs, k_loc) + mask
        s_ctx = _scores(qs, ck_ref[0][:, ls])
        sink = _sink_column(sink_ref, 4 * g, 4, blk)
        o = _joint_softmax_pv([s_loc, s_ctx], [v_loc, cv_ref[0][:, ls]], sink)
        outs += [_merge_halves(o[:2 * blk], blk), _merge_halves(o[2 * blk:], blk)]
    o_ref[0] = jnp.concatenate(outs, axis=-1).astype(o_ref.dtype)


def _window_mask_table(seq_len):
    blk = A_BLOCK
    rows = np.arange(4 * blk)[:, None] % blk
    rel = np.arange(3 * blk)[None, :] - blk
    near = np.abs(rows - rel) <= A_WINDOW
    tabs = [near & (rel >= 0), near, near & (rel < blk)]
    return jnp.asarray(np.where(np.stack(tabs), 0.0, NEG).astype(np.float32))


def _window_attn(q, kd, vd, ckd, cvd, sink):
    B, S, _ = q.shape
    nb = S // A_BLOCK
    mask = _window_mask_table(S)
    pat = lambda i: jnp.where(i == 0, 0, jnp.where(i == nb - 1, 2, 1))
    kv_spec = lambda f: pl.BlockSpec((1, A_BLOCK, 2 * LANES), f)
    prev_map = lambda b, i: (b, jnp.maximum(i - 1, 0), 0)
    cur_map = lambda b, i: (b, i, 0)
    next_map = lambda b, i: (b, jnp.minimum(i + 1, nb - 1), 0)
    ctx_spec = pl.BlockSpec((1, CTX_LEN, 2 * LANES), lambda b, i: (b, 0, 0))
    return pl.pallas_call(
        _window_attn_kernel,
        grid=(B, nb),
        in_specs=[pl.BlockSpec(memory_space=pltpu.SMEM),
                  pl.BlockSpec((1, A_BLOCK, 4 * LANES), cur_map),
                  kv_spec(prev_map), kv_spec(cur_map), kv_spec(next_map),
                  kv_spec(prev_map), kv_spec(cur_map), kv_spec(next_map),
                  ctx_spec, ctx_spec,
                  pl.BlockSpec((1,) + mask.shape[1:], lambda b, i: (pat(i), 0, 0))],
        out_specs=pl.BlockSpec((1, A_BLOCK, 4 * LANES), cur_map),
        out_shape=jax.ShapeDtypeStruct((B, S, 4 * LANES), BF16),
        compiler_params=_cparams(("parallel", "parallel")),
        name="window_attn",
    )(sink, q, kd, kd, kd, vd, vd, vd, ckd, cvd, mask)


NBR_ROWS = 4
NBR_KROWS = 12
NBR_PAIRS = 4


def _nbr_start_row(i, n_rows):
    return jnp.clip(i * NBR_ROWS - B_WIN_H // 2, 0, n_rows - NBR_KROWS)


def _nbr_attn_kernel(n_rows, q_ref, k_ref, v_ref, ck_ref, cv_ref, bias_ref, o_ref):
    i = pl.program_id(2)
    nq = NBR_ROWS * GRID_W
    nk = NBR_KROWS * GRID_W
    start = pl.multiple_of(_nbr_start_row(i, n_rows) * GRID_W, GRID_W)
    outs = []
    for pp in range(NBR_PAIRS):
        ls = slice(pp * LANES, (pp + 1) * LANES)
        k_loc = k_ref[0, pl.ds(start, nk), ls]
        v_loc = v_ref[0, pl.ds(start, nk), ls]
        qs = _stack_halves(q_ref[0][:, ls])
        s_loc = _scores(qs, k_loc) + bias_ref[0, 2 * pp:2 * pp + 2].reshape(2 * nq, nk).astype(F32)
        s_ctx = _scores(qs, ck_ref[0][:, ls])
        o = _joint_softmax_pv([s_loc, s_ctx], [v_loc, cv_ref[0][:, ls]])
        outs.append(_merge_halves(o, nq))
    o_ref[0] = jnp.concatenate(outs, axis=-1).astype(o_ref.dtype)


def _nbr_bias_table(rpb, n_rows):
    kh = B_WIN_H
    n_heads = rpb.shape[0]
    col = np.arange(GRID_W)
    cs = np.clip(col - B_WIN_W // 2, 0, GRID_W - B_WIN_W)
    col_ok = (col[None, :] >= cs[:, None]) & (col[None, :] < cs[:, None] + B_WIN_W)
    dc = np.clip(col[None, :] - col[:, None], -(B_WIN_W - 1), B_WIN_W - 1) + B_WIN_W - 1
    pick_col = (dc[..., None] == np.arange(2 * B_WIN_W - 1)).astype(np.float32)
    r0 = np.array([0, NBR_ROWS, n_rows - NBR_ROWS])
    start = np.clip(r0 - kh // 2, 0, n_rows - NBR_KROWS)
    r = r0[:, None] + np.arange(NBR_ROWS)[None]
    rs = np.clip(r - kh // 2, 0, n_rows - kh)
    kr = start[:, None] + np.arange(NBR_KROWS)[None]
    row_ok = (kr[:, None, :] >= rs[:, :, None]) & (kr[:, None, :] < rs[:, :, None] + kh)
    dr = np.clip(kr[:, None, :] - r[:, :, None] + kh - 1, 0, 2 * kh - 2)
    pick_row = (dr[..., None] == np.arange(2 * kh - 1)).astype(np.float32)
    table = jnp.einsum("prkd,hdc,qjc->phrqkj", pick_row, rpb.astype(F32), pick_col,
                       precision=lax.Precision.HIGHEST)
    ok = row_ok[:, None, :, None, :, None] & col_ok[None, None, None, :, None, :]
    return jnp.where(ok, table, NEG).astype(BF16).reshape(3, n_heads, NBR_ROWS * GRID_W, NBR_KROWS * GRID_W)


def _nbr_attn(q, k, v, ck, cv, bias):
    B, S, _ = q.shape
    n_rows = S // GRID_W
    nsteps = n_rows // NBR_ROWS
    nq = NBR_ROWS * GRID_W
    nk = NBR_KROWS * GRID_W
    pat = lambda i: jnp.where(i == 0, 0, jnp.where(i == nsteps - 1, 2, 1))
    wl = NBR_PAIRS * LANES
    return pl.pallas_call(
        functools.partial(_nbr_attn_kernel, n_rows),
        grid=(B, 4 // NBR_PAIRS, nsteps),
        in_specs=[pl.BlockSpec((1, nq, wl), lambda b, p, i: (b, i, p)),
                  pl.BlockSpec((1, S, wl), lambda b, p, i: (b, 0, p)),
                  pl.BlockSpec((1, S, wl), lambda b, p, i: (b, 0, p)),
                  pl.BlockSpec((1, CTX_LEN, wl), lambda b, p, i: (b, 0, p)),
                  pl.BlockSpec((1, CTX_LEN, wl), lambda b, p, i: (b, 0, p)),
                  pl.BlockSpec((1, 2 * NBR_PAIRS, nq, nk), lambda b, p, i: (pat(i), p, 0, 0))],
        out_specs=pl.BlockSpec((1, nq, wl), lambda b, p, i: (b, i, p)),
        out_shape=jax.ShapeDtypeStruct((B, S, 4 * LANES), BF16),
        compiler_params=_cparams(("parallel", "parallel", "arbitrary")),
        name="nbr_attn",
    )(q, k, v, ck, cv, bias)


FULL_TQ = 512
FULL_TK = 512
FULL_NOMAX_LOG2_BOUND = 60.0


def _full_attn_kernel(bounded, q_ref, k_ref, v_ref, ck_ref, cv_ref, o_ref):
    tq = FULL_TQ
    q = q_ref[0]
    groups = range(2)
    lanes = [slice(g * LANES, (g + 1) * LANES) for g in groups]
    qs = [jnp.concatenate([_stack_halves(q[:, (2 * g + j) * LANES:(2 * g + j + 1) * LANES]) for j in range(2)],
                          axis=0) for g in groups]

    def step_bounded(g, acc, k, v):
        v_ones = jnp.where(_half_mask(v.shape), v, jnp.ones_like(v))
        p = jnp.exp2(_scores(qs[g], k))
        return acc + jnp.dot(p.astype(BF16), v_ones, preferred_element_type=F32)

    def step_online(g, carry, k, v):
        m, l, acc = carry
        s = _scores(qs[g], k)
        m_new = jnp.maximum(m, jnp.max(s, axis=-1, keepdims=True))
        alpha = jnp.exp2(m - m_new)
        p = jnp.exp2(s - m_new)
        l = l * alpha + jnp.sum(p, axis=-1, keepdims=True)
        acc = acc * alpha + jnp.dot(p.astype(BF16), v, preferred_element_type=F32)
        return m_new, l, acc

    step = step_bounded if bounded else step_online
    init = jnp.zeros((4 * tq, LANES), F32)
    if not bounded:
        init = (jnp.full((4 * tq, 1), NEG, F32), jnp.zeros((4 * tq, 1), F32), init)
    carry = tuple(step(g, init, ck_ref[0][:, lanes[g]], cv_ref[0][:, lanes[g]]) for g in groups)

    def body(j, carry):
        rows = pl.ds(pl.multiple_of(j * FULL_TK, FULL_TK), FULL_TK)
        return tuple(step(g, carry[g], k_ref[0, rows, lanes[g]], v_ref[0, rows, lanes[g]]) for g in groups)

    carry = lax.fori_loop(0, k_ref.shape[1] // FULL_TK, body, carry)
    pairs = []
    for g in groups:
        if bounded:
            o = carry[g] * pltpu.roll(1.0 / carry[g], HEAD_DIM, axis=1)
            merge = lambda a, b: jnp.where(_half_mask((tq, LANES)), a, pltpu.roll(b, HEAD_DIM, axis=1))
            pairs += [merge(o[0:tq], o[tq:2 * tq]), merge(o[2 * tq:3 * tq], o[3 * tq:])]
        else:
            o = carry[g][2] / carry[g][1]
            pairs += [_merge_halves(o[:2 * tq], tq), _merge_halves(o[2 * tq:], tq)]
    o_ref[0] = jnp.concatenate(pairs, axis=-1).astype(o_ref.dtype)


def _full_attn(q, kd, vd, ckd, cvd, bounded):
    B, S, W = q.shape
    return pl.pallas_call(
        functools.partial(_full_attn_kernel, bounded),
        grid=(B, S // FULL_TQ),
        in_specs=[pl.BlockSpec((1, FULL_TQ, W), lambda b, i: (b, i, 0)),
                  pl.BlockSpec((1, S, 2 * LANES), lambda b, i: (b, 0, 0)),
                  pl.BlockSpec((1, S, 2 * LANES), lambda b, i: (b, 0, 0)),
                  pl.BlockSpec((1, CTX_LEN, 2 * LANES), lambda b, i: (b, 0, 0)),
                  pl.BlockSpec((1, CTX_LEN, 2 * LANES), lambda b, i: (b, 0, 0))],
        out_specs=pl.BlockSpec((1, FULL_TQ, W), lambda b, i: (b, i, 0)),
        out_shape=jax.ShapeDtypeStruct((B, S, W), BF16),
        compiler_params=_cparams(("parallel", "arbitrary")),
        name="full_attn_bounded" if bounded else "full_attn_online",
    )(q, kd, vd, ckd, cvd)


def _ctx_attn_kernel(sink_ref, aq_ref, akd_ref, avd_ref, bq_ref, bk_ref, bv_ref, o_ref):
    n = CTX_LEN
    aq = aq_ref[0]
    bq = bq_ref[0]
    outs = []
    for g in range(2):
        ls = slice(g * LANES, (g + 1) * LANES)
        qs = jnp.concatenate([_stack_halves(aq[:, (2 * g + j) * LANES:(2 * g + j + 1) * LANES])
                              for j in range(2)], axis=0)
        sink = _sink_column(sink_ref, 4 * g, 4, n)
        o = _joint_softmax_pv([_scores(qs, akd_ref[0][:, ls])], [avd_ref[0][:, ls]], sink)
        outs += [_merge_halves(o[:2 * n], n), _merge_halves(o[2 * n:], n)]
    for p in range(4):
        ls = slice(p * LANES, (p + 1) * LANES)
        qs = _stack_halves(bq[:, ls])
        o = _joint_softmax_pv([_scores(qs, bk_ref[0][:, ls])], [bv_ref[0][:, ls]])
        outs.append(_merge_halves(o, n))
    o_ref[0] = jnp.concatenate(outs, axis=-1).astype(o_ref.dtype)


def _ctx_attn(sink, aq, akd, avd, bq, bk, bv):
    B = aq.shape[0]
    spec = lambda a: pl.BlockSpec((1,) + a.shape[1:], lambda b: (b, 0, 0))
    args = (aq, akd, avd, bq, bk, bv)
    return pl.pallas_call(
        _ctx_attn_kernel,
        grid=(B,),
        in_specs=[pl.BlockSpec(memory_space=pltpu.SMEM)] + [spec(a) for a in args],
        out_specs=pl.BlockSpec((1, CTX_LEN, 8 * LANES), lambda b: (b, 0, 0)),
        out_shape=jax.ShapeDtypeStruct((B, CTX_LEN, 8 * LANES), BF16),
        compiler_params=_cparams(("parallel",)),
        name="ctx_attn",
    )(sink, *args)


def _pick4(idx, vals):
    return jnp.where(idx == 0, vals[0], jnp.where(idx == 1, vals[1], jnp.where(idx == 2, vals[2], vals[3])))


def _route_rows(lg_t, b_ref):
    n_tok = lg_t.shape[1]
    s = [jax.nn.sigmoid(lg_t[e:e + 1, :]) for e in range(N_EXPERTS)]
    sel = [s[e] + b_ref[e] for e in range(N_EXPERTS)]
    n = EXPERTS_PER_GROUP
    gscore = []
    for j in range(N_GROUPS):
        v = sel[n * j:n * (j + 1)]
        pair_sums = [v[a] + v[b] for a in range(n) for b in range(a + 1, n)]
        gscore.append(functools.reduce(jnp.maximum, pair_sums))
    best, gbest = gscore[0], jnp.zeros((1, n_tok), jnp.int32)
    for j in range(1, N_GROUPS):
        upd = gscore[j] > best
        best = jnp.where(upd, gscore[j], best)
        gbest = jnp.where(upd, j, gbest)
    v = [_pick4(gbest, [sel[n * j + i] for j in range(N_GROUPS)]) for i in range(n)]
    u = [_pick4(gbest, [s[n * j + i] for j in range(N_GROUPS)]) for i in range(n)]
    m1, i1 = v[0], jnp.zeros((1, n_tok), jnp.int32)
    for i in range(1, n):
        upd = v[i] > m1
        m1 = jnp.where(upd, v[i], m1)
        i1 = jnp.where(upd, i, i1)
    m2, i2 = jnp.full((1, n_tok), -jnp.inf, F32), jnp.zeros((1, n_tok), jnp.int32)
    for i in range(n):
        upd = (i1 != i) & (v[i] > m2)
        m2 = jnp.where(upd, v[i], m2)
        i2 = jnp.where(upd, i, i2)
    u1, u2 = _pick4(i1, u), _pick4(i2, u)
    tot = u1 + u2
    return n * gbest + i1, n * gbest + i2, u1 / tot, u2 / tot


def _out_proj_kernel(n_y, sparse, *refs):
    y_refs = refs[:n_y]
    br_ref, w_ref, x_ref, gate_ref, g_ref, shift_ref, scale_ref, wrh_ref, wrl_ref = refs[n_y:n_y + 9]
    n_in = n_y + 9
    if sparse:
        tri_ref, tril_ref = refs[n_in:n_in + 2]
        n_in += 2
    outs = refs[n_in:]
    xo_ref = outs[0]
    off = 0
    acc = None
    for y_ref in y_refs:
        wdt = y_ref.shape[-1]
        part = jnp.dot(y_ref[0].astype(BF16), w_ref[off:off + wdt, :], preferred_element_type=F32)
        acc = part if acc is None else acc + part
        off += wdt
    x = x_ref[0] + gate_ref[0] * acc
    xo_ref[0] = x
    ms = jnp.mean(x * x, axis=-1, keepdims=True)
    h = x * lax.rsqrt(ms + EPS) * g_ref[...]
    h = h * (1.0 + scale_ref[0]) + shift_ref[0]
    hh = h.astype(BF16)
    hl = (h - hh.astype(F32)).astype(BF16)
    lg = (jnp.dot(hh, wrh_ref[...], preferred_element_type=F32)
          + jnp.dot(hl, wrh_ref[...], preferred_element_type=F32)
          + jnp.dot(hh, wrl_ref[...], preferred_element_type=F32))
    lg_t = lg.T[:N_EXPERTS]
    e1, e2, w1, w2 = _route_rows(lg_t, br_ref)
    rows = lax.broadcasted_iota(jnp.int32, lg_t.shape, 0)
    if not sparse:
        h_ref, comb_ref = outs[1:]
        h_ref[0] = hh
        comb_t = jnp.where(rows == e1, w1, 0.0) + jnp.where(rows == e2, w2, 0.0)
        comb_ref[0] = jnp.concatenate(
            [comb_t, jnp.zeros((LANES - N_EXPERTS, comb_t.shape[1]), F32)], axis=0).T
        return

    h_ref, route_t_ref, route_c_ref, ng_ref = outs[1:]
    h_ref[0] = hh
    member = jnp.where((rows == e1) | (rows == e2), 1.0, 0.0)
    before = jnp.dot(member.astype(BF16), tri_ref[...], preferred_element_type=F32)
    groups = jnp.floor((jnp.sum(member, axis=1, keepdims=True) + (MOE_G - 1)) * (1.0 / MOE_G))
    groups = jnp.broadcast_to(groups, (N_EXPERTS, LANES))
    run_start = MOE_G * jnp.dot(tril_ref[...], groups.astype(BF16), preferred_element_type=F32)[:, 0:1]
    pos = run_start + before
    p1 = jnp.sum(jnp.where(rows == e1, pos, 0.0), axis=0, keepdims=True)
    p2 = jnp.sum(jnp.where(rows == e2, pos, 0.0), axis=0, keepdims=True)
    ng_ref[0] = groups
    field = lax.broadcasted_iota(jnp.int32, (LANES, h.shape[0]), 0)
    route = jnp.zeros((LANES, h.shape[0]), F32)
    for k, v in enumerate((p1, p2, w1, w2)):
        route = jnp.where(field == k, v, route)
    route_t_ref[...] = route[:ROUTE_FIELDS]
    route_c_ref[...] = route.T


ROUTE_FIELDS = 8
MOE_G = 16
MOE_TT = 512
MOE_LOCAL = 1280


def _out_proj(ys, w, x, gate, g, shift, scale, router, sparse, tm=512):
    B, S, D = x.shape
    tm = min(tm, S)
    bm = gate.shape[0]
    b_router, wr_hi, wr_lo = router
    nt = S // tm
    mod_map = (lambda b, i: (b, 0, 0)) if bm > 1 else (lambda b, i: (0, 0, 0))
    mod_spec = pl.BlockSpec((1, 1, D), mod_map)
    row_map = lambda b, i: (b, i, 0)
    in_specs = ([pl.BlockSpec((1, tm, y.shape[-1]), row_map) for y in ys]
                + [pl.BlockSpec(memory_space=pltpu.SMEM),
                   pl.BlockSpec(w.shape, lambda b, i: (0, 0)),
                   pl.BlockSpec((1, tm, D), row_map), mod_spec,
                   pl.BlockSpec((1, D), lambda b, i: (0, 0)), mod_spec, mod_spec,
                   pl.BlockSpec(wr_hi.shape, lambda b, i: (0, 0)),
                   pl.BlockSpec(wr_lo.shape, lambda b, i: (0, 0))])
    args = list(ys) + [b_router, w, x, gate.reshape(bm, 1, D), g.reshape(1, D), shift.reshape(bm, 1, D),
                       scale.reshape(bm, 1, D), wr_hi, wr_lo]
    out_specs = [pl.BlockSpec((1, tm, D), row_map)]
    out_shape = [jax.ShapeDtypeStruct((B, S, D), F32)]
    out_specs.append(pl.BlockSpec((1, tm, D), row_map))
    out_shape.append(jax.ShapeDtypeStruct((B, S, D), BF16))
    if sparse:
        assert tm == MOE_TT
        tri = jnp.asarray(np.triu(np.ones((tm, tm), np.float32), 1)).astype(BF16)
        tril = jnp.asarray(np.tril(np.ones((N_EXPERTS, N_EXPERTS), np.float32), -1)).astype(BF16)
        args += [tri, tril]
        in_specs += [pl.BlockSpec(tri.shape, lambda b, i: (0, 0)),
                     pl.BlockSpec(tril.shape, lambda b, i: (0, 0))]
        out_specs += [pl.BlockSpec((ROUTE_FIELDS, tm), lambda b, i: (0, b * nt + i)),
                      pl.BlockSpec((tm, LANES), lambda b, i: (b * nt + i, 0)),
                      pl.BlockSpec((1, N_EXPERTS, LANES), lambda b, i: (b * nt + i, 0, 0))]
        out_shape += [jax.ShapeDtypeStruct((ROUTE_FIELDS, B * S), F32),
                      jax.ShapeDtypeStruct((B * S, LANES), F32),
                      jax.ShapeDtypeStruct((B * nt, N_EXPERTS, LANES), F32)]
    else:
        out_specs.append(pl.BlockSpec((1, tm, LANES), row_map))
        out_shape.append(jax.ShapeDtypeStruct((B, S, LANES), F32))
    return pl.pallas_call(
        functools.partial(_out_proj_kernel, len(ys), sparse),
        grid=(B, nt),
        in_specs=in_specs,
        out_specs=out_specs,
        out_shape=out_shape,
        compiler_params=_cparams(("parallel", "parallel")),
        name="out_proj_sparse" if sparse else "out_proj",
    )(*args)


def _moe_kernel(h_ref, comb_ref, wg_ref, wu_ref, wd_ref, x_ref, gate_ref, o_ref, acc_ref):
    e = pl.program_id(2)

    @pl.when(e == 0)
    def _():
        acc_ref[...] = jnp.zeros_like(acc_ref)

    h = h_ref[0]
    a = jnp.dot(h, wg_ref[0, 0].astype(BF16), preferred_element_type=F32)
    u = jnp.dot(h, wu_ref[0, 0].astype(BF16), preferred_element_type=F32)
    he = (a * jax.nn.sigmoid(a) * u).astype(BF16)
    y = jnp.dot(he, wd_ref[0, 0].astype(BF16), preferred_element_type=F32)
    lane = lax.broadcasted_iota(jnp.int32, comb_ref.shape[1:], 1)
    c = jnp.sum(jnp.where(lane == e, comb_ref[0], 0.0), axis=-1, keepdims=True)
    acc_ref[...] += c * y

    @pl.when(e == pl.num_programs(2) - 1)
    def _():
        o_ref[0] = x_ref[0] + gate_ref[0] * acc_ref[...]


def _moe(h, comb, wg, wu, wd, layer, x, gate, tm=1024):
    B, S, D = x.shape
    tm = min(tm, S)
    bm = gate.shape[0]
    mod_map = (lambda b, i, e: (b, 0, 0)) if bm > 1 else (lambda b, i, e: (0, 0, 0))
    row_map = lambda b, i, e: (b, i, 0)
    return pl.pallas_call(
        _moe_kernel,
        grid=(B, S // tm, N_EXPERTS),
        in_specs=[pl.BlockSpec((1, tm, D), row_map),
                  pl.BlockSpec((1, tm, LANES), row_map),
                  pl.BlockSpec((1, 1, D, D_EXPERT), lambda b, i, e: (layer, e, 0, 0)),
                  pl.BlockSpec((1, 1, D, D_EXPERT), lambda b, i, e: (layer, e, 0, 0)),
                  pl.BlockSpec((1, 1, D_EXPERT, D), lambda b, i, e: (layer, e, 0, 0)),
                  pl.BlockSpec((1, tm, D), row_map),
                  pl.BlockSpec((1, 1, D), mod_map)],
        out_specs=pl.BlockSpec((1, tm, D), row_map),
        out_shape=jax.ShapeDtypeStruct((B, S, D), F32),
        scratch_shapes=[pltpu.VMEM((tm, D), F32)],
        compiler_params=_cparams(("parallel", "parallel", "arbitrary")),
        name="moe",
    )(h, comb, wg, wu, wd, x, gate.reshape(bm, 1, D))


MOE_TM = 1024
MOE_TG = MOE_TM // MOE_G


def _moe_rows(n_tok):
    rows = 2 * n_tok + (n_tok // MOE_TT) * N_EXPERTS * (MOE_G - 1) + N_EXPERTS * (MOE_TM - 1)
    return (rows + MOE_TM - 1) // MOE_TM * MOE_TM


def _moe_plan(ng):
    ng = ng[:, :, 0].astype(jnp.int32)
    n_tt = ng.shape[0]
    total = jnp.sum(ng, axis=0)
    region = (total + MOE_TG - 1) // MOE_TG * MOE_TG
    region_end = jnp.cumsum(region)
    region_start = region_end - region
    dst = region_start[None, :] + jnp.cumsum(ng, axis=0) - ng
    local = jnp.cumsum(ng, axis=1) - ng
    n_tiles = _moe_rows(n_tt * MOE_TT) // MOE_TM
    tile_first = jnp.arange(n_tiles, dtype=jnp.int32) * MOE_TG
    tile_expert = jnp.minimum(jnp.sum(region_end[None, :] <= tile_first[:, None], axis=1), N_EXPERTS - 1)
    n_valid = region_end[-1:] // MOE_TG
    i32 = lambda a: a.astype(jnp.int32).reshape(-1)
    return dict(ng=i32(ng), dst=i32(dst), local=i32(local), tile_groups=i32(jnp.sum(ng, axis=1)),
                pad_first=i32(region_start + total),
                pad_count=i32(region - total), tile_expert=i32(tile_expert), n_valid=i32(n_valid),
                n_tiles=n_tiles)


def _run_copies(plan_refs, tile, local_ref, sorted_ref, sem, to_sorted):
    ng_ref, dst_ref, loc_ref = plan_refs
    for e in range(N_EXPERTS):
        k = tile * N_EXPERTS + e
        loc, dst = loc_ref[k], dst_ref[k]

        def body(g, carry):
            lrows = local_ref.at[pl.ds(pl.multiple_of((loc + g) * MOE_G, MOE_G), MOE_G)]
            srows = sorted_ref.at[pl.ds(pl.multiple_of((dst + g) * MOE_G, MOE_G), MOE_G)]
            src, tgt = (lrows, srows) if to_sorted else (srows, lrows)
            pltpu.make_async_copy(src, tgt, sem).start()
            return carry
        lax.fori_loop(0, ng_ref[k], body, 0)


def _group_waits(n_groups, local_ref, sorted_ref, sem):
    def body(g, carry):
        pltpu.make_async_copy(sorted_ref.at[pl.ds(0, MOE_G)], local_ref.at[pl.ds(0, MOE_G)], sem).wait()
        return carry
    lax.fori_loop(0, n_groups, body, 0)


def _moe_dispatch_kernel(ng_ref, dst_ref, loc_ref, tot_ref, padf_ref, padc_ref, nv_ref,
                         h_ref, route_ref, xs_ref, local_ref, zero_ref, sem):
    i = pl.program_id(0)

    @pl.when(i == 0)
    def _():
        zero_ref[...] = jnp.zeros_like(zero_ref)
        n_pad = 0
        for e in range(N_EXPERTS):
            first = padf_ref[e]

            def body(g, carry):
                rows = xs_ref.at[pl.ds(pl.multiple_of((first + g) * MOE_G, MOE_G), MOE_G)]
                pltpu.make_async_copy(zero_ref.at[pl.ds(0, MOE_G)], rows, sem.at[2]).start()
                return carry
            lax.fori_loop(0, padc_ref[e], body, 0)
            n_pad = n_pad + padc_ref[e]

        def tile_copy(j):
            rows = xs_ref.at[pl.ds(pl.multiple_of(j * MOE_TM, MOE_TM), MOE_TM)]
            return pltpu.make_async_copy(zero_ref, rows, sem.at[3])

        n_tiles = xs_ref.shape[0] // MOE_TM
        lax.fori_loop(nv_ref[0], n_tiles, lambda j, c: (tile_copy(j).start(), c)[1], 0)
        _group_waits(n_pad, zero_ref, xs_ref, sem.at[2])
        lax.fori_loop(nv_ref[0], n_tiles, lambda j, c: (tile_copy(j).wait(), c)[1], 0)

    pos = lax.broadcasted_iota(jnp.int32, (MOE_LOCAL, MOE_TT), 0)
    p1 = route_ref[0:1, :].astype(jnp.int32)
    p2 = route_ref[1:2, :].astype(jnp.int32)
    pick = jnp.where((pos == p1) | (pos == p2), 1.0, 0.0).astype(BF16)
    slot = i % 2
    last = pl.num_programs(0) - 1

    @pl.when(i >= 2)
    def _():
        _group_waits(tot_ref[i - 2], local_ref.at[slot], xs_ref, sem.at[slot])

    local_ref[slot] = jnp.dot(pick, h_ref[0], preferred_element_type=F32).astype(BF16)
    _run_copies((ng_ref, dst_ref, loc_ref), i, local_ref.at[slot], xs_ref, sem.at[slot], to_sorted=True)

    @pl.when(i == last)
    def _():
        _group_waits(tot_ref[i], local_ref.at[slot], xs_ref, sem.at[slot])

    @pl.when((i == last) & (last >= 1))
    def _():
        _group_waits(tot_ref[i - 1], local_ref.at[1 - slot], xs_ref, sem.at[1 - slot])


def _moe_dispatch(h, route_t, plan):
    B, S, D = h.shape
    nt = S // MOE_TT
    grid_spec = pltpu.PrefetchScalarGridSpec(
        num_scalar_prefetch=7,
        grid=(B * nt,),
        in_specs=[pl.BlockSpec((1, MOE_TT, D), lambda i, *_: (i // nt, i % nt, 0)),
                  pl.BlockSpec((ROUTE_FIELDS, MOE_TT), lambda i, *_: (0, i))],
        out_specs=pl.BlockSpec(memory_space=pl.ANY),
        scratch_shapes=[pltpu.VMEM((2, MOE_LOCAL, D), BF16), pltpu.VMEM((MOE_TM, D), BF16),
                        pltpu.SemaphoreType.DMA((4,))])
    return pl.pallas_call(
        _moe_dispatch_kernel,
        grid_spec=grid_spec,
        out_shape=jax.ShapeDtypeStruct((plan["n_tiles"] * MOE_TM, D), BF16),
        compiler_params=_cparams(("arbitrary",)),
        name="moe_dispatch",
    )(plan["ng"], plan["dst"], plan["local"], plan["tile_groups"], plan["pad_first"], plan["pad_count"],
      plan["n_valid"], h, route_t)


def _moe_grouped_kernel(te_ref, nv_ref, x_ref, wg_ref, wu_ref, wd_ref, o_ref, wgb_ref, wub_ref, wdb_ref):
    j = pl.program_id(0)
    used = j < nv_ref[0]

    @pl.when(used & ((j == 0) | (te_ref[j] != te_ref[jnp.maximum(j - 1, 0)])))
    def _():
        wgb_ref[...] = wg_ref[0, 0].astype(BF16)
        wub_ref[...] = wu_ref[0, 0].astype(BF16)
        wdb_ref[...] = wd_ref[0, 0].astype(BF16)

    @pl.when(used)
    def _():
        x = x_ref[...]
        a = jnp.dot(x, wgb_ref[...], preferred_element_type=F32)
        u = jnp.dot(x, wub_ref[...], preferred_element_type=F32)
        he = (a * jax.nn.sigmoid(a) * u).astype(BF16)
        o_ref[...] = jnp.dot(he, wdb_ref[...], preferred_element_type=F32).astype(o_ref.dtype)

    @pl.when(jnp.logical_not(used))
    def _():
        o_ref[...] = jnp.zeros_like(o_ref)


def _moe_grouped(xs, plan, wg, wu, wd, layer):
    n_tiles = plan["n_tiles"]
    D = D_MODEL
    tile = lambda j, nv: jnp.minimum(j, nv[0] - 1)
    grid_spec = pltpu.PrefetchScalarGridSpec(
        num_scalar_prefetch=2,
        grid=(n_tiles,),
        in_specs=[pl.BlockSpec((MOE_TM, D), lambda j, te, nv: (tile(j, nv), 0)),
                  pl.BlockSpec((1, 1, D, D_EXPERT), lambda j, te, nv: (layer, te[tile(j, nv)], 0, 0)),
                  pl.BlockSpec((1, 1, D, D_EXPERT), lambda j, te, nv: (layer, te[tile(j, nv)], 0, 0)),
                  pl.BlockSpec((1, 1, D_EXPERT, D), lambda j, te, nv: (layer, te[tile(j, nv)], 0, 0))],
        out_specs=pl.BlockSpec((MOE_TM, D), lambda j, te, nv: (j, 0)),
        scratch_shapes=[pltpu.VMEM((D, D_EXPERT), BF16), pltpu.VMEM((D, D_EXPERT), BF16),
                        pltpu.VMEM((D_EXPERT, D), BF16)])
    return pl.pallas_call(
        _moe_grouped_kernel,
        grid_spec=grid_spec,
        out_shape=jax.ShapeDtypeStruct((n_tiles * MOE_TM, D), BF16),
        compiler_params=_cparams(("arbitrary",)),
        name="moe_grouped",
    )(plan["tile_expert"], plan["n_valid"], xs, wg, wu, wd)


def _moe_combine_kernel(final_norm, ng_ref, dst_ref, loc_ref, tot_ref, ys_ref, route_ref, x_ref, gate_ref,
                        *rest):
    if final_norm:
        fg_ref, o_ref, local_ref, sem = rest
    else:
        o_ref, local_ref, sem = rest
    i = pl.program_id(0)
    slot = i % 2
    plan_refs = (ng_ref, dst_ref, loc_ref)

    @pl.when(i == 0)
    def _():
        local_ref[...] = jnp.zeros_like(local_ref)
        _run_copies(plan_refs, 0, local_ref.at[0], ys_ref, sem.at[0], to_sorted=False)

    @pl.when(i + 1 < pl.num_programs(0))
    def _():
        _run_copies(plan_refs, i + 1, local_ref.at[1 - slot], ys_ref, sem.at[1 - slot], to_sorted=False)

    _group_waits(tot_ref[i], local_ref.at[slot], ys_ref, sem.at[slot])
    pos = lax.broadcasted_iota(jnp.int32, (MOE_TT, MOE_LOCAL), 1)
    route = route_ref[...]
    p1 = route[:, 0:1].astype(jnp.int32)
    p2 = route[:, 1:2].astype(jnp.int32)
    weigh = (jnp.where(pos == p1, route[:, 2:3], 0.0) + jnp.where(pos == p2, route[:, 3:4], 0.0)).astype(BF16)
    y = jnp.dot(weigh, local_ref[slot], preferred_element_type=F32)
    o = x_ref[0] + gate_ref[0] * y
    if final_norm:
        o = o * lax.rsqrt(jnp.mean(o * o, axis=-1, keepdims=True) + EPS) * fg_ref[...]
    o_ref[0] = o


def _moe_combine(ys, route_c, plan, x, gate, final_g=None):
    B, S, D = x.shape
    nt = S // MOE_TT
    in_specs = [pl.BlockSpec(memory_space=pl.ANY),
                pl.BlockSpec((MOE_TT, LANES), lambda i, *_: (i, 0)),
                pl.BlockSpec((1, MOE_TT, D), lambda i, *_: (i // nt, i % nt, 0)),
                pl.BlockSpec((1, 1, D), lambda i, *_: (i // nt, 0, 0))]
    args = [ys, route_c, x, gate.reshape(B, 1, D)]
    if final_g is not None:
        in_specs.append(pl.BlockSpec((1, D), lambda i, *_: (0, 0)))
        args.append(final_g.reshape(1, D))
    grid_spec = pltpu.PrefetchScalarGridSpec(
        num_scalar_prefetch=4,
        grid=(B * nt,),
        in_specs=in_specs,
        out_specs=pl.BlockSpec((1, MOE_TT, D), lambda i, *_: (i // nt, i % nt, 0)),
        scratch_shapes=[pltpu.VMEM((2, MOE_LOCAL, D), BF16), pltpu.SemaphoreType.DMA((2,))])
    return pl.pallas_call(
        functools.partial(_moe_combine_kernel, final_g is not None),
        grid_spec=grid_spec,
        out_shape=jax.ShapeDtypeStruct((B, S, D), F32),
        compiler_params=_cparams(("arbitrary",)),
        name="moe_combine",
    )(plan["ng"], plan["dst"], plan["local"], plan["tile_groups"], *args)


def _dft_tables():
    n1, n2, n = FFT_N1, FFT_N2, FFT_N1 * FFT_N2
    k1 = np.arange(n1)
    f1 = np.exp(-2j * np.pi * np.outer(k1, np.arange(n1)) / n1)
    tw = np.exp(-2j * np.pi * np.outer(np.arange(n2), k1) / n)
    ftw = f1[None, :, :] * tw[:, :, None]
    half = n1 // 2
    fh = ftw[:, :, :half]
    g_fwd = np.concatenate([np.concatenate([fh.real, -fh.imag], axis=2),
                            np.concatenate([fh.imag, fh.real], axis=2)], axis=1)
    back = ftw[:, :, ::-1][:, :, :half].copy()
    back[0] = np.roll(ftw[0], -1, axis=1)[:, ::-1][:, :half]
    back[0][:, 0] = 0.0
    fk = np.concatenate([fh, back], axis=2)
    g_real = np.concatenate([fk.real, fk.imag], axis=1)
    gi = np.conj(np.transpose(fh, (0, 2, 1))) / n
    g_inv = np.concatenate([np.concatenate([gi.real, -gi.imag], axis=2),
                            np.concatenate([gi.imag, gi.real], axis=2)], axis=1)
    f2 = np.exp(-2j * np.pi * np.outer(np.arange(n2), np.arange(n2)) / n2)
    f2_fwd = np.block([[f2.real, -f2.imag], [f2.imag, f2.real]])
    f2c = np.conj(f2)
    f2_inv = np.block([[f2c.real, -f2c.imag], [f2c.imag, f2c.real]])
    as_bf = lambda a: jnp.asarray(a, dtype=F32).astype(BF16)
    return as_bf(g_fwd), as_bf(g_real), as_bf(g_inv), as_bf(f2_fwd), as_bf(f2_inv)


def _fft_fast_stage(stage_ref, k1, f2):
    slab = STAGE_PITCH
    m = jnp.concatenate([stage_ref[pl.ds(k1, FFT_N2, stride=slab), :],
                         stage_ref[pl.ds(FFT_N1 + k1, FFT_N2, stride=slab), :]], axis=0)
    return jnp.dot(f2, m.astype(BF16), preferred_element_type=F32)


def _filter_fft_kernel(hf_ref, hb_ref, inv_ref, g_ref, f2_ref, h_ref, stage_ref):
    slab = STAGE_PITCH
    half = FFT_N1 // 2
    for n2 in range(FFT_N2):
        x = jnp.concatenate([hf_ref[pl.ds(n2, half, stride=FFT_N2), :],
                             hb_ref[pl.ds((FFT_N2 - n2) % FFT_N2, half, stride=FFT_N2), :]], axis=0)
        stage_ref[n2 * slab:n2 * slab + 2 * FFT_N1, :] = jnp.dot(g_ref[n2], x.astype(BF16),
                                                          preferred_element_type=F32)
    f2 = f2_ref[...]
    inv = inv_ref[...]
    for k1 in range(FFT_N1):
        h_ref[0, k1] = (_fft_fast_stage(stage_ref, k1, f2) * inv).astype(h_ref.dtype)


def _filter_fft(taps, inv_norm, g_real, f2_fwd, ct=LANES):
    L, cols = taps.shape
    C = D_CH
    n_ord = cols // (2 * C)
    nc = C // ct
    once = pl.Buffered(1)
    return pl.pallas_call(
        _filter_fft_kernel,
        grid=(n_ord, nc),
        in_specs=[pl.BlockSpec((L, ct), lambda o, c: (0, o * nc + c)),
                  pl.BlockSpec((L, ct), lambda o, c: (0, (n_ord + o) * nc + c)),
                  pl.BlockSpec((1, ct), lambda o, c: (0, o * nc + c)),
                  pl.BlockSpec(g_real.shape, lambda o, c: (0, 0, 0), pipeline_mode=once),
                  pl.BlockSpec(f2_fwd.shape, lambda o, c: (0, 0), pipeline_mode=once)],
        out_specs=pl.BlockSpec((1, FFT_N1, 2 * FFT_N2, ct), lambda o, c: (o, 0, 0, c)),
        out_shape=jax.ShapeDtypeStruct((n_ord, FFT_N1, 2 * FFT_N2, C), BF16),
        scratch_shapes=[pltpu.VMEM((FFT_N2 * STAGE_PITCH, ct), F32)],
        compiler_params=_cparams(("parallel", "parallel")),
        name="filter_fft",
    )(taps, taps, inv_norm, g_real, f2_fwd)


class _RowSets:
    def __init__(self, ref, member):
        self.ref, self.member, self.loaded = ref, member, {}

    def __call__(self, start):
        if start not in self.loaded:
            self.loaded = {k: v for k, v in self.loaded.items() if abs(k - start) <= 2}
            self.loaded[start] = self.ref[self.member, pl.ds(start, FFT_N1 // 2, stride=FFT_N2), :]
        return self.loaded[start]


def _strided_rows(at, n2, taps):
    half = FFT_N1 // 2
    cur = at(n2)
    if taps is None:
        return cur
    w_ref, b_ref = taps
    n1 = lax.broadcasted_iota(jnp.int32, cur.shape, 0)
    if n2 > 0:
        prev = at(n2 - 1)
    else:
        prev = jnp.where(n1 == 0, 0.0, pltpu.roll(at(FFT_N2 - 1), 1, axis=0))
    if n2 < FFT_N2 - 1:
        nxt = at(n2 + 1)
    else:
        nxt = jnp.where(n1 == half - 1, 0.0, pltpu.roll(at(0), half - 1, axis=0))
    return prev * w_ref[0:1, :] + cur * w_ref[1:2, :] + nxt * w_ref[2:3, :] + b_ref[...]


def _hyena_conv_kernel(conv_z, *refs):
    it = iter(refs)
    z_ref, gate_ref = next(it), next(it)
    z_taps = (next(it), next(it)) if conv_z else None
    gate_taps = (next(it), next(it))
    hb_ref, spec_ref, gf_ref, gi_ref, f2f_ref, f2i_ref, o_ref, stage_ref = it
    half = FFT_N1 // 2
    slab = STAGE_PITCH
    z_sets = [_RowSets(z_ref, m) for m in range(2)]
    for n2 in range(FFT_N2):
        x = jnp.concatenate([_strided_rows(z_sets[m], n2, z_taps) for m in range(2)], axis=0).astype(BF16)
        stage_ref[n2 * slab:n2 * slab + 2 * FFT_N1, :] = jnp.dot(gf_ref[n2], x, preferred_element_type=F32)
    f2f = f2f_ref[...]
    f2i = f2i_ref[...]
    for k1 in range(FFT_N1):
        zf = _fft_fast_stage(stage_ref, k1, f2f)
        zr, zi = zf[:FFT_N2], zf[FFT_N2:]
        hr = spec_ref[0, k1, :FFT_N2, :].astype(F32)
        hi = spec_ref[0, k1, FFT_N2:, :].astype(F32)
        p = jnp.concatenate([zr * hr - zi * hi, zr * hi + zi * hr], axis=0).astype(BF16)
        q = jnp.dot(f2i, p, preferred_element_type=F32)
        stage_ref[pl.ds(k1, FFT_N2, stride=slab), :] = q[:FFT_N2]
        stage_ref[pl.ds(FFT_N1 + k1, FFT_N2, stride=slab), :] = q[FFT_N2:]
    hb = hb_ref[...]
    z_sets = [_RowSets(z_ref, m) for m in range(2)]
    gate_sets = [_RowSets(gate_ref, m) for m in range(2)]
    for n2 in range(FFT_N2):
        y_in = stage_ref[n2 * slab:n2 * slab + 2 * FFT_N1, :].astype(BF16)
        y = jnp.dot(gi_ref[n2], y_in, preferred_element_type=F32)
        for m in range(2):
            zm = _strided_rows(z_sets[m], n2, z_taps)
            gm = _strided_rows(gate_sets[m], n2, gate_taps)
            o_ref[m, pl.ds(n2, half, stride=FFT_N2), :] = gm * (y[m * half:(m + 1) * half] + zm * hb)


def _hyena_conv(z, z_blk, gate, gate_blk, short_w, short_b, conv_z, hbias, spec, order, tabs, ct=LANES):
    B, L, _ = z.shape
    C = D_CH
    g_fwd, _, g_inv, f2_fwd, f2_inv = tabs
    once = pl.Buffered(1)
    const3 = lambda a: pl.BlockSpec(a.shape, lambda c, p: (0, 0, 0), pipeline_mode=once)
    const2 = lambda a: pl.BlockSpec(a.shape, lambda c, p: (0, 0), pipeline_mode=once)
    taps_specs = lambda blk: [pl.BlockSpec((3, ct), lambda c, p: (0, blk + c)),
                              pl.BlockSpec((1, ct), lambda c, p: (0, blk + c))]
    in_specs = [pl.BlockSpec((2, L, ct), lambda c, p: (p, 0, z_blk + c)),
                pl.BlockSpec((2, L, ct), lambda c, p: (p, 0, gate_blk + c))]
    args = [z, gate]
    if conv_z:
        in_specs += taps_specs(z_blk)
        args += [short_w, short_b]
    in_specs += taps_specs(gate_blk)
    args += [short_w, short_b]
    in_specs += [pl.BlockSpec((1, ct), lambda c, p: (0, c)),
                 pl.BlockSpec((1, FFT_N1, 2 * FFT_N2, ct), lambda c, p: (order, 0, 0, c), pipeline_mode=once),
                 const3(g_fwd), const3(g_inv), const2(f2_fwd), const2(f2_inv)]
    args += [hbias, spec, g_fwd, g_inv, f2_fwd, f2_inv]
    return pl.pallas_call(
        functools.partial(_hyena_conv_kernel, conv_z),
        grid=(C // ct, B // 2),
        in_specs=in_specs,
        out_specs=pl.BlockSpec((2, L, ct), lambda c, p: (p, 0, c)),
        out_shape=jax.ShapeDtypeStruct((B, L, C), F32),
        scratch_shapes=[pltpu.VMEM((FFT_N2 * STAGE_PITCH, ct), F32)],
        compiler_params=_cparams(("parallel", "arbitrary")),
        name="hyena_conv",
    )(*args)


def _hyena_filters(L, w1, b1, f1, w2, b2, f2, w3, b3):
    t = jnp.arange(L, dtype=F32)
    tn = t / max(L - 1, 1)
    bands = jnp.linspace(1e-4, HY_BANDS - 1, HY_BANDS, dtype=F32)
    ang = 2.0 * math.pi * t[:, None] * bands[None] / L
    feats = jnp.concatenate([tn[:, None], jnp.cos(ang), jnp.sin(ang)], axis=-1)
    h = jnp.sin(f1 * (feats @ w1 + b1))
    h = jnp.sin(f2 * (h @ w2 + b2))
    deltas = jnp.abs(jnp.linspace(HY_MIN_DECAY, HY_MAX_DECAY, D_CH, dtype=F32))
    decay = jnp.exp(-tn[:, None] * deltas[None])
    n_rep = w3.shape[1] // D_CH
    taps = (_mm_f32(h, w3, 512) + b3) * jnp.tile(decay, (1, n_rep))
    l1 = jnp.sum(jnp.abs(taps), axis=0)
    l1 = l1[:n_rep // 2 * D_CH] + l1[n_rep // 2 * D_CH:]
    return taps, (1.0 / (l1 + EPS))[None]


def _dup_heads(w):
    a, b = w[:, :HEAD_DIM], w[:, HEAD_DIM:]
    return jnp.concatenate([a, a, b, b], axis=1)


def _rope_tables(S):
    t = jnp.arange(S)
    row = (t // GRID_W).astype(F32)
    col = (t % GRID_W).astype(F32)
    half = HEAD_DIM // 2
    inv = ROPE_THETA ** (-jnp.arange(0, half, 2, dtype=F32) / half)
    ar = row[:, None] * inv[None]
    ac = col[:, None] * inv[None]
    cos = jnp.concatenate([jnp.cos(ar), jnp.cos(ar), jnp.cos(ac), jnp.cos(ac)], axis=-1)
    sin = jnp.concatenate([-jnp.sin(ar), jnp.sin(ar), -jnp.sin(ac), jnp.sin(ac)], axis=-1)
    return jnp.tile(cos, (1, 2)), jnp.tile(sin, (1, 2))


def _head_mean_matrix(width):
    blk = np.kron(np.eye(width // HEAD_DIM), np.full((HEAD_DIM, HEAD_DIM), 1.0 / HEAD_DIM))
    return jnp.asarray(blk, dtype=F32).astype(BF16)


def kernel(x, c, ctx, c_ctx, w_ada, b_ada, norm_g, final_g, w_in_even, w_out_even, a_sink, b_rpb, w_in_odd, w_out_odd, c_qnorm, c_knorm, hy_short_w, hy_short_b, hy_w1, hy_b1, hy_f1, hy_w2, hy_b2, hy_f2, hy_w3, hy_b3, hy_bias, w_router, b_router, moe_wg, moe_wu, moe_wd):
    B, S, D = x.shape
    depth = w_ada.shape[0]
    rope = _rope_tables(S)
    wr_pad = jnp.pad(w_router.astype(F32), ((0, 0), (0, LANES - N_EXPERTS)))
    wr_hi = wr_pad.astype(BF16)
    router = (b_router.astype(F32), wr_hi, (wr_pad - wr_hi.astype(F32)).astype(BF16))

    mod_in = jnp.concatenate([jax.nn.silu(c), jax.nn.silu(c_ctx)[None],
                              jnp.zeros((8 - B - 1, D), F32)], axis=0)
    xc = ctx
    for l in range(depth):
        need_ctx = l < depth - 1
        mod = _mm_f32(mod_in, w_ada[l], 1536) + b_ada[l]
        mx = mod[:B].reshape(B, 6, D)
        mc = mod[B].reshape(6, D)
        i = l // 2
        if l % 2 == 0:
            w = w_in_even[i].astype(BF16)
            w_all = jnp.concatenate([w[:, :512], _dup_heads(w[:, 512:640]), _dup_heads(w[:, 640:768]),
                                     w[:, 768:]], axis=1)
            segs_x = ((512, "rope", Q_MULT), (256, "rope", 1.0), (256, "plain", 1.0),
                      (512, "plain", Q_MULT), (512, "plain", 1.0), (512, "plain", 1.0))
            sink = a_sink[i].astype(F32) * LOG2_E
            bias = _nbr_bias_table(b_rpb[i].astype(F32) * LOG2_E, S // GRID_W)
            aq, akd, avd, bq, bk, bv = _norm_proj(x, norm_g[l, 0], mx[:, 0], mx[:, 1], w_all, segs_x,
                                                  rope=rope, tm=1024)
            segs_c = tuple((wd, "plain", m) for wd, _, m in segs_x)
            caq, cakd, cavd, cbq, cbk, cbv = _norm_proj(xc, norm_g[l, 0], mc[0:1], mc[1:2], w_all, segs_c)
            ya = _window_attn(aq, akd, avd, cakd, cavd, sink)
            yb = _nbr_attn(bq, bk, bv, cbk, cbv, bias)
            ys = [ya, yb]
            w_out = w_out_even[i].astype(BF16)
            if need_ctx:
                yc = [_ctx_attn(sink, caq, cakd, cavd, cbq, cbk, cbv)]
        else:
            w = w_in_odd[i].astype(BF16)
            w_all = jnp.concatenate([w[:, :512], _dup_heads(w[:, 512:640]), _dup_heads(w[:, 640:768]),
                                     w[:, 768:]], axis=1)
            gains = jnp.concatenate([jnp.tile(c_qnorm[i], 8), jnp.tile(c_knorm[i], 4)])[None].astype(F32)
            norm = (_head_mean_matrix(512), gains)
            segs_x = ((512, "normrope", Q_MULT), (256, "normrope", 1.0), (256, "plain", 1.0),
                      (3 * D_CH, "plain", 1.0, F32))
            qx, kxd, vxd, ux = _norm_proj(x, norm_g[l, 0], mx[:, 0], mx[:, 1], w_all, segs_x,
                                          rope=rope, norm=norm)
            w_c = w_all[:, 512:1024]
            norm_c = (_head_mean_matrix(512), jnp.tile(c_knorm[i], 4)[None].astype(F32))
            kcd, vcd = _norm_proj(xc, norm_g[l, 0], mc[0:1], mc[1:2], w_c,
                                  ((256, "norm", 1.0), (256, "plain", 1.0)), norm=norm_c)
            logit_bound = (1.02 * HEAD_DIM * Q_MULT * jnp.max(jnp.abs(c_qnorm[i]))
                           * jnp.max(jnp.abs(c_knorm[i])))
            y_attn = lax.cond(logit_bound <= FULL_NOMAX_LOG2_BOUND,
                              lambda *a: _full_attn(*a, bounded=True),
                              lambda *a: _full_attn(*a, bounded=False),
                              qx, kxd, vxd, kcd, vcd)
            sw, sb = hy_short_w[i].astype(F32), hy_short_b[i].astype(F32)[None]
            tabs = _dft_tables()
            taps, inv_norm = _hyena_filters(S, hy_w1[i], hy_b1[i], hy_f1[i], hy_w2[i], hy_b2[i],
                                            hy_f2[i], hy_w3[i], hy_b3[i])
            spec = _filter_fft(taps, inv_norm, tabs[1], tabs[3])
            blocks = D_CH // LANES
            z = _hyena_conv(ux, 0, ux, blocks, sw, sb, True, hy_bias[i, 0:1], spec, 0, tabs)
            z = _hyena_conv(z, 0, ux, 2 * blocks, sw, sb, False, hy_bias[i, 1:2], spec, 1, tabs)
            ys = [y_attn, z]
            w_out = w_out_odd[i].astype(BF16)
            if need_ctx:
                raise NotImplementedError("context update of an odd layer is not needed at this depth")

        x, hx, route_t, route_c, ng = _out_proj(ys, w_out, x, mx[:, 2], norm_g[l, 1], mx[:, 3], mx[:, 4],
                                                router, sparse=True)
        if need_ctx:
            xc, hc, comb_c = _out_proj(yc, w_out, xc, mc[2:3], norm_g[l, 1], mc[3:4], mc[4:5], router,
                                       sparse=False)
            flat = lambda a: a.reshape(1, -1, a.shape[-1])
            xc = _moe(flat(hc), flat(comb_c), moe_wg, moe_wu, moe_wd, l, flat(xc), mc[5:6]).reshape(xc.shape)
        plan = _moe_plan(ng)
        sorted_rows = _moe_dispatch(hx, route_t, plan)
        x = _moe_combine(_moe_grouped(sorted_rows, plan, moe_wg, moe_wu, moe_wd, l), route_c, plan, x, mx[:, 5],
                         final_g=None if need_ctx else final_g)
    return x
```

```python
import functools
import math

import numpy as np
import jax
import jax.numpy as jnp
from jax import lax
from jax.experimental import pallas as pl
from jax.experimental.pallas import tpu as pltpu

F32 = jnp.float32
BF16 = jnp.bfloat16

D_MODEL = 1024
GRID_W = 64
CTX_LEN = 256
HEAD_DIM = 64
ROPE_THETA = 10000.0
EPS = 1e-6
ATTN_SCALE = HEAD_DIM ** -0.5
LOG2_E = math.log2(math.e)
Q_MULT = ATTN_SCALE * LOG2_E
A_WINDOW = 128
A_BLOCK = 128
B_WIN_H = 8
B_WIN_W = 16
D_CH = 512
HY_BANDS = 16
HY_MAX_DECAY = math.log(1e-2) / 0.3
HY_MIN_DECAY = math.log(1e-2) / 1.5
N_EXPERTS = 16
N_GROUPS = 4
EXPERTS_PER_GROUP = N_EXPERTS // N_GROUPS
D_EXPERT = 512

LANES = 128
NEG = -1e30
VMEM_LIMIT = 48 * 1024 * 1024

FFT_N1 = 64
FFT_N2 = 128
STAGE_PITCH = 2 * FFT_N1 + 8


def _cparams(sem):
    return pltpu.CompilerParams(dimension_semantics=sem, vmem_limit_bytes=VMEM_LIMIT)


def _mm_f32_kernel(x_ref, w_ref, o_ref):
    o_ref[...] = jnp.dot(x_ref[...], w_ref[...], preferred_element_type=F32)


def _mm_f32(x, w, tn):
    M, K = x.shape
    N = w.shape[1]
    return pl.pallas_call(
        _mm_f32_kernel,
        grid=(N // tn,),
        in_specs=[pl.BlockSpec((M, K), lambda j: (0, 0)),
                  pl.BlockSpec((K, tn), lambda j: (0, j))],
        out_specs=pl.BlockSpec((M, tn), lambda j: (0, j)),
        out_shape=jax.ShapeDtypeStruct((M, N), F32),
        compiler_params=_cparams(("arbitrary",)),
        name="mm_f32",
    )(x, w)


def _swap16(y):
    n = y.shape[-1]
    lane = lax.broadcasted_iota(jnp.int32, y.shape, y.ndim - 1)
    up = pltpu.roll(y, n - 16, axis=y.ndim - 1)
    dn = pltpu.roll(y, 16, axis=y.ndim - 1)
    return jnp.where((lane % 32) < 16, up, dn)


def _tile_lanes(t, width):
    reps = width // t.shape[-1]
    return t if reps == 1 else jnp.concatenate([t] * reps, axis=-1)


def _norm_proj_kernel(segs, has_rope, has_norm, *refs):
    it = iter(refs)
    x_ref, g_ref, shift_ref, scale_ref, w_ref = (next(it) for _ in range(5))
    cos_ref = sin_ref = bd_ref = gain_ref = None
    if has_rope:
        cos_ref, sin_ref = next(it), next(it)
    if has_norm:
        bd_ref, gain_ref = next(it), next(it)
    out_refs = list(it)

    x = x_ref[0]
    ms = jnp.mean(x * x, axis=-1, keepdims=True)
    h = x * lax.rsqrt(ms + EPS) * g_ref[...]
    h = h * (1.0 + scale_ref[0]) + shift_ref[0]
    y = jnp.dot(h.astype(BF16), w_ref[...], preferred_element_type=F32)

    off = 0
    goff = 0
    for (width, kind, mult, *_), o_ref in zip(segs, out_refs):
        ys = y[:, off:off + width]
        if kind in ("norm", "normrope"):
            bd = bd_ref[...][:width, :width]
            hms = jnp.dot((ys * ys).astype(BF16), bd, preferred_element_type=F32)
            ys = ys * lax.rsqrt(hms + EPS) * gain_ref[:, goff:goff + width]
            goff += width
        if kind in ("rope", "normrope"):
            c = _tile_lanes(cos_ref[...], width)
            s = _tile_lanes(sin_ref[...], width)
            ys = ys * c + _swap16(ys) * s
        if mult != 1.0:
            ys = ys * mult
        o_ref[0] = ys.astype(o_ref.dtype)
        off += width


def _norm_proj(x, g, shift, scale, w, segs, rope=None, norm=None, tm=512):
    B, S, D = x.shape
    N = w.shape[1]
    tm = min(tm, S)
    bm = shift.shape[0]
    mod_map = (lambda b, i: (b, 0, 0)) if bm > 1 else (lambda b, i: (0, 0, 0))
    args = [x, g.reshape(1, D), shift.reshape(bm, 1, D), scale.reshape(bm, 1, D), w]
    in_specs = [pl.BlockSpec((1, tm, D), lambda b, i: (b, i, 0)),
                pl.BlockSpec((1, D), lambda b, i: (0, 0)),
                pl.BlockSpec((1, 1, D), mod_map),
                pl.BlockSpec((1, 1, D), mod_map),
                pl.BlockSpec((D, N), lambda b, i: (0, 0))]
    if rope is not None:
        args += [rope[0], rope[1]]
        in_specs += [pl.BlockSpec((tm, LANES), lambda b, i: (i, 0))] * 2
    if norm is not None:
        args += [norm[0], norm[1]]
        in_specs += [pl.BlockSpec(norm[0].shape, lambda b, i: (0, 0)),
                     pl.BlockSpec(norm[1].shape, lambda b, i: (0, 0))]
    out_shape = [jax.ShapeDtypeStruct((B, S, seg[0]), seg[3] if len(seg) > 3 else BF16) for seg in segs]
    out_specs = [pl.BlockSpec((1, tm, seg[0]), lambda b, i: (b, i, 0)) for seg in segs]
    return pl.pallas_call(
        functools.partial(_norm_proj_kernel, segs, rope is not None, norm is not None),
        grid=(B, S // tm),
        in_specs=in_specs,
        out_specs=out_specs,
        out_shape=out_shape,
        compiler_params=_cparams(("parallel", "parallel")),
        name="norm_proj",
    )(*args)


def _half_mask(shape):
    return lax.broadcasted_iota(jnp.int32, shape, len(shape) - 1) < HEAD_DIM


def _stack_halves(qp):
    lo = _half_mask(qp.shape)
    zero = jnp.zeros_like(qp)
    return jnp.concatenate([jnp.where(lo, qp, zero), jnp.where(lo, zero, qp)], axis=0)


def _merge_halves(o, m):
    return jnp.where(_half_mask((m, LANES)), o[:m], o[m:])


def _scores(q, k):
    return lax.dot_general(q, k, (((1,), (1,)), ((), ())), preferred_element_type=F32)


def _joint_softmax_pv(score_parts, value_parts, extra_logit=None):
    m = functools.reduce(jnp.maximum, [jnp.max(s, axis=-1, keepdims=True) for s in score_parts])
    if extra_logit is not None:
        m = jnp.maximum(m, extra_logit)
    den = jnp.exp2(extra_logit - m) if extra_logit is not None else 0.0
    acc = None
    for s, v in zip(score_parts, value_parts):
        p = jnp.exp2(s - m)
        den = den + jnp.sum(p, axis=-1, keepdims=True)
        pv = jnp.dot(p.astype(BF16), v, preferred_element_type=F32)
        acc = pv if acc is None else acc + pv
    return acc / den


def _sink_column(sink_ref, first_head, n_heads, rows_per_head):
    rows = lax.broadcasted_iota(jnp.int32, (n_heads * rows_per_head, 1), 0)
    col = jnp.zeros((n_heads * rows_per_head, 1), F32)
    for j in range(n_heads):
        in_head = (rows >= j * rows_per_head) & (rows < (j + 1) * rows_per_head)
        col = jnp.where(in_head, sink_ref[first_head + j], col)
    return col


def _window_attn_kernel(sink_ref, q_ref, kp_ref, kc_ref, kn_ref, vp_ref, vc_ref, vn_ref,
                        ck_ref, cv_ref, mask_ref, o_ref):
    blk = A_BLOCK
    q = q_ref[0]
    mask = mask_ref[0]
    outs = []
    for g in range(2):
        ls = slice(g * LANES, (g + 1) * LANES)
        k_loc = jnp.concatenate([kp_ref[0][:, ls], kc_ref[0][:, ls], kn_ref[0][:, ls]], axis=0)
        v_loc = jnp.concatenate([vp_ref[0][:, ls], vc_ref[0][:, ls], vn_ref[0][:, ls]], axis=0)
        qs = jnp.concatenate([_stack_halves(q[:, (2 * g + j) * LANES:(2 * g + j + 1) * LANES])
                              for j in range(2)], axis=0)
        s_loc = _scores(qs, k_loc) + mask
        s_ctx = _scores(qs, ck_ref[0][:, ls])
        sink = _sink_column(sink_ref, 4 * g, 4, blk)
        o = _joint_softmax_pv([s_loc, s_ctx], [v_loc, cv_ref[0][:, ls]], sink)
        outs += [_merge_halves(o[:2 * blk], blk), _merge_halves(o[2 * blk:], blk)]
    o_ref[0] = jnp.concatenate(outs, axis=-1).astype(o_ref.dtype)


def _window_mask_table(seq_len):
    blk = A_BLOCK
    rows = np.arange(4 * blk)[:, None] % blk
    rel = np.arange(3 * blk)[None, :] - blk
    near = np.abs(rows - rel) <= A_WINDOW
    tabs = [near & (rel >= 0), near, near & (rel < blk)]
    return jnp.asarray(np.where(np.stack(tabs), 0.0, NEG).astype(np.float32))


def _window_attn(q, kd, vd, ckd, cvd, sink):
    B, S, _ = q.shape
    nb = S // A_BLOCK
    mask = _window_mask_table(S)
    pat = lambda i: jnp.where(i == 0, 0, jnp.where(i == nb - 1, 2, 1))
    kv_spec = lambda f: pl.BlockSpec((1, A_BLOCK, 2 * LANES), f)
    prev_map = lambda b, i: (b, jnp.maximum(i - 1, 0), 0)
    cur_map = lambda b, i: (b, i, 0)
    next_map = lambda b, i: (b, jnp.minimum(i + 1, nb - 1), 0)
    ctx_spec = pl.BlockSpec((1, CTX_LEN, 2 * LANES), lambda b, i: (b, 0, 0))
    return pl.pallas_call(
        _window_attn_kernel,
        grid=(B, nb),
        in_specs=[pl.BlockSpec(memory_space=pltpu.SMEM),
                  pl.BlockSpec((1, A_BLOCK, 4 * LANES), cur_map),
                  kv_spec(prev_map), kv_spec(cur_map), kv_spec(next_map),
                  kv_spec(prev_map), kv_spec(cur_map), kv_spec(next_map),
                  ctx_spec, ctx_spec,
                  pl.BlockSpec((1,) + mask.shape[1:], lambda b, i: (pat(i), 0, 0))],
        out_specs=pl.BlockSpec((1, A_BLOCK, 4 * LANES), cur_map),
        out_shape=jax.ShapeDtypeStruct((B, S, 4 * LANES), BF16),
        compiler_params=_cparams(("parallel", "parallel")),
        name="window_attn",
    )(sink, q, kd, kd, kd, vd, vd, vd, ckd, cvd, mask)


NBR_ROWS = 4
NBR_KROWS = 12
NBR_PAIRS = 4


def _nbr_start_row(i, n_rows):
    return jnp.clip(i * NBR_ROWS - B_WIN_H // 2, 0, n_rows - NBR_KROWS)


def _nbr_attn_kernel(n_rows, q_ref, k_ref, v_ref, ck_ref, cv_ref, bias_ref, o_ref):
    i = pl.program_id(2)
    nq = NBR_ROWS * GRID_W
    nk = NBR_KROWS * GRID_W
    start = pl.multiple_of(_nbr_start_row(i, n_rows) * GRID_W, GRID_W)
    outs = []
    for pp in range(NBR_PAIRS):
        ls = slice(pp * LANES, (pp + 1) * LANES)
        k_loc = k_ref[0, pl.ds(start, nk), ls]
        v_loc = v_ref[0, pl.ds(start, nk), ls]
        qs = _stack_halves(q_ref[0][:, ls])
        s_loc = _scores(qs, k_loc) + bias_ref[0, 2 * pp:2 * pp + 2].reshape(2 * nq, nk)
        s_ctx = _scores(qs, ck_ref[0][:, ls])
        o = _joint_softmax_pv([s_loc, s_ctx], [v_loc, cv_ref[0][:, ls]])
        outs.append(_merge_halves(o, nq))
    o_ref[0] = jnp.concatenate(outs, axis=-1).astype(o_ref.dtype)


def _nbr_bias_table(rpb, n_rows):
    kh = B_WIN_H
    n_heads = rpb.shape[0]
    col = np.arange(GRID_W)
    cs = np.clip(col - B_WIN_W // 2, 0, GRID_W - B_WIN_W)
    col_ok = (col[None, :] >= cs[:, None]) & (col[None, :] < cs[:, None] + B_WIN_W)
    dc = np.clip(col[None, :] - col[:, None], -(B_WIN_W - 1), B_WIN_W - 1) + B_WIN_W - 1
    pick_col = (dc[..., None] == np.arange(2 * B_WIN_W - 1)).astype(np.float32)
    r0 = np.array([0, NBR_ROWS, n_rows - NBR_ROWS])
    start = np.clip(r0 - kh // 2, 0, n_rows - NBR_KROWS)
    r = r0[:, None] + np.arange(NBR_ROWS)[None]
    rs = np.clip(r - kh // 2, 0, n_rows - kh)
    kr = start[:, None] + np.arange(NBR_KROWS)[None]
    row_ok = (kr[:, None, :] >= rs[:, :, None]) & (kr[:, None, :] < rs[:, :, None] + kh)
    dr = np.clip(kr[:, None, :] - r[:, :, None] + kh - 1, 0, 2 * kh - 2)
    pick_row = (dr[..., None] == np.arange(2 * kh - 1)).astype(np.float32)
    table = jnp.einsum("prkd,hdc,qjc->phrqkj", pick_row, rpb.astype(F32), pick_col,
                       precision=lax.Precision.HIGHEST)
    ok = row_ok[:, None, :, None, :, None] & col_ok[None, None, None, :, None, :]
    return jnp.where(ok, table, NEG).reshape(3, n_heads, NBR_ROWS * GRID_W, NBR_KROWS * GRID_W)


def _nbr_attn(q, k, v, ck, cv, bias):
    B, S, _ = q.shape
    n_rows = S // GRID_W
    nsteps = n_rows // NBR_ROWS
    nq = NBR_ROWS * GRID_W
    nk = NBR_KROWS * GRID_W
    pat = lambda i: jnp.where(i == 0, 0, jnp.where(i == nsteps - 1, 2, 1))
    wl = NBR_PAIRS * LANES
    return pl.pallas_call(
        functools.partial(_nbr_attn_kernel, n_rows),
        grid=(B, 4 // NBR_PAIRS, nsteps),
        in_specs=[pl.BlockSpec((1, nq, wl), lambda b, p, i: (b, i, p)),
                  pl.BlockSpec((1, S, wl), lambda b, p, i: (b, 0, p)),
                  pl.BlockSpec((1, S, wl), lambda b, p, i: (b, 0, p)),
                  pl.BlockSpec((1, CTX_LEN, wl), lambda b, p, i: (b, 0, p)),
                  pl.BlockSpec((1, CTX_LEN, wl), lambda b, p, i: (b, 0, p)),
                  pl.BlockSpec((1, 2 * NBR_PAIRS, nq, nk), lambda b, p, i: (pat(i), p, 0, 0))],
        out_specs=pl.BlockSpec((1, nq, wl), lambda b, p, i: (b, i, p)),
        out_shape=jax.ShapeDtypeStruct((B, S, 4 * LANES), BF16),
        compiler_params=_cparams(("parallel", "parallel", "arbitrary")),
        name="nbr_attn",
    )(q, k, v, ck, cv, bias)


FULL_TQ = 512
FULL_TK = 512
FULL_NOMAX_LOG2_BOUND = 60.0


def _full_attn_kernel(bounded, q_ref, k_ref, v_ref, ck_ref, cv_ref, o_ref):
    tq = FULL_TQ
    q = q_ref[0]
    groups = range(2)
    lanes = [slice(g * LANES, (g + 1) * LANES) for g in groups]
    qs = [jnp.concatenate([_stack_halves(q[:, (2 * g + j) * LANES:(2 * g + j + 1) * LANES]) for j in range(2)],
                          axis=0) for g in groups]

    def step_bounded(g, acc, k, v):
        v_ones = jnp.where(_half_mask(v.shape), v, jnp.ones_like(v))
        p = jnp.exp2(_scores(qs[g], k))
        return acc + jnp.dot(p.astype(BF16), v_ones, preferred_element_type=F32)

    def step_online(g, carry, k, v):
        m, l, acc = carry
        s = _scores(qs[g], k)
        m_new = jnp.maximum(m, jnp.max(s, axis=-1, keepdims=True))
        alpha = jnp.exp2(m - m_new)
        p = jnp.exp2(s - m_new)
        l = l * alpha + jnp.sum(p, axis=-1, keepdims=True)
        acc = acc * alpha + jnp.dot(p.astype(BF16), v, preferred_element_type=F32)
        return m_new, l, acc

    step = step_bounded if bounded else step_online
    init = jnp.zeros((4 * tq, LANES), F32)
    if not bounded:
        init = (jnp.full((4 * tq, 1), NEG, F32), jnp.zeros((4 * tq, 1), F32), init)
    carry = tuple(step(g, init, ck_ref[0][:, lanes[g]], cv_ref[0][:, lanes[g]]) for g in groups)

    def body(j, carry):
        rows = pl.ds(pl.multiple_of(j * FULL_TK, FULL_TK), FULL_TK)
        return tuple(step(g, carry[g], k_ref[0, rows, lanes[g]], v_ref[0, rows, lanes[g]]) for g in groups)

    carry = lax.fori_loop(0, k_ref.shape[1] // FULL_TK, body, carry)
    pairs = []
    for g in groups:
        if bounded:
            o = carry[g] * pltpu.roll(1.0 / carry[g], HEAD_DIM, axis=1)
            merge = lambda a, b: jnp.where(_half_mask((tq, LANES)), a, pltpu.roll(b, HEAD_DIM, axis=1))
            pairs += [merge(o[0:tq], o[tq:2 * tq]), merge(o[2 * tq:3 * tq], o[3 * tq:])]
        else:
            o = carry[g][2] / carry[g][1]
            pairs += [_merge_halves(o[:2 * tq], tq), _merge_halves(o[2 * tq:], tq)]
    o_ref[0] = jnp.concatenate(pairs, axis=-1).astype(o_ref.dtype)


def _full_attn(q, kd, vd, ckd, cvd, bounded):
    B, S, W = q.shape
    return pl.pallas_call(
        functools.partial(_full_attn_kernel, bounded),
        grid=(B, S // FULL_TQ),
        in_specs=[pl.BlockSpec((1, FULL_TQ, W), lambda b, i: (b, i, 0)),
                  pl.BlockSpec((1, S, 2 * LANES), lambda b, i: (b, 0, 0)),
                  pl.BlockSpec((1, S, 2 * LANES), lambda b, i: (b, 0, 0)),
                  pl.BlockSpec((1, CTX_LEN, 2 * LANES), lambda b, i: (b, 0, 0)),
                  pl.BlockSpec((1, CTX_LEN, 2 * LANES), lambda b, i: (b, 0, 0))],
        out_specs=pl.BlockSpec((1, FULL_TQ, W), lambda b, i: (b, i, 0)),
        out_shape=jax.ShapeDtypeStruct((B, S, W), BF16),
        compiler_params=_cparams(("parallel", "arbitrary")),
        name="full_attn_bounded" if bounded else "full_attn_online",
    )(q, kd, vd, ckd, cvd)


def _ctx_attn_kernel(sink_ref, aq_ref, akd_ref, avd_ref, bq_ref, bk_ref, bv_ref, o_ref):
    n = CTX_LEN
    aq = aq_ref[0]
    bq = bq_ref[0]
    outs = []
    for g in range(2):
        ls = slice(g * LANES, (g + 1) * LANES)
        qs = jnp.concatenate([_stack_halves(aq[:, (2 * g + j) * LANES:(2 * g + j + 1) * LANES])
                              for j in range(2)], axis=0)
        sink = _sink_column(sink_ref, 4 * g, 4, n)
        o = _joint_softmax_pv([_scores(qs, akd_ref[0][:, ls])], [avd_ref[0][:, ls]], sink)
        outs += [_merge_halves(o[:2 * n], n), _merge_halves(o[2 * n:], n)]
    for p in range(4):
        ls = slice(p * LANES, (p + 1) * LANES)
        qs = _stack_halves(bq[:, ls])
        o = _joint_softmax_pv([_scores(qs, bk_ref[0][:, ls])], [bv_ref[0][:, ls]])
        outs.append(_merge_halves(o, n))
    o_ref[0] = jnp.concatenate(outs, axis=-1).astype(o_ref.dtype)


def _ctx_attn(sink, aq, akd, avd, bq, bk, bv):
    B = aq.shape[0]
    spec = lambda a: pl.BlockSpec((1,) + a.shape[1:], lambda b: (b, 0, 0))
    args = (aq, akd, avd, bq, bk, bv)
    return pl.pallas_call(
        _ctx_attn_kernel,
        grid=(B,),
        in_specs=[pl.BlockSpec(memory_space=pltpu.SMEM)] + [spec(a) for a in args],
        out_specs=pl.BlockSpec((1, CTX_LEN, 8 * LANES), lambda b: (b, 0, 0)),
        out_shape=jax.ShapeDtypeStruct((B, CTX_LEN, 8 * LANES), BF16),
        compiler_params=_cparams(("parallel",)),
        name="ctx_attn",
    )(sink, *args)


def _pick4(idx, vals):
    return jnp.where(idx == 0, vals[0], jnp.where(idx == 1, vals[1], jnp.where(idx == 2, vals[2], vals[3])))


def _route_rows(lg_t, b_ref):
    n_tok = lg_t.shape[1]
    s = [jax.nn.sigmoid(lg_t[e:e + 1, :]) for e in range(N_EXPERTS)]
    sel = [s[e] + b_ref[e] for e in range(N_EXPERTS)]
    n = EXPERTS_PER_GROUP
    gscore = []
    for j in range(N_GROUPS):
        v = sel[n * j:n * (j + 1)]
        pair_sums = [v[a] + v[b] for a in range(n) for b in range(a + 1, n)]
        gscore.append(functools.reduce(jnp.maximum, pair_sums))
    best, gbest = gscore[0], jnp.zeros((1, n_tok), jnp.int32)
    for j in range(1, N_GROUPS):
        upd = gscore[j] > best
        best = jnp.where(upd, gscore[j], best)
        gbest = jnp.where(upd, j, gbest)
    v = [_pick4(gbest, [sel[n * j + i] for j in range(N_GROUPS)]) for i in range(n)]
    u = [_pick4(gbest, [s[n * j + i] for j in range(N_GROUPS)]) for i in range(n)]
    m1, i1 = v[0], jnp.zeros((1, n_tok), jnp.int32)
    for i in range(1, n):
        upd = v[i] > m1
        m1 = jnp.where(upd, v[i], m1)
        i1 = jnp.where(upd, i, i1)
    m2, i2 = jnp.full((1, n_tok), -jnp.inf, F32), jnp.zeros((1, n_tok), jnp.int32)
    for i in range(n):
        upd = (i1 != i) & (v[i] > m2)
        m2 = jnp.where(upd, v[i], m2)
        i2 = jnp.where(upd, i, i2)
    u1, u2 = _pick4(i1, u), _pick4(i2, u)
    tot = u1 + u2
    return n * gbest + i1, n * gbest + i2, u1 / tot, u2 / tot


def _out_proj_kernel(n_y, sparse, *refs):
    y_refs = refs[:n_y]
    br_ref, w_ref, x_ref, gate_ref, g_ref, shift_ref, scale_ref, wrh_ref, wrl_ref = refs[n_y:n_y + 9]
    n_in = n_y + 9
    if sparse:
        tri_ref, tril_ref = refs[n_in:n_in + 2]
        n_in += 2
    outs = refs[n_in:]
    xo_ref = outs[0]
    off = 0
    acc = None
    for y_ref in y_refs:
        wdt = y_ref.shape[-1]
        part = jnp.dot(y_ref[0].astype(BF16), w_ref[off:off + wdt, :], preferred_element_type=F32)
        acc = part if acc is None else acc + part
        off += wdt
    x = x_ref[0] + gate_ref[0] * acc
    xo_ref[0] = x
    ms = jnp.mean(x * x, axis=-1, keepdims=True)
    h = x * lax.rsqrt(ms + EPS) * g_ref[...]
    h = h * (1.0 + scale_ref[0]) + shift_ref[0]
    hh = h.astype(BF16)
    hl = (h - hh.astype(F32)).astype(BF16)
    lg = (jnp.dot(hh, wrh_ref[...], preferred_element_type=F32)
          + jnp.dot(hl, wrh_ref[...], preferred_element_type=F32)
          + jnp.dot(hh, wrl_ref[...], preferred_element_type=F32))
    lg_t = lg.T[:N_EXPERTS]
    e1, e2, w1, w2 = _route_rows(lg_t, br_ref)
    rows = lax.broadcasted_iota(jnp.int32, lg_t.shape, 0)
    if not sparse:
        h_ref, comb_ref = outs[1:]
        h_ref[0] = hh
        comb_t = jnp.where(rows == e1, w1, 0.0) + jnp.where(rows == e2, w2, 0.0)
        comb_ref[0] = jnp.concatenate(
            [comb_t, jnp.zeros((LANES - N_EXPERTS, comb_t.shape[1]), F32)], axis=0).T
        return

    h_ref, route_t_ref, route_c_ref, ng_ref = outs[1:]
    h_ref[0] = hh
    member = jnp.where((rows == e1) | (rows == e2), 1.0, 0.0)
    before = jnp.dot(member.astype(BF16), tri_ref[...], preferred_element_type=F32)
    groups = jnp.floor((jnp.sum(member, axis=1, keepdims=True) + (MOE_G - 1)) * (1.0 / MOE_G))
    groups = jnp.broadcast_to(groups, (N_EXPERTS, LANES))
    run_start = MOE_G * jnp.dot(tril_ref[...], groups.astype(BF16), preferred_element_type=F32)[:, 0:1]
    pos = run_start + before
    p1 = jnp.sum(jnp.where(rows == e1, pos, 0.0), axis=0, keepdims=True)
    p2 = jnp.sum(jnp.where(rows == e2, pos, 0.0), axis=0, keepdims=True)
    ng_ref[0] = groups
    field = lax.broadcasted_iota(jnp.int32, (LANES, h.shape[0]), 0)
    route = jnp.zeros((LANES, h.shape[0]), F32)
    for k, v in enumerate((p1, p2, w1, w2)):
        route = jnp.where(field == k, v, route)
    route_t_ref[...] = route[:ROUTE_FIELDS]
    route_c_ref[...] = route.T


ROUTE_FIELDS = 8
MOE_G = 16
MOE_TT = 512
MOE_LOCAL = 1280


def _out_proj(ys, w, x, gate, g, shift, scale, router, sparse, tm=512):
    B, S, D = x.shape
    tm = min(tm, S)
    bm = gate.shape[0]
    b_router, wr_hi, wr_lo = router
    nt = S // tm
    mod_map = (lambda b, i: (b, 0, 0)) if bm > 1 else (lambda b, i: (0, 0, 0))
    mod_spec = pl.BlockSpec((1, 1, D), mod_map)
    row_map = lambda b, i: (b, i, 0)
    in_specs = ([pl.BlockSpec((1, tm, y.shape[-1]), row_map) for y in ys]
                + [pl.BlockSpec(memory_space=pltpu.SMEM),
                   pl.BlockSpec(w.shape, lambda b, i: (0, 0)),
                   pl.BlockSpec((1, tm, D), row_map), mod_spec,
                   pl.BlockSpec((1, D), lambda b, i: (0, 0)), mod_spec, mod_spec,
                   pl.BlockSpec(wr_hi.shape, lambda b, i: (0, 0)),
                   pl.BlockSpec(wr_lo.shape, lambda b, i: (0, 0))])
    args = list(ys) + [b_router, w, x, gate.reshape(bm, 1, D), g.reshape(1, D), shift.reshape(bm, 1, D),
                       scale.reshape(bm, 1, D), wr_hi, wr_lo]
    out_specs = [pl.BlockSpec((1, tm, D), row_map)]
    out_shape = [jax.ShapeDtypeStruct((B, S, D), F32)]
    out_specs.append(pl.BlockSpec((1, tm, D), row_map))
    out_shape.append(jax.ShapeDtypeStruct((B, S, D), BF16))
    if sparse:
        assert tm == MOE_TT
        tri = jnp.asarray(np.triu(np.ones((tm, tm), np.float32), 1)).astype(BF16)
        tril = jnp.asarray(np.tril(np.ones((N_EXPERTS, N_EXPERTS), np.float32), -1)).astype(BF16)
        args += [tri, tril]
        in_specs += [pl.BlockSpec(tri.shape, lambda b, i: (0, 0)),
                     pl.BlockSpec(tril.shape, lambda b, i: (0, 0))]
        out_specs += [pl.BlockSpec((ROUTE_FIELDS, tm), lambda b, i: (0, b * nt + i)),
                      pl.BlockSpec((tm, LANES), lambda b, i: (b * nt + i, 0)),
                      pl.BlockSpec((1, N_EXPERTS, LANES), lambda b, i: (b * nt + i, 0, 0))]
        out_shape += [jax.ShapeDtypeStruct((ROUTE_FIELDS, B * S), F32),
                      jax.ShapeDtypeStruct((B * S, LANES), F32),
                      jax.ShapeDtypeStruct((B * nt, N_EXPERTS, LANES), F32)]
    else:
        out_specs.append(pl.BlockSpec((1, tm, LANES), row_map))
        out_shape.append(jax.ShapeDtypeStruct((B, S, LANES), F32))
    return pl.pallas_call(
        functools.partial(_out_proj_kernel, len(ys), sparse),
        grid=(B, nt),
        in_specs=in_specs,
        out_specs=out_specs,
        out_shape=out_shape,
        compiler_params=_cparams(("parallel", "parallel")),
        name="out_proj_sparse" if sparse else "out_proj",
    )(*args)


def _moe_kernel(h_ref, comb_ref, wg_ref, wu_ref, wd_ref, x_ref, gate_ref, o_ref, acc_ref):
    e = pl.program_id(2)

    @pl.when(e == 0)
    def _():
        acc_ref[...] = jnp.zeros_like(acc_ref)

    h = h_ref[0]
    a = jnp.dot(h, wg_ref[0, 0].astype(BF16), preferred_element_type=F32)
    u = jnp.dot(h, wu_ref[0, 0].astype(BF16), preferred_element_type=F32)
    he = (a * jax.nn.sigmoid(a) * u).astype(BF16)
    y = jnp.dot(he, wd_ref[0, 0].astype(BF16), preferred_element_type=F32)
    lane = lax.broadcasted_iota(jnp.int32, comb_ref.shape[1:], 1)
    c = jnp.sum(jnp.where(lane == e, comb_ref[0], 0.0), axis=-1, keepdims=True)
    acc_ref[...] += c * y

    @pl.when(e == pl.num_programs(2) - 1)
    def _():
        o_ref[0] = x_ref[0] + gate_ref[0] * acc_ref[...]


def _moe(h, comb, wg, wu, wd, layer, x, gate, tm=1024):
    B, S, D = x.shape
    tm = min(tm, S)
    bm = gate.shape[0]
    mod_map = (lambda b, i, e: (b, 0, 0)) if bm > 1 else (lambda b, i, e: (0, 0, 0))
    row_map = lambda b, i, e: (b, i, 0)
    return pl.pallas_call(
        _moe_kernel,
        grid=(B, S // tm, N_EXPERTS),
        in_specs=[pl.BlockSpec((1, tm, D), row_map),
                  pl.BlockSpec((1, tm, LANES), row_map),
                  pl.BlockSpec((1, 1, D, D_EXPERT), lambda b, i, e: (layer, e, 0, 0)),
                  pl.BlockSpec((1, 1, D, D_EXPERT), lambda b, i, e: (layer, e, 0, 0)),
                  pl.BlockSpec((1, 1, D_EXPERT, D), lambda b, i, e: (layer, e, 0, 0)),
                  pl.BlockSpec((1, tm, D), row_map),
                  pl.BlockSpec((1, 1, D), mod_map)],
        out_specs=pl.BlockSpec((1, tm, D), row_map),
        out_shape=jax.ShapeDtypeStruct((B, S, D), F32),
        scratch_shapes=[pltpu.VMEM((tm, D), F32)],
        compiler_params=_cparams(("parallel", "parallel", "arbitrary")),
        name="moe",
    )(h, comb, wg, wu, wd, x, gate.reshape(bm, 1, D))


MOE_TM = 1024
MOE_TG = MOE_TM // MOE_G


def _moe_rows(n_tok):
    rows = 2 * n_tok + (n_tok // MOE_TT) * N_EXPERTS * (MOE_G - 1) + N_EXPERTS * (MOE_TM - 1)
    return (rows + MOE_TM - 1) // MOE_TM * MOE_TM


def _moe_plan(ng):
    ng = ng[:, :, 0].astype(jnp.int32)
    n_tt = ng.shape[0]
    total = jnp.sum(ng, axis=0)
    region = (total + MOE_TG - 1) // MOE_TG * MOE_TG
    region_end = jnp.cumsum(region)
    region_start = region_end - region
    dst = region_start[None, :] + jnp.cumsum(ng, axis=0) - ng
    local = jnp.cumsum(ng, axis=1) - ng
    n_tiles = _moe_rows(n_tt * MOE_TT) // MOE_TM
    tile_first = jnp.arange(n_tiles, dtype=jnp.int32) * MOE_TG
    tile_expert = jnp.minimum(jnp.sum(region_end[None, :] <= tile_first[:, None], axis=1), N_EXPERTS - 1)
    n_valid = region_end[-1:] // MOE_TG
    i32 = lambda a: a.astype(jnp.int32).reshape(-1)
    return dict(ng=i32(ng), dst=i32(dst), local=i32(local), tile_groups=i32(jnp.sum(ng, axis=1)),
                pad_first=i32(region_start + total),
                pad_count=i32(region - total), tile_expert=i32(tile_expert), n_valid=i32(n_valid),
                n_tiles=n_tiles)


def _run_copies(plan_refs, tile, local_ref, sorted_ref, sem, to_sorted):
    ng_ref, dst_ref, loc_ref = plan_refs
    for e in range(N_EXPERTS):
        k = tile * N_EXPERTS + e
        loc, dst = loc_ref[k], dst_ref[k]

        def body(g, carry):
            lrows = local_ref.at[pl.ds(pl.multiple_of((loc + g) * MOE_G, MOE_G), MOE_G)]
            srows = sorted_ref.at[pl.ds(pl.multiple_of((dst + g) * MOE_G, MOE_G), MOE_G)]
            src, tgt = (lrows, srows) if to_sorted else (srows, lrows)
            pltpu.make_async_copy(src, tgt, sem).start()
            return carry
        lax.fori_loop(0, ng_ref[k], body, 0)


def _group_waits(n_groups, local_ref, sorted_ref, sem):
    def body(g, carry):
        pltpu.make_async_copy(sorted_ref.at[pl.ds(0, MOE_G)], local_ref.at[pl.ds(0, MOE_G)], sem).wait()
        return carry
    lax.fori_loop(0, n_groups, body, 0)


def _moe_dispatch_kernel(ng_ref, dst_ref, loc_ref, tot_ref, padf_ref, padc_ref, nv_ref,
                         h_ref, route_ref, xs_ref, local_ref, zero_ref, sem):
    i = pl.program_id(0)

    @pl.when(i == 0)
    def _():
        zero_ref[...] = jnp.zeros_like(zero_ref)
        n_pad = 0
        for e in range(N_EXPERTS):
            first = padf_ref[e]

            def body(g, carry):
                rows = xs_ref.at[pl.ds(pl.multiple_of((first + g) * MOE_G, MOE_G), MOE_G)]
                pltpu.make_async_copy(zero_ref.at[pl.ds(0, MOE_G)], rows, sem.at[2]).start()
                return carry
            lax.fori_loop(0, padc_ref[e], body, 0)
            n_pad = n_pad + padc_ref[e]

        def tile_copy(j):
            rows = xs_ref.at[pl.ds(pl.multiple_of(j * MOE_TM, MOE_TM), MOE_TM)]
            return pltpu.make_async_copy(zero_ref, rows, sem.at[3])

        n_tiles = xs_ref.shape[0] // MOE_TM
        lax.fori_loop(nv_ref[0], n_tiles, lambda j, c: (tile_copy(j).start(), c)[1], 0)
        _group_waits(n_pad, zero_ref, xs_ref, sem.at[2])
        lax.fori_loop(nv_ref[0], n_tiles, lambda j, c: (tile_copy(j).wait(), c)[1], 0)

    pos = lax.broadcasted_iota(jnp.int32, (MOE_LOCAL, MOE_TT), 0)
    p1 = route_ref[0:1, :].astype(jnp.int32)
    p2 = route_ref[1:2, :].astype(jnp.int32)
    pick = jnp.where((pos == p1) | (pos == p2), 1.0, 0.0).astype(BF16)
    slot = i % 2
    last = pl.num_programs(0) - 1

    @pl.when(i >= 2)
    def _():
        _group_waits(tot_ref[i - 2], local_ref.at[slot], xs_ref, sem.at[slot])

    local_ref[slot] = jnp.dot(pick, h_ref[0], preferred_element_type=F32).astype(BF16)
    _run_copies((ng_ref, dst_ref, loc_ref), i, local_ref.at[slot], xs_ref, sem.at[slot], to_sorted=True)

    @pl.when(i == last)
    def _():
        _group_waits(tot_ref[i], local_ref.at[slot], xs_ref, sem.at[slot])

    @pl.when((i == last) & (last >= 1))
    def _():
        _group_waits(tot_ref[i - 1], local_ref.at[1 - slot], xs_ref, sem.at[1 - slot])


def _moe_dispatch(h, route_t, plan):
    B, S, D = h.shape
    nt = S // MOE_TT
    grid_spec = pltpu.PrefetchScalarGridSpec(
        num_scalar_prefetch=7,
        grid=(B * nt,),
        in_specs=[pl.BlockSpec((1, MOE_TT, D), lambda i, *_: (i // nt, i % nt, 0)),
                  pl.BlockSpec((ROUTE_FIELDS, MOE_TT), lambda i, *_: (0, i))],
        out_specs=pl.BlockSpec(memory_space=pl.ANY),
        scratch_shapes=[pltpu.VMEM((2, MOE_LOCAL, D), BF16), pltpu.VMEM((MOE_TM, D), BF16),
                        pltpu.SemaphoreType.DMA((4,))])
    return pl.pallas_call(
        _moe_dispatch_kernel,
        grid_spec=grid_spec,
        out_shape=jax.ShapeDtypeStruct((plan["n_tiles"] * MOE_TM, D), BF16),
        compiler_params=_cparams(("arbitrary",)),
        name="moe_dispatch",
    )(plan["ng"], plan["dst"], plan["local"], plan["tile_groups"], plan["pad_first"], plan["pad_count"],
      plan["n_valid"], h, route_t)


def _moe_grouped_kernel(te_ref, nv_ref, x_ref, wg_ref, wu_ref, wd_ref, o_ref, wgb_ref, wub_ref, wdb_ref):
    j = pl.program_id(0)
    used = j < nv_ref[0]

    @pl.when(used & ((j == 0) | (te_ref[j] != te_ref[jnp.maximum(j - 1, 0)])))
    def _():
        wgb_ref[...] = wg_ref[0, 0].astype(BF16)
        wub_ref[...] = wu_ref[0, 0].astype(BF16)
        wdb_ref[...] = wd_ref[0, 0].astype(BF16)

    @pl.when(used)
    def _():
        x = x_ref[...]
        a = jnp.dot(x, wgb_ref[...], preferred_element_type=F32)
        u = jnp.dot(x, wub_ref[...], preferred_element_type=F32)
        he = (a * jax.nn.sigmoid(a) * u).astype(BF16)
        o_ref[...] = jnp.dot(he, wdb_ref[...], preferred_element_type=F32).astype(o_ref.dtype)

    @pl.when(jnp.logical_not(used))
    def _():
        o_ref[...] = jnp.zeros_like(o_ref)


def _moe_grouped(xs, plan, wg, wu, wd, layer):
    n_tiles = plan["n_tiles"]
    D = D_MODEL
    tile = lambda j, nv: jnp.minimum(j, nv[0] - 1)
    grid_spec = pltpu.PrefetchScalarGridSpec(
        num_scalar_prefetch=2,
        grid=(n_tiles,),
        in_specs=[pl.BlockSpec((MOE_TM, D), lambda j, te, nv: (tile(j, nv), 0)),
                  pl.BlockSpec((1, 1, D, D_EXPERT), lambda j, te, nv: (layer, te[tile(j, nv)], 0, 0)),
                  pl.BlockSpec((1, 1, D, D_EXPERT), lambda j, te, nv: (layer, te[tile(j, nv)], 0, 0)),
                  pl.BlockSpec((1, 1, D_EXPERT, D), lambda j, te, nv: (layer, te[tile(j, nv)], 0, 0))],
        out_specs=pl.BlockSpec((MOE_TM, D), lambda j, te, nv: (j, 0)),
        scratch_shapes=[pltpu.VMEM((D, D_EXPERT), BF16), pltpu.VMEM((D, D_EXPERT), BF16),
                        pltpu.VMEM((D_EXPERT, D), BF16)])
    return pl.pallas_call(
        _moe_grouped_kernel,
        grid_spec=grid_spec,
        out_shape=jax.ShapeDtypeStruct((n_tiles * MOE_TM, D), BF16),
        compiler_params=_cparams(("arbitrary",)),
        name="moe_grouped",
    )(plan["tile_expert"], plan["n_valid"], xs, wg, wu, wd)


def _moe_combine_kernel(final_norm, ng_ref, dst_ref, loc_ref, tot_ref, ys_ref, route_ref, x_ref, gate_ref,
                        *rest):
    if final_norm:
        fg_ref, o_ref, local_ref, sem = rest
    else:
        o_ref, local_ref, sem = rest
    i = pl.program_id(0)
    slot = i % 2
    plan_refs = (ng_ref, dst_ref, loc_ref)

    @pl.when(i == 0)
    def _():
        local_ref[...] = jnp.zeros_like(local_ref)
        _run_copies(plan_refs, 0, local_ref.at[0], ys_ref, sem.at[0], to_sorted=False)

    @pl.when(i + 1 < pl.num_programs(0))
    def _():
        _run_copies(plan_refs, i + 1, local_ref.at[1 - slot], ys_ref, sem.at[1 - slot], to_sorted=False)

    _group_waits(tot_ref[i], local_ref.at[slot], ys_ref, sem.at[slot])
    pos = lax.broadcasted_iota(jnp.int32, (MOE_TT, MOE_LOCAL), 1)
    route = route_ref[...]
    p1 = route[:, 0:1].astype(jnp.int32)
    p2 = route[:, 1:2].astype(jnp.int32)
    weigh = (jnp.where(pos == p1, route[:, 2:3], 0.0) + jnp.where(pos == p2, route[:, 3:4], 0.0)).astype(BF16)
    y = jnp.dot(weigh, local_ref[slot], preferred_element_type=F32)
    o = x_ref[0] + gate_ref[0] * y
    if final_norm:
        o = o * lax.rsqrt(jnp.mean(o * o, axis=-1, keepdims=True) + EPS) * fg_ref[...]
    o_ref[0] = o


def _moe_combine(ys, route_c, plan, x, gate, final_g=None):
    B, S, D = x.shape
    nt = S // MOE_TT
    in_specs = [pl.BlockSpec(memory_space=pl.ANY),
                pl.BlockSpec((MOE_TT, LANES), lambda i, *_: (i, 0)),
                pl.BlockSpec((1, MOE_TT, D), lambda i, *_: (i // nt, i % nt, 0)),
                pl.BlockSpec((1, 1, D), lambda i, *_: (i // nt, 0, 0))]
    args = [ys, route_c, x, gate.reshape(B, 1, D)]
    if final_g is not None:
        in_specs.append(pl.BlockSpec((1, D), lambda i, *_: (0, 0)))
        args.append(final_g.reshape(1, D))
    grid_spec = pltpu.PrefetchScalarGridSpec(
        num_scalar_prefetch=4,
        grid=(B * nt,),
        in_specs=in_specs,
        out_specs=pl.BlockSpec((1, MOE_TT, D), lambda i, *_: (i // nt, i % nt, 0)),
        scratch_shapes=[pltpu.VMEM((2, MOE_LOCAL, D), BF16), pltpu.SemaphoreType.DMA((2,))])
    return pl.pallas_call(
        functools.partial(_moe_combine_kernel, final_g is not None),
        grid_spec=grid_spec,
        out_shape=jax.ShapeDtypeStruct((B, S, D), F32),
        compiler_params=_cparams(("arbitrary",)),
        name="moe_combine",
    )(plan["ng"], plan["dst"], plan["local"], plan["tile_groups"], *args)


def _dft_tables():
    n1, n2, n = FFT_N1, FFT_N2, FFT_N1 * FFT_N2
    k1 = np.arange(n1)
    f1 = np.exp(-2j * np.pi * np.outer(k1, np.arange(n1)) / n1)
    tw = np.exp(-2j * np.pi * np.outer(np.arange(n2), k1) / n)
    ftw = f1[None, :, :] * tw[:, :, None]
    half = n1 // 2
    fh = ftw[:, :, :half]
    g_fwd = np.concatenate([np.concatenate([fh.real, -fh.imag], axis=2),
                            np.concatenate([fh.imag, fh.real], axis=2)], axis=1)
    back = ftw[:, :, ::-1][:, :, :half].copy()
    back[0] = np.roll(ftw[0], -1, axis=1)[:, ::-1][:, :half]
    back[0][:, 0] = 0.0
    fk = np.concatenate([fh, back], axis=2)
    g_real = np.concatenate([fk.real, fk.imag], axis=1)
    gi = np.conj(np.transpose(fh, (0, 2, 1))) / n
    g_inv = np.concatenate([np.concatenate([gi.real, -gi.imag], axis=2),
                            np.concatenate([gi.imag, gi.real], axis=2)], axis=1)
    f2 = np.exp(-2j * np.pi * np.outer(np.arange(n2), np.arange(n2)) / n2)
    f2_fwd = np.block([[f2.real, -f2.imag], [f2.imag, f2.real]])
    f2c = np.conj(f2)
    f2_inv = np.block([[f2c.real, -f2c.imag], [f2c.imag, f2c.real]])
    as_bf = lambda a: jnp.asarray(a, dtype=F32).astype(BF16)
    return as_bf(g_fwd), as_bf(g_real), as_bf(g_inv), as_bf(f2_fwd), as_bf(f2_inv)


def _fft_fast_stage(stage_ref, k1, f2):
    slab = STAGE_PITCH
    m = jnp.concatenate([stage_ref[pl.ds(k1, FFT_N2, stride=slab), :],
                         stage_ref[pl.ds(FFT_N1 + k1, FFT_N2, stride=slab), :]], axis=0)
    return jnp.dot(f2, m.astype(BF16), preferred_element_type=F32)


def _filter_fft_kernel(hf_ref, hb_ref, inv_ref, g_ref, f2_ref, h_ref, stage_ref):
    slab = STAGE_PITCH
    half = FFT_N1 // 2
    for n2 in range(FFT_N2):
        x = jnp.concatenate([hf_ref[pl.ds(n2, half, stride=FFT_N2), :],
                             hb_ref[pl.ds((FFT_N2 - n2) % FFT_N2, half, stride=FFT_N2), :]], axis=0)
        stage_ref[n2 * slab:n2 * slab + 2 * FFT_N1, :] = jnp.dot(g_ref[n2], x.astype(BF16),
                                                          preferred_element_type=F32)
    f2 = f2_ref[...]
    inv = inv_ref[...]
    for k1 in range(FFT_N1):
        h_ref[0, k1] = (_fft_fast_stage(stage_ref, k1, f2) * inv).astype(h_ref.dtype)


def _filter_fft(taps, inv_norm, g_real, f2_fwd, ct=LANES):
    L, cols = taps.shape
    C = D_CH
    n_ord = cols // (2 * C)
    nc = C // ct
    once = pl.Buffered(1)
    return pl.pallas_call(
        _filter_fft_kernel,
        grid=(n_ord, nc),
        in_specs=[pl.BlockSpec((L, ct), lambda o, c: (0, o * nc + c)),
                  pl.BlockSpec((L, ct), lambda o, c: (0, (n_ord + o) * nc + c)),
                  pl.BlockSpec((1, ct), lambda o, c: (0, o * nc + c)),
                  pl.BlockSpec(g_real.shape, lambda o, c: (0, 0, 0), pipeline_mode=once),
                  pl.BlockSpec(f2_fwd.shape, lambda o, c: (0, 0), pipeline_mode=once)],
        out_specs=pl.BlockSpec((1, FFT_N1, 2 * FFT_N2, ct), lambda o, c: (o, 0, 0, c)),
        out_shape=jax.ShapeDtypeStruct((n_ord, FFT_N1, 2 * FFT_N2, C), BF16),
        scratch_shapes=[pltpu.VMEM((FFT_N2 * STAGE_PITCH, ct), F32)],
        compiler_params=_cparams(("parallel", "parallel")),
        name="filter_fft",
    )(taps, taps, inv_norm, g_real, f2_fwd)


class _RowSets:
    def __init__(self, ref, member):
        self.ref, self.member, self.loaded = ref, member, {}

    def __call__(self, start):
        if start not in self.loaded:
            self.loaded = {k: v for k, v in self.loaded.items() if abs(k - start) <= 2}
            self.loaded[start] = self.ref[self.member, pl.ds(start, FFT_N1 // 2, stride=FFT_N2), :]
        return self.loaded[start]


def _strided_rows(at, n2, taps):
    half = FFT_N1 // 2
    cur = at(n2)
    if taps is None:
        return cur
    w_ref, b_ref = taps
    n1 = lax.broadcasted_iota(jnp.int32, cur.shape, 0)
    if n2 > 0:
        prev = at(n2 - 1)
    else:
        prev = jnp.where(n1 == 0, 0.0, pltpu.roll(at(FFT_N2 - 1), 1, axis=0))
    if n2 < FFT_N2 - 1:
        nxt = at(n2 + 1)
    else:
        nxt = jnp.where(n1 == half - 1, 0.0, pltpu.roll(at(0), half - 1, axis=0))
    return prev * w_ref[0:1, :] + cur * w_ref[1:2, :] + nxt * w_ref[2:3, :] + b_ref[...]


def _hyena_conv_kernel(conv_z, *refs):
    it = iter(refs)
    z_ref, gate_ref = next(it), next(it)
    z_taps = (next(it), next(it)) if conv_z else None
    gate_taps = (next(it), next(it))
    hb_ref, spec_ref, gf_ref, gi_ref, f2f_ref, f2i_ref, o_ref, stage_ref = it
    half = FFT_N1 // 2
    slab = STAGE_PITCH
    z_sets = [_RowSets(z_ref, m) for m in range(2)]
    for n2 in range(FFT_N2):
        x = jnp.concatenate([_strided_rows(z_sets[m], n2, z_taps) for m in range(2)], axis=0).astype(BF16)
        stage_ref[n2 * slab:n2 * slab + 2 * FFT_N1, :] = jnp.dot(gf_ref[n2], x, preferred_element_type=F32)
    f2f = f2f_ref[...]
    f2i = f2i_ref[...]
    for k1 in range(FFT_N1):
        zf = _fft_fast_stage(stage_ref, k1, f2f)
        zr, zi = zf[:FFT_N2], zf[FFT_N2:]
        hr = spec_ref[0, k1, :FFT_N2, :].astype(F32)
        hi = spec_ref[0, k1, FFT_N2:, :].astype(F32)
        p = jnp.concatenate([zr * hr - zi * hi, zr * hi + zi * hr], axis=0).astype(BF16)
        q = jnp.dot(f2i, p, preferred_element_type=F32)
        stage_ref[pl.ds(k1, FFT_N2, stride=slab), :] = q[:FFT_N2]
        stage_ref[pl.ds(FFT_N1 + k1, FFT_N2, stride=slab), :] = q[FFT_N2:]
    hb = hb_ref[...]
    z_sets = [_RowSets(z_ref, m) for m in range(2)]
    gate_sets = [_RowSets(gate_ref, m) for m in range(2)]
    for n2 in range(FFT_N2):
        y_in = stage_ref[n2 * slab:n2 * slab + 2 * FFT_N1, :].astype(BF16)
        y = jnp.dot(gi_ref[n2], y_in, preferred_element_type=F32)
        for m in range(2):
            zm = _strided_rows(z_sets[m], n2, z_taps)
            gm = _strided_rows(gate_sets[m], n2, gate_taps)
            o_ref[m, pl.ds(n2, half, stride=FFT_N2), :] = gm * (y[m * half:(m + 1) * half] + zm * hb)


def _hyena_conv(z, z_blk, gate, gate_blk, short_w, short_b, conv_z, hbias, spec, order, tabs, ct=LANES):
    B, L, _ = z.shape
    C = D_CH
    g_fwd, _, g_inv, f2_fwd, f2_inv = tabs
    once = pl.Buffered(1)
    const3 = lambda a: pl.BlockSpec(a.shape, lambda c, p: (0, 0, 0), pipeline_mode=once)
    const2 = lambda a: pl.BlockSpec(a.shape, lambda c, p: (0, 0), pipeline_mode=once)
    taps_specs = lambda blk: [pl.BlockSpec((3, ct), lambda c, p: (0, blk + c)),
                              pl.BlockSpec((1, ct), lambda c, p: (0, blk + c))]
    in_specs = [pl.BlockSpec((2, L, ct), lambda c, p: (p, 0, z_blk + c)),
                pl.BlockSpec((2, L, ct), lambda c, p: (p, 0, gate_blk + c))]
    args = [z, gate]
    if conv_z:
        in_specs += taps_specs(z_blk)
        args += [short_w, short_b]
    in_specs += taps_specs(gate_blk)
    args += [short_w, short_b]
    in_specs += [pl.BlockSpec((1, ct), lambda c, p: (0, c)),
                 pl.BlockSpec((1, FFT_N1, 2 * FFT_N2, ct), lambda c, p: (order, 0, 0, c), pipeline_mode=once),
                 const3(g_fwd), const3(g_inv), const2(f2_fwd), const2(f2_inv)]
    args += [hbias, spec, g_fwd, g_inv, f2_fwd, f2_inv]
    return pl.pallas_call(
        functools.partial(_hyena_conv_kernel, conv_z),
        grid=(C // ct, B // 2),
        in_specs=in_specs,
        out_specs=pl.BlockSpec((2, L, ct), lambda c, p: (p, 0, c)),
        out_shape=jax.ShapeDtypeStruct((B, L, C), F32),
        scratch_shapes=[pltpu.VMEM((FFT_N2 * STAGE_PITCH, ct), F32)],
        compiler_params=_cparams(("parallel", "arbitrary")),
        name="hyena_conv",
    )(*args)


def _hyena_filters(L, w1, b1, f1, w2, b2, f2, w3, b3):
    t = jnp.arange(L, dtype=F32)
    tn = t / max(L - 1, 1)
    bands = jnp.linspace(1e-4, HY_BANDS - 1, HY_BANDS, dtype=F32)
    ang = 2.0 * math.pi * t[:, None] * bands[None] / L
    feats = jnp.concatenate([tn[:, None], jnp.cos(ang), jnp.sin(ang)], axis=-1)
    h = jnp.sin(f1 * (feats @ w1 + b1))
    h = jnp.sin(f2 * (h @ w2 + b2))
    deltas = jnp.abs(jnp.linspace(HY_MIN_DECAY, HY_MAX_DECAY, D_CH, dtype=F32))
    decay = jnp.exp(-tn[:, None] * deltas[None])
    n_rep = w3.shape[1] // D_CH
    taps, l1 = _filter_taps(h, w3, b3[None], decay)
    l1 = l1[0, :n_rep // 2 * D_CH] + l1[0, n_rep // 2 * D_CH:]
    return taps, (1.0 / (l1 + EPS))[None]


def _filter_taps_kernel(h_ref, w_ref, b_ref, decay_ref, taps_ref, l1_ref):
    taps = (jnp.dot(h_ref[...], w_ref[...], preferred_element_type=F32) + b_ref[...]) * decay_ref[...]
    taps_ref[...] = taps
    l1_ref[...] = jnp.sum(jnp.abs(taps), axis=0, keepdims=True)


def _filter_taps(h, w, b, decay):
    L, K = h.shape
    N = w.shape[1]
    return pl.pallas_call(
        _filter_taps_kernel,
        grid=(N // D_CH,),
        in_specs=[pl.BlockSpec((L, K), lambda j: (0, 0)),
                  pl.BlockSpec((K, D_CH), lambda j: (0, j)),
                  pl.BlockSpec((1, D_CH), lambda j: (0, j)),
                  pl.BlockSpec((L, D_CH), lambda j: (0, 0))],
        out_specs=[pl.BlockSpec((L, D_CH), lambda j: (0, j)), pl.BlockSpec((1, D_CH), lambda j: (0, j))],
        out_shape=[jax.ShapeDtypeStruct((L, N), F32), jax.ShapeDtypeStruct((1, N), F32)],
        compiler_params=_cparams(("parallel",)),
        name="filter_taps",
    )(h, w, b, decay)


def _dup_heads(w):
    a, b = w[:, :HEAD_DIM], w[:, HEAD_DIM:]
    return jnp.concatenate([a, a, b, b], axis=1)


def _rope_tables(S):
    t = jnp.arange(S)
    row = (t // GRID_W).astype(F32)
    col = (t % GRID_W).astype(F32)
    half = HEAD_DIM // 2
    inv = ROPE_THETA ** (-jnp.arange(0, half, 2, dtype=F32) / half)
    ar = row[:, None] * inv[None]
    ac = col[:, None] * inv[None]
    cos = jnp.concatenate([jnp.cos(ar), jnp.cos(ar), jnp.cos(ac), jnp.cos(ac)], axis=-1)
    sin = jnp.concatenate([-jnp.sin(ar), jnp.sin(ar), -jnp.sin(ac), jnp.sin(ac)], axis=-1)
    return jnp.tile(cos, (1, 2)), jnp.tile(sin, (1, 2))


def _head_mean_matrix(width):
    blk = np.kron(np.eye(width // HEAD_DIM), np.full((HEAD_DIM, HEAD_DIM), 1.0 / HEAD_DIM))
    return jnp.asarray(blk, dtype=F32).astype(BF16)


def kernel(x, c, ctx, c_ctx, w_ada, b_ada, norm_g, final_g, w_in_even, w_out_even, a_sink, b_rpb, w_in_odd, w_out_odd, c_qnorm, c_knorm, hy_short_w, hy_short_b, hy_w1, hy_b1, hy_f1, hy_w2, hy_b2, hy_f2, hy_w3, hy_b3, hy_bias, w_router, b_router, moe_wg, moe_wu, moe_wd):
    B, S, D = x.shape
    depth = w_ada.shape[0]
    rope = _rope_tables(S)
    wr_pad = jnp.pad(w_router.astype(F32), ((0, 0), (0, LANES - N_EXPERTS)))
    wr_hi = wr_pad.astype(BF16)
    router = (b_router.astype(F32), wr_hi, (wr_pad - wr_hi.astype(F32)).astype(BF16))

    mod_in = jnp.concatenate([jax.nn.silu(c), jax.nn.silu(c_ctx)[None],
                              jnp.zeros((8 - B - 1, D), F32)], axis=0)
    xc = ctx
    for l in range(depth):
        need_ctx = l < depth - 1
        mod = _mm_f32(mod_in, w_ada[l], 1536) + b_ada[l]
        mx = mod[:B].reshape(B, 6, D)
        mc = mod[B].reshape(6, D)
        i = l // 2
        if l % 2 == 0:
            w = w_in_even[i].astype(BF16)
            w_all = jnp.concatenate([w[:, :512], _dup_heads(w[:, 512:640]), _dup_heads(w[:, 640:768]),
                                     w[:, 768:]], axis=1)
            segs_x = ((512, "rope", Q_MULT), (256, "rope", 1.0), (256, "plain", 1.0),
                      (512, "plain", Q_MULT), (512, "plain", 1.0), (512, "plain", 1.0))
            sink = a_sink[i].astype(F32) * LOG2_E
            bias = _nbr_bias_table(b_rpb[i].astype(F32) * LOG2_E, S // GRID_W)
            aq, akd, avd, bq, bk, bv = _norm_proj(x, norm_g[l, 0], mx[:, 0], mx[:, 1], w_all, segs_x,
                                                  rope=rope)
            segs_c = tuple((wd, "plain", m) for wd, _, m in segs_x)
            caq, cakd, cavd, cbq, cbk, cbv = _norm_proj(xc, norm_g[l, 0], mc[0:1], mc[1:2], w_all, segs_c)
            ya = _window_attn(aq, akd, avd, cakd, cavd, sink)
            yb = _nbr_attn(bq, bk, bv, cbk, cbv, bias)
            ys = [ya, yb]
            w_out = w_out_even[i].astype(BF16)
            if need_ctx:
                yc = [_ctx_attn(sink, caq, cakd, cavd, cbq, cbk, cbv)]
        else:
            w = w_in_odd[i].astype(BF16)
            w_all = jnp.concatenate([w[:, :512], _dup_heads(w[:, 512:640]), _dup_heads(w[:, 640:768]),
                                     w[:, 768:]], axis=1)
            gains = jnp.concatenate([jnp.tile(c_qnorm[i], 8), jnp.tile(c_knorm[i], 4)])[None].astype(F32)
            norm = (_head_mean_matrix(512), gains)
            segs_x = ((512, "normrope", Q_MULT), (256, "normrope", 1.0), (256, "plain", 1.0),
                      (3 * D_CH, "plain", 1.0, F32))
            qx, kxd, vxd, ux = _norm_proj(x, norm_g[l, 0], mx[:, 0], mx[:, 1], w_all, segs_x,
                                          rope=rope, norm=norm)
            w_c = w_all[:, 512:1024]
            norm_c = (_head_mean_matrix(512), jnp.tile(c_knorm[i], 4)[None].astype(F32))
            kcd, vcd = _norm_proj(xc, norm_g[l, 0], mc[0:1], mc[1:2], w_c,
                                  ((256, "norm", 1.0), (256, "plain", 1.0)), norm=norm_c)
            logit_bound = (1.02 * HEAD_DIM * Q_MULT * jnp.max(jnp.abs(c_qnorm[i]))
                           * jnp.max(jnp.abs(c_knorm[i])))
            y_attn = lax.cond(logit_bound <= FULL_NOMAX_LOG2_BOUND,
                              lambda *a: _full_attn(*a, bounded=True),
                              lambda *a: _full_attn(*a, bounded=False),
                              qx, kxd, vxd, kcd, vcd)
            sw, sb = hy_short_w[i].astype(F32), hy_short_b[i].astype(F32)[None]
            tabs = _dft_tables()
            taps, inv_norm = _hyena_filters(S, hy_w1[i], hy_b1[i], hy_f1[i], hy_w2[i], hy_b2[i],
                                            hy_f2[i], hy_w3[i], hy_b3[i])
            spec = _filter_fft(taps, inv_norm, tabs[1], tabs[3])
            blocks = D_CH // LANES
            z = _hyena_conv(ux, 0, ux, blocks, sw, sb, True, hy_bias[i, 0:1], spec, 0, tabs)
            z = _hyena_conv(z, 0, ux, 2 * blocks, sw, sb, False, hy_bias[i, 1:2], spec, 1, tabs)
            ys = [y_attn, z]
            w_out = w_out_odd[i].astype(BF16)
            if need_ctx:
                raise NotImplementedError("context update of an odd layer is not needed at this depth")

        x, hx, route_t, route_c, ng = _out_proj(ys, w_out, x, mx[:, 2], norm_g[l, 1], mx[:, 3], mx[:, 4],
                                                router, sparse=True)
        if need_ctx:
            xc, hc, comb_c = _out_proj(yc, w_out, xc, mc[2:3], norm_g[l, 1], mc[3:4], mc[4:5], router,
                                       sparse=False)
            flat = lambda a: a.reshape(1, -1, a.shape[-1])
            xc = _moe(flat(hc), flat(comb_c), moe_wg, moe_wu, moe_wd, l, flat(xc), mc[5:6]).reshape(xc.shape)
        plan = _moe_plan(ng)
        sorted_rows = _moe_dispatch(hx, route_t, plan)
        x = _moe_combine(_moe_grouped(sorted_rows, plan, moe_wg, moe_wu, moe_wd, l), route_c, plan, x, mx[:, 5],
                         final_g=None if need_ctx else final_g)
    return x
```

```python
import functools
import math

import numpy as np
import jax
import jax.numpy as jnp
from jax import lax
from jax.experimental import pallas as pl
from jax.experimental.pallas import tpu as pltpu

F32 = jnp.float32
BF16 = jnp.bfloat16

D_MODEL = 1024
GRID_W = 64
CTX_LEN = 256
HEAD_DIM = 64
ROPE_THETA = 10000.0
EPS = 1e-6
ATTN_SCALE = HEAD_DIM ** -0.5
LOG2_E = math.log2(math.e)
Q_MULT = ATTN_SCALE * LOG2_E
A_WINDOW = 128
A_BLOCK = 128
B_WIN_H = 8
B_WIN_W = 16
D_CH = 512
HY_BANDS = 16
HY_MAX_DECAY = math.log(1e-2) / 0.3
HY_MIN_DECAY = math.log(1e-2) / 1.5
N_EXPERTS = 16
N_GROUPS = 4
EXPERTS_PER_GROUP = N_EXPERTS // N_GROUPS
D_EXPERT = 512

LANES = 128
NEG = -1e30
VMEM_LIMIT = 48 * 1024 * 1024

FFT_N1 = 64
FFT_N2 = 128
STAGE_PITCH = 2 * FFT_N1 + 8


def _cparams(sem):
    return pltpu.CompilerParams(dimension_semantics=sem, vmem_limit_bytes=VMEM_LIMIT)


def _mm_f32_kernel(x_ref, w_ref, o_ref):
    o_ref[...] = jnp.dot(x_ref[...], w_ref[...], preferred_element_type=F32)


def _mm_f32(x, w, tn):
    M, K = x.shape
    N = w.shape[1]
    return pl.pallas_call(
        _mm_f32_kernel,
        grid=(N // tn,),
        in_specs=[pl.BlockSpec((M, K), lambda j: (0, 0)),
                  pl.BlockSpec((K, tn), lambda j: (0, j))],
        out_specs=pl.BlockSpec((M, tn), lambda j: (0, j)),
        out_shape=jax.ShapeDtypeStruct((M, N), F32),
        compiler_params=_cparams(("arbitrary",)),
        name="mm_f32",
    )(x, w)


def _swap16(y):
    n = y.shape[-1]
    lane = lax.broadcasted_iota(jnp.int32, y.shape, y.ndim - 1)
    up = pltpu.roll(y, n - 16, axis=y.ndim - 1)
    dn = pltpu.roll(y, 16, axis=y.ndim - 1)
    return jnp.where((lane % 32) < 16, up, dn)


def _tile_lanes(t, width):
    reps = width // t.shape[-1]
    return t if reps == 1 else jnp.concatenate([t] * reps, axis=-1)


def _norm_proj_kernel(segs, has_rope, has_norm, *refs):
    it = iter(refs)
    x_ref, g_ref, shift_ref, scale_ref, w_ref = (next(it) for _ in range(5))
    cos_ref = sin_ref = bd_ref = gain_ref = None
    if has_rope:
        cos_ref, sin_ref = next(it), next(it)
    if has_norm:
        bd_ref, gain_ref = next(it), next(it)
    out_refs = list(it)

    x = x_ref[0]
    ms = jnp.mean(x * x, axis=-1, keepdims=True)
    h = x * lax.rsqrt(ms + EPS) * g_ref[...]
    h = h * (1.0 + scale_ref[0]) + shift_ref[0]
    y = jnp.dot(h.astype(BF16), w_ref[...], preferred_element_type=F32)

    off = 0
    goff = 0
    for (width, kind, mult, *_), o_ref in zip(segs, out_refs):
        ys = y[:, off:off + width]
        if kind in ("norm", "normrope"):
            bd = bd_ref[...][:width, :width]
            hms = jnp.dot((ys * ys).astype(BF16), bd, preferred_element_type=F32)
            ys = ys * lax.rsqrt(hms + EPS) * gain_ref[:, goff:goff + width]
            goff += width
        if kind in ("rope", "normrope"):
            c = _tile_lanes(cos_ref[...], width)
            s = _tile_lanes(sin_ref[...], width)
            ys = ys * c + _swap16(ys) * s
        if mult != 1.0:
            ys = ys * mult
        o_ref[0] = ys.astype(o_ref.dtype)
        off += width


def _norm_proj(x, g, shift, scale, w, segs, rope=None, norm=None, tm=512):
    B, S, D = x.shape
    N = w.shape[1]
    tm = min(tm, S)
    bm = shift.shape[0]
    mod_map = (lambda b, i: (b, 0, 0)) if bm > 1 else (lambda b, i: (0, 0, 0))
    args = [x, g.reshape(1, D), shift.reshape(bm, 1, D), scale.reshape(bm, 1, D), w]
    in_specs = [pl.BlockSpec((1, tm, D), lambda b, i: (b, i, 0)),
                pl.BlockSpec((1, D), lambda b, i: (0, 0)),
                pl.BlockSpec((1, 1, D), mod_map),
                pl.BlockSpec((1, 1, D), mod_map),
                pl.BlockSpec((D, N), lambda b, i: (0, 0))]
    if rope is not None:
        args += [rope[0], rope[1]]
        in_specs += [pl.BlockSpec((tm, LANES), lambda b, i: (i, 0))] * 2
    if norm is not None:
        args += [norm[0], norm[1]]
        in_specs += [pl.BlockSpec(norm[0].shape, lambda b, i: (0, 0)),
                     pl.BlockSpec(norm[1].shape, lambda b, i: (0, 0))]
    out_shape = [jax.ShapeDtypeStruct((B, S, seg[0]), seg[3] if len(seg) > 3 else BF16) for seg in segs]
    out_specs = [pl.BlockSpec((1, tm, seg[0]), lambda b, i: (b, i, 0)) for seg in segs]
    return pl.pallas_call(
        functools.partial(_norm_proj_kernel, segs, rope is not None, norm is not None),
        grid=(B, S // tm),
        in_specs=in_specs,
        out_specs=out_specs,
        out_shape=out_shape,
        compiler_params=_cparams(("parallel", "parallel")),
        name="norm_proj",
    )(*args)


def _half_mask(shape):
    return lax.broadcasted_iota(jnp.int32, shape, len(shape) - 1) < HEAD_DIM


def _stack_halves(qp):
    lo = _half_mask(qp.shape)
    zero = jnp.zeros_like(qp)
    return jnp.concatenate([jnp.where(lo, qp, zero), jnp.where(lo, zero, qp)], axis=0)


def _merge_halves(o, m):
    return jnp.where(_half_mask((m, LANES)), o[:m], o[m:])


def _scores(q, k):
    return lax.dot_general(q, k, (((1,), (1,)), ((), ())), preferred_element_type=F32)


def _joint_softmax_pv(score_parts, value_parts, extra_logit=None):
    m = functools.reduce(jnp.maximum, [jnp.max(s, axis=-1, keepdims=True) for s in score_parts])
    if extra_logit is not None:
        m = jnp.maximum(m, extra_logit)
    den = jnp.exp2(extra_logit - m) if extra_logit is not None else 0.0
    acc = None
    for s, v in zip(score_parts, value_parts):
        p = jnp.exp2(s - m)
        den = den + jnp.sum(p, axis=-1, keepdims=True)
        pv = jnp.dot(p.astype(BF16), v, preferred_element_type=F32)
        acc = pv if acc is None else acc + pv
    return acc / den


def _sink_column(sink_ref, first_head, n_heads, rows_per_head):
    rows = lax.broadcasted_iota(jnp.int32, (n_heads * rows_per_head, 1), 0)
    col = jnp.zeros((n_heads * rows_per_head, 1), F32)
    for j in range(n_heads):
        in_head = (rows >= j * rows_per_head) & (rows < (j + 1) * rows_per_head)
        col = jnp.where(in_head, sink_ref[first_head + j], col)
    return col


def _window_attn_kernel(sink_ref, q_ref, kp_ref, kc_ref, kn_ref, vp_ref, vc_ref, vn_ref,
                        ck_ref, cv_ref, mask_ref, o_ref):
    blk = A_BLOCK
    q = q_ref[0]
    mask = mask_ref[0]
    outs = []
    for g in range(2):
        ls = slice(g * LANES, (g + 1) * LANES)
        k_loc = jnp.concatenate([kp_ref[0][:, ls], kc_ref[0][:, ls], kn_ref[0][:, ls]], axis=0)
        v_loc = jnp.concatenate([vp_ref[0][:, ls], vc_ref[0][:, ls], vn_ref[0][:, ls]], axis=0)
        qs = jnp.concatenate([_stack_halves(q[:, (2 * g + j) * LANES:(2 * g + j + 1) * LANES])
                              for j in range(2)], axis=0)
        s_loc = _scores(qs, k_loc) + mask
        s_ctx = _scores(qs, ck_ref[0][:, ls])
        sink = _sink_column(sink_ref, 4 * g, 4, blk)
        o = _joint_softmax_pv([s_loc, s_ctx], [v_loc, cv_ref[0][:, ls]], sink)
        outs += [_merge_halves(o[:2 * blk], blk), _merge_halves(o[2 * blk:], blk)]
    o_ref[0] = jnp.concatenate(outs, axis=-1).astype(o_ref.dtype)


def _window_mask_table(seq_len):
    blk = A_BLOCK
    rows = np.arange(4 * blk)[:, None] % blk
    rel = np.arange(3 * blk)[None, :] - blk
    near = np.abs(rows - rel) <= A_WINDOW
    tabs = [near & (rel >= 0), near, near & (rel < blk)]
    return jnp.asarray(np.where(np.stack(tabs), 0.0, NEG).astype(np.float32))


def _window_attn(q, kd, vd, ckd, cvd, sink):
    B, S, _ = q.shape
    nb = S // A_BLOCK
    mask = _window_mask_table(S)
    pat = lambda i: jnp.where(i == 0, 0, jnp.where(i == nb - 1, 2, 1))
    kv_spec = lambda f: pl.BlockSpec((1, A_BLOCK, 2 * LANES), f)
    prev_map = lambda b, i: (b, jnp.maximum(i - 1, 0), 0)
    cur_map = lambda b, i: (b, i, 0)
    next_map = lambda b, i: (b, jnp.minimum(i + 1, nb - 1), 0)
    ctx_spec = pl.BlockSpec((1, CTX_LEN, 2 * LANES), lambda b, i: (b, 0, 0))
    return pl.pallas_call(
        _window_attn_kernel,
        grid=(B, nb),
        in_specs=[pl.BlockSpec(memory_space=pltpu.SMEM),
                  pl.BlockSpec((1, A_BLOCK, 4 * LANES), cur_map),
                  kv_spec(prev_map), kv_spec(cur_map), kv_spec(next_map),
                  kv_spec(prev_map), kv_spec(cur_map), kv_spec(next_map),
                  ctx_spec, ctx_spec,
                  pl.BlockSpec((1,) + mask.shape[1:], lambda b, i: (pat(i), 0, 0))],
        out_specs=pl.BlockSpec((1, A_BLOCK, 4 * LANES), cur_map),
        out_shape=jax.ShapeDtypeStruct((B, S, 4 * LANES), BF16),
        compiler_params=_cparams(("parallel", "parallel")),
        name="window_attn",
    )(sink, q, kd, kd, kd, vd, vd, vd, ckd, cvd, mask)


NBR_ROWS = 4
NBR_KROWS = 12
NBR_PAIRS = 4


def _nbr_start_row(i, n_rows):
    return jnp.clip(i * NBR_ROWS - B_WIN_H // 2, 0, n_rows - NBR_KROWS)


def _nbr_attn_kernel(n_rows, q_ref, k_ref, v_ref, ck_ref, cv_ref, bias_ref, o_ref):
    i = pl.program_id(2)
    nq = NBR_ROWS * GRID_W
    nk = NBR_KROWS * GRID_W
    start = pl.multiple_of(_nbr_start_row(i, n_rows) * GRID_W, GRID_W)
    outs = []
    for pp in range(NBR_PAIRS):
        ls = slice(pp * LANES, (pp + 1) * LANES)
        k_loc = k_ref[0, pl.ds(start, nk), ls]
        v_loc = v_ref[0, pl.ds(start, nk), ls]
        qs = _stack_halves(q_ref[0][:, ls])
        s_loc = _scores(qs, k_loc) + bias_ref[0, 2 * pp:2 * pp + 2].reshape(2 * nq, nk)
        s_ctx = _scores(qs, ck_ref[0][:, ls])
        o = _joint_softmax_pv([s_loc, s_ctx], [v_loc, cv_ref[0][:, ls]])
        outs.append(_merge_halves(o, nq))
    o_ref[0] = jnp.concatenate(outs, axis=-1).astype(o_ref.dtype)


def _nbr_bias_table(rpb, n_rows):
    kh = B_WIN_H
    n_heads = rpb.shape[0]
    col = np.arange(GRID_W)
    cs = np.clip(col - B_WIN_W // 2, 0, GRID_W - B_WIN_W)
    col_ok = (col[None, :] >= cs[:, None]) & (col[None, :] < cs[:, None] + B_WIN_W)
    dc = np.clip(col[None, :] - col[:, None], -(B_WIN_W - 1), B_WIN_W - 1) + B_WIN_W - 1
    pick_col = (dc[..., None] == np.arange(2 * B_WIN_W - 1)).astype(np.float32)
    r0 = np.array([0, NBR_ROWS, n_rows - NBR_ROWS])
    start = np.clip(r0 - kh // 2, 0, n_rows - NBR_KROWS)
    r = r0[:, None] + np.arange(NBR_ROWS)[None]
    rs = np.clip(r - kh // 2, 0, n_rows - kh)
    kr = start[:, None] + np.arange(NBR_KROWS)[None]
    row_ok = (kr[:, None, :] >= rs[:, :, None]) & (kr[:, None, :] < rs[:, :, None] + kh)
    dr = np.clip(kr[:, None, :] - r[:, :, None] + kh - 1, 0, 2 * kh - 2)
    pick_row = (dr[..., None] == np.arange(2 * kh - 1)).astype(np.float32)
    table = jnp.einsum("prkd,hdc,qjc->phrqkj", pick_row, rpb.astype(F32), pick_col,
                       precision=lax.Precision.HIGHEST)
    ok = row_ok[:, None, :, None, :, None] & col_ok[None, None, None, :, None, :]
    return jnp.where(ok, table, NEG).reshape(3, n_heads, NBR_ROWS * GRID_W, NBR_KROWS * GRID_W)


def _nbr_attn(q, k, v, ck, cv, bias):
    B, S, _ = q.shape
    n_rows = S // GRID_W
    nsteps = n_rows // NBR_ROWS
    nq = NBR_ROWS * GRID_W
    nk = NBR_KROWS * GRID_W
    pat = lambda i: jnp.where(i == 0, 0, jnp.where(i == nsteps - 1, 2, 1))
    wl = NBR_PAIRS * LANES
    return pl.pallas_call(
        functools.partial(_nbr_attn_kernel, n_rows),
        grid=(B, 4 // NBR_PAIRS, nsteps),
        in_specs=[pl.BlockSpec((1, nq, wl), lambda b, p, i: (b, i, p)),
                  pl.BlockSpec((1, S, wl), lambda b, p, i: (b, 0, p)),
                  pl.BlockSpec((1, S, wl), lambda b, p, i: (b, 0, p)),
                  pl.BlockSpec((1, CTX_LEN, wl), lambda b, p, i: (b, 0, p)),
                  pl.BlockSpec((1, CTX_LEN, wl), lambda b, p, i: (b, 0, p)),
                  pl.BlockSpec((1, 2 * NBR_PAIRS, nq, nk), lambda b, p, i: (pat(i), p, 0, 0))],
        out_specs=pl.BlockSpec((1, nq, wl), lambda b, p, i: (b, i, p)),
        out_shape=jax.ShapeDtypeStruct((B, S, 4 * LANES), BF16),
        compiler_params=_cparams(("parallel", "parallel", "arbitrary")),
        name="nbr_attn",
    )(q, k, v, ck, cv, bias)


FULL_TQ = 512
FULL_TK = 512
FULL_NOMAX_LOG2_BOUND = 60.0


def _full_attn_kernel(bounded, q_ref, k_ref, v_ref, ck_ref, cv_ref, o_ref):
    tq = FULL_TQ
    q = q_ref[0]
    groups = range(2)
    lanes = [slice(g * LANES, (g + 1) * LANES) for g in groups]
    qs = [jnp.concatenate([_stack_halves(q[:, (2 * g + j) * LANES:(2 * g + j + 1) * LANES]) for j in range(2)],
                          axis=0) for g in groups]

    def step_bounded(g, acc, k, v):
        v_ones = jnp.where(_half_mask(v.shape), v, jnp.ones_like(v))
        p = jnp.exp2(_scores(qs[g], k))
        return acc + jnp.dot(p.astype(BF16), v_ones, preferred_element_type=F32)

    def step_online(g, carry, k, v):
        m, l, acc = carry
        s = _scores(qs[g], k)
        m_new = jnp.maximum(m, jnp.max(s, axis=-1, keepdims=True))
        alpha = jnp.exp2(m - m_new)
        p = jnp.exp2(s - m_new)
        l = l * alpha + jnp.sum(p, axis=-1, keepdims=True)
        acc = acc * alpha + jnp.dot(p.astype(BF16), v, preferred_element_type=F32)
        return m_new, l, acc

    step = step_bounded if bounded else step_online
    init = jnp.zeros((4 * tq, LANES), F32)
    if not bounded:
        init = (jnp.full((4 * tq, 1), NEG, F32), jnp.zeros((4 * tq, 1), F32), init)
    carry = tuple(step(g, init, ck_ref[0][:, lanes[g]], cv_ref[0][:, lanes[g]]) for g in groups)

    def body(j, carry):
        rows = pl.ds(pl.multiple_of(j * FULL_TK, FULL_TK), FULL_TK)
        return tuple(step(g, carry[g], k_ref[0, rows, lanes[g]], v_ref[0, rows, lanes[g]]) for g in groups)

    carry = lax.fori_loop(0, k_ref.shape[1] // FULL_TK, body, carry)
    pairs = []
    for g in groups:
        if bounded:
            o = carry[g] * pltpu.roll(1.0 / carry[g], HEAD_DIM, axis=1)
            merge = lambda a, b: jnp.where(_half_mask((tq, LANES)), a, pltpu.roll(b, HEAD_DIM, axis=1))
            pairs += [merge(o[0:tq], o[tq:2 * tq]), merge(o[2 * tq:3 * tq], o[3 * tq:])]
        else:
            o = carry[g][2] / carry[g][1]
            pairs += [_merge_halves(o[:2 * tq], tq), _merge_halves(o[2 * tq:], tq)]
    o_ref[0] = jnp.concatenate(pairs, axis=-1).astype(o_ref.dtype)


def _full_attn(q, kd, vd, ckd, cvd, bounded):
    B, S, W = q.shape
    return pl.pallas_call(
        functools.partial(_full_attn_kernel, bounded),
        grid=(B, S // FULL_TQ),
        in_specs=[pl.BlockSpec((1, FULL_TQ, W), lambda b, i: (b, i, 0)),
                  pl.BlockSpec((1, S, 2 * LANES), lambda b, i: (b, 0, 0)),
                  pl.BlockSpec((1, S, 2 * LANES), lambda b, i: (b, 0, 0)),
                  pl.BlockSpec((1, CTX_LEN, 2 * LANES), lambda b, i: (b, 0, 0)),
                  pl.BlockSpec((1, CTX_LEN, 2 * LANES), lambda b, i: (b, 0, 0))],
        out_specs=pl.BlockSpec((1, FULL_TQ, W), lambda b, i: (b, i, 0)),
        out_shape=jax.ShapeDtypeStruct((B, S, W), BF16),
        compiler_params=_cparams(("parallel", "arbitrary")),
        name="full_attn_bounded" if bounded else "full_attn_online",
    )(q, kd, vd, ckd, cvd)


def _ctx_attn_kernel(sink_ref, aq_ref, akd_ref, avd_ref, bq_ref, bk_ref, bv_ref, o_ref):
    n = CTX_LEN
    aq = aq_ref[0]
    bq = bq_ref[0]
    outs = []
    for g in range(2):
        ls = slice(g * LANES, (g + 1) * LANES)
        qs = jnp.concatenate([_stack_halves(aq[:, (2 * g + j) * LANES:(2 * g + j + 1) * LANES])
                              for j in range(2)], axis=0)
        sink = _sink_column(sink_ref, 4 * g, 4, n)
        o = _joint_softmax_pv([_scores(qs, akd_ref[0][:, ls])], [avd_ref[0][:, ls]], sink)
        outs += [_merge_halves(o[:2 * n], n), _merge_halves(o[2 * n:], n)]
    for p in range(4):
        ls = slice(p * LANES, (p + 1) * LANES)
        qs = _stack_halves(bq[:, ls])
        o = _joint_softmax_pv([_scores(qs, bk_ref[0][:, ls])], [bv_ref[0][:, ls]])
        outs.append(_merge_halves(o, n))
    o_ref[0] = jnp.concatenate(outs, axis=-1).astype(o_ref.dtype)


def _ctx_attn(sink, aq, akd, avd, bq, bk, bv):
    B = aq.shape[0]
    spec = lambda a: pl.BlockSpec((1,) + a.shape[1:], lambda b: (b, 0, 0))
    args = (aq, akd, avd, bq, bk, bv)
    return pl.pallas_call(
        _ctx_attn_kernel,
        grid=(B,),
        in_specs=[pl.BlockSpec(memory_space=pltpu.SMEM)] + [spec(a) for a in args],
        out_specs=pl.BlockSpec((1, CTX_LEN, 8 * LANES), lambda b: (b, 0, 0)),
        out_shape=jax.ShapeDtypeStruct((B, CTX_LEN, 8 * LANES), BF16),
        compiler_params=_cparams(("parallel",)),
        name="ctx_attn",
    )(sink, *args)


def _pick4(idx, vals):
    return jnp.where(idx == 0, vals[0], jnp.where(idx == 1, vals[1], jnp.where(idx == 2, vals[2], vals[3])))


def _route_rows(lg_t, b_ref):
    n_tok = lg_t.shape[1]
    s = [jax.nn.sigmoid(lg_t[e:e + 1, :]) for e in range(N_EXPERTS)]
    sel = [s[e] + b_ref[e] for e in range(N_EXPERTS)]
    n = EXPERTS_PER_GROUP
    gscore = []
    for j in range(N_GROUPS):
        v = sel[n * j:n * (j + 1)]
        pair_sums = [v[a] + v[b] for a in range(n) for b in range(a + 1, n)]
        gscore.append(functools.reduce(jnp.maximum, pair_sums))
    best, gbest = gscore[0], jnp.zeros((1, n_tok), jnp.int32)
    for j in range(1, N_GROUPS):
        upd = gscore[j] > best
        best = jnp.where(upd, gscore[j], best)
        gbest = jnp.where(upd, j, gbest)
    v = [_pick4(gbest, [sel[n * j + i] for j in range(N_GROUPS)]) for i in range(n)]
    u = [_pick4(gbest, [s[n * j + i] for j in range(N_GROUPS)]) for i in range(n)]
    m1, i1 = v[0], jnp.zeros((1, n_tok), jnp.int32)
    for i in range(1, n):
        upd = v[i] > m1
        m1 = jnp.where(upd, v[i], m1)
        i1 = jnp.where(upd, i, i1)
    m2, i2 = jnp.full((1, n_tok), -jnp.inf, F32), jnp.zeros((1, n_tok), jnp.int32)
    for i in range(n):
        upd = (i1 != i) & (v[i] > m2)
        m2 = jnp.where(upd, v[i], m2)
        i2 = jnp.where(upd, i, i2)
    u1, u2 = _pick4(i1, u), _pick4(i2, u)
    tot = u1 + u2
    return n * gbest + i1, n * gbest + i2, u1 / tot, u2 / tot


def _out_proj_kernel(n_y, sparse, *refs):
    y_refs = refs[:n_y]
    br_ref, w_ref, x_ref, gate_ref, g_ref, shift_ref, scale_ref, wrh_ref, wrl_ref = refs[n_y:n_y + 9]
    n_in = n_y + 9
    if sparse:
        tri_ref, tril_ref = refs[n_in:n_in + 2]
        n_in += 2
    outs = refs[n_in:]
    xo_ref = outs[0]
    off = 0
    acc = None
    for y_ref in y_refs:
        wdt = y_ref.shape[-1]
        part = jnp.dot(y_ref[0].astype(BF16), w_ref[off:off + wdt, :], preferred_element_type=F32)
        acc = part if acc is None else acc + part
        off += wdt
    x = x_ref[0] + gate_ref[0] * acc
    xo_ref[0] = x
    ms = jnp.mean(x * x, axis=-1, keepdims=True)
    h = x * lax.rsqrt(ms + EPS) * g_ref[...]
    h = h * (1.0 + scale_ref[0]) + shift_ref[0]
    hh = h.astype(BF16)
    hl = (h - hh.astype(F32)).astype(BF16)
    lg = (jnp.dot(hh, wrh_ref[...], preferred_element_type=F32)
          + jnp.dot(hl, wrh_ref[...], preferred_element_type=F32)
          + jnp.dot(hh, wrl_ref[...], preferred_element_type=F32))
    lg_t = lg.T[:N_EXPERTS]
    e1, e2, w1, w2 = _route_rows(lg_t, br_ref)
    rows = lax.broadcasted_iota(jnp.int32, lg_t.shape, 0)
    if not sparse:
        h_ref, comb_ref = outs[1:]
        h_ref[0] = hh
        comb_t = jnp.where(rows == e1, w1, 0.0) + jnp.where(rows == e2, w2, 0.0)
        comb_ref[0] = jnp.concatenate(
            [comb_t, jnp.zeros((LANES - N_EXPERTS, comb_t.shape[1]), F32)], axis=0).T
        return

    h_ref, route_t_ref, route_c_ref, ng_ref = outs[1:]
    h_ref[0] = hh
    member = jnp.where((rows == e1) | (rows == e2), 1.0, 0.0)
    before = jnp.dot(member.astype(BF16), tri_ref[...], preferred_element_type=F32)
    groups = jnp.floor((jnp.sum(member, axis=1, keepdims=True) + (MOE_G - 1)) * (1.0 / MOE_G))
    groups = jnp.broadcast_to(groups, (N_EXPERTS, LANES))
    run_start = MOE_G * jnp.dot(tril_ref[...], groups.astype(BF16), preferred_element_type=F32)[:, 0:1]
    pos = run_start + before
    p1 = jnp.sum(jnp.where(rows == e1, pos, 0.0), axis=0, keepdims=True)
    p2 = jnp.sum(jnp.where(rows == e2, pos, 0.0), axis=0, keepdims=True)
    ng_ref[0] = groups
    field = lax.broadcasted_iota(jnp.int32, (LANES, h.shape[0]), 0)
    route = jnp.zeros((LANES, h.shape[0]), F32)
    for k, v in enumerate((p1, p2, w1, w2)):
        route = jnp.where(field == k, v, route)
    route_t_ref[...] = route[:ROUTE_FIELDS]
    route_c_ref[...] = route.T


ROUTE_FIELDS = 8
MOE_G = 16
MOE_TT = 512
MOE_LOCAL = 1280


def _out_proj(ys, w, x, gate, g, shift, scale, router, sparse, tm=512):
    B, S, D = x.shape
    tm = min(tm, S)
    bm = gate.shape[0]
    b_router, wr_hi, wr_lo = router
    nt = S // tm
    mod_map = (lambda b, i: (b, 0, 0)) if bm > 1 else (lambda b, i: (0, 0, 0))
    mod_spec = pl.BlockSpec((1, 1, D), mod_map)
    row_map = lambda b, i: (b, i, 0)
    in_specs = ([pl.BlockSpec((1, tm, y.shape[-1]), row_map) for y in ys]
                + [pl.BlockSpec(memory_space=pltpu.SMEM),
                   pl.BlockSpec(w.shape, lambda b, i: (0, 0)),
                   pl.BlockSpec((1, tm, D), row_map), mod_spec,
                   pl.BlockSpec((1, D), lambda b, i: (0, 0)), mod_spec, mod_spec,
                   pl.BlockSpec(wr_hi.shape, lambda b, i: (0, 0)),
                   pl.BlockSpec(wr_lo.shape, lambda b, i: (0, 0))])
    args = list(ys) + [b_router, w, x, gate.reshape(bm, 1, D), g.reshape(1, D), shift.reshape(bm, 1, D),
                       scale.reshape(bm, 1, D), wr_hi, wr_lo]
    out_specs = [pl.BlockSpec((1, tm, D), row_map)]
    out_shape = [jax.ShapeDtypeStruct((B, S, D), F32)]
    out_specs.append(pl.BlockSpec((1, tm, D), row_map))
    out_shape.append(jax.ShapeDtypeStruct((B, S, D), BF16))
    if sparse:
        assert tm == MOE_TT
        tri = jnp.asarray(np.triu(np.ones((tm, tm), np.float32), 1)).astype(BF16)
        tril = jnp.asarray(np.tril(np.ones((N_EXPERTS, N_EXPERTS), np.float32), -1)).astype(BF16)
        args += [tri, tril]
        in_specs += [pl.BlockSpec(tri.shape, lambda b, i: (0, 0)),
                     pl.BlockSpec(tril.shape, lambda b, i: (0, 0))]
        out_specs += [pl.BlockSpec((ROUTE_FIELDS, tm), lambda b, i: (0, b * nt + i)),
                      pl.BlockSpec((tm, LANES), lambda b, i: (b * nt + i, 0)),
                      pl.BlockSpec((1, N_EXPERTS, LANES), lambda b, i: (b * nt + i, 0, 0))]
        out_shape += [jax.ShapeDtypeStruct((ROUTE_FIELDS, B * S), F32),
                      jax.ShapeDtypeStruct((B * S, LANES), F32),
                      jax.ShapeDtypeStruct((B * nt, N_EXPERTS, LANES), F32)]
    else:
        out_specs.append(pl.BlockSpec((1, tm, LANES), row_map))
        out_shape.append(jax.ShapeDtypeStruct((B, S, LANES), F32))
    return pl.pallas_call(
        functools.partial(_out_proj_kernel, len(ys), sparse),
        grid=(B, nt),
        in_specs=in_specs,
        out_specs=out_specs,
        out_shape=out_shape,
        compiler_params=_cparams(("parallel", "parallel")),
        name="out_proj_sparse" if sparse else "out_proj",
    )(*args)


def _moe_kernel(h_ref, comb_ref, wg_ref, wu_ref, wd_ref, x_ref, gate_ref, o_ref, acc_ref):
    e = pl.program_id(2)

    @pl.when(e == 0)
    def _():
        acc_ref[...] = jnp.zeros_like(acc_ref)

    h = h_ref[0]
    a = jnp.dot(h, wg_ref[0, 0].astype(BF16), preferred_element_type=F32)
    u = jnp.dot(h, wu_ref[0, 0].astype(BF16), preferred_element_type=F32)
    he = (a * jax.nn.sigmoid(a) * u).astype(BF16)
    y = jnp.dot(he, wd_ref[0, 0].astype(BF16), preferred_element_type=F32)
    lane = lax.broadcasted_iota(jnp.int32, comb_ref.shape[1:], 1)
    c = jnp.sum(jnp.where(lane == e, comb_ref[0], 0.0), axis=-1, keepdims=True)
    acc_ref[...] += c * y

    @pl.when(e == pl.num_programs(2) - 1)
    def _():
        o_ref[0] = x_ref[0] + gate_ref[0] * acc_ref[...]


def _moe(h, comb, wg, wu, wd, layer, x, gate, tm=1024):
    B, S, D = x.shape
    tm = min(tm, S)
    bm = gate.shape[0]
    mod_map = (lambda b, i, e: (b, 0, 0)) if bm > 1 else (lambda b, i, e: (0, 0, 0))
    row_map = lambda b, i, e: (b, i, 0)
    return pl.pallas_call(
        _moe_kernel,
        grid=(B, S // tm, N_EXPERTS),
        in_specs=[pl.BlockSpec((1, tm, D), row_map),
                  pl.BlockSpec((1, tm, LANES), row_map),
                  pl.BlockSpec((1, 1, D, D_EXPERT), lambda b, i, e: (layer, e, 0, 0)),
                  pl.BlockSpec((1, 1, D, D_EXPERT), lambda b, i, e: (layer, e, 0, 0)),
                  pl.BlockSpec((1, 1, D_EXPERT, D), lambda b, i, e: (layer, e, 0, 0)),
                  pl.BlockSpec((1, tm, D), row_map),
                  pl.BlockSpec((1, 1, D), mod_map)],
        out_specs=pl.BlockSpec((1, tm, D), row_map),
        out_shape=jax.ShapeDtypeStruct((B, S, D), F32),
        scratch_shapes=[pltpu.VMEM((tm, D), F32)],
        compiler_params=_cparams(("parallel", "parallel", "arbitrary")),
        name="moe",
    )(h, comb, wg, wu, wd, x, gate.reshape(bm, 1, D))


MOE_TM = 1024
MOE_TG = MOE_TM // MOE_G


def _moe_rows(n_tok):
    rows = 2 * n_tok + (n_tok // MOE_TT) * N_EXPERTS * (MOE_G - 1) + N_EXPERTS * (MOE_TM - 1)
    return (rows + MOE_TM - 1) // MOE_TM * MOE_TM


def _moe_plan(ng):
    ng = ng[:, :, 0].astype(jnp.int32)
    n_tt = ng.shape[0]
    total = jnp.sum(ng, axis=0)
    region = (total + MOE_TG - 1) // MOE_TG * MOE_TG
    region_end = jnp.cumsum(region)
    region_start = region_end - region
    dst = region_start[None, :] + jnp.cumsum(ng, axis=0) - ng
    local = jnp.cumsum(ng, axis=1) - ng
    n_tiles = _moe_rows(n_tt * MOE_TT) // MOE_TM
    tile_first = jnp.arange(n_tiles, dtype=jnp.int32) * MOE_TG
    tile_expert = jnp.minimum(jnp.sum(region_end[None, :] <= tile_first[:, None], axis=1), N_EXPERTS - 1)
    n_valid = region_end[-1:] // MOE_TG
    i32 = lambda a: a.astype(jnp.int32).reshape(-1)
    return dict(ng=i32(ng), dst=i32(dst), local=i32(local), tile_groups=i32(jnp.sum(ng, axis=1)),
                pad_first=i32(region_start + total),
                pad_count=i32(region - total), tile_expert=i32(tile_expert), n_valid=i32(n_valid),
                n_tiles=n_tiles)


def _run_copies(plan_refs, tile, local_ref, sorted_ref, sem, to_sorted):
    ng_ref, dst_ref, loc_ref = plan_refs
    for e in range(N_EXPERTS):
        k = tile * N_EXPERTS + e
        loc, dst = loc_ref[k], dst_ref[k]

        def body(g, carry):
            lrows = local_ref.at[pl.ds(pl.multiple_of((loc + g) * MOE_G, MOE_G), MOE_G)]
            srows = sorted_ref.at[pl.ds(pl.multiple_of((dst + g) * MOE_G, MOE_G), MOE_G)]
            src, tgt = (lrows, srows) if to_sorted else (srows, lrows)
            pltpu.make_async_copy(src, tgt, sem).start(priority=e % 2)
            return carry
        lax.fori_loop(0, ng_ref[k], body, 0)


def _group_waits(n_groups, local_ref, sorted_ref, sem):
    def body(g, carry):
        pltpu.make_async_copy(sorted_ref.at[pl.ds(0, MOE_G)], local_ref.at[pl.ds(0, MOE_G)], sem).wait()
        return carry
    lax.fori_loop(0, n_groups, body, 0)


def _moe_dispatch_kernel(ng_ref, dst_ref, loc_ref, tot_ref, padf_ref, padc_ref, nv_ref,
                         h_ref, route_ref, xs_ref, local_ref, zero_ref, sem):
    i = pl.program_id(0)

    @pl.when(i == 0)
    def _():
        zero_ref[...] = jnp.zeros_like(zero_ref)
        n_pad = 0
        for e in range(N_EXPERTS):
            first = padf_ref[e]

            def body(g, carry):
                rows = xs_ref.at[pl.ds(pl.multiple_of((first + g) * MOE_G, MOE_G), MOE_G)]
                pltpu.make_async_copy(zero_ref.at[pl.ds(0, MOE_G)], rows, sem.at[2]).start()
                return carry
            lax.fori_loop(0, padc_ref[e], body, 0)
            n_pad = n_pad + padc_ref[e]

        def tile_copy(j):
            rows = xs_ref.at[pl.ds(pl.multiple_of(j * MOE_TM, MOE_TM), MOE_TM)]
            return pltpu.make_async_copy(zero_ref, rows, sem.at[3])

        n_tiles = xs_ref.shape[0] // MOE_TM
        lax.fori_loop(nv_ref[0], n_tiles, lambda j, c: (tile_copy(j).start(), c)[1], 0)
        _group_waits(n_pad, zero_ref, xs_ref, sem.at[2])
        lax.fori_loop(nv_ref[0], n_tiles, lambda j, c: (tile_copy(j).wait(), c)[1], 0)

    pos = lax.broadcasted_iota(jnp.int32, (MOE_LOCAL, MOE_TT), 0)
    p1 = route_ref[0:1, :].astype(jnp.int32)
    p2 = route_ref[1:2, :].astype(jnp.int32)
    pick = jnp.where((pos == p1) | (pos == p2), 1.0, 0.0).astype(BF16)
    slot = i % 2
    last = pl.num_programs(0) - 1

    @pl.when(i >= 2)
    def _():
        _group_waits(tot_ref[i - 2], local_ref.at[slot], xs_ref, sem.at[slot])

    local_ref[slot] = jnp.dot(pick, h_ref[0], preferred_element_type=F32).astype(BF16)
    _run_copies((ng_ref, dst_ref, loc_ref), i, local_ref.at[slot], xs_ref, sem.at[slot], to_sorted=True)

    @pl.when(i == last)
    def _():
        _group_waits(tot_ref[i], local_ref.at[slot], xs_ref, sem.at[slot])

    @pl.when((i == last) & (last >= 1))
    def _():
        _group_waits(tot_ref[i - 1], local_ref.at[1 - slot], xs_ref, sem.at[1 - slot])


def _moe_dispatch(h, route_t, plan):
    B, S, D = h.shape
    nt = S // MOE_TT
    grid_spec = pltpu.PrefetchScalarGridSpec(
        num_scalar_prefetch=7,
        grid=(B * nt,),
        in_specs=[pl.BlockSpec((1, MOE_TT, D), lambda i, *_: (i // nt, i % nt, 0)),
                  pl.BlockSpec((ROUTE_FIELDS, MOE_TT), lambda i, *_: (0, i))],
        out_specs=pl.BlockSpec(memory_space=pl.ANY),
        scratch_shapes=[pltpu.VMEM((2, MOE_LOCAL, D), BF16), pltpu.VMEM((MOE_TM, D), BF16),
                        pltpu.SemaphoreType.DMA((4,))])
    return pl.pallas_call(
        _moe_dispatch_kernel,
        grid_spec=grid_spec,
        out_shape=jax.ShapeDtypeStruct((plan["n_tiles"] * MOE_TM, D), BF16),
        compiler_params=_cparams(("arbitrary",)),
        name="moe_dispatch",
    )(plan["ng"], plan["dst"], plan["local"], plan["tile_groups"], plan["pad_first"], plan["pad_count"],
      plan["n_valid"], h, route_t)


def _moe_grouped_kernel(te_ref, nv_ref, x_ref, wg_ref, wu_ref, wd_ref, o_ref, wgb_ref, wub_ref, wdb_ref):
    j = pl.program_id(0)
    used = j < nv_ref[0]

    @pl.when(used & ((j == 0) | (te_ref[j] != te_ref[jnp.maximum(j - 1, 0)])))
    def _():
        wgb_ref[...] = wg_ref[0, 0].astype(BF16)
        wub_ref[...] = wu_ref[0, 0].astype(BF16)
        wdb_ref[...] = wd_ref[0, 0].astype(BF16)

    @pl.when(used)
    def _():
        x = x_ref[...]
        a = jnp.dot(x, wgb_ref[...], preferred_element_type=F32)
        u = jnp.dot(x, wub_ref[...], preferred_element_type=F32)
        he = (a * jax.nn.sigmoid(a) * u).astype(BF16)
        o_ref[...] = jnp.dot(he, wdb_ref[...], preferred_element_type=F32).astype(o_ref.dtype)

    @pl.when(jnp.logical_not(used))
    def _():
        o_ref[...] = jnp.zeros_like(o_ref)


def _moe_grouped(xs, plan, wg, wu, wd, layer):
    n_tiles = plan["n_tiles"]
    D = D_MODEL
    tile = lambda j, nv: jnp.minimum(j, nv[0] - 1)
    grid_spec = pltpu.PrefetchScalarGridSpec(
        num_scalar_prefetch=2,
        grid=(n_tiles,),
        in_specs=[pl.BlockSpec((MOE_TM, D), lambda j, te, nv: (tile(j, nv), 0)),
                  pl.BlockSpec((1, 1, D, D_EXPERT), lambda j, te, nv: (layer, te[tile(j, nv)], 0, 0)),
                  pl.BlockSpec((1, 1, D, D_EXPERT), lambda j, te, nv: (layer, te[tile(j, nv)], 0, 0)),
                  pl.BlockSpec((1, 1, D_EXPERT, D), lambda j, te, nv: (layer, te[tile(j, nv)], 0, 0))],
        out_specs=pl.BlockSpec((MOE_TM, D), lambda j, te, nv: (j, 0)),
        scratch_shapes=[pltpu.VMEM((D, D_EXPERT), BF16), pltpu.VMEM((D, D_EXPERT), BF16),
                        pltpu.VMEM((D_EXPERT, D), BF16)])
    return pl.pallas_call(
        _moe_grouped_kernel,
        grid_spec=grid_spec,
        out_shape=jax.ShapeDtypeStruct((n_tiles * MOE_TM, D), BF16),
        compiler_params=_cparams(("arbitrary",)),
        name="moe_grouped",
    )(plan["tile_expert"], plan["n_valid"], xs, wg, wu, wd)


def _moe_combine_kernel(final_norm, ng_ref, dst_ref, loc_ref, tot_ref, ys_ref, route_ref, x_ref, gate_ref,
                        *rest):
    if final_norm:
        fg_ref, o_ref, local_ref, sem = rest
    else:
        o_ref, local_ref, sem = rest
    i = pl.program_id(0)
    slot = i % 2
    plan_refs = (ng_ref, dst_ref, loc_ref)

    @pl.when(i == 0)
    def _():
        local_ref[...] = jnp.zeros_like(local_ref)
        _run_copies(plan_refs, 0, local_ref.at[0], ys_ref, sem.at[0], to_sorted=False)

    @pl.when(i + 1 < pl.num_programs(0))
    def _():
        _run_copies(plan_refs, i + 1, local_ref.at[1 - slot], ys_ref, sem.at[1 - slot], to_sorted=False)

    _group_waits(tot_ref[i], local_ref.at[slot], ys_ref, sem.at[slot])
    pos = lax.broadcasted_iota(jnp.int32, (MOE_TT, MOE_LOCAL), 1)
    route = route_ref[...]
    p1 = route[:, 0:1].astype(jnp.int32)
    p2 = route[:, 1:2].astype(jnp.int32)
    weigh = (jnp.where(pos == p1, route[:, 2:3], 0.0) + jnp.where(pos == p2, route[:, 3:4], 0.0)).astype(BF16)
    y = jnp.dot(weigh, local_ref[slot], preferred_element_type=F32)
    o = x_ref[0] + gate_ref[0] * y
    if final_norm:
        o = o * lax.rsqrt(jnp.mean(o * o, axis=-1, keepdims=True) + EPS) * fg_ref[...]
    o_ref[0] = o


def _moe_combine(ys, route_c, plan, x, gate, final_g=None):
    B, S, D = x.shape
    nt = S // MOE_TT
    in_specs = [pl.BlockSpec(memory_space=pl.ANY),
                pl.BlockSpec((MOE_TT, LANES), lambda i, *_: (i, 0)),
                pl.BlockSpec((1, MOE_TT, D), lambda i, *_: (i // nt, i % nt, 0)),
                pl.BlockSpec((1, 1, D), lambda i, *_: (i // nt, 0, 0))]
    args = [ys, route_c, x, gate.reshape(B, 1, D)]
    if final_g is not None:
        in_specs.append(pl.BlockSpec((1, D), lambda i, *_: (0, 0)))
        args.append(final_g.reshape(1, D))
    grid_spec = pltpu.PrefetchScalarGridSpec(
        num_scalar_prefetch=4,
        grid=(B * nt,),
        in_specs=in_specs,
        out_specs=pl.BlockSpec((1, MOE_TT, D), lambda i, *_: (i // nt, i % nt, 0)),
        scratch_shapes=[pltpu.VMEM((2, MOE_LOCAL, D), BF16), pltpu.SemaphoreType.DMA((2,))])
    return pl.pallas_call(
        functools.partial(_moe_combine_kernel, final_g is not None),
        grid_spec=grid_spec,
        out_shape=jax.ShapeDtypeStruct((B, S, D), F32),
        compiler_params=_cparams(("arbitrary",)),
        name="moe_combine",
    )(plan["ng"], plan["dst"], plan["local"], plan["tile_groups"], *args)


def _dft_tables():
    n1, n2, n = FFT_N1, FFT_N2, FFT_N1 * FFT_N2
    k1 = np.arange(n1)
    f1 = np.exp(-2j * np.pi * np.outer(k1, np.arange(n1)) / n1)
    tw = np.exp(-2j * np.pi * np.outer(np.arange(n2), k1) / n)
    ftw = f1[None, :, :] * tw[:, :, None]
    half = n1 // 2
    fh = ftw[:, :, :half]
    g_fwd = np.concatenate([np.concatenate([fh.real, -fh.imag], axis=2),
                            np.concatenate([fh.imag, fh.real], axis=2)], axis=1)
    back = ftw[:, :, ::-1][:, :, :half].copy()
    back[0] = np.roll(ftw[0], -1, axis=1)[:, ::-1][:, :half]
    back[0][:, 0] = 0.0
    fk = np.concatenate([fh, back], axis=2)
    g_real = np.concatenate([fk.real, fk.imag], axis=1)
    gi = np.conj(np.transpose(fh, (0, 2, 1))) / n
    g_inv = np.concatenate([np.concatenate([gi.real, -gi.imag], axis=2),
                            np.concatenate([gi.imag, gi.real], axis=2)], axis=1)
    f2 = np.exp(-2j * np.pi * np.outer(np.arange(n2), np.arange(n2)) / n2)
    f2_fwd = np.block([[f2.real, -f2.imag], [f2.imag, f2.real]])
    f2c = np.conj(f2)
    f2_inv = np.block([[f2c.real, -f2c.imag], [f2c.imag, f2c.real]])
    as_bf = lambda a: jnp.asarray(a, dtype=F32).astype(BF16)
    return as_bf(g_fwd), as_bf(g_real), as_bf(g_inv), as_bf(f2_fwd), as_bf(f2_inv)


def _fft_fast_stage(stage_ref, k1, f2):
    slab = STAGE_PITCH
    m = jnp.concatenate([stage_ref[pl.ds(k1, FFT_N2, stride=slab), :],
                         stage_ref[pl.ds(FFT_N1 + k1, FFT_N2, stride=slab), :]], axis=0)
    return jnp.dot(f2, m.astype(BF16), preferred_element_type=F32)


def _filter_fft_kernel(hf_ref, hb_ref, inv_ref, g_ref, f2_ref, h_ref, stage_ref):
    slab = STAGE_PITCH
    half = FFT_N1 // 2
    for n2 in range(FFT_N2):
        x = jnp.concatenate([hf_ref[pl.ds(n2, half, stride=FFT_N2), :],
                             hb_ref[pl.ds((FFT_N2 - n2) % FFT_N2, half, stride=FFT_N2), :]], axis=0)
        stage_ref[n2 * slab:n2 * slab + 2 * FFT_N1, :] = jnp.dot(g_ref[n2], x.astype(BF16),
                                                          preferred_element_type=F32)
    f2 = f2_ref[...]
    inv = inv_ref[...]
    for k1 in range(FFT_N1):
        h_ref[0, k1] = (_fft_fast_stage(stage_ref, k1, f2) * inv).astype(h_ref.dtype)


def _filter_fft(taps, inv_norm, g_real, f2_fwd, ct=LANES):
    L, cols = taps.shape
    C = D_CH
    n_ord = cols // (2 * C)
    nc = C // ct
    once = pl.Buffered(1)
    return pl.pallas_call(
        _filter_fft_kernel,
        grid=(n_ord, nc),
        in_specs=[pl.BlockSpec((L, ct), lambda o, c: (0, o * nc + c)),
                  pl.BlockSpec((L, ct), lambda o, c: (0, (n_ord + o) * nc + c)),
                  pl.BlockSpec((1, ct), lambda o, c: (0, o * nc + c)),
                  pl.BlockSpec(g_real.shape, lambda o, c: (0, 0, 0), pipeline_mode=once),
                  pl.BlockSpec(f2_fwd.shape, lambda o, c: (0, 0), pipeline_mode=once)],
        out_specs=pl.BlockSpec((1, FFT_N1, 2 * FFT_N2, ct), lambda o, c: (o, 0, 0, c)),
        out_shape=jax.ShapeDtypeStruct((n_ord, FFT_N1, 2 * FFT_N2, C), BF16),
        scratch_shapes=[pltpu.VMEM((FFT_N2 * STAGE_PITCH, ct), F32)],
        compiler_params=_cparams(("parallel", "parallel")),
        name="filter_fft",
    )(taps, taps, inv_norm, g_real, f2_fwd)


class _RowSets:
    def __init__(self, ref, member):
        self.ref, self.member, self.loaded = ref, member, {}

    def __call__(self, start):
        if start not in self.loaded:
            self.loaded = {k: v for k, v in self.loaded.items() if abs(k - start) <= 2}
            self.loaded[start] = self.ref[self.member, pl.ds(start, FFT_N1 // 2, stride=FFT_N2), :]
        return self.loaded[start]


def _strided_rows(at, n2, taps):
    half = FFT_N1 // 2
    cur = at(n2)
    if taps is None:
        return cur
    w_ref, b_ref = taps
    n1 = lax.broadcasted_iota(jnp.int32, cur.shape, 0)
    if n2 > 0:
        prev = at(n2 - 1)
    else:
        prev = jnp.where(n1 == 0, 0.0, pltpu.roll(at(FFT_N2 - 1), 1, axis=0))
    if n2 < FFT_N2 - 1:
        nxt = at(n2 + 1)
    else:
        nxt = jnp.where(n1 == half - 1, 0.0, pltpu.roll(at(0), half - 1, axis=0))
    return prev * w_ref[0:1, :] + cur * w_ref[1:2, :] + nxt * w_ref[2:3, :] + b_ref[...]


def _hyena_conv_kernel(conv_z, *refs):
    it = iter(refs)
    z_ref, gate_ref = next(it), next(it)
    z_taps = (next(it), next(it)) if conv_z else None
    gate_taps = (next(it), next(it))
    hb_ref, spec_ref, gf_ref, gi_ref, f2f_ref, f2i_ref, o_ref, stage_ref = it
    half = FFT_N1 // 2
    slab = STAGE_PITCH
    z_sets = [_RowSets(z_ref, m) for m in range(2)]
    for n2 in range(FFT_N2):
        x = jnp.concatenate([_strided_rows(z_sets[m], n2, z_taps) for m in range(2)], axis=0).astype(BF16)
        stage_ref[n2 * slab:n2 * slab + 2 * FFT_N1, :] = jnp.dot(gf_ref[n2], x, preferred_element_type=F32)
    f2f = f2f_ref[...]
    f2i = f2i_ref[...]
    for k1 in range(FFT_N1):
        zf = _fft_fast_stage(stage_ref, k1, f2f)
        zr, zi = zf[:FFT_N2], zf[FFT_N2:]
        hr = spec_ref[0, k1, :FFT_N2, :].astype(F32)
        hi = spec_ref[0, k1, FFT_N2:, :].astype(F32)
        p = jnp.concatenate([zr * hr - zi * hi, zr * hi + zi * hr], axis=0).astype(BF16)
        q = jnp.dot(f2i, p, preferred_element_type=F32)
        stage_ref[pl.ds(k1, FFT_N2, stride=slab), :] = q[:FFT_N2]
        stage_ref[pl.ds(FFT_N1 + k1, FFT_N2, stride=slab), :] = q[FFT_N2:]
    hb = hb_ref[...]
    z_sets = [_RowSets(z_ref, m) for m in range(2)]
    gate_sets = [_RowSets(gate_ref, m) for m in range(2)]
    for n2 in range(FFT_N2):
        y_in = stage_ref[n2 * slab:n2 * slab + 2 * FFT_N1, :].astype(BF16)
        y = jnp.dot(gi_ref[n2], y_in, preferred_element_type=F32)
        for m in range(2):
            zm = _strided_rows(z_sets[m], n2, z_taps)
            gm = _strided_rows(gate_sets[m], n2, gate_taps)
            o_ref[m, pl.ds(n2, half, stride=FFT_N2), :] = gm * (y[m * half:(m + 1) * half] + zm * hb)


def _hyena_conv(z, z_blk, gate, gate_blk, short_w, short_b, conv_z, hbias, spec, order, tabs, ct=LANES):
    B, L, _ = z.shape
    C = D_CH
    g_fwd, _, g_inv, f2_fwd, f2_inv = tabs
    once = pl.Buffered(1)
    const3 = lambda a: pl.BlockSpec(a.shape, lambda c, p: (0, 0, 0), pipeline_mode=once)
    const2 = lambda a: pl.BlockSpec(a.shape, lambda c, p: (0, 0), pipeline_mode=once)
    taps_specs = lambda blk: [pl.BlockSpec((3, ct), lambda c, p: (0, blk + c)),
                              pl.BlockSpec((1, ct), lambda c, p: (0, blk + c))]
    in_specs = [pl.BlockSpec((2, L, ct), lambda c, p: (p, 0, z_blk + c)),
                pl.BlockSpec((2, L, ct), lambda c, p: (p, 0, gate_blk + c))]
    args = [z, gate]
    if conv_z:
        in_specs += taps_specs(z_blk)
        args += [short_w, short_b]
    in_specs += taps_specs(gate_blk)
    args += [short_w, short_b]
    in_specs += [pl.BlockSpec((1, ct), lambda c, p: (0, c)),
                 pl.BlockSpec((1, FFT_N1, 2 * FFT_N2, ct), lambda c, p: (order, 0, 0, c), pipeline_mode=once),
                 const3(g_fwd), const3(g_inv), const2(f2_fwd), const2(f2_inv)]
    args += [hbias, spec, g_fwd, g_inv, f2_fwd, f2_inv]
    return pl.pallas_call(
        functools.partial(_hyena_conv_kernel, conv_z),
        grid=(C // ct, B // 2),
        in_specs=in_specs,
        out_specs=pl.BlockSpec((2, L, ct), lambda c, p: (p, 0, c)),
        out_shape=jax.ShapeDtypeStruct((B, L, C), F32),
        scratch_shapes=[pltpu.VMEM((FFT_N2 * STAGE_PITCH, ct), F32)],
        compiler_params=_cparams(("parallel", "arbitrary")),
        name="hyena_conv",
    )(*args)


def _hyena_filters(L, w1, b1, f1, w2, b2, f2, w3, b3):
    t = jnp.arange(L, dtype=F32)
    tn = t / max(L - 1, 1)
    bands = jnp.linspace(1e-4, HY_BANDS - 1, HY_BANDS, dtype=F32)
    ang = 2.0 * math.pi * t[:, None] * bands[None] / L
    feats = jnp.concatenate([tn[:, None], jnp.cos(ang), jnp.sin(ang)], axis=-1)
    h = jnp.sin(f1 * (feats @ w1 + b1))
    h = jnp.sin(f2 * (h @ w2 + b2))
    deltas = jnp.abs(jnp.linspace(HY_MIN_DECAY, HY_MAX_DECAY, D_CH, dtype=F32))
    decay = jnp.exp(-tn[:, None] * deltas[None])
    n_rep = w3.shape[1] // D_CH
    taps, l1 = _filter_taps(h, w3, b3[None], decay)
    l1 = l1[0, :n_rep // 2 * D_CH] + l1[0, n_rep // 2 * D_CH:]
    return taps, (1.0 / (l1 + EPS))[None]


def _filter_taps_kernel(h_ref, w_ref, b_ref, decay_ref, taps_ref, l1_ref):
    taps = (jnp.dot(h_ref[...], w_ref[...], preferred_element_type=F32) + b_ref[...]) * decay_ref[...]
    taps_ref[...] = taps
    l1_ref[...] = jnp.sum(jnp.abs(taps), axis=0, keepdims=True)


def _filter_taps(h, w, b, decay):
    L, K = h.shape
    N = w.shape[1]
    return pl.pallas_call(
        _filter_taps_kernel,
        grid=(N // D_CH,),
        in_specs=[pl.BlockSpec((L, K), lambda j: (0, 0)),
                  pl.BlockSpec((K, D_CH), lambda j: (0, j)),
                  pl.BlockSpec((1, D_CH), lambda j: (0, j)),
                  pl.BlockSpec((L, D_CH), lambda j: (0, 0))],
        out_specs=[pl.BlockSpec((L, D_CH), lambda j: (0, j)), pl.BlockSpec((1, D_CH), lambda j: (0, j))],
        out_shape=[jax.ShapeDtypeStruct((L, N), F32), jax.ShapeDtypeStruct((1, N), F32)],
        compiler_params=_cparams(("parallel",)),
        name="filter_taps",
    )(h, w, b, decay)


def _dup_heads(w):
    a, b = w[:, :HEAD_DIM], w[:, HEAD_DIM:]
    return jnp.concatenate([a, a, b, b], axis=1)


def _rope_tables(S):
    t = jnp.arange(S)
    row = (t // GRID_W).astype(F32)
    col = (t % GRID_W).astype(F32)
    half = HEAD_DIM // 2
    inv = ROPE_THETA ** (-jnp.arange(0, half, 2, dtype=F32) / half)
    ar = row[:, None] * inv[None]
    ac = col[:, None] * inv[None]
    cos = jnp.concatenate([jnp.cos(ar), jnp.cos(ar), jnp.cos(ac), jnp.cos(ac)], axis=-1)
    sin = jnp.concatenate([-jnp.sin(ar), jnp.sin(ar), -jnp.sin(ac), jnp.sin(ac)], axis=-1)
    return jnp.tile(cos, (1, 2)), jnp.tile(sin, (1, 2))


def _head_mean_matrix(width):
    blk = np.kron(np.eye(width // HEAD_DIM), np.full((HEAD_DIM, HEAD_DIM), 1.0 / HEAD_DIM))
    return jnp.asarray(blk, dtype=F32).astype(BF16)


def kernel(x, c, ctx, c_ctx, w_ada, b_ada, norm_g, final_g, w_in_even, w_out_even, a_sink, b_rpb, w_in_odd, w_out_odd, c_qnorm, c_knorm, hy_short_w, hy_short_b, hy_w1, hy_b1, hy_f1, hy_w2, hy_b2, hy_f2, hy_w3, hy_b3, hy_bias, w_router, b_router, moe_wg, moe_wu, moe_wd):
    B, S, D = x.shape
    depth = w_ada.shape[0]
    rope = _rope_tables(S)
    wr_pad = jnp.pad(w_router.astype(F32), ((0, 0), (0, LANES - N_EXPERTS)))
    wr_hi = wr_pad.astype(BF16)
    router = (b_router.astype(F32), wr_hi, (wr_pad - wr_hi.astype(F32)).astype(BF16))

    mod_in = jnp.concatenate([jax.nn.silu(c), jax.nn.silu(c_ctx)[None],
                              jnp.zeros((8 - B - 1, D), F32)], axis=0)
    xc = ctx
    for l in range(depth):
        need_ctx = l < depth - 1
        mod = _mm_f32(mod_in, w_ada[l], 1536) + b_ada[l]
        mx = mod[:B].reshape(B, 6, D)
        mc = mod[B].reshape(6, D)
        i = l // 2
        if l % 2 == 0:
            w = w_in_even[i].astype(BF16)
            w_all = jnp.concatenate([w[:, :512], _dup_heads(w[:, 512:640]), _dup_heads(w[:, 640:768]),
                                     w[:, 768:]], axis=1)
            segs_x = ((512, "rope", Q_MULT), (256, "rope", 1.0), (256, "plain", 1.0),
                      (512, "plain", Q_MULT), (512, "plain", 1.0), (512, "plain", 1.0))
            sink = a_sink[i].astype(F32) * LOG2_E
            bias = _nbr_bias_table(b_rpb[i].astype(F32) * LOG2_E, S // GRID_W)
            aq, akd, avd, bq, bk, bv = _norm_proj(x, norm_g[l, 0], mx[:, 0], mx[:, 1], w_all, segs_x,
                                                  rope=rope)
            segs_c = tuple((wd, "plain", m) for wd, _, m in segs_x)
            caq, cakd, cavd, cbq, cbk, cbv = _norm_proj(xc, norm_g[l, 0], mc[0:1], mc[1:2], w_all, segs_c)
            ya = _window_attn(aq, akd, avd, cakd, cavd, sink)
            yb = _nbr_attn(bq, bk, bv, cbk, cbv, bias)
            ys = [ya, yb]
            w_out = w_out_even[i].astype(BF16)
            if need_ctx:
                yc = [_ctx_attn(sink, caq, cakd, cavd, cbq, cbk, cbv)]
        else:
            w = w_in_odd[i].astype(BF16)
            w_all = jnp.concatenate([w[:, :512], _dup_heads(w[:, 512:640]), _dup_heads(w[:, 640:768]),
                                     w[:, 768:]], axis=1)
            gains = jnp.concatenate([jnp.tile(c_qnorm[i], 8), jnp.tile(c_knorm[i], 4)])[None].astype(F32)
            norm = (_head_mean_matrix(512), gains)
            segs_x = ((512, "normrope", Q_MULT), (256, "normrope", 1.0), (256, "plain", 1.0),
                      (3 * D_CH, "plain", 1.0, F32))
            qx, kxd, vxd, ux = _norm_proj(x, norm_g[l, 0], mx[:, 0], mx[:, 1], w_all, segs_x,
                                          rope=rope, norm=norm)
            w_c = w_all[:, 512:1024]
            norm_c = (_head_mean_matrix(512), jnp.tile(c_knorm[i], 4)[None].astype(F32))
            kcd, vcd = _norm_proj(xc, norm_g[l, 0], mc[0:1], mc[1:2], w_c,
                                  ((256, "norm", 1.0), (256, "plain", 1.0)), norm=norm_c)
            logit_bound = (1.02 * HEAD_DIM * Q_MULT * jnp.max(jnp.abs(c_qnorm[i]))
                           * jnp.max(jnp.abs(c_knorm[i])))
            y_attn = lax.cond(logit_bound <= FULL_NOMAX_LOG2_BOUND,
                              lambda *a: _full_attn(*a, bounded=True),
                              lambda *a: _full_attn(*a, bounded=False),
                              qx, kxd, vxd, kcd, vcd)
            sw, sb = hy_short_w[i].astype(F32), hy_short_b[i].astype(F32)[None]
            tabs = _dft_tables()
            taps, inv_norm = _hyena_filters(S, hy_w1[i], hy_b1[i], hy_f1[i], hy_w2[i], hy_b2[i],
                                            hy_f2[i], hy_w3[i], hy_b3[i])
            spec = _filter_fft(taps, inv_norm, tabs[1], tabs[3])
            blocks = D_CH // LANES
            z = _hyena_conv(ux, 0, ux, blocks, sw, sb, True, hy_bias[i, 0:1], spec, 0, tabs)
            z = _hyena_conv(z, 0, ux, 2 * blocks, sw, sb, False, hy_bias[i, 1:2], spec, 1, tabs)
            ys = [y_attn, z]
            w_out = w_out_odd[i].astype(BF16)
            if need_ctx:
                raise NotImplementedError("context update of an odd layer is not needed at this depth")

        x, hx, route_t, route_c, ng = _out_proj(ys, w_out, x, mx[:, 2], norm_g[l, 1], mx[:, 3], mx[:, 4],
                                                router, sparse=True)
        if need_ctx:
            xc, hc, comb_c = _out_proj(yc, w_out, xc, mc[2:3], norm_g[l, 1], mc[3:4], mc[4:5], router,
                                       sparse=False)
            flat = lambda a: a.reshape(1, -1, a.shape[-1])
            xc = _moe(flat(hc), flat(comb_c), moe_wg, moe_wu, moe_wd, l, flat(xc), mc[5:6]).reshape(xc.shape)
        plan = _moe_plan(ng)
        sorted_rows = _moe_dispatch(hx, route_t, plan)
        x = _moe_combine(_moe_grouped(sorted_rows, plan, moe_wg, moe_wu, moe_wd, l), route_c, plan, x, mx[:, 5],
                         final_g=None if need_ctx else final_g)
    return x
```
